```python
import jax, jax.numpy as jnp
from jax import lax
import numpy as np

D_MODEL = 2048
BATCH = 2
SEQ = 4096
DEPTH = 4
DEC_BATCH = 8
DEC_SEQ = 4
PAST_LEN = 16384
PAGE_SIZE = 128

HEAD_DIM = 64
N_BRANCH = 4
BRANCH_W = D_MODEL // N_BRANCH
N_HEADS = BRANCH_W // HEAD_DIM
CONV_W = 3
NSA_G = 2
NSA_R = N_HEADS // NSA_G
MOBA_BLOCK = 256
MOBA_TOPK = 3
CMP_L = 32
CMP_S = 16
CMP_HID = 4 * HEAD_DIM
SLC_BLOCK = 64
SLC_N = 16
WINDOW = 512
D_FF = 4 * D_MODEL
Q_BLOCK = 128
SPARSE_Q_BLOCK = 32
RMS_EPS = 1e-6
NEG = -1e30
FORCE_SCORE = 1e4
FORGET_BIAS_INIT = 3.0
KV_W = NSA_G * HEAD_DIM
SEGMENTS = (
    ('conv_x', BRANCH_W), ('conv_b', BRANCH_W), ('conv_c', BRANCH_W),
    ('fox_q', BRANCH_W), ('fox_k', BRANCH_W), ('fox_v', BRANCH_W), ('fox_f', N_HEADS),
    ('moba_q', BRANCH_W), ('moba_k', BRANCH_W), ('moba_v', BRANCH_W),
    ('nsa_q', BRANCH_W), ('nsa_kc', KV_W), ('nsa_vc', KV_W), ('nsa_ks', KV_W), ('nsa_vs', KV_W),
    ('nsa_kw', KV_W), ('nsa_vw', KV_W), ('nsa_gate', 3 * N_HEADS),
    ('merge_gate', N_BRANCH * D_MODEL),
)
IN_W = sum(size for _, size in SEGMENTS)

kernel_name = 'hybrid_conv_fox_moba_nsa_decoder_step'


def split_proj(z):
    parts = {}
    off = 0
    for name, size in SEGMENTS:
        parts[name] = z[..., off:off + size]
        off += size
    return parts


def rms_norm(x, g):
    xf = x.astype(jnp.float32)
    y = xf * lax.rsqrt(jnp.mean(xf * xf, axis=-1, keepdims=True) + RMS_EPS)
    return (y * g.astype(jnp.float32)).astype(x.dtype)


def alibi_slopes(n):
    return jnp.exp2(-8.0 * jnp.arange(1, n + 1, dtype=jnp.float32) / n)


def masked_softmax(s, mask, axis=-1):
    p = jax.nn.softmax(jnp.where(mask, s, NEG), axis=axis)
    return p * mask


def sweep_queries(fn, block, *qs):
    n, t = qs[0].shape[:2]
    qb = block if t % block == 0 else t
    nb = t // qb
    xs = tuple(jnp.moveaxis(a.reshape((n, nb, qb) + a.shape[2:]), 1, 0) for a in qs)
    out = lax.map(lambda args: fn(args[0] * qb, *args[1:]), (jnp.arange(nb, dtype=jnp.int32),) + xs)
    return jnp.moveaxis(out, 0, 1).reshape((n, t) + out.shape[3:])


def gather_pages(pool, page_table):
    g = pool[page_table]
    return g.reshape((g.shape[0], g.shape[1] * g.shape[2]) + g.shape[3:])


def fox_attention(q, k, v, logf, q0):
    d = q.shape[-1]
    L = k.shape[1]
    scale = d ** -0.5
    F = jnp.cumsum(logf.astype(jnp.float32), axis=1)
    Fk = jnp.moveaxis(F, 1, 2)
    Fq = F[:, q0:]
    kpos = jnp.arange(L)

    def block(start, qb, fq):
        qpos = q0 + start + jnp.arange(qb.shape[1])
        s = jnp.einsum('nqhd,nkhd->nhqk', qb, k).astype(jnp.float32) * scale
        s = s + jnp.moveaxis(fq, 1, 2)[..., None] - Fk[:, :, None, :]
        p = masked_softmax(s, kpos[None, :] <= qpos[:, None])
        return jnp.einsum('nhqk,nkhd->nqhd', p.astype(q.dtype), v)

    return sweep_queries(block, Q_BLOCK, q, Fq)


def moba_attention(q, k, v, q0, slopes):
    n, tq, nh, d = q.shape
    L = k.shape[1]
    scale = d ** -0.5
    nbk = -(-L // MOBA_BLOCK)
    pad = nbk * MOBA_BLOCK - L

    def to_blocks(a):
        a = jnp.pad(a, ((0, 0), (0, pad), (0, 0), (0, 0)))
        return a.reshape(n, nbk, MOBA_BLOCK, nh, d).transpose(0, 3, 1, 2, 4)

    kb, vb = to_blocks(k), to_blocks(v)
    kmean = jnp.mean(kb.astype(jnp.float32), axis=3)
    topk = min(MOBA_TOPK, nbk)
    bidx = jnp.arange(nbk)
    ni = jnp.arange(n)[:, None, None, None]
    hi = jnp.arange(nh)[None, :, None, None]
    sl = slopes[None, :, None, None, None]

    def block(start, qb):
        m = qb.shape[1]
        qpos = q0 + start + jnp.arange(m)
        own = qpos // MOBA_BLOCK
        gs = jnp.einsum('nqhd,nhbd->nhqb', qb.astype(jnp.float32), kmean)
        gs = jnp.where(bidx[None, :] < own[:, None], gs, NEG)
        _, top = lax.top_k(gs, topk)
        sel_ok = top < own[:, None]
        idx = jnp.concatenate([top, jnp.broadcast_to(own[:, None], (n, nh, m, 1))], axis=-1)
        ok = jnp.concatenate([sel_ok, jnp.ones((n, nh, m, 1), bool)], axis=-1)
        kg = kb[ni, hi, idx]
        vg = vb[ni, hi, idx]
        kpos = idx[..., None] * MOBA_BLOCK + jnp.arange(MOBA_BLOCK)
        s = jnp.einsum('nqhd,nhqsjd->nhqsj', qb, kg).astype(jnp.float32) * scale
        s = s - sl * (qpos[:, None, None] - kpos).astype(jnp.float32)
        mask = ok[..., None] & (kpos <= qpos[:, None, None])
        p = masked_softmax(s, mask, axis=(-2, -1))
        return jnp.einsum('nhqsj,nhqsjd->nqhd', p.astype(q.dtype), vg)

    return sweep_queries(block, SPARSE_Q_BLOCK, q)


def compress_tokens(a, w1, w2, pe):
    n, L, g, d = a.shape
    nc = (L - CMP_L) // CMP_S + 1
    idx = jnp.arange(nc)[:, None] * CMP_S + jnp.arange(CMP_L)[None, :]
    blocks = a[:, idx] + pe[:, None, :].astype(a.dtype)
    flat = jnp.swapaxes(blocks, 2, 3).reshape(n, nc, g, CMP_L * d)
    return jax.nn.gelu(flat @ w1) @ w2


def nsa_attention(q, kc, vc, ks, vs, kw, vw, gate, q0, w_ofs, slopes, cmp_w1, cmp_w2, cmp_pe):
    n, tq, nh, d = q.shape
    L = kc.shape[1]
    dt = q.dtype
    scale = d ** -0.5
    qg = q.reshape(n, tq, NSA_G, NSA_R, d)
    k_cmp = compress_tokens(kc, cmp_w1[0], cmp_w2[0], cmp_pe[0])
    v_cmp = compress_tokens(vc, cmp_w1[1], cmp_w2[1], cmp_pe[1])
    nc = k_cmp.shape[1]
    cpos = jnp.arange(nc) * CMP_S + (CMP_L - 1)
    nsb = -(-L // SLC_BLOCK)
    padl = nsb * SLC_BLOCK - L

    def to_blocks(a):
        a = jnp.pad(a, ((0, 0), (0, padl), (0, 0), (0, 0)))
        return a.reshape(n, nsb, SLC_BLOCK, NSA_G, d).transpose(0, 3, 1, 2, 4)

    ks_b, vs_b = to_blocks(ks), to_blocks(vs)
    nsel = min(SLC_N, nsb)
    front = CMP_L // CMP_S - 1
    ratio = SLC_BLOCK // CMP_S
    width = ratio + front
    back = ratio * nsb + width - front - nc
    kw_p = jnp.pad(kw, ((0, 0), (WINDOW, 0), (0, 0), (0, 0)))
    vw_p = jnp.pad(vw, ((0, 0), (WINDOW, 0), (0, 0), (0, 0)))
    sl = slopes.reshape(NSA_G, NSA_R)[None, :, :, None, None]
    ni = jnp.arange(n)[:, None, None, None]
    gi = jnp.arange(NSA_G)[None, :, None, None]
    bj = jnp.arange(nsb)

    def block(start, qb, gb):
        m = qb.shape[1]
        qpos = q0 + start + jnp.arange(m)
        s = jnp.einsum('nqgrd,ncgd->ngrqc', qb, k_cmp).astype(jnp.float32) * scale
        s = s - sl * (qpos[:, None] - cpos[None, :]).astype(jnp.float32)
        p_cmp = masked_softmax(s, cpos[None, :] <= qpos[:, None])
        o_cmp = jnp.einsum('ngrqc,ncgd->nqgrd', p_cmp.astype(dt), v_cmp)
        imp = jnp.pad(p_cmp.sum(axis=2), ((0, 0), (0, 0), (0, 0), (front, back)))
        p_slc = imp[..., 0:ratio * nsb:ratio]
        for u in range(1, width):
            p_slc = p_slc + imp[..., u:u + ratio * nsb:ratio]
        own = qpos // SLC_BLOCK
        forced = (bj[None, :] == 0) | (bj[None, :] >= own[:, None] - 1)
        allowed = bj[None, :] <= own[:, None]
        score = jnp.where(allowed, jnp.where(forced, p_slc + FORCE_SCORE, p_slc), -1.0)
        _, top = lax.top_k(score, nsel)
        ok = top <= own[:, None]
        kg = ks_b[ni, gi, top]
        vg = vs_b[ni, gi, top]
        kpos = top[..., None] * SLC_BLOCK + jnp.arange(SLC_BLOCK)
        dist = (qpos[:, None, None] - kpos)[:, :, None].astype(jnp.float32)
        s2 = jnp.einsum('nqgrd,ngqsjd->ngrqsj', qb, kg).astype(jnp.float32) * scale - sl[..., None] * dist
        mask2 = (ok[..., None] & (kpos <= qpos[:, None, None]))[:, :, None]
        p2 = masked_softmax(s2, mask2, axis=(-2, -1))
        o_slc = jnp.einsum('ngrqsj,ngqsjd->nqgrd', p2.astype(dt), vg)
        off = q0 + start - w_ofs
        kwin = lax.dynamic_slice_in_dim(kw_p, off, WINDOW + m, axis=1)
        vwin = lax.dynamic_slice_in_dim(vw_p, off, WINDOW + m, axis=1)
        wpos = q0 + start - WINDOW + jnp.arange(WINDOW + m)
        s3 = jnp.einsum('nqgrd,nkgd->ngrqk', qb, kwin).astype(jnp.float32) * scale
        s3 = s3 - sl * (qpos[:, None] - wpos[None, :]).astype(jnp.float32)
        wmask = (wpos[None, :] <= qpos[:, None]) & (wpos[None, :] > qpos[:, None] - WINDOW) & (wpos[None, :] >= 0)
        p3 = masked_softmax(s3, wmask)
        o_win = jnp.einsum('ngrqk,nkgd->nqgrd', p3.astype(dt), vwin)
        return gb[..., 0:1] * o_cmp + gb[..., 1:2] * o_slc + gb[..., 2:3] * o_win

    return sweep_queries(block, SPARSE_Q_BLOCK, qg, gate)


def trunk_layer(x, q0, past, g_mix, w_in, b_forget, conv_w, cmp_w1, cmp_w2, cmp_pe,
                w_branch, w_out, g_mlp, w_up, w_down):
    n, t, _ = x.shape
    dt = x.dtype
    h = rms_norm(x, g_mix)
    z = split_proj(h @ w_in)

    def heads(a, nh):
        return a.reshape(n, t, nh, HEAD_DIM)

    def cat(old, new):
        return new if past is None else jnp.concatenate([old.astype(new.dtype), new], axis=1)

    u = z['conv_c'] * z['conv_x']
    pre = jnp.zeros((n, CONV_W - 1, BRANCH_W), dt) if past is None else past['conv'].astype(dt)
    ext = jnp.concatenate([pre, u], axis=1)
    y_conv = ext[:, 0:t] * conv_w[0]
    for j in range(1, CONV_W):
        y_conv = y_conv + ext[:, j:j + t] * conv_w[j]
    out_a = z['conv_b'] * y_conv
    new_conv = ext[:, -(CONV_W - 1):]

    fq, fk, fv = heads(z['fox_q'], N_HEADS), heads(z['fox_k'], N_HEADS), heads(z['fox_v'], N_HEADS)
    logf = jax.nn.log_sigmoid((z['fox_f'] + b_forget).astype(jnp.float32))
    lf_all = logf if past is None else jnp.concatenate([past['fox_logf'].astype(jnp.float32), logf], axis=1)
    out_b = fox_attention(fq, cat(past and past['fox_k'], fk), cat(past and past['fox_v'], fv), lf_all, q0)

    mq, mk, mv = heads(z['moba_q'], N_HEADS), heads(z['moba_k'], N_HEADS), heads(z['moba_v'], N_HEADS)
    out_c = moba_attention(mq, cat(past and past['moba_k'], mk), cat(past and past['moba_v'], mv), q0, alibi_slopes(N_HEADS))

    nq = heads(z['nsa_q'], N_HEADS)
    kc, vc, ks, vs, kw, vw = [heads(z[nm], NSA_G) for nm in ('nsa_kc', 'nsa_vc', 'nsa_ks', 'nsa_vs', 'nsa_kw', 'nsa_vw')]
    gate = jax.nn.sigmoid(z['nsa_gate'].astype(jnp.float32)).astype(dt).reshape(n, t, NSA_G, NSA_R, 3)
    kw_all = cat(past and past['win_k'], kw)
    vw_all = cat(past and past['win_v'], vw)
    w_ofs = 0 if past is None else q0 - past['win_k'].shape[1]
    out_d = nsa_attention(nq, cat(past and past['nsa_kc'], kc), cat(past and past['nsa_vc'], vc),
                          cat(past and past['nsa_ks'], ks), cat(past and past['nsa_vs'], vs),
                          kw_all, vw_all, gate, q0, w_ofs, alibi_slopes(N_HEADS), cmp_w1, cmp_w2, cmp_pe)

    outs = jnp.stack([out_a, out_b.reshape(n, t, BRANCH_W), out_c.reshape(n, t, BRANCH_W),
                      out_d.reshape(n, t, BRANCH_W)], axis=2)
    br = jnp.einsum('ntbc,bcd->ntbd', outs, w_branch)
    g = jax.nn.sigmoid(z['merge_gate'].astype(jnp.float32)).astype(dt).reshape(n, t, N_BRANCH, D_MODEL)
    x = x + jnp.sum(g * br, axis=2) @ w_out
    hm = rms_norm(x, g_mlp)
    x = x + jnp.square(jax.nn.relu(hm @ w_up)) @ w_down

    wb = min(WINDOW, t) if past is None else past['win_k'].shape[1]
    new_state = (new_conv,
                 jnp.stack([fk, fv], axis=2),
                 logf.astype(dt),
                 jnp.stack([mk, mv], axis=2),
                 jnp.stack([jnp.stack([kc, vc], axis=2), jnp.stack([ks, vs], axis=2)], axis=2),
                 jnp.stack([kw_all, vw_all], axis=2)[:, -wb:])
    return x, new_state


def setup_inputs(seed: int = 0) -> dict:
    key = jax.random.key(seed)
    ks = jax.random.split(key, 24)
    f32 = jnp.float32
    n_pages = PAST_LEN // PAGE_SIZE
    pool = (5 * DEC_BATCH * n_pages + 3) // 4
    wbuf = min(WINDOW, PAST_LEN)

    def nrm(k, shape, scale=1.0):
        return scale * jax.random.normal(k, shape, f32)

    return {
        'x_prompt': nrm(ks[0], (BATCH, SEQ, D_MODEL)),
        'x_sample': nrm(ks[1], (DEC_BATCH, DEC_SEQ, D_MODEL)),
        'state_conv': nrm(ks[2], (DEPTH, DEC_BATCH, CONV_W - 1, BRANCH_W)),
        'cache_fox_kv': nrm(ks[3], (DEPTH, pool, PAGE_SIZE, 2, N_HEADS, HEAD_DIM)),
        'cache_fox_logf': jax.nn.log_sigmoid(FORGET_BIAS_INIT + nrm(ks[4], (DEPTH, pool, PAGE_SIZE, N_HEADS))),
        'cache_moba_kv': nrm(ks[5], (DEPTH, pool, PAGE_SIZE, 2, N_HEADS, HEAD_DIM)),
        'cache_nsa_kv': nrm(ks[6], (DEPTH, pool, PAGE_SIZE, 2, 2, NSA_G, HEAD_DIM)),
        'state_nsa_win': nrm(ks[7], (DEPTH, DEC_BATCH, wbuf, 2, NSA_G, HEAD_DIM)),
        'page_table': jax.random.permutation(ks[8], pool)[:DEC_BATCH * n_pages].reshape(DEC_BATCH, n_pages).astype(jnp.int32),
        'g_mix': 1.0 + nrm(ks[9], (DEPTH, D_MODEL), 0.02),
        'w_in': nrm(ks[10], (DEPTH, D_MODEL, IN_W), D_MODEL ** -0.5),
        'b_forget': FORGET_BIAS_INIT + nrm(ks[11], (DEPTH, N_HEADS), 0.5),
        'conv_w': nrm(ks[12], (DEPTH, CONV_W, BRANCH_W), CONV_W ** -0.5),
        'cmp_w1': nrm(ks[13], (DEPTH, 2, CMP_L * HEAD_DIM, CMP_HID), (CMP_L * HEAD_DIM) ** -0.5),
        'cmp_w2': nrm(ks[14], (DEPTH, 2, CMP_HID, HEAD_DIM), CMP_HID ** -0.5),
        'cmp_pe': nrm(ks[15], (DEPTH, 2, CMP_L, HEAD_DIM), 0.1),
        'w_branch': nrm(ks[16], (DEPTH, N_BRANCH, BRANCH_W, D_MODEL), BRANCH_W ** -0.5),
        'w_out': nrm(ks[17], (DEPTH, D_MODEL, D_MODEL), (N_BRANCH * D_MODEL) ** -0.5),
        'g_mlp': 1.0 + nrm(ks[18], (DEPTH, D_MODEL), 0.02),
        'w_up': nrm(ks[19], (DEPTH, D_MODEL, D_FF), D_MODEL ** -0.5),
        'w_down': nrm(ks[20], (DEPTH, D_FF, D_MODEL), D_FF ** -0.5),
        'g_final': 1.0 + nrm(ks[21], (D_MODEL,), 0.02),
    }


def reference(x_prompt, x_sample, state_conv, cache_fox_kv, cache_fox_logf, cache_moba_kv, cache_nsa_kv,
              state_nsa_win, page_table, g_mix, w_in, b_forget, conv_w, cmp_w1, cmp_w2, cmp_pe,
              w_branch, w_out, g_mlp, w_up, w_down, g_final):
    q0_sample = page_table.shape[1] * cache_fox_kv.shape[2]
    xp, xs = x_prompt, x_sample
    new_p, new_s = [], []
    for l in range(DEPTH):
        params = (g_mix[l], w_in[l], b_forget[l], conv_w[l], cmp_w1[l], cmp_w2[l], cmp_pe[l],
                  w_branch[l], w_out[l], g_mlp[l], w_up[l], w_down[l])
        xp, st_p = trunk_layer(xp, 0, None, *params)
        new_p.append(st_p)
        fox_kv = gather_pages(cache_fox_kv[l], page_table)
        moba_kv = gather_pages(cache_moba_kv[l], page_table)
        nsa_kv = gather_pages(cache_nsa_kv[l], page_table)
        win = state_nsa_win[l]
        past = {'conv': state_conv[l],
                'fox_k': fox_kv[:, :, 0], 'fox_v': fox_kv[:, :, 1],
                'fox_logf': gather_pages(cache_fox_logf[l], page_table),
                'moba_k': moba_kv[:, :, 0], 'moba_v': moba_kv[:, :, 1],
                'nsa_kc': nsa_kv[:, :, 0, 0], 'nsa_vc': nsa_kv[:, :, 0, 1],
                'nsa_ks': nsa_kv[:, :, 1, 0], 'nsa_vs': nsa_kv[:, :, 1, 1],
                'win_k': win[:, :, 0], 'win_v': win[:, :, 1]}
        xs, st_s = trunk_layer(xs, q0_sample, past, *params)
        new_s.append(st_s)
    y_prompt = rms_norm(xp, g_final)
    y_sample = rms_norm(xs, g_final)
    conv_p, fox_kv_p, fox_logf_p, moba_kv_p, nsa_kv_p, win_p = [jnp.stack(a) for a in zip(*new_p)]
    conv_s, fox_kv_s, fox_logf_s, moba_kv_s, nsa_kv_s, win_s = [jnp.stack(a) for a in zip(*new_s)]
    return (y_prompt, y_sample, conv_p, conv_s, fox_kv_p, fox_kv_s, fox_logf_p, fox_logf_s,
            moba_kv_p, moba_kv_s, nsa_kv_p, nsa_kv_s, win_p, win_s)
```

```python
import functools

import jax
import jax.numpy as jnp
from jax import lax
from jax.experimental import pallas as pl
from jax.experimental.pallas import tpu as pltpu

F32 = jnp.float32
BF16 = jnp.bfloat16

D_MODEL = 2048
HEAD_DIM = 64
N_BRANCH = 4
BRANCH_W = D_MODEL // N_BRANCH
N_HEADS = BRANCH_W // HEAD_DIM
CONV_W = 3
NSA_G = 2
NSA_R = N_HEADS // NSA_G
MOBA_BLOCK = 256
MOBA_TOPK = 3
CMP_L = 32
CMP_S = 16
CMP_HID = 4 * HEAD_DIM
SLC_BLOCK = 64
SLC_N = 16
WINDOW = 512
D_FF = 4 * D_MODEL
Q_BLOCK = 128
SPARSE_Q_BLOCK = 32
RMS_EPS = 1e-6
NEG = -1e30
FORCE_SCORE = 1e4
KV_W = NSA_G * HEAD_DIM

OFF_CONV = 0
OFF_FOX = 3 * BRANCH_W
OFF_FOX_F = OFF_FOX + 3 * BRANCH_W
OFF_MOBA = OFF_FOX_F + N_HEADS
OFF_NSA = OFF_MOBA + 3 * BRANCH_W
OFF_NSA_GATE = OFF_NSA + BRANCH_W + 6 * KV_W
OFF_MERGE = OFF_NSA_GATE + 3 * N_HEADS
IN_W = OFF_MERGE + N_BRANCH * D_MODEL
SMALL_W = 128

VMEM_LIMIT = 56 * 1024 * 1024


def _cparams(*sem):
    return pltpu.CompilerParams(dimension_semantics=sem, vmem_limit_bytes=VMEM_LIMIT)


def _rms(x, g):
    return x * lax.rsqrt(jnp.mean(x * x, axis=-1, keepdims=True) + RMS_EPS) * g


def _norm_kernel(x_ref, g_ref, o_ref):
    o_ref[...] = _rms(x_ref[...], g_ref[...]).astype(o_ref.dtype)


def rms_norm_call(x, g, out_dtype, tm):
    t, d = x.shape
    return pl.pallas_call(
        _norm_kernel,
        grid=(t // tm,),
        in_specs=[pl.BlockSpec((tm, d), lambda i: (i, 0)),
                  pl.BlockSpec((1, d), lambda i: (0, 0))],
        out_specs=pl.BlockSpec((tm, d), lambda i: (i, 0)),
        out_shape=jax.ShapeDtypeStruct((t, d), out_dtype),
        compiler_params=_cparams("parallel"),
        name="rms_norm",
    )(x, g.reshape(1, d))


def _proj_kernel(h_ref, w_ref, *out_refs, outs):
    h = h_ref[...]
    for (c0, c1, kind), o_ref in zip(outs, out_refs):
        for a in range(c0, c1, 512):
            b = min(a + 512, c1)
            z = jnp.dot(h, w_ref[:, a:b], preferred_element_type=F32)
            if kind == "sigmoid":
                z = jax.nn.sigmoid(z)
            if kind == "bf16T":
                o_ref[a - c0:b - c0, :] = z.T.astype(BF16)
            else:
                o_ref[:, a - c0:b - c0] = z.astype(o_ref.dtype)


def proj_call(h, w, outs, tm, name):
    t, d = h.shape
    n = w.shape[1]
    out_specs, out_shapes = [], []
    for c0, c1, kind in outs:
        if kind == "bf16T":
            out_specs.append(pl.BlockSpec((c1 - c0, tm), lambda i: (0, i)))
            out_shapes.append(jax.ShapeDtypeStruct((c1 - c0, t), BF16))
        else:
            out_specs.append(pl.BlockSpec((tm, c1 - c0), lambda i: (i, 0)))
            out_shapes.append(jax.ShapeDtypeStruct((t, c1 - c0), BF16 if kind == "bf16" else F32))
    return pl.pallas_call(
        functools.partial(_proj_kernel, outs=tuple(outs)),
        grid=(t // tm,),
        in_specs=[pl.BlockSpec((tm, d), lambda i: (i, 0)),
                  pl.BlockSpec((d, n), lambda i: (0, 0))],
        out_specs=out_specs,
        out_shape=out_shapes,
        compiler_params=_cparams("parallel"),
        name=name,
    )(h, w)


def _gate_kernel(h_ref, w_ref, o_ref):
    z = jnp.dot(h_ref[...], w_ref[...], preferred_element_type=F32)
    o_ref[...] = jax.nn.sigmoid(z)


def gate_call(h, w, tm, tn):
    t, d = h.shape
    n = w.shape[1]
    return pl.pallas_call(
        _gate_kernel,
        grid=(t // tm, n // tn),
        in_specs=[pl.BlockSpec((tm, d), lambda i, j: (i, 0)),
                  pl.BlockSpec((d, tn), lambda i, j: (0, j))],
        out_specs=pl.BlockSpec((tm, tn), lambda i, j: (i, j)),
        out_shape=jax.ShapeDtypeStruct((t, n), F32),
        compiler_params=_cparams("parallel", "arbitrary"),
        name="merge_gate_proj",
    )(h, w)


def _merge_kernel(oa_ref, ob_ref, oc_ref, od_ref, g0_ref, g1_ref, g2_ref, g3_ref, wb_ref, o_ref):
    acc = None
    for b, (o, g) in enumerate(zip((oa_ref, ob_ref, oc_ref, od_ref), (g0_ref, g1_ref, g2_ref, g3_ref))):
        br = jnp.dot(o[...].astype(BF16), wb_ref[b], preferred_element_type=F32)
        term = g[...] * br
        acc = term if acc is None else acc + term
    o_ref[...] = acc.astype(o_ref.dtype)


def merge_call(branches, gate, w_branch, tm, tn):
    t = branches[0].shape[0]
    nj = D_MODEL // tn
    gate_specs = [pl.BlockSpec((tm, tn), functools.partial(lambda i, j, b: (i, b * nj + j), b=b))
                  for b in range(N_BRANCH)]
    return pl.pallas_call(
        _merge_kernel,
        grid=(t // tm, nj),
        in_specs=[pl.BlockSpec((tm, BRANCH_W), lambda i, j: (i, 0))] * N_BRANCH + gate_specs
        + [pl.BlockSpec((N_BRANCH, BRANCH_W, tn), lambda i, j: (0, 0, j))],
        out_specs=pl.BlockSpec((tm, tn), lambda i, j: (i, j)),
        out_shape=jax.ShapeDtypeStruct((t, D_MODEL), BF16),
        compiler_params=_cparams("parallel", "arbitrary"),
        name="branch_merge",
    )(*branches, gate, gate, gate, gate, w_branch)


def _outproj_kernel(m_ref, w_ref, x_ref, g_ref, xo_ref, hn_ref):
    xn = x_ref[...] + jnp.dot(m_ref[...], w_ref[...], preferred_element_type=F32)
    xo_ref[...] = xn
    hn_ref[...] = _rms(xn, g_ref[...]).astype(hn_ref.dtype)


def outproj_call(merged, w_out, x, g_next, tm):
    t = x.shape[0]
    return pl.pallas_call(
        _outproj_kernel,
        grid=(t // tm,),
        in_specs=[pl.BlockSpec((tm, D_MODEL), lambda i: (i, 0)),
                  pl.BlockSpec((D_MODEL, D_MODEL), lambda i: (0, 0)),
                  pl.BlockSpec((tm, D_MODEL), lambda i: (i, 0)),
                  pl.BlockSpec((1, D_MODEL), lambda i: (0, 0))],
        out_specs=[pl.BlockSpec((tm, D_MODEL), lambda i: (i, 0)),
                   pl.BlockSpec((tm, D_MODEL), lambda i: (i, 0))],
        out_shape=[jax.ShapeDtypeStruct((t, D_MODEL), F32),
                   jax.ShapeDtypeStruct((t, D_MODEL), BF16)],
        compiler_params=_cparams("parallel"),
        name="out_proj",
    )(merged, w_out, x, g_next.reshape(1, D_MODEL))


def _mlp_kernel(h_ref, wu_ref, wd_ref, x_ref, g_ref, xo_ref, hn_ref, acc_ref):
    j = pl.program_id(1)

    @pl.when(j == 0)
    def _():
        acc_ref[...] = jnp.zeros_like(acc_ref)

    a = jnp.dot(h_ref[...], wu_ref[...], preferred_element_type=F32)
    a = jnp.square(jnp.maximum(a, 0.0)).astype(BF16)
    acc_ref[...] += jnp.dot(a, wd_ref[...], preferred_element_type=F32)

    @pl.when(j == pl.num_programs(1) - 1)
    def _():
        xn = x_ref[...] + acc_ref[...]
        xo_ref[...] = xn
        hn_ref[...] = _rms(xn, g_ref[...]).astype(hn_ref.dtype)


def mlp_call(h, w_up, w_down, x, g_next, next_dtype, tm, tf):
    t = x.shape[0]
    return pl.pallas_call(
        _mlp_kernel,
        grid=(t // tm, D_FF // tf),
        in_specs=[pl.BlockSpec((tm, D_MODEL), lambda i, j: (i, 0)),
                  pl.BlockSpec((D_MODEL, tf), lambda i, j: (0, j)),
                  pl.BlockSpec((tf, D_MODEL), lambda i, j: (j, 0)),
                  pl.BlockSpec((tm, D_MODEL), lambda i, j: (i, 0)),
                  pl.BlockSpec((1, D_MODEL), lambda i, j: (0, 0))],
        out_specs=[pl.BlockSpec((tm, D_MODEL), lambda i, j: (i, 0)),
                   pl.BlockSpec((tm, D_MODEL), lambda i, j: (i, 0))],
        out_shape=[jax.ShapeDtypeStruct((t, D_MODEL), F32),
                   jax.ShapeDtypeStruct((t, D_MODEL), next_dtype)],
        scratch_shapes=[pltpu.VMEM((tm, D_MODEL), F32)],
        compiler_params=_cparams("parallel", "arbitrary"),
        name="mlp",
    )(h, w_up, w_down, x, g_next.reshape(1, D_MODEL))


def alibi_slopes(n):
    return jnp.exp2(-8.0 * jnp.arange(1, n + 1, dtype=jnp.float32) / n)


def masked_softmax(s, mask, axis=-1):
    p = jax.nn.softmax(jnp.where(mask, s, NEG), axis=axis)
    return p * mask


def sweep_queries(fn, block, *qs):
    n, t = qs[0].shape[:2]
    qb = block if t % block == 0 else t
    nb = t // qb
    xs = tuple(jnp.moveaxis(a.reshape((n, nb, qb) + a.shape[2:]), 1, 0) for a in qs)
    out = lax.map(lambda args: fn(args[0] * qb, *args[1:]), (jnp.arange(nb, dtype=jnp.int32),) + xs)
    return jnp.moveaxis(out, 0, 1).reshape((n, t) + out.shape[3:])


def gather_pages(pool, page_table):
    g = pool[page_table]
    return g.reshape((g.shape[0], g.shape[1] * g.shape[2]) + g.shape[3:])


def fox_attention(q, k, v, logf, q0):
    d = q.shape[-1]
    L = k.shape[1]
    scale = d ** -0.5
    F = jnp.cumsum(logf.astype(jnp.float32), axis=1)
    Fk = jnp.moveaxis(F, 1, 2)
    Fq = F[:, q0:]
    kpos = jnp.arange(L)

    def block(start, qb, fq):
        qpos = q0 + start + jnp.arange(qb.shape[1])
        s = jnp.einsum('nqhd,nkhd->nhqk', qb, k).astype(jnp.float32) * scale
        s = s + jnp.moveaxis(fq, 1, 2)[..., None] - Fk[:, :, None, :]
        p = masked_softmax(s, kpos[None, :] <= qpos[:, None])
        return jnp.einsum('nhqk,nkhd->nqhd', p.astype(q.dtype), v)

    return sweep_queries(block, Q_BLOCK, q, Fq)


def moba_attention(q, k, v, q0, slopes):
    n, tq, nh, d = q.shape
    L = k.shape[1]
    scale = d ** -0.5
    nbk = -(-L // MOBA_BLOCK)
    pad = nbk * MOBA_BLOCK - L

    def to_blocks(a):
        a = jnp.pad(a, ((0, 0), (0, pad), (0, 0), (0, 0)))
        return a.reshape(n, nbk, MOBA_BLOCK, nh, d).transpose(0, 3, 1, 2, 4)

    kb, vb = to_blocks(k), to_blocks(v)
    kmean = jnp.mean(kb.astype(jnp.float32), axis=3)
    topk = min(MOBA_TOPK, nbk)
    bidx = jnp.arange(nbk)
    ni = jnp.arange(n)[:, None, None, None]
    hi = jnp.arange(nh)[None, :, None, None]
    sl = slopes[None, :, None, None, None]

    def block(start, qb):
        m = qb.shape[1]
        qpos = q0 + start + jnp.arange(m)
        own = qpos // MOBA_BLOCK
        gs = jnp.einsum('nqhd,nhbd->nhqb', qb.astype(jnp.float32), kmean)
        gs = jnp.where(bidx[None, :] < own[:, None], gs, NEG)
        _, top = lax.top_k(gs, topk)
        sel_ok = top < own[:, None]
        idx = jnp.concatenate([top, jnp.broadcast_to(own[:, None], (n, nh, m, 1))], axis=-1)
        ok = jnp.concatenate([sel_ok, jnp.ones((n, nh, m, 1), bool)], axis=-1)
        kg = kb[ni, hi, idx]
        vg = vb[ni, hi, idx]
        kpos = idx[..., None] * MOBA_BLOCK + jnp.arange(MOBA_BLOCK)
        s = jnp.einsum('nqhd,nhqsjd->nhqsj', qb, kg).astype(jnp.float32) * scale
        s = s - sl * (qpos[:, None, None] - kpos).astype(jnp.float32)
        mask = ok[..., None] & (kpos <= qpos[:, None, None])
        p = masked_softmax(s, mask, axis=(-2, -1))
        return jnp.einsum('nhqsj,nhqsjd->nqhd', p.astype(q.dtype), vg)

    return sweep_queries(block, SPARSE_Q_BLOCK, q)


def compress_tokens(a, w1, w2, pe):
    n, L, g, d = a.shape
    nc = (L - CMP_L) // CMP_S + 1
    idx = jnp.arange(nc)[:, None] * CMP_S + jnp.arange(CMP_L)[None, :]
    blocks = a[:, idx] + pe[:, None, :].astype(a.dtype)
    flat = jnp.swapaxes(blocks, 2, 3).reshape(n, nc, g, CMP_L * d)
    return jax.nn.gelu(flat @ w1) @ w2


def nsa_attention(q, kc, vc, ks, vs, kw, vw, gate, q0, w_ofs, slopes, cmp_w1, cmp_w2, cmp_pe):
    n, tq, nh, d = q.shape
    L = kc.shape[1]
    dt = q.dtype
    scale = d ** -0.5
    qg = q.reshape(n, tq, NSA_G, NSA_R, d)
    k_cmp = compress_tokens(kc, cmp_w1[0], cmp_w2[0], cmp_pe[0])
    v_cmp = compress_tokens(vc, cmp_w1[1], cmp_w2[1], cmp_pe[1])
    nc = k_cmp.shape[1]
    cpos = jnp.arange(nc) * CMP_S + (CMP_L - 1)
    nsb = -(-L // SLC_BLOCK)
    padl = nsb * SLC_BLOCK - L

    def to_blocks(a):
        a = jnp.pad(a, ((0, 0), (0, padl), (0, 0), (0, 0)))
        return a.reshape(n, nsb, SLC_BLOCK, NSA_G, d).transpose(0, 3, 1, 2, 4)

    ks_b, vs_b = to_blocks(ks), to_blocks(vs)
    nsel = min(SLC_N, nsb)
    front = CMP_L // CMP_S - 1
    ratio = SLC_BLOCK // CMP_S
    width = ratio + front
    back = ratio * nsb + width - front - nc
    kw_p = jnp.pad(kw, ((0, 0), (WINDOW, 0), (0, 0), (0, 0)))
    vw_p = jnp.pad(vw, ((0, 0), (WINDOW, 0), (0, 0), (0, 0)))
    sl = slopes.reshape(NSA_G, NSA_R)[None, :, :, None, None]
    ni = jnp.arange(n)[:, None, None, None]
    gi = jnp.arange(NSA_G)[None, :, None, None]
    bj = jnp.arange(nsb)

    def block(start, qb, gb):
        m = qb.shape[1]
        qpos = q0 + start + jnp.arange(m)
        s = jnp.einsum('nqgrd,ncgd->ngrqc', qb, k_cmp).astype(jnp.float32) * scale
        s = s - sl * (qpos[:, None] - cpos[None, :]).astype(jnp.float32)
        p_cmp = masked_softmax(s, cpos[None, :] <= qpos[:, None])
        o_cmp = jnp.einsum('ngrqc,ncgd->nqgrd', p_cmp.astype(dt), v_cmp)
        imp = jnp.pad(p_cmp.sum(axis=2), ((0, 0), (0, 0), (0, 0), (front, back)))
        p_slc = imp[..., 0:ratio * nsb:ratio]
        for u in range(1, width):
            p_slc = p_slc + imp[..., u:u + ratio * nsb:ratio]
        own = qpos // SLC_BLOCK
        forced = (bj[None, :] == 0) | (bj[None, :] >= own[:, None] - 1)
        allowed = bj[None, :] <= own[:, None]
        score = jnp.where(allowed, jnp.where(forced, p_slc + FORCE_SCORE, p_slc), -1.0)
        _, top = lax.top_k(score, nsel)
        ok = top <= own[:, None]
        kg = ks_b[ni, gi, top]
        vg = vs_b[ni, gi, top]
        kpos = top[..., None] * SLC_BLOCK + jnp.arange(SLC_BLOCK)
        dist = (qpos[:, None, None] - kpos)[:, :, None].astype(jnp.float32)
        s2 = jnp.einsum('nqgrd,ngqsjd->ngrqsj', qb, kg).astype(jnp.float32) * scale - sl[..., None] * dist
        mask2 = (ok[..., None] & (kpos <= qpos[:, None, None]))[:, :, None]
        p2 = masked_softmax(s2, mask2, axis=(-2, -1))
        o_slc = jnp.einsum('ngrqsj,ngqsjd->nqgrd', p2.astype(dt), vg)
        off = q0 + start - w_ofs
        kwin = lax.dynamic_slice_in_dim(kw_p, off, WINDOW + m, axis=1)
        vwin = lax.dynamic_slice_in_dim(vw_p, off, WINDOW + m, axis=1)
        wpos = q0 + start - WINDOW + jnp.arange(WINDOW + m)
        s3 = jnp.einsum('nqgrd,nkgd->ngrqk', qb, kwin).astype(jnp.float32) * scale
        s3 = s3 - sl * (qpos[:, None] - wpos[None, :]).astype(jnp.float32)
        wmask = (wpos[None, :] <= qpos[:, None]) & (wpos[None, :] > qpos[:, None] - WINDOW) & (wpos[None, :] >= 0)
        p3 = masked_softmax(s3, wmask)
        o_win = jnp.einsum('ngrqk,nkgd->nqgrd', p3.astype(dt), vwin)
        return gb[..., 0:1] * o_cmp + gb[..., 1:2] * o_slc + gb[..., 2:3] * o_win

    return sweep_queries(block, SPARSE_Q_BLOCK, qg, gate)


def _layer_weights(l, w_in, w_branch, w_out, w_up, w_down):
    wl = w_in[l]
    small = jnp.concatenate(
        [wl[:, OFF_FOX_F:OFF_FOX_F + N_HEADS], wl[:, OFF_NSA_GATE:OFF_NSA_GATE + 3 * N_HEADS],
         jnp.zeros((D_MODEL, SMALL_W - 4 * N_HEADS), wl.dtype)], axis=1)
    return dict(
        conv=wl[:, OFF_CONV:OFF_FOX].astype(BF16),
        fox=wl[:, OFF_FOX:OFF_FOX_F].astype(BF16),
        moba=wl[:, OFF_MOBA:OFF_NSA].astype(BF16),
        nsa=wl[:, OFF_NSA:OFF_NSA_GATE].astype(BF16),
        small=small.astype(BF16),
        gate=wl[:, OFF_MERGE:].astype(BF16),
        branch=w_branch[l].astype(BF16),
        out=w_out[l].astype(BF16),
        up=w_up[l].astype(BF16),
        down=w_down[l].astype(BF16),
    )


def _trunk_layer(x, h, n, t, q0, past, w, b_forget, conv_w, cmp_w1, cmp_w2, cmp_pe, g_mlp, g_next, next_dtype, tm):
    dt = F32
    bw = BRANCH_W
    (z_conv,) = proj_call(h, w["conv"], [(0, 3 * bw, "f32")], tm, "proj_conv")
    fox_q, fox_kv = proj_call(h, w["fox"], [(0, bw, "f32"), (bw, 3 * bw, "f32")], tm, "proj_fox")
    moba_q, moba_kv = proj_call(h, w["moba"], [(0, bw, "f32"), (bw, 3 * bw, "f32")], tm, "proj_moba")
    nsa_q, nsa_kv, nsa_win = proj_call(
        h, w["nsa"], [(0, bw, "f32"), (bw, bw + 4 * KV_W, "f32"), (bw + 4 * KV_W, bw + 6 * KV_W, "f32")],
        tm, "proj_nsa")
    (z_small,) = proj_call(h, w["small"], [(0, SMALL_W, "f32")], tm, "proj_small")
    gate = gate_call(h, w["gate"], tm, 1024)

    def heads(a, nh):
        return a.reshape(n, t, nh, HEAD_DIM)

    def cat(old, new):
        return new if past is None else jnp.concatenate([old.astype(new.dtype), new], axis=1)

    zc = z_conv.reshape(n, t, 3 * bw)
    conv_x, conv_b, conv_c = zc[..., :bw], zc[..., bw:2 * bw], zc[..., 2 * bw:]
    u = conv_c * conv_x
    pre = jnp.zeros((n, CONV_W - 1, bw), dt) if past is None else past['conv'].astype(dt)
    ext = jnp.concatenate([pre, u], axis=1)
    y_conv = ext[:, 0:t] * conv_w[0]
    for j in range(1, CONV_W):
        y_conv = y_conv + ext[:, j:j + t] * conv_w[j]
    out_a = conv_b * y_conv
    new_conv = ext[:, -(CONV_W - 1):]

    fkv = fox_kv.reshape(n, t, 2, N_HEADS, HEAD_DIM)
    fq, fk, fv = heads(fox_q, N_HEADS), fkv[:, :, 0], fkv[:, :, 1]
    fox_f = z_small.reshape(n, t, SMALL_W)[..., :N_HEADS]
    logf = jax.nn.log_sigmoid(fox_f + b_forget)
    lf_all = logf if past is None else jnp.concatenate([past['fox_logf'].astype(F32), logf], axis=1)
    out_b = fox_attention(fq, cat(past and past['fox_k'], fk), cat(past and past['fox_v'], fv), lf_all, q0)

    mkv = moba_kv.reshape(n, t, 2, N_HEADS, HEAD_DIM)
    mq, mk, mv = heads(moba_q, N_HEADS), mkv[:, :, 0], mkv[:, :, 1]
    out_c = moba_attention(mq, cat(past and past['moba_k'], mk), cat(past and past['moba_v'], mv), q0,
                           alibi_slopes(N_HEADS))

    nq = heads(nsa_q, N_HEADS)
    nkv = nsa_kv.reshape(n, t, 2, 2, NSA_G, HEAD_DIM)
    kc, vc, ks, vs = nkv[:, :, 0, 0], nkv[:, :, 0, 1], nkv[:, :, 1, 0], nkv[:, :, 1, 1]
    nwin = nsa_win.reshape(n, t, 2, NSA_G, HEAD_DIM)
    kw, vw = nwin[:, :, 0], nwin[:, :, 1]
    ngate = jax.nn.sigmoid(z_small.reshape(n, t, SMALL_W)[..., N_HEADS:4 * N_HEADS]).reshape(n, t, NSA_G, NSA_R, 3)
    kw_all = cat(past and past['win_k'], kw)
    vw_all = cat(past and past['win_v'], vw)
    w_ofs = 0 if past is None else q0 - past['win_k'].shape[1]
    out_d = nsa_attention(nq, cat(past and past['nsa_kc'], kc), cat(past and past['nsa_vc'], vc),
                          cat(past and past['nsa_ks'], ks), cat(past and past['nsa_vs'], vs),
                          kw_all, vw_all, ngate, q0, w_ofs, alibi_slopes(N_HEADS), cmp_w1, cmp_w2, cmp_pe)

    branches = [a.reshape(n * t, bw) for a in (out_a, out_b, out_c, out_d)]
    merged = merge_call(branches, gate, w["branch"], tm, 512)
    x1, hm = outproj_call(merged, w["out"], x, g_mlp, tm)
    x2, hn = mlp_call(hm, w["up"], w["down"], x1, g_next, next_dtype, tm, 512)

    wb = min(WINDOW, t) if past is None else past['win_k'].shape[1]
    new_state = (new_conv, fkv, logf.astype(dt), mkv, nkv,
                 jnp.stack([kw_all, vw_all], axis=2)[:, -wb:])
    return x2, hn, new_state


def kernel(x_prompt, x_sample, state_conv, cache_fox_kv, cache_fox_logf, cache_moba_kv, cache_nsa_kv,
           state_nsa_win, page_table, g_mix, w_in, b_forget, conv_w, cmp_w1, cmp_w2, cmp_pe,
           w_branch, w_out, g_mlp, w_up, w_down, g_final):
    depth = w_in.shape[0]
    nb, seq, _ = x_prompt.shape
    db, dseq, _ = x_sample.shape
    q0_sample = page_table.shape[1] * cache_fox_kv.shape[2]
    tm_p, tm_s = 512, db * dseq

    xp = x_prompt.reshape(nb * seq, D_MODEL)
    xs = x_sample.reshape(db * dseq, D_MODEL)
    hp = rms_norm_call(xp, g_mix[0], BF16, tm_p)
    hs = rms_norm_call(xs, g_mix[0], BF16, tm_s)
    new_p, new_s = [], []
    for l in range(depth):
        w = _layer_weights(l, w_in, w_branch, w_out, w_up, w_down)
        last = l == depth - 1
        g_next = g_final if last else g_mix[l + 1]
        next_dtype = F32 if last else BF16
        params = (w, b_forget[l], conv_w[l], cmp_w1[l], cmp_w2[l], cmp_pe[l], g_mlp[l], g_next, next_dtype)
        xp, hp, st_p = _trunk_layer(xp, hp, nb, seq, 0, None, *params, tm_p)
        new_p.append(st_p)
        fox_kv = gather_pages(cache_fox_kv[l], page_table)
        moba_kv = gather_pages(cache_moba_kv[l], page_table)
        nsa_kv = gather_pages(cache_nsa_kv[l], page_table)
        win = state_nsa_win[l]
        past = {'conv': state_conv[l],
                'fox_k': fox_kv[:, :, 0], 'fox_v': fox_kv[:, :, 1],
                'fox_logf': gather_pages(cache_fox_logf[l], page_table),
                'moba_k': moba_kv[:, :, 0], 'moba_v': moba_kv[:, :, 1],
                'nsa_kc': nsa_kv[:, :, 0, 0], 'nsa_vc': nsa_kv[:, :, 0, 1],
                'nsa_ks': nsa_kv[:, :, 1, 0], 'nsa_vs': nsa_kv[:, :, 1, 1],
                'win_k': win[:, :, 0], 'win_v': win[:, :, 1]}
        xs, hs, st_s = _trunk_layer(xs, hs, db, dseq, q0_sample, past, *params, tm_s)
        new_s.append(st_s)
    y_prompt = hp.reshape(nb, seq, D_MODEL)
    y_sample = hs.reshape(db, dseq, D_MODEL)
    conv_p, fox_kv_p, fox_logf_p, moba_kv_p, nsa_kv_p, win_p = [jnp.stack(a) for a in zip(*new_p)]
    conv_s, fox_kv_s, fox_logf_s, moba_kv_s, nsa_kv_s, win_s = [jnp.stack(a) for a in zip(*new_s)]
    return (y_prompt, y_sample, conv_p, conv_s, fox_kv_p, fox_kv_s, fox_logf_p, fox_logf_s,
            moba_kv_p, moba_kv_s, nsa_kv_p, nsa_kv_s, win_p, win_s)
```

```python
import functools

import jax
import jax.numpy as jnp
from jax import lax
from jax.experimental import pallas as pl
from jax.experimental.pallas import tpu as pltpu

F32 = jnp.float32
BF16 = jnp.bfloat16
HIGHEST = lax.Precision.HIGHEST

D_MODEL = 2048
HEAD_DIM = 64
N_BRANCH = 4
BRANCH_W = D_MODEL // N_BRANCH
N_HEADS = BRANCH_W // HEAD_DIM
CONV_W = 3
NSA_G = 2
NSA_R = N_HEADS // NSA_G
MOBA_BLOCK = 256
MOBA_TOPK = 3
CMP_L = 32
CMP_S = 16
CMP_HID = 4 * HEAD_DIM
SLC_BLOCK = 64
SLC_N = 16
WINDOW = 512
D_FF = 4 * D_MODEL
Q_BLOCK = 128
SPARSE_Q_BLOCK = 32
RMS_EPS = 1e-6
NEG = -1e30
MASKED = 2.0 * NEG
FORCE_SCORE = 1e4
KV_W = NSA_G * HEAD_DIM
SCALE = HEAD_DIM ** -0.5
PAIR_W = 2 * HEAD_DIM

OFF_CONV = 0
OFF_FOX = 3 * BRANCH_W
OFF_FOX_F = OFF_FOX + 3 * BRANCH_W
OFF_MOBA = OFF_FOX_F + N_HEADS
OFF_NSA = OFF_MOBA + 3 * BRANCH_W
OFF_NSA_GATE = OFF_NSA + BRANCH_W + 6 * KV_W
OFF_MERGE = OFF_NSA_GATE + 3 * N_HEADS
IN_W = OFF_MERGE + N_BRANCH * D_MODEL
SMALL_W = 128

TQ = 256
TK = 256
CMP_ROW = CMP_S * 2 * KV_W
PAGE_ROWS = 8

VMEM_LIMIT = 56 * 1024 * 1024


def _cparams(*sem):
    return pltpu.CompilerParams(dimension_semantics=sem, vmem_limit_bytes=VMEM_LIMIT)


def _rms(x, g):
    return x * lax.rsqrt(jnp.mean(x * x, axis=-1, keepdims=True) + RMS_EPS) * g


def _dot_nt(a, b, precision=None):
    return lax.dot_general(a, b, (((1,), (1,)), ((), ())), preferred_element_type=F32, precision=precision)


def _norm_kernel(x_ref, g_ref, o_ref):
    o_ref[...] = _rms(x_ref[...], g_ref[...]).astype(o_ref.dtype)


def rms_norm_call(x, g, out_dtype, tm):
    t, d = x.shape
    return pl.pallas_call(
        _norm_kernel,
        grid=(t // tm,),
        in_specs=[pl.BlockSpec((tm, d), lambda i: (i, 0)),
                  pl.BlockSpec((1, d), lambda i: (0, 0))],
        out_specs=pl.BlockSpec((tm, d), lambda i: (i, 0)),
        out_shape=jax.ShapeDtypeStruct((t, d), out_dtype),
        compiler_params=_cparams("parallel"),
        name="rms_norm",
    )(x, g.reshape(1, d))


def _proj_kernel(h_ref, w_ref, *out_refs, kinds, plan):
    h = h_ref[...]
    for c0, c1, dests in plan:
        z = jnp.dot(h, w_ref[:, c0:c1], preferred_element_type=F32)
        for idx, off in dests:
            o_ref, kind = out_refs[idx], kinds[idx]
            if kind == "bf16T":
                for r in range(z.shape[0] // TK):
                    o_ref[r, off:off + c1 - c0, :] = z[r * TK:(r + 1) * TK].T.astype(BF16)
            elif kind == "blockmean":
                for r in range(z.shape[0] // MOBA_BLOCK):
                    o_ref[r, :, off:off + c1 - c0] = jnp.mean(
                        z[r * MOBA_BLOCK:(r + 1) * MOBA_BLOCK], axis=0, keepdims=True)
            else:
                o_ref[:, off:off + c1 - c0] = z.astype(o_ref.dtype)


def proj_call(h, w, out_defs, plan, tm, name):
    t, d = h.shape
    n = w.shape[1]
    out_specs, out_shapes = [], []
    for kind, width in out_defs:
        if kind == "bf16T":
            out_specs.append(pl.BlockSpec((tm // TK, width, TK), lambda i: (i, 0, 0)))
            out_shapes.append(jax.ShapeDtypeStruct((t // TK, width, TK), BF16))
        elif kind == "blockmean":
            out_specs.append(pl.BlockSpec((tm // MOBA_BLOCK, 1, width), lambda i: (i, 0, 0)))
            out_shapes.append(jax.ShapeDtypeStruct((t // MOBA_BLOCK, 1, width), F32))
        else:
            out_specs.append(pl.BlockSpec((tm, width), lambda i: (i, 0)))
            out_shapes.append(jax.ShapeDtypeStruct((t, width), BF16 if kind == "bf16" else F32))
    return pl.pallas_call(
        functools.partial(_proj_kernel, kinds=tuple(k for k, _ in out_defs), plan=tuple(plan)),
        grid=(t // tm,),
        in_specs=[pl.BlockSpec((tm, d), lambda i: (i, 0)),
                  pl.BlockSpec((d, n), lambda i: (0, 0))],
        out_specs=out_specs,
        out_shape=out_shapes,
        compiler_params=_cparams("parallel"),
        name=name,
    )(h, w)


def _gate_kernel(h_ref, w_ref, o_ref):
    z = jnp.dot(h_ref[...], w_ref[...], preferred_element_type=F32)
    o_ref[...] = jax.nn.sigmoid(z)


def gate_call(h, w, tm, tn):
    t, d = h.shape
    n = w.shape[1]
    return pl.pallas_call(
        _gate_kernel,
        grid=(t // tm, n // tn),
        in_specs=[pl.BlockSpec((tm, d), lambda i, j: (i, 0)),
                  pl.BlockSpec((d, tn), lambda i, j: (0, j))],
        out_specs=pl.BlockSpec((tm, tn), lambda i, j: (i, j)),
        out_shape=jax.ShapeDtypeStruct((t, n), F32),
        compiler_params=_cparams("parallel", "arbitrary"),
        name="merge_gate_proj",
    )(h, w)


def _merge_kernel(oa_ref, ob_ref, oc_ref, od_ref, g0_ref, g1_ref, g2_ref, g3_ref, wb_ref, o_ref):
    acc = None
    for b, (o, g) in enumerate(zip((oa_ref, ob_ref, oc_ref, od_ref), (g0_ref, g1_ref, g2_ref, g3_ref))):
        br = jnp.dot(o[...].astype(BF16), wb_ref[b], preferred_element_type=F32)
        term = g[...] * br
        acc = term if acc is None else acc + term
    o_ref[...] = acc.astype(o_ref.dtype)


def merge_call(branches, gate, w_branch, tm, tn):
    t = branches[0].shape[0]
    nj = D_MODEL // tn
    gate_specs = [pl.BlockSpec((tm, tn), functools.partial(lambda i, j, b: (i, b * nj + j), b=b))
                  for b in range(N_BRANCH)]
    return pl.pallas_call(
        _merge_kernel,
        grid=(t // tm, nj),
        in_specs=[pl.BlockSpec((tm, BRANCH_W), lambda i, j: (i, 0))] * N_BRANCH + gate_specs
        + [pl.BlockSpec((N_BRANCH, BRANCH_W, tn), lambda i, j: (0, 0, j))],
        out_specs=pl.BlockSpec((tm, tn), lambda i, j: (i, j)),
        out_shape=jax.ShapeDtypeStruct((t, D_MODEL), BF16),
        compiler_params=_cparams("parallel", "arbitrary"),
        name="branch_merge",
    )(*branches, gate, gate, gate, gate, w_branch)


def _outproj_kernel(m_ref, w_ref, x_ref, g_ref, xo_ref, hn_ref):
    xn = x_ref[...] + jnp.dot(m_ref[...], w_ref[...], preferred_element_type=F32)
    xo_ref[...] = xn
    hn_ref[...] = _rms(xn, g_ref[...]).astype(hn_ref.dtype)


def outproj_call(merged, w_out, x, g_next, tm):
    t = x.shape[0]
    return pl.pallas_call(
        _outproj_kernel,
        grid=(t // tm,),
        in_specs=[pl.BlockSpec((tm, D_MODEL), lambda i: (i, 0)),
                  pl.BlockSpec((D_MODEL, D_MODEL), lambda i: (0, 0)),
                  pl.BlockSpec((tm, D_MODEL), lambda i: (i, 0)),
                  pl.BlockSpec((1, D_MODEL), lambda i: (0, 0))],
        out_specs=[pl.BlockSpec((tm, D_MODEL), lambda i: (i, 0)),
                   pl.BlockSpec((tm, D_MODEL), lambda i: (i, 0))],
        out_shape=[jax.ShapeDtypeStruct((t, D_MODEL), F32),
                   jax.ShapeDtypeStruct((t, D_MODEL), BF16)],
        compiler_params=_cparams("parallel"),
        name="out_proj",
    )(merged, w_out, x, g_next.reshape(1, D_MODEL))


def _mlp_kernel(h_ref, wu_ref, wd_ref, x_ref, g_ref, xo_ref, hn_ref, acc_ref):
    j = pl.program_id(1)

    @pl.when(j == 0)
    def _():
        acc_ref[...] = jnp.zeros_like(acc_ref)

    a = jnp.dot(h_ref[...], wu_ref[...], preferred_element_type=F32)
    a = jnp.square(jnp.maximum(a, 0.0)).astype(BF16)
    acc_ref[...] += jnp.dot(a, wd_ref[...], preferred_element_type=F32)

    @pl.when(j == pl.num_programs(1) - 1)
    def _():
        xn = x_ref[...] + acc_ref[...]
        xo_ref[...] = xn
        hn_ref[...] = _rms(xn, g_ref[...]).astype(hn_ref.dtype)


def mlp_call(h, w_up, w_down, x, g_next, next_dtype, tm, tf):
    t = x.shape[0]
    return pl.pallas_call(
        _mlp_kernel,
        grid=(t // tm, D_FF // tf),
        in_specs=[pl.BlockSpec((tm, D_MODEL), lambda i, j: (i, 0)),
                  pl.BlockSpec((D_MODEL, tf), lambda i, j: (0, j)),
                  pl.BlockSpec((tf, D_MODEL), lambda i, j: (j, 0)),
                  pl.BlockSpec((tm, D_MODEL), lambda i, j: (i, 0)),
                  pl.BlockSpec((1, D_MODEL), lambda i, j: (0, 0))],
        out_specs=[pl.BlockSpec((tm, D_MODEL), lambda i, j: (i, 0)),
                   pl.BlockSpec((tm, D_MODEL), lambda i, j: (i, 0))],
        out_shape=[jax.ShapeDtypeStruct((t, D_MODEL), F32),
                   jax.ShapeDtypeStruct((t, D_MODEL), next_dtype)],
        scratch_shapes=[pltpu.VMEM((tm, D_MODEL), F32)],
        compiler_params=_cparams("parallel", "arbitrary"),
        name="mlp",
    )(h, w_up, w_down, x, g_next.reshape(1, D_MODEL))


def _small_kernel(z_ref, b_ref, a_ref, f_ref, carry_ref):
    @pl.when(pl.program_id(1) == 0)
    def _():
        carry_ref[...] = jnp.zeros_like(carry_ref)

    z = z_ref[...]
    tm = z.shape[0]
    lane = lax.broadcasted_iota(jnp.int32, z.shape, 1)
    pre = z + b_ref[...]
    lf = jnp.minimum(pre, 0.0) - jnp.log1p(jnp.exp(-jnp.abs(pre)))
    lf = jnp.where(lane < N_HEADS, lf, 0.0)
    a_ref[...] = jnp.where(lane < N_HEADS, lf, jnp.where(lane < 4 * N_HEADS, jax.nn.sigmoid(z), 0.0))
    row = lax.broadcasted_iota(jnp.int32, (tm, tm), 0)
    col = lax.broadcasted_iota(jnp.int32, (tm, tm), 1)
    tril = jnp.where(col <= row, 1.0, 0.0)
    f = jnp.dot(tril, lf, preferred_element_type=F32, precision=HIGHEST) + carry_ref[0:1, :]
    f_ref[...] = f
    carry_ref[0:1, :] = f[tm - 1:tm, :]


def small_call(z_small, b_forget, n, t, tm):
    bias = jnp.zeros((1, SMALL_W), F32).at[0, :N_HEADS].set(b_forget)
    nt = t // tm
    return pl.pallas_call(
        _small_kernel,
        grid=(n, nt),
        in_specs=[pl.BlockSpec((tm, SMALL_W), lambda b, j: (b * nt + j, 0)),
                  pl.BlockSpec((1, SMALL_W), lambda b, j: (0, 0))],
        out_specs=[pl.BlockSpec((tm, SMALL_W), lambda b, j: (b * nt + j, 0)),
                   pl.BlockSpec((tm, SMALL_W), lambda b, j: (b * nt + j, 0))],
        out_shape=[jax.ShapeDtypeStruct((n * t, SMALL_W), F32),
                   jax.ShapeDtypeStruct((n * t, SMALL_W), F32)],
        scratch_shapes=[pltpu.VMEM((8, SMALL_W), F32)],
        compiler_params=_cparams("parallel", "arbitrary"),
        name="forget_and_gates",
    )(z_small, bias)


def _conv_prompt_kernel(z_ref, w_ref, o_ref, st_ref, prev_ref):
    @pl.when(pl.program_id(1) == 0)
    def _():
        prev_ref[...] = jnp.zeros_like(prev_ref)

    bw = BRANCH_W
    u = z_ref[:, 2 * bw:3 * bw] * z_ref[:, 0:bw]
    tm = u.shape[0]
    row = lax.broadcasted_iota(jnp.int32, u.shape, 0)
    u1 = jnp.where(row == 0, prev_ref[7:8, :], pltpu.roll(u, 1, 0))
    u2 = jnp.where(row == 0, prev_ref[6:7, :], jnp.where(row == 1, prev_ref[7:8, :], pltpu.roll(u, 2, 0)))
    y = u2 * w_ref[0:1, :] + u1 * w_ref[1:2, :] + u * w_ref[2:3, :]
    o_ref[...] = z_ref[:, bw:2 * bw] * y
    prev_ref[...] = u[tm - 8:tm]
    st_ref[0] = u[tm - 2:tm]


def conv_prompt_call(z_conv, conv_w, n, t, tm):
    nt = t // tm
    return pl.pallas_call(
        _conv_prompt_kernel,
        grid=(n, nt),
        in_specs=[pl.BlockSpec((tm, 3 * BRANCH_W), lambda b, j: (b * nt + j, 0)),
                  pl.BlockSpec((CONV_W, BRANCH_W), lambda b, j: (0, 0))],
        out_specs=[pl.BlockSpec((tm, BRANCH_W), lambda b, j: (b * nt + j, 0)),
                   pl.BlockSpec((1, CONV_W - 1, BRANCH_W), lambda b, j: (b, 0, 0))],
        out_shape=[jax.ShapeDtypeStruct((n * t, BRANCH_W), F32),
                   jax.ShapeDtypeStruct((n, CONV_W - 1, BRANCH_W), F32)],
        scratch_shapes=[pltpu.VMEM((8, BRANCH_W), F32)],
        compiler_params=_cparams("parallel", "arbitrary"),
        name="conv_prompt",
    )(z_conv, conv_w)


def _online_step(s, vt, carry):
    m, l, acc = carry
    m_new = jnp.maximum(m, jnp.max(s, axis=0, keepdims=True))
    alpha = jnp.exp(m - m_new)
    p = jnp.exp(s - m_new)
    l = alpha * l + jnp.sum(p, axis=0, keepdims=True)
    acc = alpha * acc + jnp.dot(vt, p.astype(BF16), preferred_element_type=F32)
    return m_new, l, acc


def _softmax_init():
    return (jnp.full((1, TQ), NEG, F32), jnp.zeros((1, TQ), F32), jnp.zeros((HEAD_DIM, TQ), F32))


def _query_pair(q_ref, h, half):
    hp, e = divmod(h, 2)
    qp = q_ref[:, hp * PAIR_W:(hp + 1) * PAIR_W]
    if e != half:
        qp = pltpu.roll(qp, HEAD_DIM, 1)
    lane = lax.broadcasted_iota(jnp.int32, qp.shape, 1)
    return jnp.where((lane // HEAD_DIM) == half, qp, 0.0)


def _tile_masks():
    sub = lax.broadcasted_iota(jnp.int32, (TK, TQ), 0)
    lane = lax.broadcasted_iota(jnp.int32, (TK, TQ), 1)
    return sub <= lane, sub > lane


def _key_col():
    return lax.broadcasted_iota(jnp.int32, (TK, 1), 0).astype(F32)


def _fox_kernel(q_ref, k_ref, vt_ref, fk_ref, o_ref, ot_ref):
    i = pl.program_id(1)
    causal, _ = _tile_masks()
    for h in range(N_HEADS):
        hp, e = divmod(h, 2)
        qpad = (_query_pair(q_ref, h, e) * SCALE).astype(BF16)

        def tile(j, carry, diag):
            r0 = pl.multiple_of(j * TK, TK)
            kp = k_ref[pl.ds(r0, TK), hp * PAIR_W:(hp + 1) * PAIR_W]
            s = _dot_nt(kp, qpad) - fk_ref[pl.ds(r0, TK), h:h + 1]
            if diag:
                s = jnp.where(causal, s, MASKED)
            return _online_step(s, vt_ref[j, h * HEAD_DIM:(h + 1) * HEAD_DIM, :], carry)

        carry = lax.fori_loop(0, i, lambda j, c: tile(j, c, False), _softmax_init())
        _, l, acc = tile(i, carry, True)
        ot_ref[h * HEAD_DIM:(h + 1) * HEAD_DIM, :] = acc * (1.0 / l)
    o_ref[...] = ot_ref[...].T


def fox_prompt_call(q, kb, vt, fk, n, t):
    nq = t // TQ
    return pl.pallas_call(
        _fox_kernel,
        grid=(n, nq),
        in_specs=[pl.BlockSpec((TQ, BRANCH_W), lambda b, i: (b * nq + i, 0)),
                  pl.BlockSpec((t, BRANCH_W), lambda b, i: (b, 0)),
                  pl.BlockSpec((t // TK, BRANCH_W, TK), lambda b, i: (b, 0, 0)),
                  pl.BlockSpec((t, SMALL_W), lambda b, i: (b, 0))],
        out_specs=pl.BlockSpec((TQ, BRANCH_W), lambda b, i: (b * nq + i, 0)),
        out_shape=jax.ShapeDtypeStruct((n * t, BRANCH_W), F32),
        scratch_shapes=[pltpu.VMEM((BRANCH_W, TQ), F32)],
        compiler_params=_cparams("parallel", "arbitrary"),
        name="fox_prompt",
    )(q, kb, vt, fk)


def _rank_before(score, bidx, nblk):
    cnt = jnp.zeros(score.shape, F32)
    for b2 in range(nblk):
        row = score[b2:b2 + 1, :]
        beats = (row > score) | ((row == score) & (b2 < bidx))
        cnt = cnt + jnp.where(beats, 1.0, 0.0)
    return cnt


def _moba_kernel(q_ref, k_ref, vt_ref, km_ref, o_ref, ot_ref, sel_ref, *, nblk):
    i = pl.program_id(1)
    causal, _ = _tile_masks()
    kcol = _key_col()
    bidx = lax.broadcasted_iota(jnp.int32, (nblk, TQ), 0)
    for h in range(N_HEADS):
        hp, e = divmod(h, 2)
        q32 = _query_pair(q_ref, h, e)
        qpad = (q32 * SCALE).astype(BF16)
        gs = _dot_nt(km_ref[0, :, hp * PAIR_W:(hp + 1) * PAIR_W], q32, precision=HIGHEST)
        gs = jnp.where(bidx < i, gs, NEG)
        sel = (_rank_before(gs, bidx, nblk) < MOBA_TOPK) & (bidx < i)
        sel_ref[...] = jnp.where(sel, 1.0, 0.0)
        slope = 2.0 ** (-8.0 * (h + 1) / N_HEADS)

        def tile(j, carry, diag):
            r0 = pl.multiple_of(j * TK, TK)
            kp = k_ref[pl.ds(r0, TK), hp * PAIR_W:(hp + 1) * PAIR_W]
            bias = slope * kcol - slope * ((i - j) * TK).astype(F32)
            s = _dot_nt(kp, qpad) + bias
            if diag:
                s = jnp.where(causal, s, MASKED)
            else:
                s = jnp.where(sel_ref[pl.ds(j, 1), :] > 0.5, s, MASKED)
            return _online_step(s, vt_ref[j, h * HEAD_DIM:(h + 1) * HEAD_DIM, :], carry)

        carry = lax.fori_loop(0, i, lambda j, c: tile(j, c, False), _softmax_init())
        _, l, acc = tile(i, carry, True)
        ot_ref[h * HEAD_DIM:(h + 1) * HEAD_DIM, :] = acc * (1.0 / l)
    o_ref[...] = ot_ref[...].T


def moba_prompt_call(q, kb, vt, kmean, n, t):
    nq = t // TQ
    nblk = t // MOBA_BLOCK
    return pl.pallas_call(
        functools.partial(_moba_kernel, nblk=nblk),
        grid=(n, nq),
        in_specs=[pl.BlockSpec((TQ, BRANCH_W), lambda b, i: (b * nq + i, 0)),
                  pl.BlockSpec((t, BRANCH_W), lambda b, i: (b, 0)),
                  pl.BlockSpec((t // TK, BRANCH_W, TK), lambda b, i: (b, 0, 0)),
                  pl.BlockSpec((1, nblk, BRANCH_W), lambda b, i: (b, 0, 0))],
        out_specs=pl.BlockSpec((TQ, BRANCH_W), lambda b, i: (b * nq + i, 0)),
        out_shape=jax.ShapeDtypeStruct((n * t, BRANCH_W), F32),
        scratch_shapes=[pltpu.VMEM((BRANCH_W, TQ), F32), pltpu.VMEM((nblk, TQ), F32)],
        compiler_params=_cparams("parallel", "arbitrary"),
        name="moba_prompt",
    )(q, kb, vt, kmean.reshape(n, nblk, BRANCH_W))


def _gelu_tanh(x):
    return 0.5 * x * (1.0 + jnp.tanh(0.7978845608028654 * (x + 0.044715 * x * x * x)))


def _compress_kernel(pt_ref, a_ref, pet_ref, peb_ref, wt_ref, wb_ref, w2k_ref, w2vt_ref,
                     kc_ref, vct_ref, at_ref, ab_ref, *, rows, page0_fn):
    del pt_ref, page0_fn
    j = pl.program_id(1)
    r0 = pl.multiple_of(j * PAGE_ROWS, PAGE_ROWS)
    for i in range(CMP_S):
        a = a_ref[0, :, i * 4 * KV_W:i * 4 * KV_W + 2 * KV_W]
        at_ref[pl.ds(r0, PAGE_ROWS), i * 2 * KV_W:(i + 1) * 2 * KV_W] = a + pet_ref[:, i * 2 * KV_W:(i + 1) * 2 * KV_W]
        ab_ref[pl.ds(r0, PAGE_ROWS), i * 2 * KV_W:(i + 1) * 2 * KV_W] = a + peb_ref[:, i * 2 * KV_W:(i + 1) * 2 * KV_W]

    @pl.when(j == pl.num_programs(1) - 1)
    def _():
        chunk = min(256, rows)
        prev_top = jnp.zeros((1, 4 * CMP_HID), F32)
        for c in range(rows // chunk):
            ht = jnp.dot(at_ref[c * chunk:(c + 1) * chunk, :].astype(BF16), wt_ref[...], preferred_element_type=F32)
            hb = jnp.dot(ab_ref[c * chunk:(c + 1) * chunk, :].astype(BF16), wb_ref[...], preferred_element_type=F32)
            row = lax.broadcasted_iota(jnp.int32, ht.shape, 0)
            shifted = jnp.where(row == 0, prev_top, pltpu.roll(ht, 1, 0))
            prev_top = ht[chunk - 1:chunk, :]
            g = _gelu_tanh(shifted + hb).astype(BF16)
            kc_ref[0, c * chunk:(c + 1) * chunk, :] = jnp.dot(
                g[:, :2 * CMP_HID], w2k_ref[...], preferred_element_type=F32).astype(BF16)
            vct_ref[0, :, c * chunk:(c + 1) * chunk] = _dot_nt(w2vt_ref[...], g[:, 2 * CMP_HID:]).astype(BF16)


def compress_call(pages, page_idx, pe_top, pe_bot, wt, wb, w2k, w2vt, n, npages):
    rows = npages * PAGE_ROWS
    grid_spec = pltpu.PrefetchScalarGridSpec(
        num_scalar_prefetch=1,
        grid=(n, npages),
        in_specs=[pl.BlockSpec((1, PAGE_ROWS, CMP_S * 4 * KV_W), lambda b, j, pt: (pt[b, j], 0, 0)),
                  pl.BlockSpec((1, CMP_ROW), lambda b, j, pt: (0, 0)),
                  pl.BlockSpec((1, CMP_ROW), lambda b, j, pt: (0, 0)),
                  pl.BlockSpec((CMP_ROW, 4 * CMP_HID), lambda b, j, pt: (0, 0), pipeline_mode=pl.Buffered(1)),
                  pl.BlockSpec((CMP_ROW, 4 * CMP_HID), lambda b, j, pt: (0, 0), pipeline_mode=pl.Buffered(1)),
                  pl.BlockSpec((2 * CMP_HID, KV_W), lambda b, j, pt: (0, 0)),
                  pl.BlockSpec((KV_W, 2 * CMP_HID), lambda b, j, pt: (0, 0))],
        out_specs=[pl.BlockSpec((1, rows, KV_W), lambda b, j, pt: (b, 0, 0)),
                   pl.BlockSpec((1, KV_W, rows), lambda b, j, pt: (b, 0, 0))],
        scratch_shapes=[pltpu.VMEM((rows, CMP_ROW), F32), pltpu.VMEM((rows, CMP_ROW), F32)],
    )
    return pl.pallas_call(
        functools.partial(_compress_kernel, rows=rows, page0_fn=None),
        grid_spec=grid_spec,
        out_shape=[jax.ShapeDtypeStruct((n, rows, KV_W), BF16),
                   jax.ShapeDtypeStruct((n, KV_W, rows), BF16)],
        compiler_params=_cparams("parallel", "arbitrary"),
        name="nsa_compress",
    )(page_idx, pages, pe_top, pe_bot, wt, wb, w2k, w2vt)


def _nsa_kernel(q_ref, kc_ref, vct_ref, ks_ref, vst_ref, kw_ref, vwt_ref, gt_ref, o_ref,
                ot_ref, sel_ref, *, ncmp, nsb):
    i = pl.program_id(1)
    causal, below = _tile_masks()
    kcol = _key_col()
    lane_q = lax.broadcasted_iota(jnp.int32, (1, TQ), 1)
    qpos = i * TQ + lane_q
    crow = lax.broadcasted_iota(jnp.int32, (ncmp, 1), 0)
    cpos = CMP_S * crow + (CMP_S - 1)
    cvalid = (crow >= 1) & (cpos <= qpos)
    cposf = cpos.astype(F32)
    mj = lax.broadcasted_iota(jnp.int32, (nsb, ncmp), 0)
    mc = lax.broadcasted_iota(jnp.int32, (nsb, ncmp), 1)
    ratio = SLC_BLOCK // CMP_S
    gather_m = jnp.where((mc >= ratio * mj) & (mc <= ratio * mj + ratio) & (mc >= 1), 1.0, 0.0)
    bj = lax.broadcasted_iota(jnp.int32, (nsb, TQ), 0)
    own = qpos // SLC_BLOCK
    forced = (bj == 0) | (bj >= own - 1)
    allowed = bj <= own
    ecol = lax.broadcasted_iota(jnp.int32, (TK, nsb), 1)
    erow = lax.broadcasted_iota(jnp.int32, (TK, nsb), 0) // SLC_BLOCK

    for g in range(NSA_G):
        imp = jnp.zeros((ncmp, TQ), F32)
        qpads = []
        for r in range(NSA_R):
            h = g * NSA_R + r
            slope = 2.0 ** (-8.0 * (h + 1) / N_HEADS)
            qpad = (_query_pair(q_ref, h, g) * SCALE).astype(BF16)
            qpads.append(qpad)
            s = _dot_nt(kc_ref[0], qpad) + slope * (cposf - (i * TQ).astype(F32))
            s = jnp.where(cvalid, s, MASKED)
            p = jnp.where(cvalid, jnp.exp(s - jnp.max(s, axis=0, keepdims=True)), 0.0)
            l = jnp.sum(p, axis=0, keepdims=True)
            p = p * (1.0 / jnp.where(l > 0.0, l, 1.0))
            imp = imp + p
            o_cmp = jnp.dot(vct_ref[0, g * HEAD_DIM:(g + 1) * HEAD_DIM, :], p.astype(BF16),
                            preferred_element_type=F32)
            ot_ref[h * HEAD_DIM:(h + 1) * HEAD_DIM, :] = gt_ref[3 * h:3 * h + 1, :] * o_cmp
        p_slc = jnp.dot(gather_m, imp, preferred_element_type=F32, precision=HIGHEST)
        score = jnp.where(allowed, jnp.where(forced, p_slc + FORCE_SCORE, p_slc), -1.0)
        sel = (_rank_before(score, bj, nsb) < SLC_N) & allowed
        sel_ref[...] = jnp.where(sel, 1.0, 0.0).astype(BF16)

        for r in range(NSA_R):
            h = g * NSA_R + r
            slope = 2.0 ** (-8.0 * (h + 1) / N_HEADS)
            qpad = qpads[r]

            def alibi(j):
                return slope * kcol - slope * ((i - j) * TK).astype(F32)

            def slc_tile(j, carry, diag):
                r0 = pl.multiple_of(j * TK, TK)
                expand = jnp.where(ecol == erow + j * (TK // SLC_BLOCK), 1.0, 0.0).astype(BF16)
                keep = jnp.dot(expand, sel_ref[...], preferred_element_type=F32) > 0.5
                if diag:
                    keep = keep & causal
                s = jnp.where(keep, _dot_nt(ks_ref[pl.ds(r0, TK), :], qpad) + alibi(j), MASKED)
                return _online_step(s, vst_ref[j, g * HEAD_DIM:(g + 1) * HEAD_DIM, :], carry)

            carry = lax.fori_loop(0, i, lambda j, c: slc_tile(j, c, False), _softmax_init())
            _, l, acc = slc_tile(i, carry, True)
            o_slc = acc * (1.0 / l)

            def win_tile(j, carry, mask):
                r0 = pl.multiple_of(j * TK, TK)
                s = _dot_nt(kw_ref[pl.ds(r0, TK), :], qpad) + alibi(j)
                if mask is not None:
                    s = jnp.where(mask, s, MASKED)
                return _online_step(s, vwt_ref[j, g * HEAD_DIM:(g + 1) * HEAD_DIM, :], carry)

            carry = _softmax_init()
            carry = lax.cond(i >= 2, lambda c: win_tile(i - 2, c, below), lambda c: c, carry)
            carry = lax.cond(i >= 1, lambda c: win_tile(i - 1, c, None), lambda c: c, carry)
            _, l, acc = win_tile(i, carry, causal)
            o_win = acc * (1.0 / l)
            rows = slice(h * HEAD_DIM, (h + 1) * HEAD_DIM)
            ot_ref[rows, :] = (ot_ref[rows, :] + gt_ref[3 * h + 1:3 * h + 2, :] * o_slc
                               + gt_ref[3 * h + 2:3 * h + 3, :] * o_win)
    o_ref[...] = ot_ref[...].T


def nsa_prompt_call(q, kcmp, vcmpt, ks, vst, kw, vwt, gate_t, n, t):
    nq = t // TQ
    ncmp = kcmp.shape[1]
    nsb = t // SLC_BLOCK
    return pl.pallas_call(
        functools.partial(_nsa_kernel, ncmp=ncmp, nsb=nsb),
        grid=(n, nq),
        in_specs=[pl.BlockSpec((TQ, BRANCH_W), lambda b, i: (b * nq + i, 0)),
                  pl.BlockSpec((1, ncmp, KV_W), lambda b, i: (b, 0, 0)),
                  pl.BlockSpec((1, KV_W, ncmp), lambda b, i: (b, 0, 0)),
                  pl.BlockSpec((t, KV_W), lambda b, i: (b, 0)),
                  pl.BlockSpec((t // TK, KV_W, TK), lambda b, i: (b, 0, 0)),
                  pl.BlockSpec((t, KV_W), lambda b, i: (b, 0)),
                  pl.BlockSpec((t // TK, KV_W, TK), lambda b, i: (b, 0, 0)),
                  pl.BlockSpec((3 * N_HEADS, TQ), lambda b, i: (0, b * nq + i))],
        out_specs=pl.BlockSpec((TQ, BRANCH_W), lambda b, i: (b * nq + i, 0)),
        out_shape=jax.ShapeDtypeStruct((n * t, BRANCH_W), F32),
        scratch_shapes=[pltpu.VMEM((BRANCH_W, TQ), F32), pltpu.VMEM((nsb, TQ), BF16)],
        compiler_params=_cparams("parallel", "arbitrary"),
        name="nsa_prompt",
    )(q, kcmp, vcmpt, ks, vst, kw, vwt, gate_t)


def alibi_slopes(n):
    return jnp.exp2(-8.0 * jnp.arange(1, n + 1, dtype=jnp.float32) / n)


def masked_softmax(s, mask, axis=-1):
    p = jax.nn.softmax(jnp.where(mask, s, NEG), axis=axis)
    return p * mask


def sweep_queries(fn, block, *qs):
    n, t = qs[0].shape[:2]
    qb = block if t % block == 0 else t
    nb = t // qb
    xs = tuple(jnp.moveaxis(a.reshape((n, nb, qb) + a.shape[2:]), 1, 0) for a in qs)
    out = lax.map(lambda args: fn(args[0] * qb, *args[1:]), (jnp.arange(nb, dtype=jnp.int32),) + xs)
    return jnp.moveaxis(out, 0, 1).reshape((n, t) + out.shape[3:])


def gather_pages(pool, page_table):
    g = pool[page_table]
    return g.reshape((g.shape[0], g.shape[1] * g.shape[2]) + g.shape[3:])


def fox_attention(q, k, v, logf, q0):
    d = q.shape[-1]
    L = k.shape[1]
    scale = d ** -0.5
    F = jnp.cumsum(logf.astype(jnp.float32), axis=1)
    Fk = jnp.moveaxis(F, 1, 2)
    Fq = F[:, q0:]
    kpos = jnp.arange(L)

    def block(start, qb, fq):
        qpos = q0 + start + jnp.arange(qb.shape[1])
        s = jnp.einsum('nqhd,nkhd->nhqk', qb, k).astype(jnp.float32) * scale
        s = s + jnp.moveaxis(fq, 1, 2)[..., None] - Fk[:, :, None, :]
        p = masked_softmax(s, kpos[None, :] <= qpos[:, None])
        return jnp.einsum('nhqk,nkhd->nqhd', p.astype(q.dtype), v)

    return sweep_queries(block, Q_BLOCK, q, Fq)


def moba_attention(q, k, v, q0, slopes):
    n, tq, nh, d = q.shape
    L = k.shape[1]
    scale = d ** -0.5
    nbk = -(-L // MOBA_BLOCK)
    pad = nbk * MOBA_BLOCK - L

    def to_blocks(a):
        a = jnp.pad(a, ((0, 0), (0, pad), (0, 0), (0, 0)))
        return a.reshape(n, nbk, MOBA_BLOCK, nh, d).transpose(0, 3, 1, 2, 4)

    kb, vb = to_blocks(k), to_blocks(v)
    kmean = jnp.mean(kb.astype(jnp.float32), axis=3)
    topk = min(MOBA_TOPK, nbk)
    bidx = jnp.arange(nbk)
    ni = jnp.arange(n)[:, None, None, None]
    hi = jnp.arange(nh)[None, :, None, None]
    sl = slopes[None, :, None, None, None]

    def block(start, qb):
        m = qb.shape[1]
        qpos = q0 + start + jnp.arange(m)
        own = qpos // MOBA_BLOCK
        gs = jnp.einsum('nqhd,nhbd->nhqb', qb.astype(jnp.float32), kmean)
        gs = jnp.where(bidx[None, :] < own[:, None], gs, NEG)
        _, top = lax.top_k(gs, topk)
        sel_ok = top < own[:, None]
        idx = jnp.concatenate([top, jnp.broadcast_to(own[:, None], (n, nh, m, 1))], axis=-1)
        ok = jnp.concatenate([sel_ok, jnp.ones((n, nh, m, 1), bool)], axis=-1)
        kg = kb[ni, hi, idx]
        vg = vb[ni, hi, idx]
        kpos = idx[..., None] * MOBA_BLOCK + jnp.arange(MOBA_BLOCK)
        s = jnp.einsum('nqhd,nhqsjd->nhqsj', qb, kg).astype(jnp.float32) * scale
        s = s - sl * (qpos[:, None, None] - kpos).astype(jnp.float32)
        mask = ok[..., None] & (kpos <= qpos[:, None, None])
        p = masked_softmax(s, mask, axis=(-2, -1))
        return jnp.einsum('nhqsj,nhqsjd->nqhd', p.astype(q.dtype), vg)

    return sweep_queries(block, SPARSE_Q_BLOCK, q)


def compress_tokens(a, w1, w2, pe):
    n, L, g, d = a.shape
    nc = (L - CMP_L) // CMP_S + 1
    idx = jnp.arange(nc)[:, None] * CMP_S + jnp.arange(CMP_L)[None, :]
    blocks = a[:, idx] + pe[:, None, :].astype(a.dtype)
    flat = jnp.swapaxes(blocks, 2, 3).reshape(n, nc, g, CMP_L * d)
    return jax.nn.gelu(flat @ w1) @ w2


def nsa_attention(q, kc, vc, ks, vs, kw, vw, gate, q0, w_ofs, slopes, cmp_w1, cmp_w2, cmp_pe):
    n, tq, nh, d = q.shape
    L = kc.shape[1]
    dt = q.dtype
    scale = d ** -0.5
    qg = q.reshape(n, tq, NSA_G, NSA_R, d)
    k_cmp = compress_tokens(kc, cmp_w1[0], cmp_w2[0], cmp_pe[0])
    v_cmp = compress_tokens(vc, cmp_w1[1], cmp_w2[1], cmp_pe[1])
    nc = k_cmp.shape[1]
    cpos = jnp.arange(nc) * CMP_S + (CMP_L - 1)
    nsb = -(-L // SLC_BLOCK)
    padl = nsb * SLC_BLOCK - L

    def to_blocks(a):
        a = jnp.pad(a, ((0, 0), (0, padl), (0, 0), (0, 0)))
        return a.reshape(n, nsb, SLC_BLOCK, NSA_G, d).transpose(0, 3, 1, 2, 4)

    ks_b, vs_b = to_blocks(ks), to_blocks(vs)
    nsel = min(SLC_N, nsb)
    front = CMP_L // CMP_S - 1
    ratio = SLC_BLOCK // CMP_S
    width = ratio + front
    back = ratio * nsb + width - front - nc
    kw_p = jnp.pad(kw, ((0, 0), (WINDOW, 0), (0, 0), (0, 0)))
    vw_p = jnp.pad(vw, ((0, 0), (WINDOW, 0), (0, 0), (0, 0)))
    sl = slopes.reshape(NSA_G, NSA_R)[None, :, :, None, None]
    ni = jnp.arange(n)[:, None, None, None]
    gi = jnp.arange(NSA_G)[None, :, None, None]
    bj = jnp.arange(nsb)

    def block(start, qb, gb):
        m = qb.shape[1]
        qpos = q0 + start + jnp.arange(m)
        s = jnp.einsum('nqgrd,ncgd->ngrqc', qb, k_cmp).astype(jnp.float32) * scale
        s = s - sl * (qpos[:, None] - cpos[None, :]).astype(jnp.float32)
        p_cmp = masked_softmax(s, cpos[None, :] <= qpos[:, None])
        o_cmp = jnp.einsum('ngrqc,ncgd->nqgrd', p_cmp.astype(dt), v_cmp)
        imp = jnp.pad(p_cmp.sum(axis=2), ((0, 0), (0, 0), (0, 0), (front, back)))
        p_slc = imp[..., 0:ratio * nsb:ratio]
        for u in range(1, width):
            p_slc = p_slc + imp[..., u:u + ratio * nsb:ratio]
        own = qpos // SLC_BLOCK
        forced = (bj[None, :] == 0) | (bj[None, :] >= own[:, None] - 1)
        allowed = bj[None, :] <= own[:, None]
        score = jnp.where(allowed, jnp.where(forced, p_slc + FORCE_SCORE, p_slc), -1.0)
        _, top = lax.top_k(score, nsel)
        ok = top <= own[:, None]
        kg = ks_b[ni, gi, top]
        vg = vs_b[ni, gi, top]
        kpos = top[..., None] * SLC_BLOCK + jnp.arange(SLC_BLOCK)
        dist = (qpos[:, None, None] - kpos)[:, :, None].astype(jnp.float32)
        s2 = jnp.einsum('nqgrd,ngqsjd->ngrqsj', qb, kg).astype(jnp.float32) * scale - sl[..., None] * dist
        mask2 = (ok[..., None] & (kpos <= qpos[:, None, None]))[:, :, None]
        p2 = masked_softmax(s2, mask2, axis=(-2, -1))
        o_slc = jnp.einsum('ngrqsj,ngqsjd->nqgrd', p2.astype(dt), vg)
        off = q0 + start - w_ofs
        kwin = lax.dynamic_slice_in_dim(kw_p, off, WINDOW + m, axis=1)
        vwin = lax.dynamic_slice_in_dim(vw_p, off, WINDOW + m, axis=1)
        wpos = q0 + start - WINDOW + jnp.arange(WINDOW + m)
        s3 = jnp.einsum('nqgrd,nkgd->ngrqk', qb, kwin).astype(jnp.float32) * scale
        s3 = s3 - sl * (qpos[:, None] - wpos[None, :]).astype(jnp.float32)
        wmask = (wpos[None, :] <= qpos[:, None]) & (wpos[None, :] > qpos[:, None] - WINDOW) & (wpos[None, :] >= 0)
        p3 = masked_softmax(s3, wmask)
        o_win = jnp.einsum('ngrqk,nkgd->nqgrd', p3.astype(dt), vwin)
        return gb[..., 0:1] * o_cmp + gb[..., 1:2] * o_slc + gb[..., 2:3] * o_win

    return sweep_queries(block, SPARSE_Q_BLOCK, qg, gate)


def _compress_weights(cmp_w1, cmp_w2, cmp_pe):
    w1 = cmp_w1.reshape(2, CMP_L, HEAD_DIM, CMP_HID)
    eye_w = jnp.eye(2, dtype=F32)
    eye_g = jnp.eye(NSA_G, dtype=F32)
    big = jnp.einsum('widh,sw,gk->isgdwkh', w1, eye_w, eye_g)
    big = big.reshape(CMP_L, 2 * KV_W, 4 * CMP_HID)
    wt = big[:CMP_S].reshape(CMP_ROW, 4 * CMP_HID).astype(BF16)
    wb = big[CMP_S:].reshape(CMP_ROW, 4 * CMP_HID).astype(BF16)
    w2k = jnp.einsum('hd,gk->ghkd', cmp_w2[0], eye_g).reshape(2 * CMP_HID, KV_W).astype(BF16)
    w2vt = jnp.einsum('hd,gk->kdgh', cmp_w2[1], eye_g).reshape(KV_W, 2 * CMP_HID).astype(BF16)
    pe = jnp.broadcast_to(cmp_pe[:, :, None, :], (2, CMP_L, NSA_G, HEAD_DIM))
    pe = jnp.transpose(pe, (1, 0, 2, 3)).reshape(CMP_L, 2 * KV_W)
    pe_top = pe[:CMP_S].reshape(1, CMP_ROW)
    pe_bot = pe[CMP_S:].reshape(1, CMP_ROW)
    return pe_top, pe_bot, wt, wb, w2k, w2vt


def _layer_weights(l, w_in, w_branch, w_out, w_up, w_down):
    wl = w_in[l]
    small = jnp.concatenate(
        [wl[:, OFF_FOX_F:OFF_FOX_F + N_HEADS], wl[:, OFF_NSA_GATE:OFF_NSA_GATE + 3 * N_HEADS],
         jnp.zeros((D_MODEL, SMALL_W - 4 * N_HEADS), wl.dtype)], axis=1)
    return dict(
        conv=wl[:, OFF_CONV:OFF_FOX].astype(BF16),
        fox=wl[:, OFF_FOX:OFF_FOX_F].astype(BF16),
        moba=wl[:, OFF_MOBA:OFF_NSA].astype(BF16),
        nsa=wl[:, OFF_NSA:OFF_NSA_GATE].astype(BF16),
        small=small.astype(BF16),
        gate=wl[:, OFF_MERGE:].astype(BF16),
        branch=w_branch[l].astype(BF16),
        out=w_out[l].astype(BF16),
        up=w_up[l].astype(BF16),
        down=w_down[l].astype(BF16),
    )


def _finish_layer(x, branches, gate, w, g_mlp, g_next, next_dtype, tm):
    merged = merge_call(branches, gate, w["branch"], tm, 512)
    x1, hm = outproj_call(merged, w["out"], x, g_mlp, tm)
    return mlp_call(hm, w["up"], w["down"], x1, g_next, next_dtype, tm, 512)


def prompt_mixers(h, n, t, w, b_forget, conv_w, cmp_w, tm):
    bw = BRANCH_W
    (z_conv,) = proj_call(h, w["conv"], [("f32", 3 * bw)], [(a, a + 512, ((0, a),)) for a in (0, 512, 1024)],
                          tm, "proj_conv")
    qkv_defs = [("f32", bw), ("f32", 2 * bw), ("bf16", bw), ("bf16T", bw)]
    qkv_plan = [(0, bw, ((0, 0),)), (bw, 2 * bw, ((1, 0), (2, 0))), (2 * bw, 3 * bw, ((1, bw), (3, 0)))]
    fox_q, fox_kv, fox_kb, fox_vt = proj_call(h, w["fox"], qkv_defs, qkv_plan, tm, "proj_fox")
    moba_q, moba_kv, moba_kb, moba_vt, moba_km = proj_call(
        h, w["moba"], qkv_defs + [("blockmean", bw)],
        [qkv_plan[0], (bw, 2 * bw, ((1, 0), (2, 0), (4, 0))), qkv_plan[2]], tm, "proj_moba")
    kv = KV_W
    nsa_q, nsa_kv, nsa_win, nsa_ks, nsa_vst, nsa_kw, nsa_vwt = proj_call(
        h, w["nsa"],
        [("f32", bw), ("f32", 4 * kv), ("f32", 2 * kv), ("bf16", kv), ("bf16T", kv), ("bf16", kv), ("bf16T", kv)],
        [(0, bw, ((0, 0),)), (bw, bw + 2 * kv, ((1, 0),)),
         (bw + 2 * kv, bw + 3 * kv, ((1, 2 * kv), (3, 0))), (bw + 3 * kv, bw + 4 * kv, ((1, 3 * kv), (4, 0))),
         (bw + 4 * kv, bw + 5 * kv, ((2, 0), (5, 0))), (bw + 5 * kv, bw + 6 * kv, ((2, kv), (6, 0)))],
        tm, "proj_nsa")
    (z_small,) = proj_call(h, w["small"], [("f32", SMALL_W)], [(0, SMALL_W, ((0, 0),))], tm, "proj_small")

    out_a, new_conv = conv_prompt_call(z_conv, conv_w, n, t, tm)
    lg, fk = small_call(z_small, b_forget, n, t, tm)
    logf = lg[:, :N_HEADS].reshape(n, t, N_HEADS)
    gate_t = lg[:, N_HEADS:4 * N_HEADS].T

    out_b = fox_prompt_call(fox_q, fox_kb, fox_vt, fk, n, t)
    out_c = moba_prompt_call(moba_q, moba_kb, moba_vt, moba_km, n, t)

    npages = t // 128
    pages = nsa_kv.reshape(n * npages, PAGE_ROWS, CMP_S * 4 * kv)
    page_idx = jnp.arange(n * npages, dtype=jnp.int32).reshape(n, npages)
    kcmp, vcmpt = compress_call(pages, page_idx, *cmp_w, n, npages)
    out_d = nsa_prompt_call(nsa_q, kcmp, vcmpt, nsa_ks, nsa_vst, nsa_kw, nsa_vwt, gate_t, n, t)

    wb = min(WINDOW, t)
    new_state = (new_conv,
                 fox_kv.reshape(n, t, 2, N_HEADS, HEAD_DIM),
                 logf,
                 moba_kv.reshape(n, t, 2, N_HEADS, HEAD_DIM),
                 nsa_kv.reshape(n, t, 2, 2, NSA_G, HEAD_DIM),
                 nsa_win.reshape(n, t, 2, NSA_G, HEAD_DIM)[:, t - wb:])
    return [out_a, out_b, out_c, out_d], new_state


def sample_mixers(h, n, t, q0, past, w, b_forget, conv_w, cmp_w1, cmp_w2, cmp_pe, tm):
    dt = F32
    bw = BRANCH_W
    kv = KV_W
    (z_conv,) = proj_call(h, w["conv"], [("f32", 3 * bw)], [(0, 3 * bw, ((0, 0),))], tm, "proj_conv_s")
    fox_q, fox_kv = proj_call(h, w["fox"], [("f32", bw), ("f32", 2 * bw)],
                              [(0, bw, ((0, 0),)), (bw, 3 * bw, ((1, 0),))], tm, "proj_fox_s")
    moba_q, moba_kv = proj_call(h, w["moba"], [("f32", bw), ("f32", 2 * bw)],
                                [(0, bw, ((0, 0),)), (bw, 3 * bw, ((1, 0),))], tm, "proj_moba_s")
    nsa_q, nsa_kv, nsa_win = proj_call(
        h, w["nsa"], [("f32", bw), ("f32", 4 * kv), ("f32", 2 * kv)],
        [(0, bw, ((0, 0),)), (bw, bw + 4 * kv, ((1, 0),)), (bw + 4 * kv, bw + 6 * kv, ((2, 0),))], tm, "proj_nsa_s")
    (z_small,) = proj_call(h, w["small"], [("f32", SMALL_W)], [(0, SMALL_W, ((0, 0),))], tm, "proj_small_s")

    def heads(a, nh):
        return a.reshape(n, t, nh, HEAD_DIM)

    def cat(old, new):
        return jnp.concatenate([old.astype(new.dtype), new], axis=1)

    zc = z_conv.reshape(n, t, 3 * bw)
    conv_x, conv_b, conv_c = zc[..., :bw], zc[..., bw:2 * bw], zc[..., 2 * bw:]
    u = conv_c * conv_x
    ext = jnp.concatenate([past['conv'].astype(dt), u], axis=1)
    y_conv = ext[:, 0:t] * conv_w[0]
    for j in range(1, CONV_W):
        y_conv = y_conv + ext[:, j:j + t] * conv_w[j]
    out_a = conv_b * y_conv
    new_conv = ext[:, -(CONV_W - 1):]

    fkv = fox_kv.reshape(n, t, 2, N_HEADS, HEAD_DIM)
    fq, fk, fv = heads(fox_q, N_HEADS), fkv[:, :, 0], fkv[:, :, 1]
    fox_f = z_small.reshape(n, t, SMALL_W)[..., :N_HEADS]
    logf = jax.nn.log_sigmoid(fox_f + b_forget)
    lf_all = jnp.concatenate([past['fox_logf'].astype(F32), logf], axis=1)
    out_b = fox_attention(fq, cat(past['fox_k'], fk), cat(past['fox_v'], fv), lf_all, q0)

    mkv = moba_kv.reshape(n, t, 2, N_HEADS, HEAD_DIM)
    mq, mk, mv = heads(moba_q, N_HEADS), mkv[:, :, 0], mkv[:, :, 1]
    out_c = moba_attention(mq, cat(past['moba_k'], mk), cat(past['moba_v'], mv), q0, alibi_slopes(N_HEADS))

    nq = heads(nsa_q, N_HEADS)
    nkv = nsa_kv.reshape(n, t, 2, 2, NSA_G, HEAD_DIM)
    kc, vc, ks, vs = nkv[:, :, 0, 0], nkv[:, :, 0, 1], nkv[:, :, 1, 0], nkv[:, :, 1, 1]
    nwin = nsa_win.reshape(n, t, 2, NSA_G, HEAD_DIM)
    kw, vw = nwin[:, :, 0], nwin[:, :, 1]
    ngate = jax.nn.sigmoid(z_small.reshape(n, t, SMALL_W)[..., N_HEADS:4 * N_HEADS]).reshape(n, t, NSA_G, NSA_R, 3)
    kw_all = cat(past['win_k'], kw)
    vw_all = cat(past['win_v'], vw)
    w_ofs = q0 - past['win_k'].shape[1]
    out_d = nsa_attention(nq, cat(past['nsa_kc'], kc), cat(past['nsa_vc'], vc),
                          cat(past['nsa_ks'], ks), cat(past['nsa_vs'], vs),
                          kw_all, vw_all, ngate, q0, w_ofs, alibi_slopes(N_HEADS), cmp_w1, cmp_w2, cmp_pe)
    branches = [a.reshape(n * t, bw) for a in (out_a, out_b, out_c, out_d)]
    wb = past['win_k'].shape[1]
    new_state = (new_conv, fkv, logf.astype(dt), mkv, nkv,
                 jnp.stack([kw_all, vw_all], axis=2)[:, -wb:])
    return branches, new_state


def kernel(x_prompt, x_sample, state_conv, cache_fox_kv, cache_fox_logf, cache_moba_kv, cache_nsa_kv,
           state_nsa_win, page_table, g_mix, w_in, b_forget, conv_w, cmp_w1, cmp_w2, cmp_pe,
           w_branch, w_out, g_mlp, w_up, w_down, g_final):
    depth = w_in.shape[0]
    nb, seq, _ = x_prompt.shape
    db, dseq, _ = x_sample.shape
    q0_sample = page_table.shape[1] * cache_fox_kv.shape[2]
    tm_p, tm_s = 512, db * dseq

    xp = x_prompt.reshape(nb * seq, D_MODEL)
    xs = x_sample.reshape(db * dseq, D_MODEL)
    hp = rms_norm_call(xp, g_mix[0], BF16, tm_p)
    hs = rms_norm_call(xs, g_mix[0], BF16, tm_s)
    new_p, new_s = [], []
    for l in range(depth):
        w = _layer_weights(l, w_in, w_branch, w_out, w_up, w_down)
        cmp_w = _compress_weights(cmp_w1[l], cmp_w2[l], cmp_pe[l])
        last = l == depth - 1
        g_next = g_final if last else g_mix[l + 1]
        next_dtype = F32 if last else BF16

        branches, st_p = prompt_mixers(hp, nb, seq, w, b_forget[l], conv_w[l], cmp_w, tm_p)
        gate_p = gate_call(hp, w["gate"], tm_p, 1024)
        xp, hp = _finish_layer(xp, branches, gate_p, w, g_mlp[l], g_next, next_dtype, tm_p)
        new_p.append(st_p)

        fox_kv = gather_pages(cache_fox_kv[l], page_table)
        moba_kv = gather_pages(cache_moba_kv[l], page_table)
        nsa_kv = gather_pages(cache_nsa_kv[l], page_table)
        win = state_nsa_win[l]
        past = {'conv': state_conv[l],
                'fox_k': fox_kv[:, :, 0], 'fox_v': fox_kv[:, :, 1],
                'fox_logf': gather_pages(cache_fox_logf[l], page_table),
                'moba_k': moba_kv[:, :, 0], 'moba_v': moba_kv[:, :, 1],
                'nsa_kc': nsa_kv[:, :, 0, 0], 'nsa_vc': nsa_kv[:, :, 0, 1],
                'nsa_ks': nsa_kv[:, :, 1, 0], 'nsa_vs': nsa_kv[:, :, 1, 1],
                'win_k': win[:, :, 0], 'win_v': win[:, :, 1]}
        branches, st_s = sample_mixers(hs, db, dseq, q0_sample, past, w, b_forget[l], conv_w[l],
                                       cmp_w1[l], cmp_w2[l], cmp_pe[l], tm_s)
        gate_s = gate_call(hs, w["gate"], tm_s, 1024)
        xs, hs = _finish_layer(xs, branches, gate_s, w, g_mlp[l], g_next, next_dtype, tm_s)
        new_s.append(st_s)
    y_prompt = hp.reshape(nb, seq, D_MODEL)
    y_sample = hs.reshape(db, dseq, D_MODEL)
    conv_p, fox_kv_p, fox_logf_p, moba_kv_p, nsa_kv_p, win_p = [jnp.stack(a) for a in zip(*new_p)]
    conv_s, fox_kv_s, fox_logf_s, moba_kv_s, nsa_kv_s, win_s = [jnp.stack(a) for a in zip(*new_s)]
    return (y_prompt, y_sample, conv_p, conv_s, fox_kv_p, fox_kv_s, fox_logf_p, fox_logf_s,
            moba_kv_p, moba_kv_s, nsa_kv_p, nsa_kv_s, win_p, win_s)
```

```python
import functools

import jax
import jax.numpy as jnp
from jax import lax
from jax.experimental import pallas as pl
from jax.experimental.pallas import tpu as pltpu

F32 = jnp.float32
BF16 = jnp.bfloat16
HIGHEST = lax.Precision.HIGHEST

D_MODEL = 2048
HEAD_DIM = 64
N_BRANCH = 4
BRANCH_W = D_MODEL // N_BRANCH
N_HEADS = BRANCH_W // HEAD_DIM
CONV_W = 3
NSA_G = 2
NSA_R = N_HEADS // NSA_G
MOBA_BLOCK = 256
MOBA_TOPK = 3
CMP_L = 32
CMP_S = 16
CMP_HID = 4 * HEAD_DIM
SLC_BLOCK = 64
SLC_N = 16
WINDOW = 512
D_FF = 4 * D_MODEL
Q_BLOCK = 128
SPARSE_Q_BLOCK = 32
RMS_EPS = 1e-6
NEG = -1e30
MASKED = 2.0 * NEG
FORCE_SCORE = 1e4
KV_W = NSA_G * HEAD_DIM
SCALE = HEAD_DIM ** -0.5
PAIR_W = 2 * HEAD_DIM

OFF_CONV = 0
OFF_FOX = 3 * BRANCH_W
OFF_FOX_F = OFF_FOX + 3 * BRANCH_W
OFF_MOBA = OFF_FOX_F + N_HEADS
OFF_NSA = OFF_MOBA + 3 * BRANCH_W
OFF_NSA_GATE = OFF_NSA + BRANCH_W + 6 * KV_W
OFF_MERGE = OFF_NSA_GATE + 3 * N_HEADS
IN_W = OFF_MERGE + N_BRANCH * D_MODEL
SMALL_W = 128

TQ = 256
TK = 256
CMP_ROW = CMP_S * 2 * KV_W
PAGE_ROWS = 8

VMEM_LIMIT = 56 * 1024 * 1024


def _cparams(*sem):
    return pltpu.CompilerParams(dimension_semantics=sem, vmem_limit_bytes=VMEM_LIMIT)


def _rms(x, g):
    return x * lax.rsqrt(jnp.mean(x * x, axis=-1, keepdims=True) + RMS_EPS) * g


def _dot_nt(a, b, precision=None):
    return lax.dot_general(a, b, (((1,), (1,)), ((), ())), preferred_element_type=F32, precision=precision)


def _norm_kernel(x_ref, g_ref, o_ref):
    o_ref[...] = _rms(x_ref[...], g_ref[...]).astype(o_ref.dtype)


def rms_norm_call(x, g, out_dtype, tm):
    t, d = x.shape
    return pl.pallas_call(
        _norm_kernel,
        grid=(t // tm,),
        in_specs=[pl.BlockSpec((tm, d), lambda i: (i, 0)),
                  pl.BlockSpec((1, d), lambda i: (0, 0))],
        out_specs=pl.BlockSpec((tm, d), lambda i: (i, 0)),
        out_shape=jax.ShapeDtypeStruct((t, d), out_dtype),
        compiler_params=_cparams("parallel"),
        name="rms_norm",
    )(x, g.reshape(1, d))


def _proj_kernel(h_ref, w_ref, *out_refs, kinds, plan):
    h = h_ref[...]
    for c0, c1, dests in plan:
        z = jnp.dot(h, w_ref[:, c0:c1], preferred_element_type=F32)
        for idx, off in dests:
            o_ref, kind = out_refs[idx], kinds[idx]
            if kind == "bf16T":
                for r in range(z.shape[0] // TK):
                    o_ref[r, off:off + c1 - c0, :] = z[r * TK:(r + 1) * TK].T.astype(BF16)
            elif kind == "blockmean":
                for r in range(z.shape[0] // MOBA_BLOCK):
                    o_ref[r, :, off:off + c1 - c0] = jnp.mean(
                        z[r * MOBA_BLOCK:(r + 1) * MOBA_BLOCK], axis=0, keepdims=True)
            else:
                o_ref[:, off:off + c1 - c0] = z.astype(o_ref.dtype)


def proj_call(h, w, out_defs, plan, tm, name):
    t, d = h.shape
    n = w.shape[1]
    out_specs, out_shapes = [], []
    for kind, width in out_defs:
        if kind == "bf16T":
            out_specs.append(pl.BlockSpec((tm // TK, width, TK), lambda i: (i, 0, 0)))
            out_shapes.append(jax.ShapeDtypeStruct((t // TK, width, TK), BF16))
        elif kind == "blockmean":
            out_specs.append(pl.BlockSpec((tm // MOBA_BLOCK, 1, width), lambda i: (i, 0, 0)))
            out_shapes.append(jax.ShapeDtypeStruct((t // MOBA_BLOCK, 1, width), F32))
        else:
            out_specs.append(pl.BlockSpec((tm, width), lambda i: (i, 0)))
            out_shapes.append(jax.ShapeDtypeStruct((t, width), BF16 if kind == "bf16" else F32))
    return pl.pallas_call(
        functools.partial(_proj_kernel, kinds=tuple(k for k, _ in out_defs), plan=tuple(plan)),
        grid=(t // tm,),
        in_specs=[pl.BlockSpec((tm, d), lambda i: (i, 0)),
                  pl.BlockSpec((d, n), lambda i: (0, 0))],
        out_specs=out_specs,
        out_shape=out_shapes,
        compiler_params=_cparams("parallel"),
        name=name,
    )(h, w)


def _gate_kernel(h_ref, w_ref, o_ref):
    z = jnp.dot(h_ref[...], w_ref[...], preferred_element_type=F32)
    o_ref[...] = jax.nn.sigmoid(z)


def gate_call(h, w, tm, tn):
    t, d = h.shape
    n = w.shape[1]
    return pl.pallas_call(
        _gate_kernel,
        grid=(t // tm, n // tn),
        in_specs=[pl.BlockSpec((tm, d), lambda i, j: (i, 0)),
                  pl.BlockSpec((d, tn), lambda i, j: (0, j))],
        out_specs=pl.BlockSpec((tm, tn), lambda i, j: (i, j)),
        out_shape=jax.ShapeDtypeStruct((t, n), F32),
        compiler_params=_cparams("parallel", "arbitrary"),
        name="merge_gate_proj",
    )(h, w)


def _merge_kernel(oa_ref, ob_ref, oc_ref, od_ref, g0_ref, g1_ref, g2_ref, g3_ref, wb_ref, o_ref):
    acc = None
    for b, (o, g) in enumerate(zip((oa_ref, ob_ref, oc_ref, od_ref), (g0_ref, g1_ref, g2_ref, g3_ref))):
        br = jnp.dot(o[...].astype(BF16), wb_ref[b], preferred_element_type=F32)
        term = g[...] * br
        acc = term if acc is None else acc + term
    o_ref[...] = acc.astype(o_ref.dtype)


def merge_call(branches, gate, w_branch, tm, tn):
    t = branches[0].shape[0]
    nj = D_MODEL // tn
    gate_specs = [pl.BlockSpec((tm, tn), functools.partial(lambda i, j, b: (i, b * nj + j), b=b))
                  for b in range(N_BRANCH)]
    return pl.pallas_call(
        _merge_kernel,
        grid=(t // tm, nj),
        in_specs=[pl.BlockSpec((tm, BRANCH_W), lambda i, j: (i, 0))] * N_BRANCH + gate_specs
        + [pl.BlockSpec((N_BRANCH, BRANCH_W, tn), lambda i, j: (0, 0, j))],
        out_specs=pl.BlockSpec((tm, tn), lambda i, j: (i, j)),
        out_shape=jax.ShapeDtypeStruct((t, D_MODEL), BF16),
        compiler_params=_cparams("parallel", "arbitrary"),
        name="branch_merge",
    )(*branches, gate, gate, gate, gate, w_branch)


def _outproj_kernel(m_ref, w_ref, x_ref, g_ref, xo_ref, hn_ref):
    xn = x_ref[...] + jnp.dot(m_ref[...], w_ref[...], preferred_element_type=F32)
    xo_ref[...] = xn
    hn_ref[...] = _rms(xn, g_ref[...]).astype(hn_ref.dtype)


def outproj_call(merged, w_out, x, g_next, tm):
    t = x.shape[0]
    return pl.pallas_call(
        _outproj_kernel,
        grid=(t // tm,),
        in_specs=[pl.BlockSpec((tm, D_MODEL), lambda i: (i, 0)),
                  pl.BlockSpec((D_MODEL, D_MODEL), lambda i: (0, 0)),
                  pl.BlockSpec((tm, D_MODEL), lambda i: (i, 0)),
                  pl.BlockSpec((1, D_MODEL), lambda i: (0, 0))],
        out_specs=[pl.BlockSpec((tm, D_MODEL), lambda i: (i, 0)),
                   pl.BlockSpec((tm, D_MODEL), lambda i: (i, 0))],
        out_shape=[jax.ShapeDtypeStruct((t, D_MODEL), F32),
                   jax.ShapeDtypeStruct((t, D_MODEL), BF16)],
        compiler_params=_cparams("parallel"),
        name="out_proj",
    )(merged, w_out, x, g_next.reshape(1, D_MODEL))


def _mlp_kernel(h_ref, wu_ref, wd_ref, x_ref, g_ref, xo_ref, hn_ref, acc_ref):
    j = pl.program_id(1)

    @pl.when(j == 0)
    def _():
        acc_ref[...] = jnp.zeros_like(acc_ref)

    a = jnp.dot(h_ref[...], wu_ref[...], preferred_element_type=F32)
    a = jnp.square(jnp.maximum(a, 0.0)).astype(BF16)
    acc_ref[...] += jnp.dot(a, wd_ref[...], preferred_element_type=F32)

    @pl.when(j == pl.num_programs(1) - 1)
    def _():
        xn = x_ref[...] + acc_ref[...]
        xo_ref[...] = xn
        hn_ref[...] = _rms(xn, g_ref[...]).astype(hn_ref.dtype)


def mlp_call(h, w_up, w_down, x, g_next, next_dtype, tm, tf):
    t = x.shape[0]
    return pl.pallas_call(
        _mlp_kernel,
        grid=(t // tm, D_FF // tf),
        in_specs=[pl.BlockSpec((tm, D_MODEL), lambda i, j: (i, 0)),
                  pl.BlockSpec((D_MODEL, tf), lambda i, j: (0, j)),
                  pl.BlockSpec((tf, D_MODEL), lambda i, j: (j, 0)),
                  pl.BlockSpec((tm, D_MODEL), lambda i, j: (i, 0)),
                  pl.BlockSpec((1, D_MODEL), lambda i, j: (0, 0))],
        out_specs=[pl.BlockSpec((tm, D_MODEL), lambda i, j: (i, 0)),
                   pl.BlockSpec((tm, D_MODEL), lambda i, j: (i, 0))],
        out_shape=[jax.ShapeDtypeStruct((t, D_MODEL), F32),
                   jax.ShapeDtypeStruct((t, D_MODEL), next_dtype)],
        scratch_shapes=[pltpu.VMEM((tm, D_MODEL), F32)],
        compiler_params=_cparams("parallel", "arbitrary"),
        name="mlp",
    )(h, w_up, w_down, x, g_next.reshape(1, D_MODEL))


def _small_kernel(z_ref, b_ref, a_ref, f_ref, carry_ref):
    @pl.when(pl.program_id(1) == 0)
    def _():
        carry_ref[...] = jnp.zeros_like(carry_ref)

    z = z_ref[...]
    tm = z.shape[0]
    lane = lax.broadcasted_iota(jnp.int32, z.shape, 1)
    pre = z + b_ref[...]
    lf = jnp.minimum(pre, 0.0) - jnp.log1p(jnp.exp(-jnp.abs(pre)))
    lf = jnp.where(lane < N_HEADS, lf, 0.0)
    a_ref[...] = jnp.where(lane < N_HEADS, lf, jnp.where(lane < 4 * N_HEADS, jax.nn.sigmoid(z), 0.0))
    row = lax.broadcasted_iota(jnp.int32, (tm, tm), 0)
    col = lax.broadcasted_iota(jnp.int32, (tm, tm), 1)
    tril = jnp.where(col <= row, 1.0, 0.0)
    f = jnp.dot(tril, lf, preferred_element_type=F32, precision=HIGHEST) + carry_ref[0:1, :]
    f_ref[...] = f
    carry_ref[0:1, :] = f[tm - 1:tm, :]


def small_call(z_small, b_forget, n, t, tm):
    bias = jnp.zeros((1, SMALL_W), F32).at[0, :N_HEADS].set(b_forget)
    nt = t // tm
    return pl.pallas_call(
        _small_kernel,
        grid=(n, nt),
        in_specs=[pl.BlockSpec((tm, SMALL_W), lambda b, j: (b * nt + j, 0)),
                  pl.BlockSpec((1, SMALL_W), lambda b, j: (0, 0))],
        out_specs=[pl.BlockSpec((tm, SMALL_W), lambda b, j: (b * nt + j, 0)),
                   pl.BlockSpec((tm, SMALL_W), lambda b, j: (b * nt + j, 0))],
        out_shape=[jax.ShapeDtypeStruct((n * t, SMALL_W), F32),
                   jax.ShapeDtypeStruct((n * t, SMALL_W), F32)],
        scratch_shapes=[pltpu.VMEM((8, SMALL_W), F32)],
        compiler_params=_cparams("parallel", "arbitrary"),
        name="forget_and_gates",
    )(z_small, bias)


def _conv_prompt_kernel(z_ref, w_ref, o_ref, st_ref, prev_ref):
    @pl.when(pl.program_id(1) == 0)
    def _():
        prev_ref[...] = jnp.zeros_like(prev_ref)

    bw = BRANCH_W
    u = z_ref[:, 2 * bw:3 * bw] * z_ref[:, 0:bw]
    tm = u.shape[0]
    row = lax.broadcasted_iota(jnp.int32, u.shape, 0)
    u1 = jnp.where(row == 0, prev_ref[7:8, :], pltpu.roll(u, 1, 0))
    u2 = jnp.where(row == 0, prev_ref[6:7, :], jnp.where(row == 1, prev_ref[7:8, :], pltpu.roll(u, 2, 0)))
    y = u2 * w_ref[0:1, :] + u1 * w_ref[1:2, :] + u * w_ref[2:3, :]
    o_ref[...] = z_ref[:, bw:2 * bw] * y
    prev_ref[...] = u[tm - 8:tm]
    st_ref[0] = u[tm - 2:tm]


def conv_prompt_call(z_conv, conv_w, n, t, tm):
    nt = t // tm
    return pl.pallas_call(
        _conv_prompt_kernel,
        grid=(n, nt),
        in_specs=[pl.BlockSpec((tm, 3 * BRANCH_W), lambda b, j: (b * nt + j, 0)),
                  pl.BlockSpec((CONV_W, BRANCH_W), lambda b, j: (0, 0))],
        out_specs=[pl.BlockSpec((tm, BRANCH_W), lambda b, j: (b * nt + j, 0)),
                   pl.BlockSpec((1, CONV_W - 1, BRANCH_W), lambda b, j: (b, 0, 0))],
        out_shape=[jax.ShapeDtypeStruct((n * t, BRANCH_W), F32),
                   jax.ShapeDtypeStruct((n, CONV_W - 1, BRANCH_W), F32)],
        scratch_shapes=[pltpu.VMEM((8, BRANCH_W), F32)],
        compiler_params=_cparams("parallel", "arbitrary"),
        name="conv_prompt",
    )(z_conv, conv_w)


def _online_step(s, vt, carry):
    m, l, acc = carry
    m_new = jnp.maximum(m, jnp.max(s, axis=0, keepdims=True))
    alpha = jnp.exp(m - m_new)
    p = jnp.exp(s - m_new)
    l = alpha * l + jnp.sum(p, axis=0, keepdims=True)
    acc = alpha * acc + jnp.dot(vt, p.astype(BF16), preferred_element_type=F32)
    return m_new, l, acc


def _softmax_init():
    return (jnp.full((1, TQ), NEG, F32), jnp.zeros((1, TQ), F32), jnp.zeros((HEAD_DIM, TQ), F32))


def _query_pair(q_ref, h, half):
    hp, e = divmod(h, 2)
    qp = q_ref[:, hp * PAIR_W:(hp + 1) * PAIR_W]
    if e != half:
        qp = pltpu.roll(qp, HEAD_DIM, 1)
    lane = lax.broadcasted_iota(jnp.int32, qp.shape, 1)
    return jnp.where((lane // HEAD_DIM) == half, qp, 0.0)


def _tile_masks():
    sub = lax.broadcasted_iota(jnp.int32, (TK, TQ), 0)
    lane = lax.broadcasted_iota(jnp.int32, (TK, TQ), 1)
    return sub <= lane, sub > lane


def _key_col():
    return lax.broadcasted_iota(jnp.int32, (TK, 1), 0).astype(F32)


def _fox_kernel(q_ref, k_ref, vt_ref, fk_ref, o_ref, ot_ref):
    i = pl.program_id(1)
    causal, _ = _tile_masks()
    for h in range(N_HEADS):
        hp, e = divmod(h, 2)
        qpad = (_query_pair(q_ref, h, e) * SCALE).astype(BF16)

        def tile(j, carry, diag):
            r0 = pl.multiple_of(j * TK, TK)
            kp = k_ref[pl.ds(r0, TK), hp * PAIR_W:(hp + 1) * PAIR_W]
            s = _dot_nt(kp, qpad) - fk_ref[pl.ds(r0, TK), h:h + 1]
            if diag:
                s = jnp.where(causal, s, MASKED)
            return _online_step(s, vt_ref[j, h * HEAD_DIM:(h + 1) * HEAD_DIM, :], carry)

        carry = lax.fori_loop(0, i, lambda j, c: tile(j, c, False), _softmax_init())
        _, l, acc = tile(i, carry, True)
        ot_ref[h * HEAD_DIM:(h + 1) * HEAD_DIM, :] = acc * (1.0 / l)
    o_ref[...] = ot_ref[...].T


def fox_prompt_call(q, kb, vt, fk, n, t):
    nq = t // TQ
    return pl.pallas_call(
        _fox_kernel,
        grid=(n, nq),
        in_specs=[pl.BlockSpec((TQ, BRANCH_W), lambda b, i: (b * nq + i, 0)),
                  pl.BlockSpec((t, BRANCH_W), lambda b, i: (b, 0)),
                  pl.BlockSpec((t // TK, BRANCH_W, TK), lambda b, i: (b, 0, 0)),
                  pl.BlockSpec((t, SMALL_W), lambda b, i: (b, 0))],
        out_specs=pl.BlockSpec((TQ, BRANCH_W), lambda b, i: (b * nq + i, 0)),
        out_shape=jax.ShapeDtypeStruct((n * t, BRANCH_W), F32),
        scratch_shapes=[pltpu.VMEM((BRANCH_W, TQ), F32)],
        compiler_params=_cparams("parallel", "arbitrary"),
        name="fox_prompt",
    )(q, kb, vt, fk)


def _rank_before(score, bidx, nblk):
    cnt = jnp.zeros(score.shape, F32)
    for b2 in range(nblk):
        row = score[b2:b2 + 1, :]
        beats = (row > score) | ((row == score) & (b2 < bidx))
        cnt = cnt + jnp.where(beats, 1.0, 0.0)
    return cnt


def _moba_kernel(q_ref, k_ref, vt_ref, km_ref, o_ref, ot_ref, sel_ref, *, nblk):
    i = pl.program_id(1)
    causal, _ = _tile_masks()
    kcol = _key_col()
    bidx = lax.broadcasted_iota(jnp.int32, (nblk, TQ), 0)
    for h in range(N_HEADS):
        hp, e = divmod(h, 2)
        q32 = _query_pair(q_ref, h, e)
        qpad = (q32 * SCALE).astype(BF16)
        gs = _dot_nt(km_ref[0, :, hp * PAIR_W:(hp + 1) * PAIR_W], q32, precision=HIGHEST)
        gs = jnp.where(bidx < i, gs, NEG)
        sel = (_rank_before(gs, bidx, nblk) < MOBA_TOPK) & (bidx < i)
        sel_ref[...] = jnp.where(sel, 1.0, 0.0)
        slope = 2.0 ** (-8.0 * (h + 1) / N_HEADS)

        def tile(j, carry, diag):
            r0 = pl.multiple_of(j * TK, TK)
            kp = k_ref[pl.ds(r0, TK), hp * PAIR_W:(hp + 1) * PAIR_W]
            bias = slope * kcol - slope * ((i - j) * TK).astype(F32)
            s = _dot_nt(kp, qpad) + bias
            if diag:
                s = jnp.where(causal, s, MASKED)
            else:
                s = jnp.where(sel_ref[pl.ds(j, 1), :] > 0.5, s, MASKED)
            return _online_step(s, vt_ref[j, h * HEAD_DIM:(h + 1) * HEAD_DIM, :], carry)

        carry = lax.fori_loop(0, i, lambda j, c: tile(j, c, False), _softmax_init())
        _, l, acc = tile(i, carry, True)
        ot_ref[h * HEAD_DIM:(h + 1) * HEAD_DIM, :] = acc * (1.0 / l)
    o_ref[...] = ot_ref[...].T


def moba_prompt_call(q, kb, vt, kmean, n, t):
    nq = t // TQ
    nblk = t // MOBA_BLOCK
    return pl.pallas_call(
        functools.partial(_moba_kernel, nblk=nblk),
        grid=(n, nq),
        in_specs=[pl.BlockSpec((TQ, BRANCH_W), lambda b, i: (b * nq + i, 0)),
                  pl.BlockSpec((t, BRANCH_W), lambda b, i: (b, 0)),
                  pl.BlockSpec((t // TK, BRANCH_W, TK), lambda b, i: (b, 0, 0)),
                  pl.BlockSpec((1, nblk, BRANCH_W), lambda b, i: (b, 0, 0))],
        out_specs=pl.BlockSpec((TQ, BRANCH_W), lambda b, i: (b * nq + i, 0)),
        out_shape=jax.ShapeDtypeStruct((n * t, BRANCH_W), F32),
        scratch_shapes=[pltpu.VMEM((BRANCH_W, TQ), F32), pltpu.VMEM((nblk, TQ), F32)],
        compiler_params=_cparams("parallel", "arbitrary"),
        name="moba_prompt",
    )(q, kb, vt, kmean.reshape(n, nblk, BRANCH_W))


def _gelu_tanh(x):
    return 0.5 * x * (1.0 + jnp.tanh(0.7978845608028654 * (x + 0.044715 * x * x * x)))


def _compress_kernel(pt_ref, a_ref, pet_ref, peb_ref, wt_ref, wb_ref, w2k_ref, w2v_ref, w2vt_ref,
                     kc_ref, vc_ref, vct_ref, rows_ref, *, rows):
    del pt_ref
    j = pl.program_id(1)
    r0 = pl.multiple_of(j * PAGE_ROWS, PAGE_ROWS)
    for i in range(CMP_S):
        rows_ref[pl.ds(r0, PAGE_ROWS), i * 2 * KV_W:(i + 1) * 2 * KV_W] = a_ref[0, :, i * 4 * KV_W:i * 4 * KV_W + 2 * KV_W]

    @pl.when(j == pl.num_programs(1) - 1)
    def _():
        chunk = min(256, rows)
        prev_top = jnp.zeros((1, 4 * CMP_HID), F32)
        for c in range(rows // chunk):
            a = rows_ref[c * chunk:(c + 1) * chunk, :]
            ht = jnp.dot((a + pet_ref[...]).astype(BF16), wt_ref[...], preferred_element_type=F32)
            hb = jnp.dot((a + peb_ref[...]).astype(BF16), wb_ref[...], preferred_element_type=F32)
            row = lax.broadcasted_iota(jnp.int32, ht.shape, 0)
            shifted = jnp.where(row == 0, prev_top, pltpu.roll(ht, 1, 0))
            prev_top = ht[chunk - 1:chunk, :]
            g = _gelu_tanh(shifted + hb).astype(BF16)
            kc_ref[0, c * chunk:(c + 1) * chunk, :] = jnp.dot(
                g[:, :2 * CMP_HID], w2k_ref[...], preferred_element_type=F32).astype(BF16)
            vc_ref[0, c * chunk:(c + 1) * chunk, :] = jnp.dot(
                g[:, 2 * CMP_HID:], w2v_ref[...], preferred_element_type=F32).astype(BF16)
            vct_ref[0, :, c * chunk:(c + 1) * chunk] = _dot_nt(w2vt_ref[...], g[:, 2 * CMP_HID:]).astype(BF16)


def compress_call(pages, page_idx, pe_top, pe_bot, wt, wb, w2k, w2v, w2vt, n, npages):
    rows = npages * PAGE_ROWS
    const = lambda b, j, pt: (0, 0)
    grid_spec = pltpu.PrefetchScalarGridSpec(
        num_scalar_prefetch=1,
        grid=(n, npages),
        in_specs=[pl.BlockSpec((1, PAGE_ROWS, CMP_S * 4 * KV_W), lambda b, j, pt: (pt[b, j], 0, 0)),
                  pl.BlockSpec((1, CMP_ROW), const),
                  pl.BlockSpec((1, CMP_ROW), const),
                  pl.BlockSpec((CMP_ROW, 4 * CMP_HID), const, pipeline_mode=pl.Buffered(1)),
                  pl.BlockSpec((CMP_ROW, 4 * CMP_HID), const, pipeline_mode=pl.Buffered(1)),
                  pl.BlockSpec((2 * CMP_HID, KV_W), const),
                  pl.BlockSpec((2 * CMP_HID, KV_W), const),
                  pl.BlockSpec((KV_W, 2 * CMP_HID), const)],
        out_specs=[pl.BlockSpec((1, rows, KV_W), lambda b, j, pt: (b, 0, 0)),
                   pl.BlockSpec((1, rows, KV_W), lambda b, j, pt: (b, 0, 0)),
                   pl.BlockSpec((1, KV_W, rows), lambda b, j, pt: (b, 0, 0))],
        scratch_shapes=[pltpu.VMEM((rows, CMP_ROW), F32)],
    )
    return pl.pallas_call(
        functools.partial(_compress_kernel, rows=rows),
        grid_spec=grid_spec,
        out_shape=[jax.ShapeDtypeStruct((n, rows, KV_W), BF16),
                   jax.ShapeDtypeStruct((n, rows, KV_W), BF16),
                   jax.ShapeDtypeStruct((n, KV_W, rows), BF16)],
        compiler_params=_cparams("parallel", "arbitrary"),
        name="nsa_compress",
    )(page_idx, pages, pe_top, pe_bot, wt, wb, w2k, w2v, w2vt)


def _nsa_kernel(q_ref, kc_ref, vct_ref, ks_ref, vst_ref, kw_ref, vwt_ref, gt_ref, o_ref,
                ot_ref, sel_ref, *, ncmp, nsb):
    i = pl.program_id(1)
    causal, below = _tile_masks()
    kcol = _key_col()
    lane_q = lax.broadcasted_iota(jnp.int32, (1, TQ), 1)
    qpos = i * TQ + lane_q
    crow = lax.broadcasted_iota(jnp.int32, (ncmp, 1), 0)
    cpos = CMP_S * crow + (CMP_S - 1)
    cvalid = (crow >= 1) & (cpos <= qpos)
    cposf = cpos.astype(F32)
    mj = lax.broadcasted_iota(jnp.int32, (nsb, ncmp), 0)
    mc = lax.broadcasted_iota(jnp.int32, (nsb, ncmp), 1)
    ratio = SLC_BLOCK // CMP_S
    gather_m = jnp.where((mc >= ratio * mj) & (mc <= ratio * mj + ratio) & (mc >= 1), 1.0, 0.0)
    bj = lax.broadcasted_iota(jnp.int32, (nsb, TQ), 0)
    own = qpos // SLC_BLOCK
    forced = (bj == 0) | (bj >= own - 1)
    allowed = bj <= own
    ecol = lax.broadcasted_iota(jnp.int32, (TK, nsb), 1)
    erow = lax.broadcasted_iota(jnp.int32, (TK, nsb), 0) // SLC_BLOCK

    for g in range(NSA_G):
        imp = jnp.zeros((ncmp, TQ), F32)
        qpads = []
        for r in range(NSA_R):
            h = g * NSA_R + r
            slope = 2.0 ** (-8.0 * (h + 1) / N_HEADS)
            qpad = (_query_pair(q_ref, h, g) * SCALE).astype(BF16)
            qpads.append(qpad)
            s = _dot_nt(kc_ref[0], qpad) + slope * (cposf - (i * TQ).astype(F32))
            s = jnp.where(cvalid, s, MASKED)
            p = jnp.where(cvalid, jnp.exp(s - jnp.max(s, axis=0, keepdims=True)), 0.0)
            l = jnp.sum(p, axis=0, keepdims=True)
            p = p * (1.0 / jnp.where(l > 0.0, l, 1.0))
            imp = imp + p
            o_cmp = jnp.dot(vct_ref[0, g * HEAD_DIM:(g + 1) * HEAD_DIM, :], p.astype(BF16),
                            preferred_element_type=F32)
            ot_ref[h * HEAD_DIM:(h + 1) * HEAD_DIM, :] = gt_ref[3 * h:3 * h + 1, :] * o_cmp
        p_slc = jnp.dot(gather_m, imp, preferred_element_type=F32, precision=HIGHEST)
        score = jnp.where(allowed, jnp.where(forced, p_slc + FORCE_SCORE, p_slc), -1.0)
        sel = (_rank_before(score, bj, nsb) < SLC_N) & allowed
        sel_ref[...] = jnp.where(sel, 1.0, 0.0).astype(BF16)

        for r in range(NSA_R):
            h = g * NSA_R + r
            slope = 2.0 ** (-8.0 * (h + 1) / N_HEADS)
            qpad = qpads[r]

            def alibi(j):
                return slope * kcol - slope * ((i - j) * TK).astype(F32)

            def slc_tile(j, carry, diag):
                r0 = pl.multiple_of(j * TK, TK)
                expand = jnp.where(ecol == erow + j * (TK // SLC_BLOCK), 1.0, 0.0).astype(BF16)
                keep = jnp.dot(expand, sel_ref[...], preferred_element_type=F32) > 0.5
                if diag:
                    keep = keep & causal
                s = jnp.where(keep, _dot_nt(ks_ref[pl.ds(r0, TK), :], qpad) + alibi(j), MASKED)
                return _online_step(s, vst_ref[j, g * HEAD_DIM:(g + 1) * HEAD_DIM, :], carry)

            carry = lax.fori_loop(0, i, lambda j, c: slc_tile(j, c, False), _softmax_init())
            _, l, acc = slc_tile(i, carry, True)
            o_slc = acc * (1.0 / l)

            def win_tile(j, carry, mask):
                r0 = pl.multiple_of(j * TK, TK)
                s = _dot_nt(kw_ref[pl.ds(r0, TK), :], qpad) + alibi(j)
                if mask is not None:
                    s = jnp.where(mask, s, MASKED)
                return _online_step(s, vwt_ref[j, g * HEAD_DIM:(g + 1) * HEAD_DIM, :], carry)

            carry = _softmax_init()
            carry = lax.cond(i >= 2, lambda c: win_tile(i - 2, c, below), lambda c: c, carry)
            carry = lax.cond(i >= 1, lambda c: win_tile(i - 1, c, None), lambda c: c, carry)
            _, l, acc = win_tile(i, carry, causal)
            o_win = acc * (1.0 / l)
            rows = slice(h * HEAD_DIM, (h + 1) * HEAD_DIM)
            ot_ref[rows, :] = (ot_ref[rows, :] + gt_ref[3 * h + 1:3 * h + 2, :] * o_slc
                               + gt_ref[3 * h + 2:3 * h + 3, :] * o_win)
    o_ref[...] = ot_ref[...].T


def nsa_prompt_call(q, kcmp, vcmpt, ks, vst, kw, vwt, gate_t, n, t):
    nq = t // TQ
    ncmp = kcmp.shape[1]
    nsb = t // SLC_BLOCK
    return pl.pallas_call(
        functools.partial(_nsa_kernel, ncmp=ncmp, nsb=nsb),
        grid=(n, nq),
        in_specs=[pl.BlockSpec((TQ, BRANCH_W), lambda b, i: (b * nq + i, 0)),
                  pl.BlockSpec((1, ncmp, KV_W), lambda b, i: (b, 0, 0)),
                  pl.BlockSpec((1, KV_W, ncmp), lambda b, i: (b, 0, 0)),
                  pl.BlockSpec((t, KV_W), lambda b, i: (b, 0)),
                  pl.BlockSpec((t // TK, KV_W, TK), lambda b, i: (b, 0, 0)),
                  pl.BlockSpec((t, KV_W), lambda b, i: (b, 0)),
                  pl.BlockSpec((t // TK, KV_W, TK), lambda b, i: (b, 0, 0)),
                  pl.BlockSpec((3 * N_HEADS, TQ), lambda b, i: (0, b * nq + i))],
        out_specs=pl.BlockSpec((TQ, BRANCH_W), lambda b, i: (b * nq + i, 0)),
        out_shape=jax.ShapeDtypeStruct((n * t, BRANCH_W), F32),
        scratch_shapes=[pltpu.VMEM((BRANCH_W, TQ), F32), pltpu.VMEM((nsb, TQ), BF16)],
        compiler_params=_cparams("parallel", "arbitrary"),
        name="nsa_prompt",
    )(q, kcmp, vcmpt, ks, vst, kw, vwt, gate_t)


def alibi_slopes(n):
    return jnp.exp2(-8.0 * jnp.arange(1, n + 1, dtype=jnp.float32) / n)


def masked_softmax(s, mask, axis=-1):
    p = jax.nn.softmax(jnp.where(mask, s, NEG), axis=axis)
    return p * mask


def sweep_queries(fn, block, *qs):
    n, t = qs[0].shape[:2]
    qb = block if t % block == 0 else t
    nb = t // qb
    xs = tuple(jnp.moveaxis(a.reshape((n, nb, qb) + a.shape[2:]), 1, 0) for a in qs)
    out = lax.map(lambda args: fn(args[0] * qb, *args[1:]), (jnp.arange(nb, dtype=jnp.int32),) + xs)
    return jnp.moveaxis(out, 0, 1).reshape((n, t) + out.shape[3:])


def gather_pages(pool, page_table):
    g = pool[page_table]
    return g.reshape((g.shape[0], g.shape[1] * g.shape[2]) + g.shape[3:])


def fox_attention(q, k, v, logf, q0):
    d = q.shape[-1]
    L = k.shape[1]
    scale = d ** -0.5
    F = jnp.cumsum(logf.astype(jnp.float32), axis=1)
    Fk = jnp.moveaxis(F, 1, 2)
    Fq = F[:, q0:]
    kpos = jnp.arange(L)

    def block(start, qb, fq):
        qpos = q0 + start + jnp.arange(qb.shape[1])
        s = jnp.einsum('nqhd,nkhd->nhqk', qb, k).astype(jnp.float32) * scale
        s = s + jnp.moveaxis(fq, 1, 2)[..., None] - Fk[:, :, None, :]
        p = masked_softmax(s, kpos[None, :] <= qpos[:, None])
        return jnp.einsum('nhqk,nkhd->nqhd', p.astype(q.dtype), v)

    return sweep_queries(block, Q_BLOCK, q, Fq)


def moba_attention(q, k, v, q0, slopes):
    n, tq, nh, d = q.shape
    L = k.shape[1]
    scale = d ** -0.5
    nbk = -(-L // MOBA_BLOCK)
    pad = nbk * MOBA_BLOCK - L

    def to_blocks(a):
        a = jnp.pad(a, ((0, 0), (0, pad), (0, 0), (0, 0)))
        return a.reshape(n, nbk, MOBA_BLOCK, nh, d).transpose(0, 3, 1, 2, 4)

    kb, vb = to_blocks(k), to_blocks(v)
    kmean = jnp.mean(kb.astype(jnp.float32), axis=3)
    topk = min(MOBA_TOPK, nbk)
    bidx = jnp.arange(nbk)
    ni = jnp.arange(n)[:, None, None, None]
    hi = jnp.arange(nh)[None, :, None, None]
    sl = slopes[None, :, None, None, None]

    def block(start, qb):
        m = qb.shape[1]
        qpos = q0 + start + jnp.arange(m)
        own = qpos // MOBA_BLOCK
        gs = jnp.einsum('nqhd,nhbd->nhqb', qb.astype(jnp.float32), kmean)
        gs = jnp.where(bidx[None, :] < own[:, None], gs, NEG)
        _, top = lax.top_k(gs, topk)
        sel_ok = top < own[:, None]
        idx = jnp.concatenate([top, jnp.broadcast_to(own[:, None], (n, nh, m, 1))], axis=-1)
        ok = jnp.concatenate([sel_ok, jnp.ones((n, nh, m, 1), bool)], axis=-1)
        kg = kb[ni, hi, idx]
        vg = vb[ni, hi, idx]
        kpos = idx[..., None] * MOBA_BLOCK + jnp.arange(MOBA_BLOCK)
        s = jnp.einsum('nqhd,nhqsjd->nhqsj', qb, kg).astype(jnp.float32) * scale
        s = s - sl * (qpos[:, None, None] - kpos).astype(jnp.float32)
        mask = ok[..., None] & (kpos <= qpos[:, None, None])
        p = masked_softmax(s, mask, axis=(-2, -1))
        return jnp.einsum('nhqsj,nhqsjd->nqhd', p.astype(q.dtype), vg)

    return sweep_queries(block, SPARSE_Q_BLOCK, q)


def compress_tokens(a, w1, w2, pe):
    n, L, g, d = a.shape
    nc = (L - CMP_L) // CMP_S + 1
    idx = jnp.arange(nc)[:, None] * CMP_S + jnp.arange(CMP_L)[None, :]
    blocks = a[:, idx] + pe[:, None, :].astype(a.dtype)
    flat = jnp.swapaxes(blocks, 2, 3).reshape(n, nc, g, CMP_L * d)
    return jax.nn.gelu(flat @ w1) @ w2


def nsa_attention(q, kc, vc, ks, vs, kw, vw, gate, q0, w_ofs, slopes, cmp_w1, cmp_w2, cmp_pe):
    n, tq, nh, d = q.shape
    L = kc.shape[1]
    dt = q.dtype
    scale = d ** -0.5
    qg = q.reshape(n, tq, NSA_G, NSA_R, d)
    k_cmp = compress_tokens(kc, cmp_w1[0], cmp_w2[0], cmp_pe[0])
    v_cmp = compress_tokens(vc, cmp_w1[1], cmp_w2[1], cmp_pe[1])
    nc = k_cmp.shape[1]
    cpos = jnp.arange(nc) * CMP_S + (CMP_L - 1)
    nsb = -(-L // SLC_BLOCK)
    padl = nsb * SLC_BLOCK - L

    def to_blocks(a):
        a = jnp.pad(a, ((0, 0), (0, padl), (0, 0), (0, 0)))
        return a.reshape(n, nsb, SLC_BLOCK, NSA_G, d).transpose(0, 3, 1, 2, 4)

    ks_b, vs_b = to_blocks(ks), to_blocks(vs)
    nsel = min(SLC_N, nsb)
    front = CMP_L // CMP_S - 1
    ratio = SLC_BLOCK // CMP_S
    width = ratio + front
    back = ratio * nsb + width - front - nc
    kw_p = jnp.pad(kw, ((0, 0), (WINDOW, 0), (0, 0), (0, 0)))
    vw_p = jnp.pad(vw, ((0, 0), (WINDOW, 0), (0, 0), (0, 0)))
    sl = slopes.reshape(NSA_G, NSA_R)[None, :, :, None, None]
    ni = jnp.arange(n)[:, None, None, None]
    gi = jnp.arange(NSA_G)[None, :, None, None]
    bj = jnp.arange(nsb)

    def block(start, qb, gb):
        m = qb.shape[1]
        qpos = q0 + start + jnp.arange(m)
        s = jnp.einsum('nqgrd,ncgd->ngrqc', qb, k_cmp).astype(jnp.float32) * scale
        s = s - sl * (qpos[:, None] - cpos[None, :]).astype(jnp.float32)
        p_cmp = masked_softmax(s, cpos[None, :] <= qpos[:, None])
        o_cmp = jnp.einsum('ngrqc,ncgd->nqgrd', p_cmp.astype(dt), v_cmp)
        imp = jnp.pad(p_cmp.sum(axis=2), ((0, 0), (0, 0), (0, 0), (front, back)))
        p_slc = imp[..., 0:ratio * nsb:ratio]
        for u in range(1, width):
            p_slc = p_slc + imp[..., u:u + ratio * nsb:ratio]
        own = qpos // SLC_BLOCK
        forced = (bj[None, :] == 0) | (bj[None, :] >= own[:, None] - 1)
        allowed = bj[None, :] <= own[:, None]
        score = jnp.where(allowed, jnp.where(forced, p_slc + FORCE_SCORE, p_slc), -1.0)
        _, top = lax.top_k(score, nsel)
        ok = top <= own[:, None]
        kg = ks_b[ni, gi, top]
        vg = vs_b[ni, gi, top]
        kpos = top[..., None] * SLC_BLOCK + jnp.arange(SLC_BLOCK)
        dist = (qpos[:, None, None] - kpos)[:, :, None].astype(jnp.float32)
        s2 = jnp.einsum('nqgrd,ngqsjd->ngrqsj', qb, kg).astype(jnp.float32) * scale - sl[..., None] * dist
        mask2 = (ok[..., None] & (kpos <= qpos[:, None, None]))[:, :, None]
        p2 = masked_softmax(s2, mask2, axis=(-2, -1))
        o_slc = jnp.einsum('ngrqsj,ngqsjd->nqgrd', p2.astype(dt), vg)
        off = q0 + start - w_ofs
        kwin = lax.dynamic_slice_in_dim(kw_p, off, WINDOW + m, axis=1)
        vwin = lax.dynamic_slice_in_dim(vw_p, off, WINDOW + m, axis=1)
        wpos = q0 + start - WINDOW + jnp.arange(WINDOW + m)
        s3 = jnp.einsum('nqgrd,nkgd->ngrqk', qb, kwin).astype(jnp.float32) * scale
        s3 = s3 - sl * (qpos[:, None] - wpos[None, :]).astype(jnp.float32)
        wmask = (wpos[None, :] <= qpos[:, None]) & (wpos[None, :] > qpos[:, None] - WINDOW) & (wpos[None, :] >= 0)
        p3 = masked_softmax(s3, wmask)
        o_win = jnp.einsum('ngrqk,nkgd->nqgrd', p3.astype(dt), vwin)
        return gb[..., 0:1] * o_cmp + gb[..., 1:2] * o_slc + gb[..., 2:3] * o_win

    return sweep_queries(block, SPARSE_Q_BLOCK, qg, gate)


PAGE = 128


def _head_diag_mask(width):
    sub = lax.broadcasted_iota(jnp.int32, (N_HEADS, width), 0)
    lane = lax.broadcasted_iota(jnp.int32, (N_HEADS, width), 1)
    return sub == lane // HEAD_DIM


def _block_diag_queries(q_ref, nt):
    diag = _head_diag_mask(BRANCH_W)
    return jnp.concatenate([jnp.where(diag, q_ref[0, t:t + 1, :], 0.0) for t in range(nt)], axis=0)


def _extract_heads(o, nt):
    diag = _head_diag_mask(BRANCH_W)
    return jnp.concatenate(
        [jnp.sum(jnp.where(diag, o[t * N_HEADS:(t + 1) * N_HEADS], 0.0), axis=0, keepdims=True) for t in range(nt)],
        axis=0)


def _slope_col(rows):
    h = lax.broadcasted_iota(jnp.int32, (rows, 1), 0) % N_HEADS
    col = jnp.zeros((rows, 1), F32)
    for k in range(N_HEADS):
        col = jnp.where(h == k, 2.0 ** (-8.0 * (k + 1) / N_HEADS), col)
    return col


def _row_softmax_step(s, v, m_ref, l_ref, acc_ref):
    m_old = m_ref[...]
    m_new = jnp.maximum(m_old, jnp.max(s, axis=1, keepdims=True))
    alpha = jnp.exp(m_old - m_new)
    p = jnp.exp(s - m_new)
    l_ref[...] = alpha * l_ref[...] + jnp.sum(p, axis=1, keepdims=True)
    acc_ref[...] = alpha * acc_ref[...] + jnp.dot(p.astype(BF16), v, preferred_element_type=F32)
    m_ref[...] = m_new


def _softmax_reset(m_ref, l_ref, acc_ref):
    m_ref[...] = jnp.full(m_ref.shape, NEG, F32)
    l_ref[...] = jnp.zeros(l_ref.shape, F32)
    acc_ref[...] = jnp.zeros(acc_ref.shape, F32)


def _new_token_mask(rows, nt):
    k = lax.broadcasted_iota(jnp.int32, (rows, PAGE), 1)
    t = lax.broadcasted_iota(jnp.int32, (rows, PAGE), 0) // N_HEADS
    return (k <= t) & (k < nt)


def _dfox_kernel(pt_ref, q_ref, pg_ref, lf_ref, npg_ref, nlf_ref, o_ref, qbd_ref, m_ref, l_ref, acc_ref, cf_ref, *, nt):
    del pt_ref
    j = pl.program_id(1)
    rows = nt * N_HEADS

    @pl.when(j == 0)
    def _():
        qbd_ref[...] = (_block_diag_queries(q_ref, nt) * SCALE).astype(BF16)
        _softmax_reset(m_ref, l_ref, acc_ref)
        cf_ref[...] = jnp.zeros_like(cf_ref)

    def step(kv_ref, lf, mask):
        k = kv_ref[0, :, 0:BRANCH_W].astype(BF16)
        v = kv_ref[0, :, BRANCH_W:2 * BRANCH_W].astype(BF16)
        rr = lax.broadcasted_iota(jnp.int32, (rows, N_HEADS), 0) % N_HEADS
        rc = lax.broadcasted_iota(jnp.int32, (rows, N_HEADS), 1)
        lfe = _dot_nt(jnp.where(rr == rc, 1.0, 0.0), lf, precision=HIGHEST)
        a = lax.broadcasted_iota(jnp.int32, (PAGE, PAGE), 0)
        b = lax.broadcasted_iota(jnp.int32, (PAGE, PAGE), 1)
        fk = jnp.dot(lfe, jnp.where(a <= b, 1.0, 0.0), preferred_element_type=F32, precision=HIGHEST) + cf_ref[...]
        cf_ref[...] = fk[:, PAGE - 1:PAGE]
        s = _dot_nt(qbd_ref[...], k) - fk
        if mask is not None:
            s = jnp.where(mask, s, MASKED)
        _row_softmax_step(s, v, m_ref, l_ref, acc_ref)

    step(pg_ref, lf_ref[0], None)

    @pl.when(j == pl.num_programs(1) - 1)
    def _():
        step(npg_ref, nlf_ref[0], _new_token_mask(rows, nt))
        o_ref[0] = _extract_heads(acc_ref[...] * (1.0 / l_ref[...]), nt)


def fox_decode_call(pt, q, cache_kv, cache_lf, new_kv, new_lf):
    n, nt, _ = q.shape
    npages = pt.shape[1]
    rows = nt * N_HEADS
    grid_spec = pltpu.PrefetchScalarGridSpec(
        num_scalar_prefetch=1,
        grid=(n, npages),
        in_specs=[pl.BlockSpec((1, nt, BRANCH_W), lambda b, j, pt: (b, 0, 0)),
                  pl.BlockSpec((1, PAGE, 2 * BRANCH_W), lambda b, j, pt: (pt[b, j], 0, 0)),
                  pl.BlockSpec((1, PAGE, N_HEADS), lambda b, j, pt: (pt[b, j], 0, 0)),
                  pl.BlockSpec((1, PAGE, 2 * BRANCH_W), lambda b, j, pt: (b, 0, 0)),
                  pl.BlockSpec((1, PAGE, N_HEADS), lambda b, j, pt: (b, 0, 0))],
        out_specs=pl.BlockSpec((1, nt, BRANCH_W), lambda b, j, pt: (b, 0, 0)),
        scratch_shapes=[pltpu.VMEM((rows, BRANCH_W), BF16), pltpu.VMEM((rows, 1), F32), pltpu.VMEM((rows, 1), F32),
                        pltpu.VMEM((rows, BRANCH_W), F32), pltpu.VMEM((rows, 1), F32)],
    )
    return pl.pallas_call(
        functools.partial(_dfox_kernel, nt=nt),
        grid_spec=grid_spec,
        out_shape=jax.ShapeDtypeStruct((n, nt, BRANCH_W), F32),
        compiler_params=_cparams("parallel", "arbitrary"),
        name="fox_decode",
    )(pt, q, cache_kv, cache_lf, new_kv, new_lf)


def _dmoba_sel_kernel(pt_ref, q_ref, kp_ref, sel_ref, q32_ref, g_ref, *, nt):
    del pt_ref
    j = pl.program_id(1)
    rows = nt * N_HEADS
    pages_per_block = MOBA_BLOCK // PAGE

    @pl.when(j == 0)
    def _():
        q32_ref[...] = _block_diag_queries(q_ref, nt)
        g_ref[...] = jnp.zeros_like(g_ref)

    ksum = jnp.sum(kp_ref[0], axis=0, keepdims=True)
    prow = lax.broadcasted_iota(jnp.int32, (g_ref.shape[1], BRANCH_W), 0)
    g_ref[...] += _dot_nt(q32_ref[...], jnp.where(prow == j, ksum, 0.0), precision=HIGHEST)

    @pl.when(j == pl.num_programs(1) - 1)
    def _():
        npg = g_ref.shape[1]
        g = g_ref[...]
        lane = lax.broadcasted_iota(jnp.int32, (rows, npg), 1)
        blk = g
        for u in range(1, pages_per_block):
            blk = blk + pltpu.roll(g, npg - u, 1)
        gs = jnp.where(lane % pages_per_block == 0, blk * (1.0 / MOBA_BLOCK), NEG)
        sel = jnp.zeros((rows, npg), F32)
        for _ in range(MOBA_TOPK):
            mx = jnp.max(gs, axis=1, keepdims=True)
            idx = jnp.min(jnp.where(gs == mx, lane, npg), axis=1, keepdims=True)
            hit = lane == idx
            sel = jnp.where(hit, 1.0, sel)
            gs = jnp.where(hit, MASKED, gs)
        out = sel
        for u in range(1, pages_per_block):
            out = out + pltpu.roll(sel, u, 1)
        sel_ref[0] = out


def moba_select_call(pt, q, cache_kv):
    n, nt, _ = q.shape
    npages = pt.shape[1]
    rows = nt * N_HEADS
    grid_spec = pltpu.PrefetchScalarGridSpec(
        num_scalar_prefetch=1,
        grid=(n, npages),
        in_specs=[pl.BlockSpec((1, nt, BRANCH_W), lambda b, j, pt: (b, 0, 0)),
                  pl.BlockSpec((1, PAGE, BRANCH_W), lambda b, j, pt: (pt[b, j], 0, 0))],
        out_specs=pl.BlockSpec((1, rows, npages), lambda b, j, pt: (b, 0, 0)),
        scratch_shapes=[pltpu.VMEM((rows, BRANCH_W), F32), pltpu.VMEM((rows, npages), F32)],
    )
    return pl.pallas_call(
        functools.partial(_dmoba_sel_kernel, nt=nt),
        grid_spec=grid_spec,
        out_shape=jax.ShapeDtypeStruct((n, rows, npages), F32),
        compiler_params=_cparams("parallel", "arbitrary"),
        name="moba_select",
    )(pt, q, cache_kv)


def _dmoba_kernel(pt_ref, q_ref, sel_ref, pg_ref, npg_ref, o_ref, qbd_ref, m_ref, l_ref, acc_ref, *, nt):
    del pt_ref
    j = pl.program_id(1)
    npages = pl.num_programs(1)
    rows = nt * N_HEADS
    slope = _slope_col(rows)
    klane = lax.broadcasted_iota(jnp.int32, (1, PAGE), 1)

    @pl.when(j == 0)
    def _():
        qbd_ref[...] = (_block_diag_queries(q_ref, nt) * SCALE).astype(BF16)
        _softmax_reset(m_ref, l_ref, acc_ref)

    def step(kv_ref, rel, mask):
        k = kv_ref[0, :, 0:BRANCH_W].astype(BF16)
        v = kv_ref[0, :, BRANCH_W:2 * BRANCH_W].astype(BF16)
        s = _dot_nt(qbd_ref[...], k) + slope * rel
        _row_softmax_step(jnp.where(mask, s, MASKED), v, m_ref, l_ref, acc_ref)

    prow = lax.broadcasted_iota(jnp.int32, (sel_ref.shape[2], PAGE), 0)
    keep = jnp.dot(sel_ref[0].astype(BF16), jnp.where(prow == j, 1.0, 0.0).astype(BF16),
                   preferred_element_type=F32) > 0.5
    step(pg_ref, (klane + (j - npages) * PAGE).astype(F32), keep)

    @pl.when(j == npages - 1)
    def _():
        step(npg_ref, klane.astype(F32), _new_token_mask(rows, nt))
        o_ref[0] = _extract_heads(acc_ref[...] * (1.0 / l_ref[...]), nt)


def moba_decode_call(pt, q, sel, cache_kv, new_kv):
    n, nt, _ = q.shape
    npages = pt.shape[1]
    rows = nt * N_HEADS
    grid_spec = pltpu.PrefetchScalarGridSpec(
        num_scalar_prefetch=1,
        grid=(n, npages),
        in_specs=[pl.BlockSpec((1, nt, BRANCH_W), lambda b, j, pt: (b, 0, 0)),
                  pl.BlockSpec((1, rows, npages), lambda b, j, pt: (b, 0, 0)),
                  pl.BlockSpec((1, PAGE, 2 * BRANCH_W), lambda b, j, pt: (pt[b, j], 0, 0)),
                  pl.BlockSpec((1, PAGE, 2 * BRANCH_W), lambda b, j, pt: (b, 0, 0))],
        out_specs=pl.BlockSpec((1, nt, BRANCH_W), lambda b, j, pt: (b, 0, 0)),
        scratch_shapes=[pltpu.VMEM((rows, BRANCH_W), BF16), pltpu.VMEM((rows, 1), F32), pltpu.VMEM((rows, 1), F32),
                        pltpu.VMEM((rows, BRANCH_W), F32)],
    )
    return pl.pallas_call(
        functools.partial(_dmoba_kernel, nt=nt),
        grid_spec=grid_spec,
        out_shape=jax.ShapeDtypeStruct((n, nt, BRANCH_W), F32),
        compiler_params=_cparams("parallel", "arbitrary"),
        name="moba_decode",
    )(pt, q, sel, cache_kv, new_kv)


def _dnsa_kernel(pt_ref, q_ref, gate_ref, kc_ref, vc_ref, pg_ref, npg_ref, win_ref, nwin_ref, o_ref,
                 qbd_ref, sel_ref, ocmp_ref, m_ref, l_ref, acc_ref, *, nt, nsb_pad):
    del pt_ref
    j = pl.program_id(1)
    npages = pl.num_programs(1)
    rows = nt * N_HEADS
    ncmp = kc_ref.shape[1]
    q0 = npages * PAGE
    slope = _slope_col(rows)
    klane = lax.broadcasted_iota(jnp.int32, (1, PAGE), 1)
    blocks_per_page = PAGE // SLC_BLOCK

    @pl.when(j == 0)
    def _():
        pr = lax.broadcasted_iota(jnp.int32, (BRANCH_W, KV_W), 0)
        pc = lax.broadcasted_iota(jnp.int32, (BRANCH_W, KV_W), 1)
        place = jnp.where((pr % HEAD_DIM == pc % HEAD_DIM) & (pc // HEAD_DIM == pr // (NSA_R * HEAD_DIM)), 1.0, 0.0)
        qg = jnp.dot(_block_diag_queries(q_ref, nt), place, preferred_element_type=F32, precision=HIGHEST)
        qbd = (qg * SCALE).astype(BF16)
        qbd_ref[...] = qbd
        c = lax.broadcasted_iota(jnp.int32, (1, ncmp), 1)
        s = _dot_nt(qbd, kc_ref[0]) + slope * (CMP_S * c + (CMP_S - 1) - q0).astype(F32)
        valid = c >= 1
        s = jnp.where(valid, s, MASKED)
        p = jnp.where(valid, jnp.exp(s - jnp.max(s, axis=1, keepdims=True)), 0.0)
        p = p * (1.0 / jnp.sum(p, axis=1, keepdims=True))
        ocmp_ref[...] = gate_ref[0, :, 0:1] * jnp.dot(p.astype(BF16), vc_ref[0], preferred_element_type=F32)
        gr = lax.broadcasted_iota(jnp.int32, (nt * NSA_G, rows), 0)
        gc = lax.broadcasted_iota(jnp.int32, (nt * NSA_G, rows), 1) // NSA_R
        imp = jnp.dot(jnp.where(gr == gc, 1.0, 0.0), p, preferred_element_type=F32, precision=HIGHEST)
        ratio = SLC_BLOCK // CMP_S
        mr = lax.broadcasted_iota(jnp.int32, (ncmp, nsb_pad), 0)
        mb = lax.broadcasted_iota(jnp.int32, (ncmp, nsb_pad), 1)
        gather_m = jnp.where((mr >= ratio * mb) & (mr <= ratio * mb + ratio) & (mr >= 1), 1.0, 0.0)
        p_slc = jnp.dot(imp, gather_m, preferred_element_type=F32, precision=HIGHEST)
        bj = lax.broadcasted_iota(jnp.int32, p_slc.shape, 1)
        own = q0 // SLC_BLOCK
        forced = (bj == 0) | (bj >= own - 1)
        allowed = bj <= own
        score = jnp.where(allowed, jnp.where(forced, p_slc + FORCE_SCORE, p_slc), -1.0)
        sel = jnp.zeros(p_slc.shape, F32)
        for _ in range(SLC_N):
            mx = jnp.max(score, axis=1, keepdims=True)
            idx = jnp.min(jnp.where(score == mx, bj, nsb_pad), axis=1, keepdims=True)
            hit = bj == idx
            sel = jnp.where(hit, 1.0, sel)
            score = jnp.where(hit, -2.0, score)
        sel = jnp.where(allowed, sel, 0.0)
        er = lax.broadcasted_iota(jnp.int32, (rows, nt * NSA_G), 0) // NSA_R
        ec = lax.broadcasted_iota(jnp.int32, (rows, nt * NSA_G), 1)
        sel_ref[...] = jnp.dot(jnp.where(er == ec, 1.0, 0.0), sel, preferred_element_type=F32).astype(BF16)
        _softmax_reset(m_ref, l_ref, acc_ref)

    def step(kv_ref, rel, mask):
        k = kv_ref[0, :, 0:KV_W].astype(BF16)
        v = kv_ref[0, :, KV_W:2 * KV_W].astype(BF16)
        s = _dot_nt(qbd_ref[...], k) + slope * rel
        if mask is not None:
            s = jnp.where(mask, s, MASKED)
        _row_softmax_step(s, v, m_ref, l_ref, acc_ref)

    brow = lax.broadcasted_iota(jnp.int32, (nsb_pad, PAGE), 0)
    bcol = lax.broadcasted_iota(jnp.int32, (nsb_pad, PAGE), 1) // SLC_BLOCK
    expand = jnp.where(brow == bcol + j * blocks_per_page, 1.0, 0.0).astype(BF16)
    keep = jnp.dot(sel_ref[...], expand, preferred_element_type=F32) > 0.5
    step(pg_ref, (klane + (j - npages) * PAGE).astype(F32), keep)

    @pl.when(j == npages - 1)
    def _():
        new_mask = _new_token_mask(rows, nt)
        step(npg_ref, klane.astype(F32), new_mask)
        o_slc = acc_ref[...] * (1.0 / l_ref[...])
        _softmax_reset(m_ref, l_ref, acc_ref)
        wlen = win_ref.shape[1]
        wi = lax.broadcasted_iota(jnp.int32, (rows, wlen), 1)
        wt = lax.broadcasted_iota(jnp.int32, (rows, wlen), 0) // N_HEADS
        wrel = (lax.broadcasted_iota(jnp.int32, (1, wlen), 1) - wlen).astype(F32)
        step(win_ref, wrel, wi > wt + (wlen - WINDOW))
        step(nwin_ref, klane.astype(F32), new_mask)
        o_win = acc_ref[...] * (1.0 / l_ref[...])
        o = ocmp_ref[...] + gate_ref[0, :, 1:2] * o_slc + gate_ref[0, :, 2:3] * o_win
        ur = lax.broadcasted_iota(jnp.int32, (KV_W, BRANCH_W), 0)
        uc = lax.broadcasted_iota(jnp.int32, (KV_W, BRANCH_W), 1)
        unplace = jnp.where((ur % HEAD_DIM == uc % HEAD_DIM) & (ur // HEAD_DIM == uc // (NSA_R * HEAD_DIM)), 1.0, 0.0)
        o_ref[0] = _extract_heads(jnp.dot(o, unplace, preferred_element_type=F32, precision=HIGHEST), nt)


def nsa_decode_call(pt, q, gate, kcmp, vcmp, cache_kv, new_kv, win, new_win):
    n, nt, _ = q.shape
    npages = pt.shape[1]
    rows = nt * N_HEADS
    ncmp = kcmp.shape[1]
    nsb = (npages * PAGE) // SLC_BLOCK + 1
    nsb_pad = -(-nsb // 128) * 128
    wlen = win.shape[1]
    grid_spec = pltpu.PrefetchScalarGridSpec(
        num_scalar_prefetch=1,
        grid=(n, npages),
        in_specs=[pl.BlockSpec((1, nt, BRANCH_W), lambda b, j, pt: (b, 0, 0)),
                  pl.BlockSpec((1, rows, 3), lambda b, j, pt: (b, 0, 0)),
                  pl.BlockSpec((1, ncmp, KV_W), lambda b, j, pt: (b, 0, 0)),
                  pl.BlockSpec((1, ncmp, KV_W), lambda b, j, pt: (b, 0, 0)),
                  pl.BlockSpec((1, PAGE, 2 * KV_W), lambda b, j, pt: (pt[b, j], 0, 1)),
                  pl.BlockSpec((1, PAGE, 2 * KV_W), lambda b, j, pt: (b, 0, 0)),
                  pl.BlockSpec((1, wlen, 2 * KV_W), lambda b, j, pt: (b, 0, 0)),
                  pl.BlockSpec((1, PAGE, 2 * KV_W), lambda b, j, pt: (b, 0, 0))],
        out_specs=pl.BlockSpec((1, nt, BRANCH_W), lambda b, j, pt: (b, 0, 0)),
        scratch_shapes=[pltpu.VMEM((rows, KV_W), BF16), pltpu.VMEM((rows, nsb_pad), BF16),
                        pltpu.VMEM((rows, KV_W), F32), pltpu.VMEM((rows, 1), F32), pltpu.VMEM((rows, 1), F32),
                        pltpu.VMEM((rows, KV_W), F32)],
    )
    return pl.pallas_call(
        functools.partial(_dnsa_kernel, nt=nt, nsb_pad=nsb_pad),
        grid_spec=grid_spec,
        out_shape=jax.ShapeDtypeStruct((n, nt, BRANCH_W), F32),
        compiler_params=_cparams("parallel", "arbitrary"),
        name="nsa_decode",
    )(pt, q, gate, kcmp, vcmp, cache_kv, new_kv, win, new_win)


def _conv_sample_kernel(z_ref, pre1_ref, pre2_ref, w_ref, o_ref, u_ref, *, nt):
    bw = BRANCH_W
    u = z_ref[:, 2 * bw:3 * bw] * z_ref[:, 0:bw]
    t = lax.broadcasted_iota(jnp.int32, u.shape, 0) % nt
    u1 = jnp.where(t < 1, pre1_ref[...], pltpu.roll(u, 1, 0))
    u2 = jnp.where(t < 2, pre2_ref[...], pltpu.roll(u, 2, 0))
    y = u2 * w_ref[0:1, :] + u1 * w_ref[1:2, :] + u * w_ref[2:3, :]
    o_ref[...] = z_ref[:, bw:2 * bw] * y
    u_ref[...] = u


def conv_sample_call(z_conv, state, conv_w, n, nt):
    zero = jnp.zeros((n, nt, BRANCH_W), F32)
    pre1 = zero.at[:, 0].set(state[:, 1]).reshape(n * nt, BRANCH_W)
    pre2 = zero.at[:, 0].set(state[:, 0]).at[:, 1].set(state[:, 1]).reshape(n * nt, BRANCH_W)
    return pl.pallas_call(
        functools.partial(_conv_sample_kernel, nt=nt),
        out_shape=[jax.ShapeDtypeStruct((n * nt, BRANCH_W), F32), jax.ShapeDtypeStruct((n * nt, BRANCH_W), F32)],
        name="conv_sample",
    )(z_conv, pre1, pre2, conv_w)


def _compress_weights(cmp_w1, cmp_w2, cmp_pe):
    w1 = cmp_w1.reshape(2, CMP_L, HEAD_DIM, CMP_HID)
    eye_w = jnp.eye(2, dtype=F32)
    eye_g = jnp.eye(NSA_G, dtype=F32)
    big = jnp.einsum('widh,sw,gk->isgdwkh', w1, eye_w, eye_g)
    big = big.reshape(CMP_L, 2 * KV_W, 4 * CMP_HID)
    wt = big[:CMP_S].reshape(CMP_ROW, 4 * CMP_HID).astype(BF16)
    wb = big[CMP_S:].reshape(CMP_ROW, 4 * CMP_HID).astype(BF16)
    w2k = jnp.einsum('hd,gk->ghkd', cmp_w2[0], eye_g).reshape(2 * CMP_HID, KV_W).astype(BF16)
    w2v = jnp.einsum('hd,gk->ghkd', cmp_w2[1], eye_g).reshape(2 * CMP_HID, KV_W).astype(BF16)
    w2vt = jnp.einsum('hd,gk->kdgh', cmp_w2[1], eye_g).reshape(KV_W, 2 * CMP_HID).astype(BF16)
    pe = jnp.broadcast_to(cmp_pe[:, :, None, :], (2, CMP_L, NSA_G, HEAD_DIM))
    pe = jnp.transpose(pe, (1, 0, 2, 3)).reshape(CMP_L, 2 * KV_W)
    pe_top = pe[:CMP_S].reshape(1, CMP_ROW)
    pe_bot = pe[CMP_S:].reshape(1, CMP_ROW)
    return pe_top, pe_bot, wt, wb, w2k, w2v, w2vt


def _layer_weights(l, w_in, w_branch, w_out, w_up, w_down):
    wl = w_in[l]
    small = jnp.concatenate(
        [wl[:, OFF_FOX_F:OFF_FOX_F + N_HEADS], wl[:, OFF_NSA_GATE:OFF_NSA_GATE + 3 * N_HEADS],
         jnp.zeros((D_MODEL, SMALL_W - 4 * N_HEADS), wl.dtype)], axis=1)
    return dict(
        conv=wl[:, OFF_CONV:OFF_FOX].astype(BF16),
        fox=wl[:, OFF_FOX:OFF_FOX_F].astype(BF16),
        moba=wl[:, OFF_MOBA:OFF_NSA].astype(BF16),
        nsa=wl[:, OFF_NSA:OFF_NSA_GATE].astype(BF16),
        small=small.astype(BF16),
        gate=wl[:, OFF_MERGE:].astype(BF16),
        branch=w_branch[l].astype(BF16),
        out=w_out[l].astype(BF16),
        up=w_up[l].astype(BF16),
        down=w_down[l].astype(BF16),
    )


def _finish_layer(x, branches, gate, w, g_mlp, g_next, next_dtype, tm):
    merged = merge_call(branches, gate, w["branch"], tm, 512)
    x1, hm = outproj_call(merged, w["out"], x, g_mlp, tm)
    return mlp_call(hm, w["up"], w["down"], x1, g_next, next_dtype, tm, 512)


def prompt_mixers(h, n, t, w, b_forget, conv_w, cmp_w, tm):
    bw = BRANCH_W
    (z_conv,) = proj_call(h, w["conv"], [("f32", 3 * bw)], [(a, a + 512, ((0, a),)) for a in (0, 512, 1024)],
                          tm, "proj_conv")
    qkv_defs = [("f32", bw), ("f32", 2 * bw), ("bf16", bw), ("bf16T", bw)]
    qkv_plan = [(0, bw, ((0, 0),)), (bw, 2 * bw, ((1, 0), (2, 0))), (2 * bw, 3 * bw, ((1, bw), (3, 0)))]
    fox_q, fox_kv, fox_kb, fox_vt = proj_call(h, w["fox"], qkv_defs, qkv_plan, tm, "proj_fox")
    moba_q, moba_kv, moba_kb, moba_vt, moba_km = proj_call(
        h, w["moba"], qkv_defs + [("blockmean", bw)],
        [qkv_plan[0], (bw, 2 * bw, ((1, 0), (2, 0), (4, 0))), qkv_plan[2]], tm, "proj_moba")
    kv = KV_W
    nsa_q, nsa_kv, nsa_win, nsa_ks, nsa_vst, nsa_kw, nsa_vwt = proj_call(
        h, w["nsa"],
        [("f32", bw), ("f32", 4 * kv), ("f32", 2 * kv), ("bf16", kv), ("bf16T", kv), ("bf16", kv), ("bf16T", kv)],
        [(0, bw, ((0, 0),)), (bw, bw + 2 * kv, ((1, 0),)),
         (bw + 2 * kv, bw + 3 * kv, ((1, 2 * kv), (3, 0))), (bw + 3 * kv, bw + 4 * kv, ((1, 3 * kv), (4, 0))),
         (bw + 4 * kv, bw + 5 * kv, ((2, 0), (5, 0))), (bw + 5 * kv, bw + 6 * kv, ((2, kv), (6, 0)))],
        tm, "proj_nsa")
    (z_small,) = proj_call(h, w["small"], [("f32", SMALL_W)], [(0, SMALL_W, ((0, 0),))], tm, "proj_small")

    out_a, new_conv = conv_prompt_call(z_conv, conv_w, n, t, tm)
    lg, fk = small_call(z_small, b_forget, n, t, tm)
    logf = lg[:, :N_HEADS].reshape(n, t, N_HEADS)
    gate_t = lg[:, N_HEADS:4 * N_HEADS].T

    out_b = fox_prompt_call(fox_q, fox_kb, fox_vt, fk, n, t)
    out_c = moba_prompt_call(moba_q, moba_kb, moba_vt, moba_km, n, t)

    npages = t // 128
    pages = nsa_kv.reshape(n * npages, PAGE_ROWS, CMP_S * 4 * kv)
    page_idx = jnp.arange(n * npages, dtype=jnp.int32).reshape(n, npages)
    kcmp, _, vcmpt = compress_call(pages, page_idx, *cmp_w, n, npages)
    out_d = nsa_prompt_call(nsa_q, kcmp, vcmpt, nsa_ks, nsa_vst, nsa_kw, nsa_vwt, gate_t, n, t)

    wb = min(WINDOW, t)
    new_state = (new_conv,
                 fox_kv.reshape(n, t, 2, N_HEADS, HEAD_DIM),
                 logf,
                 moba_kv.reshape(n, t, 2, N_HEADS, HEAD_DIM),
                 nsa_kv.reshape(n, t, 2, 2, NSA_G, HEAD_DIM),
                 nsa_win.reshape(n, t, 2, NSA_G, HEAD_DIM)[:, t - wb:])
    return [out_a, out_b, out_c, out_d], new_state


def sample_mixers(h, n, t, q0, past, w, b_forget, conv_w, cmp_w1, cmp_w2, cmp_pe, tm):
    dt = F32
    bw = BRANCH_W
    kv = KV_W
    (z_conv,) = proj_call(h, w["conv"], [("f32", 3 * bw)], [(0, 3 * bw, ((0, 0),))], tm, "proj_conv_s")
    fox_q, fox_kv = proj_call(h, w["fox"], [("f32", bw), ("f32", 2 * bw)],
                              [(0, bw, ((0, 0),)), (bw, 3 * bw, ((1, 0),))], tm, "proj_fox_s")
    moba_q, moba_kv = proj_call(h, w["moba"], [("f32", bw), ("f32", 2 * bw)],
                                [(0, bw, ((0, 0),)), (bw, 3 * bw, ((1, 0),))], tm, "proj_moba_s")
    nsa_q, nsa_kv, nsa_win = proj_call(
        h, w["nsa"], [("f32", bw), ("f32", 4 * kv), ("f32", 2 * kv)],
        [(0, bw, ((0, 0),)), (bw, bw + 4 * kv, ((1, 0),)), (bw + 4 * kv, bw + 6 * kv, ((2, 0),))], tm, "proj_nsa_s")
    (z_small,) = proj_call(h, w["small"], [("f32", SMALL_W)], [(0, SMALL_W, ((0, 0),))], tm, "proj_small_s")

    def heads(a, nh):
        return a.reshape(n, t, nh, HEAD_DIM)

    def cat(old, new):
        return jnp.concatenate([old.astype(new.dtype), new], axis=1)

    zc = z_conv.reshape(n, t, 3 * bw)
    conv_x, conv_b, conv_c = zc[..., :bw], zc[..., bw:2 * bw], zc[..., 2 * bw:]
    u = conv_c * conv_x
    ext = jnp.concatenate([past['conv'].astype(dt), u], axis=1)
    y_conv = ext[:, 0:t] * conv_w[0]
    for j in range(1, CONV_W):
        y_conv = y_conv + ext[:, j:j + t] * conv_w[j]
    out_a = conv_b * y_conv
    new_conv = ext[:, -(CONV_W - 1):]

    fkv = fox_kv.reshape(n, t, 2, N_HEADS, HEAD_DIM)
    fq, fk, fv = heads(fox_q, N_HEADS), fkv[:, :, 0], fkv[:, :, 1]
    fox_f = z_small.reshape(n, t, SMALL_W)[..., :N_HEADS]
    logf = jax.nn.log_sigmoid(fox_f + b_forget)
    lf_all = jnp.concatenate([past['fox_logf'].astype(F32), logf], axis=1)
    out_b = fox_attention(fq, cat(past['fox_k'], fk), cat(past['fox_v'], fv), lf_all, q0)

    mkv = moba_kv.reshape(n, t, 2, N_HEADS, HEAD_DIM)
    mq, mk, mv = heads(moba_q, N_HEADS), mkv[:, :, 0], mkv[:, :, 1]
    out_c = moba_attention(mq, cat(past['moba_k'], mk), cat(past['moba_v'], mv), q0, alibi_slopes(N_HEADS))

    nq = heads(nsa_q, N_HEADS)
    nkv = nsa_kv.reshape(n, t, 2, 2, NSA_G, HEAD_DIM)
    kc, vc, ks, vs = nkv[:, :, 0, 0], nkv[:, :, 0, 1], nkv[:, :, 1, 0], nkv[:, :, 1, 1]
    nwin = nsa_win.reshape(n, t, 2, NSA_G, HEAD_DIM)
    kw, vw = nwin[:, :, 0], nwin[:, :, 1]
    ngate = jax.nn.sigmoid(z_small.reshape(n, t, SMALL_W)[..., N_HEADS:4 * N_HEADS]).reshape(n, t, NSA_G, NSA_R, 3)
    kw_all = cat(past['win_k'], kw)
    vw_all = cat(past['win_v'], vw)
    w_ofs = q0 - past['win_k'].shape[1]
    out_d = nsa_attention(nq, cat(past['nsa_kc'], kc), cat(past['nsa_vc'], vc),
                          cat(past['nsa_ks'], ks), cat(past['nsa_vs'], vs),
                          kw_all, vw_all, ngate, q0, w_ofs, alibi_slopes(N_HEADS), cmp_w1, cmp_w2, cmp_pe)
    branches = [a.reshape(n * t, bw) for a in (out_a, out_b, out_c, out_d)]
    wb = past['win_k'].shape[1]
    new_state = (new_conv, fkv, logf.astype(dt), mkv, nkv,
                 jnp.stack([kw_all, vw_all], axis=2)[:, -wb:])
    return branches, new_state


def decode_mixers(h, n, nt, pt, caches, state_conv, state_win, w, b_forget, conv_w, cmp_w, tm):
    bw = BRANCH_W
    kv = KV_W
    cache_fox, cache_lf, cache_moba, cache_nsa = caches
    (z_conv,) = proj_call(h, w["conv"], [("f32", 3 * bw)], [(0, 3 * bw, ((0, 0),))], tm, "proj_conv_s")
    fox_q, fox_kv = proj_call(h, w["fox"], [("f32", bw), ("f32", 2 * bw)],
                              [(0, bw, ((0, 0),)), (bw, 3 * bw, ((1, 0),))], tm, "proj_fox_s")
    moba_q, moba_kv = proj_call(h, w["moba"], [("f32", bw), ("f32", 2 * bw)],
                                [(0, bw, ((0, 0),)), (bw, 3 * bw, ((1, 0),))], tm, "proj_moba_s")
    nsa_q, nsa_kv, nsa_win = proj_call(
        h, w["nsa"], [("f32", bw), ("f32", 4 * kv), ("f32", 2 * kv)],
        [(0, bw, ((0, 0),)), (bw, bw + 4 * kv, ((1, 0),)), (bw + 4 * kv, bw + 6 * kv, ((2, 0),))], tm, "proj_nsa_s")
    (z_small,) = proj_call(h, w["small"], [("f32", SMALL_W)], [(0, SMALL_W, ((0, 0),))], tm, "proj_small_s")

    def new_page(a):
        return jnp.pad(a.reshape(n, nt, a.shape[-1]), ((0, 0), (0, PAGE - nt), (0, 0)))

    out_a, u = conv_sample_call(z_conv, state_conv, conv_w, n, nt)
    new_conv = u.reshape(n, nt, bw)[:, nt - (CONV_W - 1):]

    lg, _ = small_call(z_small, b_forget, 1, n * nt, n * nt)
    logf = lg[:, :N_HEADS]
    gate = lg[:, N_HEADS:4 * N_HEADS].reshape(n, nt * N_HEADS, 3)

    out_b = fox_decode_call(pt, fox_q.reshape(n, nt, bw), cache_fox, cache_lf, new_page(fox_kv), new_page(logf))
    mq = moba_q.reshape(n, nt, bw)
    sel = moba_select_call(pt, mq, cache_moba)
    out_c = moba_decode_call(pt, mq, sel, cache_moba, new_page(moba_kv))

    npages = pt.shape[1]
    pages = cache_nsa.reshape(cache_nsa.shape[0], PAGE_ROWS, CMP_S * 4 * kv)
    kcmp, vcmp, _ = compress_call(pages, pt, *cmp_w, n, npages)
    win = state_win.reshape(n, state_win.shape[1], 2 * kv)
    out_d = nsa_decode_call(pt, nsa_q.reshape(n, nt, bw), gate, kcmp, vcmp, cache_nsa,
                            new_page(nsa_kv[:, 2 * kv:]), win, new_page(nsa_win))

    wb = win.shape[1]
    win_all = jnp.concatenate([win, nsa_win.reshape(n, nt, 2 * kv)], axis=1)[:, nt:]
    new_state = (new_conv,
                 fox_kv.reshape(n, nt, 2, N_HEADS, HEAD_DIM),
                 logf.reshape(n, nt, N_HEADS),
                 moba_kv.reshape(n, nt, 2, N_HEADS, HEAD_DIM),
                 nsa_kv.reshape(n, nt, 2, 2, NSA_G, HEAD_DIM),
                 win_all.reshape(n, wb, 2, NSA_G, HEAD_DIM))
    branches = [out_a] + [o.reshape(n * nt, bw) for o in (out_b, out_c, out_d)]
    return branches, new_state


def kernel(x_prompt, x_sample, state_conv, cache_fox_kv, cache_fox_logf, cache_moba_kv, cache_nsa_kv,
           state_nsa_win, page_table, g_mix, w_in, b_forget, conv_w, cmp_w1, cmp_w2, cmp_pe,
           w_branch, w_out, g_mlp, w_up, w_down, g_final):
    depth = w_in.shape[0]
    nb, seq, _ = x_prompt.shape
    db, dseq, _ = x_sample.shape
    tm_p, tm_s = 512, db * dseq
    pool, page = cache_fox_kv.shape[1], cache_fox_kv.shape[2]
    caches = (cache_fox_kv.reshape(depth * pool, page, 2 * BRANCH_W),
              cache_fox_logf.reshape(depth * pool, page, N_HEADS),
              cache_moba_kv.reshape(depth * pool, page, 2 * BRANCH_W),
              cache_nsa_kv.reshape(depth * pool, page, 4 * KV_W))

    xp = x_prompt.reshape(nb * seq, D_MODEL)
    xs = x_sample.reshape(db * dseq, D_MODEL)
    hp = rms_norm_call(xp, g_mix[0], BF16, tm_p)
    hs = rms_norm_call(xs, g_mix[0], BF16, tm_s)
    new_p, new_s = [], []
    for l in range(depth):
        w = _layer_weights(l, w_in, w_branch, w_out, w_up, w_down)
        cmp_w = _compress_weights(cmp_w1[l], cmp_w2[l], cmp_pe[l])
        last = l == depth - 1
        g_next = g_final if last else g_mix[l + 1]
        next_dtype = F32 if last else BF16

        branches, st_p = prompt_mixers(hp, nb, seq, w, b_forget[l], conv_w[l], cmp_w, tm_p)
        gate_p = gate_call(hp, w["gate"], tm_p, 1024)
        xp, hp = _finish_layer(xp, branches, gate_p, w, g_mlp[l], g_next, next_dtype, tm_p)
        new_p.append(st_p)

        branches, st_s = decode_mixers(hs, db, dseq, page_table + l * pool, caches, state_conv[l],
                                       state_nsa_win[l], w, b_forget[l], conv_w[l], cmp_w, tm_s)
        gate_s = gate_call(hs, w["gate"], tm_s, 1024)
        xs, hs = _finish_layer(xs, branches, gate_s, w, g_mlp[l], g_next, next_dtype, tm_s)
        new_s.append(st_s)
    y_prompt = hp.reshape(nb, seq, D_MODEL)
    y_sample = hs.reshape(db, dseq, D_MODEL)
    conv_p, fox_kv_p, fox_logf_p, moba_kv_p, nsa_kv_p, win_p = [jnp.stack(a) for a in zip(*new_p)]
    conv_s, fox_kv_s, fox_logf_s, moba_kv_s, nsa_kv_s, win_s = [jnp.stack(a) for a in zip(*new_s)]
    return (y_prompt, y_sample, conv_p, conv_s, fox_kv_p, fox_kv_s, fox_logf_p, fox_logf_s,
            moba_kv_p, moba_kv_s, nsa_kv_p, nsa_kv_s, win_p, win_s)
```

```python
import functools

import jax
import jax.numpy as jnp
from jax import lax
from jax.experimental import pallas as pl
from jax.experimental.pallas import tpu as pltpu

F32 = jnp.float32
BF16 = jnp.bfloat16
HIGHEST = lax.Precision.HIGHEST

D_MODEL = 2048
HEAD_DIM = 64
N_BRANCH = 4
BRANCH_W = D_MODEL // N_BRANCH
N_HEADS = BRANCH_W // HEAD_DIM
CONV_W = 3
NSA_G = 2
NSA_R = N_HEADS // NSA_G
MOBA_BLOCK = 256
MOBA_TOPK = 3
CMP_L = 32
CMP_S = 16
CMP_HID = 4 * HEAD_DIM
SLC_BLOCK = 64
SLC_N = 16
WINDOW = 512
D_FF = 4 * D_MODEL
Q_BLOCK = 128
SPARSE_Q_BLOCK = 32
RMS_EPS = 1e-6
NEG = -1e30
MASKED = 2.0 * NEG
FORCE_SCORE = 1e4
KV_W = NSA_G * HEAD_DIM
SCALE = HEAD_DIM ** -0.5
PAIR_W = 2 * HEAD_DIM

OFF_CONV = 0
OFF_FOX = 3 * BRANCH_W
OFF_FOX_F = OFF_FOX + 3 * BRANCH_W
OFF_MOBA = OFF_FOX_F + N_HEADS
OFF_NSA = OFF_MOBA + 3 * BRANCH_W
OFF_NSA_GATE = OFF_NSA + BRANCH_W + 6 * KV_W
OFF_MERGE = OFF_NSA_GATE + 3 * N_HEADS
IN_W = OFF_MERGE + N_BRANCH * D_MODEL
SMALL_W = 128

TQ = 256
TK = 256
CMP_ROW = CMP_S * 2 * KV_W
PAGE_ROWS = 8

VMEM_LIMIT = 56 * 1024 * 1024


def _cparams(*sem):
    return pltpu.CompilerParams(dimension_semantics=sem, vmem_limit_bytes=VMEM_LIMIT)


def _rms(x, g):
    return x * lax.rsqrt(jnp.mean(x * x, axis=-1, keepdims=True) + RMS_EPS) * g


def _dot_nt(a, b, precision=None):
    return lax.dot_general(a, b, (((1,), (1,)), ((), ())), preferred_element_type=F32, precision=precision)


def _norm_kernel(x_ref, g_ref, o_ref):
    o_ref[...] = _rms(x_ref[...], g_ref[...]).astype(o_ref.dtype)


def rms_norm_call(x, g, out_dtype, tm):
    t, d = x.shape
    return pl.pallas_call(
        _norm_kernel,
        grid=(t // tm,),
        in_specs=[pl.BlockSpec((tm, d), lambda i: (i, 0)),
                  pl.BlockSpec((1, d), lambda i: (0, 0))],
        out_specs=pl.BlockSpec((tm, d), lambda i: (i, 0)),
        out_shape=jax.ShapeDtypeStruct((t, d), out_dtype),
        compiler_params=_cparams("parallel"),
        name="rms_norm",
    )(x, g.reshape(1, d))


def _proj_kernel(h_ref, w_ref, *out_refs, kinds, plan):
    h = h_ref[...]
    for c0, c1, dests in plan:
        z = jnp.dot(h, w_ref[:, c0:c1], preferred_element_type=F32)
        for idx, off in dests:
            o_ref, kind = out_refs[idx], kinds[idx]
            if kind == "bf16T":
                for r in range(z.shape[0] // TK):
                    o_ref[r, off:off + c1 - c0, :] = z[r * TK:(r + 1) * TK].T.astype(BF16)
            elif kind == "f32T":
                for r in range(z.shape[0] // 128):
                    o_ref[r, off:off + c1 - c0, :] = z[r * 128:(r + 1) * 128].T
            elif kind == "blockmean":
                for r in range(z.shape[0] // MOBA_BLOCK):
                    o_ref[r, :, off:off + c1 - c0] = jnp.mean(
                        z[r * MOBA_BLOCK:(r + 1) * MOBA_BLOCK], axis=0, keepdims=True)
            else:
                o_ref[:, off:off + c1 - c0] = z.astype(o_ref.dtype)


def proj_call(h, w, out_defs, plan, tm, name):
    t, d = h.shape
    n = w.shape[1]
    out_specs, out_shapes = [], []
    for kind, width in out_defs:
        if kind == "bf16T":
            out_specs.append(pl.BlockSpec((tm // TK, width, TK), lambda i: (i, 0, 0)))
            out_shapes.append(jax.ShapeDtypeStruct((t // TK, width, TK), BF16))
        elif kind == "f32T":
            out_specs.append(pl.BlockSpec((tm // 128, width, 128), lambda i: (i, 0, 0)))
            out_shapes.append(jax.ShapeDtypeStruct((t // 128, width, 128), F32))
        elif kind == "blockmean":
            out_specs.append(pl.BlockSpec((tm // MOBA_BLOCK, 1, width), lambda i: (i, 0, 0)))
            out_shapes.append(jax.ShapeDtypeStruct((t // MOBA_BLOCK, 1, width), F32))
        else:
            out_specs.append(pl.BlockSpec((tm, width), lambda i: (i, 0)))
            out_shapes.append(jax.ShapeDtypeStruct((t, width), BF16 if kind == "bf16" else F32))
    return pl.pallas_call(
        functools.partial(_proj_kernel, kinds=tuple(k for k, _ in out_defs), plan=tuple(plan)),
        grid=(t // tm,),
        in_specs=[pl.BlockSpec((tm, d), lambda i: (i, 0)),
                  pl.BlockSpec((d, n), lambda i: (0, 0))],
        out_specs=out_specs,
        out_shape=out_shapes,
        compiler_params=_cparams("parallel"),
        name=name,
    )(h, w)


def _gate_kernel(h_ref, w_ref, o_ref):
    z = jnp.dot(h_ref[...], w_ref[...], preferred_element_type=F32)
    o_ref[...] = jax.nn.sigmoid(z)


def gate_call(h, w, tm, tn):
    t, d = h.shape
    n = w.shape[1]
    return pl.pallas_call(
        _gate_kernel,
        grid=(t // tm, n // tn),
        in_specs=[pl.BlockSpec((tm, d), lambda i, j: (i, 0)),
                  pl.BlockSpec((d, tn), lambda i, j: (0, j))],
        out_specs=pl.BlockSpec((tm, tn), lambda i, j: (i, j)),
        out_shape=jax.ShapeDtypeStruct((t, n), F32),
        compiler_params=_cparams("parallel", "arbitrary"),
        name="merge_gate_proj",
    )(h, w)


def _merge_kernel(oa_ref, ob_ref, oc_ref, od_ref, g0_ref, g1_ref, g2_ref, g3_ref, wb_ref, o_ref):
    acc = None
    for b, (o, g) in enumerate(zip((oa_ref, ob_ref, oc_ref, od_ref), (g0_ref, g1_ref, g2_ref, g3_ref))):
        br = jnp.dot(o[...].astype(BF16), wb_ref[b], preferred_element_type=F32)
        term = g[...] * br
        acc = term if acc is None else acc + term
    o_ref[...] = acc.astype(o_ref.dtype)


def merge_call(branches, gate, w_branch, tm, tn):
    t = branches[0].shape[0]
    nj = D_MODEL // tn
    gate_specs = [pl.BlockSpec((tm, tn), functools.partial(lambda i, j, b: (i, b * nj + j), b=b))
                  for b in range(N_BRANCH)]
    return pl.pallas_call(
        _merge_kernel,
        grid=(t // tm, nj),
        in_specs=[pl.BlockSpec((tm, BRANCH_W), lambda i, j: (i, 0))] * N_BRANCH + gate_specs
        + [pl.BlockSpec((N_BRANCH, BRANCH_W, tn), lambda i, j: (0, 0, j))],
        out_specs=pl.BlockSpec((tm, tn), lambda i, j: (i, j)),
        out_shape=jax.ShapeDtypeStruct((t, D_MODEL), BF16),
        compiler_params=_cparams("parallel", "arbitrary"),
        name="branch_merge",
    )(*branches, gate, gate, gate, gate, w_branch)


def _outproj_kernel(m_ref, w_ref, x_ref, g_ref, xo_ref, hn_ref):
    xn = x_ref[...] + jnp.dot(m_ref[...], w_ref[...], preferred_element_type=F32)
    xo_ref[...] = xn
    hn_ref[...] = _rms(xn, g_ref[...]).astype(hn_ref.dtype)


def outproj_call(merged, w_out, x, g_next, tm):
    t = x.shape[0]
    return pl.pallas_call(
        _outproj_kernel,
        grid=(t // tm,),
        in_specs=[pl.BlockSpec((tm, D_MODEL), lambda i: (i, 0)),
                  pl.BlockSpec((D_MODEL, D_MODEL), lambda i: (0, 0)),
                  pl.BlockSpec((tm, D_MODEL), lambda i: (i, 0)),
                  pl.BlockSpec((1, D_MODEL), lambda i: (0, 0))],
        out_specs=[pl.BlockSpec((tm, D_MODEL), lambda i: (i, 0)),
                   pl.BlockSpec((tm, D_MODEL), lambda i: (i, 0))],
        out_shape=[jax.ShapeDtypeStruct((t, D_MODEL), F32),
                   jax.ShapeDtypeStruct((t, D_MODEL), BF16)],
        compiler_params=_cparams("parallel"),
        name="out_proj",
    )(merged, w_out, x, g_next.reshape(1, D_MODEL))


def _mlp_kernel(h_ref, wu_ref, wd_ref, x_ref, g_ref, xo_ref, hn_ref, acc_ref):
    j = pl.program_id(1)

    @pl.when(j == 0)
    def _():
        acc_ref[...] = jnp.zeros_like(acc_ref)

    a = jnp.dot(h_ref[...], wu_ref[...], preferred_element_type=F32)
    a = jnp.square(jnp.maximum(a, 0.0)).astype(BF16)
    acc_ref[...] += jnp.dot(a, wd_ref[...], preferred_element_type=F32)

    @pl.when(j == pl.num_programs(1) - 1)
    def _():
        xn = x_ref[...] + acc_ref[...]
        xo_ref[...] = xn
        hn_ref[...] = _rms(xn, g_ref[...]).astype(hn_ref.dtype)


def mlp_call(h, w_up, w_down, x, g_next, next_dtype, tm, tf):
    t = x.shape[0]
    return pl.pallas_call(
        _mlp_kernel,
        grid=(t // tm, D_FF // tf),
        in_specs=[pl.BlockSpec((tm, D_MODEL), lambda i, j: (i, 0)),
                  pl.BlockSpec((D_MODEL, tf), lambda i, j: (0, j)),
                  pl.BlockSpec((tf, D_MODEL), lambda i, j: (j, 0)),
                  pl.BlockSpec((tm, D_MODEL), lambda i, j: (i, 0)),
                  pl.BlockSpec((1, D_MODEL), lambda i, j: (0, 0))],
        out_specs=[pl.BlockSpec((tm, D_MODEL), lambda i, j: (i, 0)),
                   pl.BlockSpec((tm, D_MODEL), lambda i, j: (i, 0))],
        out_shape=[jax.ShapeDtypeStruct((t, D_MODEL), F32),
                   jax.ShapeDtypeStruct((t, D_MODEL), next_dtype)],
        scratch_shapes=[pltpu.VMEM((tm, D_MODEL), F32)],
        compiler_params=_cparams("parallel", "arbitrary"),
        name="mlp",
    )(h, w_up, w_down, x, g_next.reshape(1, D_MODEL))


def _small_kernel(z_ref, b_ref, a_ref, f_ref, carry_ref):
    @pl.when(pl.program_id(1) == 0)
    def _():
        carry_ref[...] = jnp.zeros_like(carry_ref)

    z = z_ref[...]
    tm = z.shape[0]
    lane = lax.broadcasted_iota(jnp.int32, z.shape, 1)
    pre = z + b_ref[...]
    lf = jnp.minimum(pre, 0.0) - jnp.log1p(jnp.exp(-jnp.abs(pre)))
    lf = jnp.where(lane < N_HEADS, lf, 0.0)
    a_ref[...] = jnp.where(lane < N_HEADS, lf, jnp.where(lane < 4 * N_HEADS, jax.nn.sigmoid(z), 0.0))
    row = lax.broadcasted_iota(jnp.int32, (tm, tm), 0)
    col = lax.broadcasted_iota(jnp.int32, (tm, tm), 1)
    tril = jnp.where(col <= row, 1.0, 0.0)
    f = jnp.dot(tril, lf, preferred_element_type=F32, precision=HIGHEST) + carry_ref[0:1, :]
    f_ref[...] = f
    carry_ref[0:1, :] = f[tm - 1:tm, :]


def small_call(z_small, b_forget, n, t, tm):
    bias = jnp.zeros((1, SMALL_W), F32).at[0, :N_HEADS].set(b_forget)
    nt = t // tm
    return pl.pallas_call(
        _small_kernel,
        grid=(n, nt),
        in_specs=[pl.BlockSpec((tm, SMALL_W), lambda b, j: (b * nt + j, 0)),
                  pl.BlockSpec((1, SMALL_W), lambda b, j: (0, 0))],
        out_specs=[pl.BlockSpec((tm, SMALL_W), lambda b, j: (b * nt + j, 0)),
                   pl.BlockSpec((tm, SMALL_W), lambda b, j: (b * nt + j, 0))],
        out_shape=[jax.ShapeDtypeStruct((n * t, SMALL_W), F32),
                   jax.ShapeDtypeStruct((n * t, SMALL_W), F32)],
        scratch_shapes=[pltpu.VMEM((8, SMALL_W), F32)],
        compiler_params=_cparams("parallel", "arbitrary"),
        name="forget_and_gates",
    )(z_small, bias)


def _conv_prompt_kernel(z_ref, w_ref, o_ref, st_ref, prev_ref):
    @pl.when(pl.program_id(1) == 0)
    def _():
        prev_ref[...] = jnp.zeros_like(prev_ref)

    bw = BRANCH_W
    u = z_ref[:, 2 * bw:3 * bw] * z_ref[:, 0:bw]
    tm = u.shape[0]
    row = lax.broadcasted_iota(jnp.int32, u.shape, 0)
    u1 = jnp.where(row == 0, prev_ref[7:8, :], pltpu.roll(u, 1, 0))
    u2 = jnp.where(row == 0, prev_ref[6:7, :], jnp.where(row == 1, prev_ref[7:8, :], pltpu.roll(u, 2, 0)))
    y = u2 * w_ref[0:1, :] + u1 * w_ref[1:2, :] + u * w_ref[2:3, :]
    o_ref[...] = z_ref[:, bw:2 * bw] * y
    prev_ref[...] = u[tm - 8:tm]
    st_ref[0] = u[tm - 2:tm]


def conv_prompt_call(z_conv, conv_w, n, t, tm):
    nt = t // tm
    return pl.pallas_call(
        _conv_prompt_kernel,
        grid=(n, nt),
        in_specs=[pl.BlockSpec((tm, 3 * BRANCH_W), lambda b, j: (b * nt + j, 0)),
                  pl.BlockSpec((CONV_W, BRANCH_W), lambda b, j: (0, 0))],
        out_specs=[pl.BlockSpec((tm, BRANCH_W), lambda b, j: (b * nt + j, 0)),
                   pl.BlockSpec((1, CONV_W - 1, BRANCH_W), lambda b, j: (b, 0, 0))],
        out_shape=[jax.ShapeDtypeStruct((n * t, BRANCH_W), F32),
                   jax.ShapeDtypeStruct((n, CONV_W - 1, BRANCH_W), F32)],
        scratch_shapes=[pltpu.VMEM((8, BRANCH_W), F32)],
        compiler_params=_cparams("parallel", "arbitrary"),
        name="conv_prompt",
    )(z_conv, conv_w)


def _online_step(s, vt, carry):
    m, l, acc = carry
    m_new = jnp.maximum(m, jnp.max(s, axis=0, keepdims=True))
    alpha = jnp.exp(m - m_new)
    p = jnp.exp(s - m_new)
    l = alpha * l + jnp.sum(p, axis=0, keepdims=True)
    acc = alpha * acc + jnp.dot(vt, p.astype(BF16), preferred_element_type=F32)
    return m_new, l, acc


def _softmax_init():
    return (jnp.full((1, TQ), NEG, F32), jnp.zeros((1, TQ), F32), jnp.zeros((HEAD_DIM, TQ), F32))


def _query_pair(q_ref, h, half):
    hp, e = divmod(h, 2)
    qp = q_ref[:, hp * PAIR_W:(hp + 1) * PAIR_W]
    if e != half:
        qp = pltpu.roll(qp, HEAD_DIM, 1)
    lane = lax.broadcasted_iota(jnp.int32, qp.shape, 1)
    return jnp.where((lane // HEAD_DIM) == half, qp, 0.0)


def _tile_masks():
    sub = lax.broadcasted_iota(jnp.int32, (TK, TQ), 0)
    lane = lax.broadcasted_iota(jnp.int32, (TK, TQ), 1)
    return sub <= lane, sub > lane


def _key_col():
    return lax.broadcasted_iota(jnp.int32, (TK, 1), 0).astype(F32)


def _fox_kernel(q_ref, k_ref, vt_ref, fk_ref, o_ref, ot_ref):
    i = pl.program_id(1)
    causal, _ = _tile_masks()
    for h in range(N_HEADS):
        hp, e = divmod(h, 2)
        qpad = (_query_pair(q_ref, h, e) * SCALE).astype(BF16)

        def tile(j, carry, diag):
            r0 = pl.multiple_of(j * TK, TK)
            kp = k_ref[pl.ds(r0, TK), hp * PAIR_W:(hp + 1) * PAIR_W]
            s = _dot_nt(kp, qpad) - fk_ref[pl.ds(r0, TK), h:h + 1]
            if diag:
                s = jnp.where(causal, s, MASKED)
            return _online_step(s, vt_ref[j, h * HEAD_DIM:(h + 1) * HEAD_DIM, :], carry)

        carry = lax.fori_loop(0, i, lambda j, c: tile(j, c, False), _softmax_init())
        _, l, acc = tile(i, carry, True)
        ot_ref[h * HEAD_DIM:(h + 1) * HEAD_DIM, :] = acc * (1.0 / l)
    o_ref[...] = ot_ref[...].T


def fox_prompt_call(q, kb, vt, fk, n, t):
    nq = t // TQ
    return pl.pallas_call(
        _fox_kernel,
        grid=(n, nq),
        in_specs=[pl.BlockSpec((TQ, BRANCH_W), lambda b, i: (b * nq + i, 0)),
                  pl.BlockSpec((t, BRANCH_W), lambda b, i: (b, 0)),
                  pl.BlockSpec((t // TK, BRANCH_W, TK), lambda b, i: (b, 0, 0)),
                  pl.BlockSpec((t, SMALL_W), lambda b, i: (b, 0))],
        out_specs=pl.BlockSpec((TQ, BRANCH_W), lambda b, i: (b * nq + i, 0)),
        out_shape=jax.ShapeDtypeStruct((n * t, BRANCH_W), F32),
        scratch_shapes=[pltpu.VMEM((BRANCH_W, TQ), F32)],
        compiler_params=_cparams("parallel", "arbitrary"),
        name="fox_prompt",
    )(q, kb, vt, fk)


def _rank_before(score, bidx, nblk):
    cnt = jnp.zeros(score.shape, F32)
    for b2 in range(nblk):
        row = score[b2:b2 + 1, :]
        beats = (row > score) | ((row == score) & (b2 < bidx))
        cnt = cnt + jnp.where(beats, 1.0, 0.0)
    return cnt


def _moba_kernel(q_ref, k_ref, vt_ref, km_ref, o_ref, ot_ref, sel_ref, *, nblk):
    i = pl.program_id(1)
    causal, _ = _tile_masks()
    kcol = _key_col()
    bidx = lax.broadcasted_iota(jnp.int32, (nblk, TQ), 0)
    for h in range(N_HEADS):
        hp, e = divmod(h, 2)
        q32 = _query_pair(q_ref, h, e)
        qpad = (q32 * SCALE).astype(BF16)
        gs = _dot_nt(km_ref[0, :, hp * PAIR_W:(hp + 1) * PAIR_W], q32, precision=HIGHEST)
        gs = jnp.where(bidx < i, gs, NEG)
        sel = (_rank_before(gs, bidx, nblk) < MOBA_TOPK) & (bidx < i)
        sel_ref[...] = jnp.where(sel, 1.0, 0.0)
        slope = 2.0 ** (-8.0 * (h + 1) / N_HEADS)

        def tile(j, carry, diag):
            r0 = pl.multiple_of(j * TK, TK)
            kp = k_ref[pl.ds(r0, TK), hp * PAIR_W:(hp + 1) * PAIR_W]
            bias = slope * kcol - slope * ((i - j) * TK).astype(F32)
            s = _dot_nt(kp, qpad) + bias
            if diag:
                s = jnp.where(causal, s, MASKED)
            else:
                s = jnp.where(sel_ref[pl.ds(j, 1), :] > 0.5, s, MASKED)
            return _online_step(s, vt_ref[j, h * HEAD_DIM:(h + 1) * HEAD_DIM, :], carry)

        carry = lax.fori_loop(0, i, lambda j, c: tile(j, c, False), _softmax_init())
        _, l, acc = tile(i, carry, True)
        ot_ref[h * HEAD_DIM:(h + 1) * HEAD_DIM, :] = acc * (1.0 / l)
    o_ref[...] = ot_ref[...].T


def moba_prompt_call(q, kb, vt, kmean, n, t):
    nq = t // TQ
    nblk = t // MOBA_BLOCK
    return pl.pallas_call(
        functools.partial(_moba_kernel, nblk=nblk),
        grid=(n, nq),
        in_specs=[pl.BlockSpec((TQ, BRANCH_W), lambda b, i: (b * nq + i, 0)),
                  pl.BlockSpec((t, BRANCH_W), lambda b, i: (b, 0)),
                  pl.BlockSpec((t // TK, BRANCH_W, TK), lambda b, i: (b, 0, 0)),
                  pl.BlockSpec((1, nblk, BRANCH_W), lambda b, i: (b, 0, 0))],
        out_specs=pl.BlockSpec((TQ, BRANCH_W), lambda b, i: (b * nq + i, 0)),
        out_shape=jax.ShapeDtypeStruct((n * t, BRANCH_W), F32),
        scratch_shapes=[pltpu.VMEM((BRANCH_W, TQ), F32), pltpu.VMEM((nblk, TQ), F32)],
        compiler_params=_cparams("parallel", "arbitrary"),
        name="moba_prompt",
    )(q, kb, vt, kmean.reshape(n, nblk, BRANCH_W))


def _gelu_tanh(x):
    return 0.5 * x * (1.0 + jnp.tanh(0.7978845608028654 * (x + 0.044715 * x * x * x)))


def _compress_kernel(pt_ref, a_ref, pet_ref, peb_ref, wt_ref, wb_ref, w2k_ref, w2v_ref, w2vt_ref,
                     kc_ref, vc_ref, vct_ref, rows_ref, *, rows):
    del pt_ref
    j = pl.program_id(1)
    r0 = pl.multiple_of(j * PAGE_ROWS, PAGE_ROWS)
    for i in range(CMP_S):
        rows_ref[pl.ds(r0, PAGE_ROWS), i * 2 * KV_W:(i + 1) * 2 * KV_W] = a_ref[0, :, i * 4 * KV_W:i * 4 * KV_W + 2 * KV_W]

    @pl.when(j == pl.num_programs(1) - 1)
    def _():
        chunk = min(256, rows)
        prev_top = jnp.zeros((1, 4 * CMP_HID), F32)
        for c in range(rows // chunk):
            a = rows_ref[c * chunk:(c + 1) * chunk, :]
            ht = jnp.dot((a + pet_ref[...]).astype(BF16), wt_ref[...], preferred_element_type=F32)
            hb = jnp.dot((a + peb_ref[...]).astype(BF16), wb_ref[...], preferred_element_type=F32)
            row = lax.broadcasted_iota(jnp.int32, ht.shape, 0)
            shifted = jnp.where(row == 0, prev_top, pltpu.roll(ht, 1, 0))
            prev_top = ht[chunk - 1:chunk, :]
            g = _gelu_tanh(shifted + hb).astype(BF16)
            kc_ref[0, c * chunk:(c + 1) * chunk, :] = jnp.dot(
                g[:, :2 * CMP_HID], w2k_ref[...], preferred_element_type=F32).astype(BF16)
            vc_ref[0, c * chunk:(c + 1) * chunk, :] = jnp.dot(
                g[:, 2 * CMP_HID:], w2v_ref[...], preferred_element_type=F32).astype(BF16)
            vct_ref[0, :, c * chunk:(c + 1) * chunk] = _dot_nt(w2vt_ref[...], g[:, 2 * CMP_HID:]).astype(BF16)


def compress_call(pages, page_idx, pe_top, pe_bot, wt, wb, w2k, w2v, w2vt, n, npages):
    rows = npages * PAGE_ROWS
    const = lambda b, j, pt: (0, 0)
    grid_spec = pltpu.PrefetchScalarGridSpec(
        num_scalar_prefetch=1,
        grid=(n, npages),
        in_specs=[pl.BlockSpec((1, PAGE_ROWS, CMP_S * 4 * KV_W), lambda b, j, pt: (pt[b, j], 0, 0)),
                  pl.BlockSpec((1, CMP_ROW), const),
                  pl.BlockSpec((1, CMP_ROW), const),
                  pl.BlockSpec((CMP_ROW, 4 * CMP_HID), const, pipeline_mode=pl.Buffered(1)),
                  pl.BlockSpec((CMP_ROW, 4 * CMP_HID), const, pipeline_mode=pl.Buffered(1)),
                  pl.BlockSpec((2 * CMP_HID, KV_W), const),
                  pl.BlockSpec((2 * CMP_HID, KV_W), const),
                  pl.BlockSpec((KV_W, 2 * CMP_HID), const)],
        out_specs=[pl.BlockSpec((1, rows, KV_W), lambda b, j, pt: (b, 0, 0)),
                   pl.BlockSpec((1, rows, KV_W), lambda b, j, pt: (b, 0, 0)),
                   pl.BlockSpec((1, KV_W, rows), lambda b, j, pt: (b, 0, 0))],
        scratch_shapes=[pltpu.VMEM((rows, CMP_ROW), F32)],
    )
    return pl.pallas_call(
        functools.partial(_compress_kernel, rows=rows),
        grid_spec=grid_spec,
        out_shape=[jax.ShapeDtypeStruct((n, rows, KV_W), BF16),
                   jax.ShapeDtypeStruct((n, rows, KV_W), BF16),
                   jax.ShapeDtypeStruct((n, KV_W, rows), BF16)],
        compiler_params=_cparams("parallel", "arbitrary"),
        name="nsa_compress",
    )(page_idx, pages, pe_top, pe_bot, wt, wb, w2k, w2v, w2vt)


def _nsa_kernel(q_ref, kc_ref, vct_ref, ks_ref, vst_ref, kw_ref, vwt_ref, gt_ref, o_ref,
                ot_ref, sel_ref, *, ncmp, nsb):
    i = pl.program_id(1)
    causal, below = _tile_masks()
    kcol = _key_col()
    lane_q = lax.broadcasted_iota(jnp.int32, (1, TQ), 1)
    qpos = i * TQ + lane_q
    crow = lax.broadcasted_iota(jnp.int32, (ncmp, 1), 0)
    cpos = CMP_S * crow + (CMP_S - 1)
    cvalid = (crow >= 1) & (cpos <= qpos)
    cposf = cpos.astype(F32)
    mj = lax.broadcasted_iota(jnp.int32, (nsb, ncmp), 0)
    mc = lax.broadcasted_iota(jnp.int32, (nsb, ncmp), 1)
    ratio = SLC_BLOCK // CMP_S
    gather_m = jnp.where((mc >= ratio * mj) & (mc <= ratio * mj + ratio) & (mc >= 1), 1.0, 0.0)
    bj = lax.broadcasted_iota(jnp.int32, (nsb, TQ), 0)
    own = qpos // SLC_BLOCK
    forced = (bj == 0) | (bj >= own - 1)
    allowed = bj <= own
    ecol = lax.broadcasted_iota(jnp.int32, (TK, nsb), 1)
    erow = lax.broadcasted_iota(jnp.int32, (TK, nsb), 0) // SLC_BLOCK

    for g in range(NSA_G):
        imp = jnp.zeros((ncmp, TQ), F32)
        qpads = []
        for r in range(NSA_R):
            h = g * NSA_R + r
            slope = 2.0 ** (-8.0 * (h + 1) / N_HEADS)
            qpad = (_query_pair(q_ref, h, g) * SCALE).astype(BF16)
            qpads.append(qpad)
            s = _dot_nt(kc_ref[0], qpad) + slope * (cposf - (i * TQ).astype(F32))
            s = jnp.where(cvalid, s, MASKED)
            p = jnp.where(cvalid, jnp.exp(s - jnp.max(s, axis=0, keepdims=True)), 0.0)
            l = jnp.sum(p, axis=0, keepdims=True)
            p = p * (1.0 / jnp.where(l > 0.0, l, 1.0))
            imp = imp + p
            o_cmp = jnp.dot(vct_ref[0, g * HEAD_DIM:(g + 1) * HEAD_DIM, :], p.astype(BF16),
                            preferred_element_type=F32)
            ot_ref[h * HEAD_DIM:(h + 1) * HEAD_DIM, :] = gt_ref[3 * h:3 * h + 1, :] * o_cmp
        p_slc = jnp.dot(gather_m, imp, preferred_element_type=F32, precision=HIGHEST)
        score = jnp.where(allowed, jnp.where(forced, p_slc + FORCE_SCORE, p_slc), -1.0)
        sel = (_rank_before(score, bj, nsb) < SLC_N) & allowed
        sel_ref[...] = jnp.where(sel, 1.0, 0.0).astype(BF16)

        for r in range(NSA_R):
            h = g * NSA_R + r
            slope = 2.0 ** (-8.0 * (h + 1) / N_HEADS)
            qpad = qpads[r]

            def alibi(j):
                return slope * kcol - slope * ((i - j) * TK).astype(F32)

            def slc_tile(j, carry, diag):
                r0 = pl.multiple_of(j * TK, TK)
                expand = jnp.where(ecol == erow + j * (TK // SLC_BLOCK), 1.0, 0.0).astype(BF16)
                keep = jnp.dot(expand, sel_ref[...], preferred_element_type=F32) > 0.5
                if diag:
                    keep = keep & causal
                s = jnp.where(keep, _dot_nt(ks_ref[pl.ds(r0, TK), :], qpad) + alibi(j), MASKED)
                return _online_step(s, vst_ref[j, g * HEAD_DIM:(g + 1) * HEAD_DIM, :], carry)

            carry = lax.fori_loop(0, i, lambda j, c: slc_tile(j, c, False), _softmax_init())
            _, l, acc = slc_tile(i, carry, True)
            o_slc = acc * (1.0 / l)

            def win_tile(j, carry, mask):
                r0 = pl.multiple_of(j * TK, TK)
                s = _dot_nt(kw_ref[pl.ds(r0, TK), :], qpad) + alibi(j)
                if mask is not None:
                    s = jnp.where(mask, s, MASKED)
                return _online_step(s, vwt_ref[j, g * HEAD_DIM:(g + 1) * HEAD_DIM, :], carry)

            carry = _softmax_init()
            carry = lax.cond(i >= 2, lambda c: win_tile(i - 2, c, below), lambda c: c, carry)
            carry = lax.cond(i >= 1, lambda c: win_tile(i - 1, c, None), lambda c: c, carry)
            _, l, acc = win_tile(i, carry, causal)
            o_win = acc * (1.0 / l)
            rows = slice(h * HEAD_DIM, (h + 1) * HEAD_DIM)
            ot_ref[rows, :] = (ot_ref[rows, :] + gt_ref[3 * h + 1:3 * h + 2, :] * o_slc
                               + gt_ref[3 * h + 2:3 * h + 3, :] * o_win)
    o_ref[...] = ot_ref[...].T


def nsa_prompt_call(q, kcmp, vcmpt, ks, vst, kw, vwt, gate_t, n, t):
    nq = t // TQ
    ncmp = kcmp.shape[1]
    nsb = t // SLC_BLOCK
    return pl.pallas_call(
        functools.partial(_nsa_kernel, ncmp=ncmp, nsb=nsb),
        grid=(n, nq),
        in_specs=[pl.BlockSpec((TQ, BRANCH_W), lambda b, i: (b * nq + i, 0)),
                  pl.BlockSpec((1, ncmp, KV_W), lambda b, i: (b, 0, 0)),
                  pl.BlockSpec((1, KV_W, ncmp), lambda b, i: (b, 0, 0)),
                  pl.BlockSpec((t, KV_W), lambda b, i: (b, 0)),
                  pl.BlockSpec((t // TK, KV_W, TK), lambda b, i: (b, 0, 0)),
                  pl.BlockSpec((t, KV_W), lambda b, i: (b, 0)),
                  pl.BlockSpec((t // TK, KV_W, TK), lambda b, i: (b, 0, 0)),
                  pl.BlockSpec((3 * N_HEADS, TQ), lambda b, i: (0, b * nq + i))],
        out_specs=pl.BlockSpec((TQ, BRANCH_W), lambda b, i: (b * nq + i, 0)),
        out_shape=jax.ShapeDtypeStruct((n * t, BRANCH_W), F32),
        scratch_shapes=[pltpu.VMEM((BRANCH_W, TQ), F32), pltpu.VMEM((nsb, TQ), BF16)],
        compiler_params=_cparams("parallel", "arbitrary"),
        name="nsa_prompt",
    )(q, kcmp, vcmpt, ks, vst, kw, vwt, gate_t)


def alibi_slopes(n):
    return jnp.exp2(-8.0 * jnp.arange(1, n + 1, dtype=jnp.float32) / n)


def masked_softmax(s, mask, axis=-1):
    p = jax.nn.softmax(jnp.where(mask, s, NEG), axis=axis)
    return p * mask


def sweep_queries(fn, block, *qs):
    n, t = qs[0].shape[:2]
    qb = block if t % block == 0 else t
    nb = t // qb
    xs = tuple(jnp.moveaxis(a.reshape((n, nb, qb) + a.shape[2:]), 1, 0) for a in qs)
    out = lax.map(lambda args: fn(args[0] * qb, *args[1:]), (jnp.arange(nb, dtype=jnp.int32),) + xs)
    return jnp.moveaxis(out, 0, 1).reshape((n, t) + out.shape[3:])


def gather_pages(pool, page_table):
    g = pool[page_table]
    return g.reshape((g.shape[0], g.shape[1] * g.shape[2]) + g.shape[3:])


def fox_attention(q, k, v, logf, q0):
    d = q.shape[-1]
    L = k.shape[1]
    scale = d ** -0.5
    F = jnp.cumsum(logf.astype(jnp.float32), axis=1)
    Fk = jnp.moveaxis(F, 1, 2)
    Fq = F[:, q0:]
    kpos = jnp.arange(L)

    def block(start, qb, fq):
        qpos = q0 + start + jnp.arange(qb.shape[1])
        s = jnp.einsum('nqhd,nkhd->nhqk', qb, k).astype(jnp.float32) * scale
        s = s + jnp.moveaxis(fq, 1, 2)[..., None] - Fk[:, :, None, :]
        p = masked_softmax(s, kpos[None, :] <= qpos[:, None])
        return jnp.einsum('nhqk,nkhd->nqhd', p.astype(q.dtype), v)

    return sweep_queries(block, Q_BLOCK, q, Fq)


def moba_attention(q, k, v, q0, slopes):
    n, tq, nh, d = q.shape
    L = k.shape[1]
    scale = d ** -0.5
    nbk = -(-L // MOBA_BLOCK)
    pad = nbk * MOBA_BLOCK - L

    def to_blocks(a):
        a = jnp.pad(a, ((0, 0), (0, pad), (0, 0), (0, 0)))
        return a.reshape(n, nbk, MOBA_BLOCK, nh, d).transpose(0, 3, 1, 2, 4)

    kb, vb = to_blocks(k), to_blocks(v)
    kmean = jnp.mean(kb.astype(jnp.float32), axis=3)
    topk = min(MOBA_TOPK, nbk)
    bidx = jnp.arange(nbk)
    ni = jnp.arange(n)[:, None, None, None]
    hi = jnp.arange(nh)[None, :, None, None]
    sl = slopes[None, :, None, None, None]

    def block(start, qb):
        m = qb.shape[1]
        qpos = q0 + start + jnp.arange(m)
        own = qpos // MOBA_BLOCK
        gs = jnp.einsum('nqhd,nhbd->nhqb', qb.astype(jnp.float32), kmean)
        gs = jnp.where(bidx[None, :] < own[:, None], gs, NEG)
        _, top = lax.top_k(gs, topk)
        sel_ok = top < own[:, None]
        idx = jnp.concatenate([top, jnp.broadcast_to(own[:, None], (n, nh, m, 1))], axis=-1)
        ok = jnp.concatenate([sel_ok, jnp.ones((n, nh, m, 1), bool)], axis=-1)
        kg = kb[ni, hi, idx]
        vg = vb[ni, hi, idx]
        kpos = idx[..., None] * MOBA_BLOCK + jnp.arange(MOBA_BLOCK)
        s = jnp.einsum('nqhd,nhqsjd->nhqsj', qb, kg).astype(jnp.float32) * scale
        s = s - sl * (qpos[:, None, None] - kpos).astype(jnp.float32)
        mask = ok[..., None] & (kpos <= qpos[:, None, None])
        p = masked_softmax(s, mask, axis=(-2, -1))
        return jnp.einsum('nhqsj,nhqsjd->nqhd', p.astype(q.dtype), vg)

    return sweep_queries(block, SPARSE_Q_BLOCK, q)


def compress_tokens(a, w1, w2, pe):
    n, L, g, d = a.shape
    nc = (L - CMP_L) // CMP_S + 1
    idx = jnp.arange(nc)[:, None] * CMP_S + jnp.arange(CMP_L)[None, :]
    blocks = a[:, idx] + pe[:, None, :].astype(a.dtype)
    flat = jnp.swapaxes(blocks, 2, 3).reshape(n, nc, g, CMP_L * d)
    return jax.nn.gelu(flat @ w1) @ w2


def nsa_attention(q, kc, vc, ks, vs, kw, vw, gate, q0, w_ofs, slopes, cmp_w1, cmp_w2, cmp_pe):
    n, tq, nh, d = q.shape
    L = kc.shape[1]
    dt = q.dtype
    scale = d ** -0.5
    qg = q.reshape(n, tq, NSA_G, NSA_R, d)
    k_cmp = compress_tokens(kc, cmp_w1[0], cmp_w2[0], cmp_pe[0])
    v_cmp = compress_tokens(vc, cmp_w1[1], cmp_w2[1], cmp_pe[1])
    nc = k_cmp.shape[1]
    cpos = jnp.arange(nc) * CMP_S + (CMP_L - 1)
    nsb = -(-L // SLC_BLOCK)
    padl = nsb * SLC_BLOCK - L

    def to_blocks(a):
        a = jnp.pad(a, ((0, 0), (0, padl), (0, 0), (0, 0)))
        return a.reshape(n, nsb, SLC_BLOCK, NSA_G, d).transpose(0, 3, 1, 2, 4)

    ks_b, vs_b = to_blocks(ks), to_blocks(vs)
    nsel = min(SLC_N, nsb)
    front = CMP_L // CMP_S - 1
    ratio = SLC_BLOCK // CMP_S
    width = ratio + front
    back = ratio * nsb + width - front - nc
    kw_p = jnp.pad(kw, ((0, 0), (WINDOW, 0), (0, 0), (0, 0)))
    vw_p = jnp.pad(vw, ((0, 0), (WINDOW, 0), (0, 0), (0, 0)))
    sl = slopes.reshape(NSA_G, NSA_R)[None, :, :, None, None]
    ni = jnp.arange(n)[:, None, None, None]
    gi = jnp.arange(NSA_G)[None, :, None, None]
    bj = jnp.arange(nsb)

    def block(start, qb, gb):
        m = qb.shape[1]
        qpos = q0 + start + jnp.arange(m)
        s = jnp.einsum('nqgrd,ncgd->ngrqc', qb, k_cmp).astype(jnp.float32) * scale
        s = s - sl * (qpos[:, None] - cpos[None, :]).astype(jnp.float32)
        p_cmp = masked_softmax(s, cpos[None, :] <= qpos[:, None])
        o_cmp = jnp.einsum('ngrqc,ncgd->nqgrd', p_cmp.astype(dt), v_cmp)
        imp = jnp.pad(p_cmp.sum(axis=2), ((0, 0), (0, 0), (0, 0), (front, back)))
        p_slc = imp[..., 0:ratio * nsb:ratio]
        for u in range(1, width):
            p_slc = p_slc + imp[..., u:u + ratio * nsb:ratio]
        own = qpos // SLC_BLOCK
        forced = (bj[None, :] == 0) | (bj[None, :] >= own[:, None] - 1)
        allowed = bj[None, :] <= own[:, None]
        score = jnp.where(allowed, jnp.where(forced, p_slc + FORCE_SCORE, p_slc), -1.0)
        _, top = lax.top_k(score, nsel)
        ok = top <= own[:, None]
        kg = ks_b[ni, gi, top]
        vg = vs_b[ni, gi, top]
        kpos = top[..., None] * SLC_BLOCK + jnp.arange(SLC_BLOCK)
        dist = (qpos[:, None, None] - kpos)[:, :, None].astype(jnp.float32)
        s2 = jnp.einsum('nqgrd,ngqsjd->ngrqsj', qb, kg).astype(jnp.float32) * scale - sl[..., None] * dist
        mask2 = (ok[..., None] & (kpos <= qpos[:, None, None]))[:, :, None]
        p2 = masked_softmax(s2, mask2, axis=(-2, -1))
        o_slc = jnp.einsum('ngrqsj,ngqsjd->nqgrd', p2.astype(dt), vg)
        off = q0 + start - w_ofs
        kwin = lax.dynamic_slice_in_dim(kw_p, off, WINDOW + m, axis=1)
        vwin = lax.dynamic_slice_in_dim(vw_p, off, WINDOW + m, axis=1)
        wpos = q0 + start - WINDOW + jnp.arange(WINDOW + m)
        s3 = jnp.einsum('nqgrd,nkgd->ngrqk', qb, kwin).astype(jnp.float32) * scale
        s3 = s3 - sl * (qpos[:, None] - wpos[None, :]).astype(jnp.float32)
        wmask = (wpos[None, :] <= qpos[:, None]) & (wpos[None, :] > qpos[:, None] - WINDOW) & (wpos[None, :] >= 0)
        p3 = masked_softmax(s3, wmask)
        o_win = jnp.einsum('ngrqk,nkgd->nqgrd', p3.astype(dt), vwin)
        return gb[..., 0:1] * o_cmp + gb[..., 1:2] * o_slc + gb[..., 2:3] * o_win

    return sweep_queries(block, SPARSE_Q_BLOCK, qg, gate)


PAGE = 128


def _head_diag_mask(width):
    sub = lax.broadcasted_iota(jnp.int32, (N_HEADS, width), 0)
    lane = lax.broadcasted_iota(jnp.int32, (N_HEADS, width), 1)
    return sub == lane // HEAD_DIM


def _block_diag_queries(q_ref, nt):
    diag = _head_diag_mask(BRANCH_W)
    return jnp.concatenate([jnp.where(diag, q_ref[0, t:t + 1, :], 0.0) for t in range(nt)], axis=0)


def _extract_heads(o, nt):
    diag = _head_diag_mask(BRANCH_W)
    return jnp.concatenate(
        [jnp.sum(jnp.where(diag, o[t * N_HEADS:(t + 1) * N_HEADS], 0.0), axis=0, keepdims=True) for t in range(nt)],
        axis=0)


def _slope_col(rows):
    h = lax.broadcasted_iota(jnp.int32, (rows, 1), 0) % N_HEADS
    col = jnp.zeros((rows, 1), F32)
    for k in range(N_HEADS):
        col = jnp.where(h == k, 2.0 ** (-8.0 * (k + 1) / N_HEADS), col)
    return col


def _row_softmax_step(s, v, m_ref, l_ref, acc_ref):
    m_old = m_ref[...]
    m_new = jnp.maximum(m_old, jnp.max(s, axis=1, keepdims=True))
    alpha = jnp.exp(m_old - m_new)
    p = jnp.exp(s - m_new)
    l_ref[...] = alpha * l_ref[...] + jnp.sum(p, axis=1, keepdims=True)
    acc_ref[...] = alpha * acc_ref[...] + jnp.dot(p.astype(BF16), v, preferred_element_type=F32)
    m_ref[...] = m_new


def _softmax_reset(m_ref, l_ref, acc_ref):
    m_ref[...] = jnp.full(m_ref.shape, NEG, F32)
    l_ref[...] = jnp.zeros(l_ref.shape, F32)
    acc_ref[...] = jnp.zeros(acc_ref.shape, F32)


def _new_token_mask(rows, nt):
    k = lax.broadcasted_iota(jnp.int32, (rows, PAGE), 1)
    t = lax.broadcasted_iota(jnp.int32, (rows, PAGE), 0) // N_HEADS
    return (k <= t) & (k < nt)


def _dfox_kernel(pt_ref, q_ref, pg_ref, lf_ref, npg_ref, nlf_ref, o_ref, qbd_ref, m_ref, l_ref, acc_ref, cf_ref, *, nt):
    del pt_ref
    j = pl.program_id(1)
    rows = nt * N_HEADS

    @pl.when(j == 0)
    def _():
        qbd_ref[...] = (_block_diag_queries(q_ref, nt) * SCALE).astype(BF16)
        _softmax_reset(m_ref, l_ref, acc_ref)
        cf_ref[...] = jnp.zeros_like(cf_ref)

    def step(kv_ref, lf, mask):
        k = kv_ref[0, :, 0:BRANCH_W].astype(BF16)
        v = kv_ref[0, :, BRANCH_W:2 * BRANCH_W].astype(BF16)
        rr = lax.broadcasted_iota(jnp.int32, (rows, N_HEADS), 0) % N_HEADS
        rc = lax.broadcasted_iota(jnp.int32, (rows, N_HEADS), 1)
        lfe = _dot_nt(jnp.where(rr == rc, 1.0, 0.0), lf, precision=HIGHEST)
        a = lax.broadcasted_iota(jnp.int32, (PAGE, PAGE), 0)
        b = lax.broadcasted_iota(jnp.int32, (PAGE, PAGE), 1)
        fk = jnp.dot(lfe, jnp.where(a <= b, 1.0, 0.0), preferred_element_type=F32, precision=HIGHEST) + cf_ref[...]
        cf_ref[...] = fk[:, PAGE - 1:PAGE]
        s = _dot_nt(qbd_ref[...], k) - fk
        if mask is not None:
            s = jnp.where(mask, s, MASKED)
        _row_softmax_step(s, v, m_ref, l_ref, acc_ref)

    step(pg_ref, lf_ref[0], None)

    @pl.when(j == pl.num_programs(1) - 1)
    def _():
        step(npg_ref, nlf_ref[0], _new_token_mask(rows, nt))
        o_ref[0] = _extract_heads(acc_ref[...] * (1.0 / l_ref[...]), nt)


def fox_decode_call(pt, q, cache_kv, cache_lf, new_kv, new_lf):
    n, nt, _ = q.shape
    npages = pt.shape[1]
    rows = nt * N_HEADS
    grid_spec = pltpu.PrefetchScalarGridSpec(
        num_scalar_prefetch=1,
        grid=(n, npages),
        in_specs=[pl.BlockSpec((1, nt, BRANCH_W), lambda b, j, pt: (b, 0, 0)),
                  pl.BlockSpec((1, PAGE, 2 * BRANCH_W), lambda b, j, pt: (pt[b, j], 0, 0)),
                  pl.BlockSpec((1, PAGE, N_HEADS), lambda b, j, pt: (pt[b, j], 0, 0)),
                  pl.BlockSpec((1, PAGE, 2 * BRANCH_W), lambda b, j, pt: (b, 0, 0)),
                  pl.BlockSpec((1, PAGE, N_HEADS), lambda b, j, pt: (b, 0, 0))],
        out_specs=pl.BlockSpec((1, nt, BRANCH_W), lambda b, j, pt: (b, 0, 0)),
        scratch_shapes=[pltpu.VMEM((rows, BRANCH_W), BF16), pltpu.VMEM((rows, 1), F32), pltpu.VMEM((rows, 1), F32),
                        pltpu.VMEM((rows, BRANCH_W), F32), pltpu.VMEM((rows, 1), F32)],
    )
    return pl.pallas_call(
        functools.partial(_dfox_kernel, nt=nt),
        grid_spec=grid_spec,
        out_shape=jax.ShapeDtypeStruct((n, nt, BRANCH_W), F32),
        compiler_params=_cparams("parallel", "arbitrary"),
        name="fox_decode",
    )(pt, q, cache_kv, cache_lf, new_kv, new_lf)


def _dmoba_sel_kernel(pt_ref, q_ref, kp_ref, sel_ref, q32_ref, g_ref, *, nt):
    del pt_ref
    j = pl.program_id(1)
    rows = nt * N_HEADS
    pages_per_block = MOBA_BLOCK // PAGE

    @pl.when(j == 0)
    def _():
        q32_ref[...] = _block_diag_queries(q_ref, nt)
        g_ref[...] = jnp.zeros_like(g_ref)

    ksum = jnp.sum(kp_ref[0], axis=0, keepdims=True)
    prow = lax.broadcasted_iota(jnp.int32, (g_ref.shape[1], BRANCH_W), 0)
    g_ref[...] += _dot_nt(q32_ref[...], jnp.where(prow == j, ksum, 0.0), precision=HIGHEST)

    @pl.when(j == pl.num_programs(1) - 1)
    def _():
        npg = g_ref.shape[1]
        g = g_ref[...]
        lane = lax.broadcasted_iota(jnp.int32, (rows, npg), 1)
        blk = g
        for u in range(1, pages_per_block):
            blk = blk + pltpu.roll(g, npg - u, 1)
        gs = jnp.where(lane % pages_per_block == 0, blk * (1.0 / MOBA_BLOCK), NEG)
        sel = jnp.zeros((rows, npg), F32)
        for _ in range(MOBA_TOPK):
            mx = jnp.max(gs, axis=1, keepdims=True)
            idx = jnp.min(jnp.where(gs == mx, lane, npg), axis=1, keepdims=True)
            hit = lane == idx
            sel = jnp.where(hit, 1.0, sel)
            gs = jnp.where(hit, MASKED, gs)
        out = sel
        for u in range(1, pages_per_block):
            out = out + pltpu.roll(sel, u, 1)
        sel_ref[0] = out


def moba_select_call(pt, q, cache_kv):
    n, nt, _ = q.shape
    npages = pt.shape[1]
    rows = nt * N_HEADS
    grid_spec = pltpu.PrefetchScalarGridSpec(
        num_scalar_prefetch=1,
        grid=(n, npages),
        in_specs=[pl.BlockSpec((1, nt, BRANCH_W), lambda b, j, pt: (b, 0, 0)),
                  pl.BlockSpec((1, PAGE, BRANCH_W), lambda b, j, pt: (pt[b, j], 0, 0))],
        out_specs=pl.BlockSpec((1, rows, npages), lambda b, j, pt: (b, 0, 0)),
        scratch_shapes=[pltpu.VMEM((rows, BRANCH_W), F32), pltpu.VMEM((rows, npages), F32)],
    )
    return pl.pallas_call(
        functools.partial(_dmoba_sel_kernel, nt=nt),
        grid_spec=grid_spec,
        out_shape=jax.ShapeDtypeStruct((n, rows, npages), F32),
        compiler_params=_cparams("parallel", "arbitrary"),
        name="moba_select",
    )(pt, q, cache_kv)


def _dmoba_kernel(pt_ref, q_ref, sel_ref, pg_ref, npg_ref, o_ref, qbd_ref, m_ref, l_ref, acc_ref, *, nt):
    del pt_ref
    j = pl.program_id(1)
    npages = pl.num_programs(1)
    rows = nt * N_HEADS
    slope = _slope_col(rows)
    klane = lax.broadcasted_iota(jnp.int32, (1, PAGE), 1)

    @pl.when(j == 0)
    def _():
        qbd_ref[...] = (_block_diag_queries(q_ref, nt) * SCALE).astype(BF16)
        _softmax_reset(m_ref, l_ref, acc_ref)

    def step(kv_ref, rel, mask):
        k = kv_ref[0, :, 0:BRANCH_W].astype(BF16)
        v = kv_ref[0, :, BRANCH_W:2 * BRANCH_W].astype(BF16)
        s = _dot_nt(qbd_ref[...], k) + slope * rel
        _row_softmax_step(jnp.where(mask, s, MASKED), v, m_ref, l_ref, acc_ref)

    prow = lax.broadcasted_iota(jnp.int32, (sel_ref.shape[2], PAGE), 0)
    keep = jnp.dot(sel_ref[0].astype(BF16), jnp.where(prow == j, 1.0, 0.0).astype(BF16),
                   preferred_element_type=F32) > 0.5
    step(pg_ref, (klane + (j - npages) * PAGE).astype(F32), keep)

    @pl.when(j == npages - 1)
    def _():
        step(npg_ref, klane.astype(F32), _new_token_mask(rows, nt))
        o_ref[0] = _extract_heads(acc_ref[...] * (1.0 / l_ref[...]), nt)


def moba_decode_call(pt, q, sel, cache_kv, new_kv):
    n, nt, _ = q.shape
    npages = pt.shape[1]
    rows = nt * N_HEADS
    grid_spec = pltpu.PrefetchScalarGridSpec(
        num_scalar_prefetch=1,
        grid=(n, npages),
        in_specs=[pl.BlockSpec((1, nt, BRANCH_W), lambda b, j, pt: (b, 0, 0)),
                  pl.BlockSpec((1, rows, npages), lambda b, j, pt: (b, 0, 0)),
                  pl.BlockSpec((1, PAGE, 2 * BRANCH_W), lambda b, j, pt: (pt[b, j], 0, 0)),
                  pl.BlockSpec((1, PAGE, 2 * BRANCH_W), lambda b, j, pt: (b, 0, 0))],
        out_specs=pl.BlockSpec((1, nt, BRANCH_W), lambda b, j, pt: (b, 0, 0)),
        scratch_shapes=[pltpu.VMEM((rows, BRANCH_W), BF16), pltpu.VMEM((rows, 1), F32), pltpu.VMEM((rows, 1), F32),
                        pltpu.VMEM((rows, BRANCH_W), F32)],
    )
    return pl.pallas_call(
        functools.partial(_dmoba_kernel, nt=nt),
        grid_spec=grid_spec,
        out_shape=jax.ShapeDtypeStruct((n, nt, BRANCH_W), F32),
        compiler_params=_cparams("parallel", "arbitrary"),
        name="moba_decode",
    )(pt, q, sel, cache_kv, new_kv)


def _dnsa_kernel(pt_ref, q_ref, gate_ref, kc_ref, vc_ref, pg_ref, npg_ref, win_ref, nwin_ref, o_ref,
                 qbd_ref, sel_ref, ocmp_ref, m_ref, l_ref, acc_ref, *, nt, nsb_pad):
    del pt_ref
    j = pl.program_id(1)
    npages = pl.num_programs(1)
    rows = nt * N_HEADS
    ncmp = kc_ref.shape[1]
    q0 = npages * PAGE
    slope = _slope_col(rows)
    klane = lax.broadcasted_iota(jnp.int32, (1, PAGE), 1)
    blocks_per_page = PAGE // SLC_BLOCK

    @pl.when(j == 0)
    def _():
        pr = lax.broadcasted_iota(jnp.int32, (BRANCH_W, KV_W), 0)
        pc = lax.broadcasted_iota(jnp.int32, (BRANCH_W, KV_W), 1)
        place = jnp.where((pr % HEAD_DIM == pc % HEAD_DIM) & (pc // HEAD_DIM == pr // (NSA_R * HEAD_DIM)), 1.0, 0.0)
        qg = jnp.dot(_block_diag_queries(q_ref, nt), place, preferred_element_type=F32, precision=HIGHEST)
        qbd = (qg * SCALE).astype(BF16)
        qbd_ref[...] = qbd
        c = lax.broadcasted_iota(jnp.int32, (1, ncmp), 1)
        s = _dot_nt(qbd, kc_ref[0]) + slope * (CMP_S * c + (CMP_S - 1) - q0).astype(F32)
        valid = c >= 1
        s = jnp.where(valid, s, MASKED)
        p = jnp.where(valid, jnp.exp(s - jnp.max(s, axis=1, keepdims=True)), 0.0)
        p = p * (1.0 / jnp.sum(p, axis=1, keepdims=True))
        ocmp_ref[...] = gate_ref[0, :, 0:1] * jnp.dot(p.astype(BF16), vc_ref[0], preferred_element_type=F32)
        gr = lax.broadcasted_iota(jnp.int32, (nt * NSA_G, rows), 0)
        gc = lax.broadcasted_iota(jnp.int32, (nt * NSA_G, rows), 1) // NSA_R
        imp = jnp.dot(jnp.where(gr == gc, 1.0, 0.0), p, preferred_element_type=F32, precision=HIGHEST)
        ratio = SLC_BLOCK // CMP_S
        mr = lax.broadcasted_iota(jnp.int32, (ncmp, nsb_pad), 0)
        mb = lax.broadcasted_iota(jnp.int32, (ncmp, nsb_pad), 1)
        gather_m = jnp.where((mr >= ratio * mb) & (mr <= ratio * mb + ratio) & (mr >= 1), 1.0, 0.0)
        p_slc = jnp.dot(imp, gather_m, preferred_element_type=F32, precision=HIGHEST)
        bj = lax.broadcasted_iota(jnp.int32, p_slc.shape, 1)
        own = q0 // SLC_BLOCK
        forced = (bj == 0) | (bj >= own - 1)
        allowed = bj <= own
        score = jnp.where(allowed, jnp.where(forced, p_slc + FORCE_SCORE, p_slc), -1.0)
        sel = jnp.zeros(p_slc.shape, F32)
        for _ in range(SLC_N):
            mx = jnp.max(score, axis=1, keepdims=True)
            idx = jnp.min(jnp.where(score == mx, bj, nsb_pad), axis=1, keepdims=True)
            hit = bj == idx
            sel = jnp.where(hit, 1.0, sel)
            score = jnp.where(hit, -2.0, score)
        sel = jnp.where(allowed, sel, 0.0)
        er = lax.broadcasted_iota(jnp.int32, (rows, nt * NSA_G), 0) // NSA_R
        ec = lax.broadcasted_iota(jnp.int32, (rows, nt * NSA_G), 1)
        sel_ref[...] = jnp.dot(jnp.where(er == ec, 1.0, 0.0), sel, preferred_element_type=F32).astype(BF16)
        _softmax_reset(m_ref, l_ref, acc_ref)

    def step(kv_ref, rel, mask):
        k = kv_ref[0, :, 0:KV_W].astype(BF16)
        v = kv_ref[0, :, KV_W:2 * KV_W].astype(BF16)
        s = _dot_nt(qbd_ref[...], k) + slope * rel
        if mask is not None:
            s = jnp.where(mask, s, MASKED)
        _row_softmax_step(s, v, m_ref, l_ref, acc_ref)

    brow = lax.broadcasted_iota(jnp.int32, (nsb_pad, PAGE), 0)
    bcol = lax.broadcasted_iota(jnp.int32, (nsb_pad, PAGE), 1) // SLC_BLOCK
    expand = jnp.where(brow == bcol + j * blocks_per_page, 1.0, 0.0).astype(BF16)
    keep = jnp.dot(sel_ref[...], expand, preferred_element_type=F32) > 0.5
    step(pg_ref, (klane + (j - npages) * PAGE).astype(F32), keep)

    @pl.when(j == npages - 1)
    def _():
        new_mask = _new_token_mask(rows, nt)
        step(npg_ref, klane.astype(F32), new_mask)
        o_slc = acc_ref[...] * (1.0 / l_ref[...])
        _softmax_reset(m_ref, l_ref, acc_ref)
        wlen = win_ref.shape[1]
        wi = lax.broadcasted_iota(jnp.int32, (rows, wlen), 1)
        wt = lax.broadcasted_iota(jnp.int32, (rows, wlen), 0) // N_HEADS
        wrel = (lax.broadcasted_iota(jnp.int32, (1, wlen), 1) - wlen).astype(F32)
        step(win_ref, wrel, wi > wt + (wlen - WINDOW))
        step(nwin_ref, klane.astype(F32), new_mask)
        o_win = acc_ref[...] * (1.0 / l_ref[...])
        o = ocmp_ref[...] + gate_ref[0, :, 1:2] * o_slc + gate_ref[0, :, 2:3] * o_win
        ur = lax.broadcasted_iota(jnp.int32, (KV_W, BRANCH_W), 0)
        uc = lax.broadcasted_iota(jnp.int32, (KV_W, BRANCH_W), 1)
        unplace = jnp.where((ur % HEAD_DIM == uc % HEAD_DIM) & (ur // HEAD_DIM == uc // (NSA_R * HEAD_DIM)), 1.0, 0.0)
        o_ref[0] = _extract_heads(jnp.dot(o, unplace, preferred_element_type=F32, precision=HIGHEST), nt)


def nsa_decode_call(pt, q, gate, kcmp, vcmp, cache_kv, new_kv, win, new_win):
    n, nt, _ = q.shape
    npages = pt.shape[1]
    rows = nt * N_HEADS
    ncmp = kcmp.shape[1]
    nsb = (npages * PAGE) // SLC_BLOCK + 1
    nsb_pad = -(-nsb // 128) * 128
    wlen = win.shape[1]
    grid_spec = pltpu.PrefetchScalarGridSpec(
        num_scalar_prefetch=1,
        grid=(n, npages),
        in_specs=[pl.BlockSpec((1, nt, BRANCH_W), lambda b, j, pt: (b, 0, 0)),
                  pl.BlockSpec((1, rows, 3), lambda b, j, pt: (b, 0, 0)),
                  pl.BlockSpec((1, ncmp, KV_W), lambda b, j, pt: (b, 0, 0)),
                  pl.BlockSpec((1, ncmp, KV_W), lambda b, j, pt: (b, 0, 0)),
                  pl.BlockSpec((1, PAGE, 2 * KV_W), lambda b, j, pt: (pt[b, j], 0, 1)),
                  pl.BlockSpec((1, PAGE, 2 * KV_W), lambda b, j, pt: (b, 0, 0)),
                  pl.BlockSpec((1, wlen, 2 * KV_W), lambda b, j, pt: (b, 0, 0)),
                  pl.BlockSpec((1, PAGE, 2 * KV_W), lambda b, j, pt: (b, 0, 0))],
        out_specs=pl.BlockSpec((1, nt, BRANCH_W), lambda b, j, pt: (b, 0, 0)),
        scratch_shapes=[pltpu.VMEM((rows, KV_W), BF16), pltpu.VMEM((rows, nsb_pad), BF16),
                        pltpu.VMEM((rows, KV_W), F32), pltpu.VMEM((rows, 1), F32), pltpu.VMEM((rows, 1), F32),
                        pltpu.VMEM((rows, KV_W), F32)],
    )
    return pl.pallas_call(
        functools.partial(_dnsa_kernel, nt=nt, nsb_pad=nsb_pad),
        grid_spec=grid_spec,
        out_shape=jax.ShapeDtypeStruct((n, nt, BRANCH_W), F32),
        compiler_params=_cparams("parallel", "arbitrary"),
        name="nsa_decode",
    )(pt, q, gate, kcmp, vcmp, cache_kv, new_kv, win, new_win)


def _conv_sample_kernel(z_ref, pre1_ref, pre2_ref, w_ref, o_ref, u_ref, *, nt):
    bw = BRANCH_W
    u = z_ref[:, 2 * bw:3 * bw] * z_ref[:, 0:bw]
    t = lax.broadcasted_iota(jnp.int32, u.shape, 0) % nt
    u1 = jnp.where(t < 1, pre1_ref[...], pltpu.roll(u, 1, 0))
    u2 = jnp.where(t < 2, pre2_ref[...], pltpu.roll(u, 2, 0))
    y = u2 * w_ref[0:1, :] + u1 * w_ref[1:2, :] + u * w_ref[2:3, :]
    o_ref[...] = z_ref[:, bw:2 * bw] * y
    u_ref[...] = u


def conv_sample_call(z_conv, state, conv_w, n, nt):
    zero = jnp.zeros((n, nt, BRANCH_W), F32)
    pre1 = zero.at[:, 0].set(state[:, 1]).reshape(n * nt, BRANCH_W)
    pre2 = zero.at[:, 0].set(state[:, 0]).at[:, 1].set(state[:, 1]).reshape(n * nt, BRANCH_W)
    return pl.pallas_call(
        functools.partial(_conv_sample_kernel, nt=nt),
        out_shape=[jax.ShapeDtypeStruct((n * nt, BRANCH_W), F32), jax.ShapeDtypeStruct((n * nt, BRANCH_W), F32)],
        name="conv_sample",
    )(z_conv, pre1, pre2, conv_w)


PPS = 8


def _split3(x):
    hi = x.astype(BF16)
    r1 = x - hi.astype(F32)
    mid = r1.astype(BF16)
    lo = (r1 - mid.astype(F32)).astype(BF16)
    return jnp.concatenate([hi, mid, lo], axis=0)


def _softmax_step_t(s, vts, m_ref, l_ref, acc_ref):
    m_old = m_ref[...]
    m_new = jnp.maximum(m_old, jnp.max(s, axis=1, keepdims=True))
    alpha = jnp.exp(m_old - m_new)
    p = jnp.exp(s - m_new)
    l_ref[...] = alpha * l_ref[...] + jnp.sum(p, axis=1, keepdims=True)
    p = p.astype(BF16)
    pv, off = None, 0
    for vt in vts:
        nk = vt.shape[1]
        term = _dot_nt(p[:, off:off + nk], vt)
        pv = term if pv is None else pv + term
        off += nk
    acc_ref[...] = alpha * acc_ref[...] + pv
    m_ref[...] = m_new


def _page_specs(rows, row_block, pps):
    return [pl.BlockSpec((1, rows, PAGE), functools.partial(lambda b, j, pt, u: (pt[b, j * pps + u], row_block, 0), u=u))
            for u in range(pps)]


def _dfox_t_kernel(pt_ref, q_ref, *refs, nt, pps):
    del pt_ref
    kv_refs, lf_refs = refs[:pps], refs[pps:2 * pps]
    nkv_ref, nlf_ref, o_ref, qbd_ref, m_ref, l_ref, acc_ref, cf_ref = refs[2 * pps:]
    j = pl.program_id(1)
    rows = nt * N_HEADS

    @pl.when(j == 0)
    def _():
        qbd_ref[...] = (_block_diag_queries(q_ref, nt) * SCALE).astype(BF16)
        _softmax_reset(m_ref, l_ref, acc_ref)
        cf_ref[...] = jnp.zeros_like(cf_ref)

    def block(kvs, lfs, mask):
        a = lax.broadcasted_iota(jnp.int32, (PAGE, PAGE), 0)
        b = lax.broadcasted_iota(jnp.int32, (PAGE, PAGE), 1)
        tri = jnp.where(a <= b, 1.0, 0.0).astype(BF16)
        carry = cf_ref[...]
        parts, vts = [], []
        for kv_ref, lf_ref in zip(kvs, lfs):
            lfe = jnp.concatenate([lf_ref[0]] * nt, axis=0)
            c3 = jnp.dot(_split3(lfe), tri, preferred_element_type=F32)
            fk = c3[0:rows] + c3[rows:2 * rows] + c3[2 * rows:3 * rows] + carry
            carry = fk[:, PAGE - 1:PAGE]
            parts.append(jnp.dot(qbd_ref[...], kv_ref[0, 0:BRANCH_W, :].astype(BF16), preferred_element_type=F32) - fk)
            vts.append(kv_ref[0, BRANCH_W:2 * BRANCH_W, :].astype(BF16))
        cf_ref[...] = carry
        s = parts[0] if len(parts) == 1 else jnp.concatenate(parts, axis=1)
        if mask is not None:
            s = jnp.where(mask, s, MASKED)
        _softmax_step_t(s, vts, m_ref, l_ref, acc_ref)

    block(kv_refs, lf_refs, None)

    @pl.when(j == pl.num_programs(1) - 1)
    def _():
        block([nkv_ref], [nlf_ref], _new_token_mask(rows, nt))
        o_ref[0] = _extract_heads(acc_ref[...] * (1.0 / l_ref[...]), nt)


def fox_decode_t_call(pt, q, cache_kvt, cache_lft, new_kvt, new_lft):
    n, nt, _ = q.shape
    npages = pt.shape[1]
    pps = PPS if npages % PPS == 0 else 1
    rows = nt * N_HEADS
    grid_spec = pltpu.PrefetchScalarGridSpec(
        num_scalar_prefetch=1,
        grid=(n, npages // pps),
        in_specs=[pl.BlockSpec((1, nt, BRANCH_W), lambda b, j, pt: (b, 0, 0))]
        + _page_specs(2 * BRANCH_W, 0, pps) + _page_specs(N_HEADS, 0, pps)
        + [pl.BlockSpec((1, 2 * BRANCH_W, PAGE), lambda b, j, pt: (b, 0, 0)),
           pl.BlockSpec((1, N_HEADS, PAGE), lambda b, j, pt: (b, 0, 0))],
        out_specs=pl.BlockSpec((1, nt, BRANCH_W), lambda b, j, pt: (b, 0, 0)),
        scratch_shapes=[pltpu.VMEM((rows, BRANCH_W), BF16), pltpu.VMEM((rows, 1), F32), pltpu.VMEM((rows, 1), F32),
                        pltpu.VMEM((rows, BRANCH_W), F32), pltpu.VMEM((rows, 1), F32)],
    )
    return pl.pallas_call(
        functools.partial(_dfox_t_kernel, nt=nt, pps=pps),
        grid_spec=grid_spec,
        out_shape=jax.ShapeDtypeStruct((n, nt, BRANCH_W), F32),
        compiler_params=_cparams("parallel", "arbitrary"),
        name="fox_decode",
    )(pt, q, *([cache_kvt] * pps), *([cache_lft] * pps), new_kvt, new_lft)


def _dmoba_sel_t_kernel(pt_ref, q_ref, *refs, nt, pps):
    del pt_ref
    k_refs = refs[:pps]
    sel_ref, q32_ref, g_ref = refs[pps:]
    j = pl.program_id(1)
    rows = nt * N_HEADS
    npg = g_ref.shape[1]
    pages_per_block = MOBA_BLOCK // PAGE
    lane = lax.broadcasted_iota(jnp.int32, (rows, npg), 1)

    @pl.when(j == 0)
    def _():
        q32_ref[...] = _block_diag_queries(q_ref, nt)
        g_ref[...] = jnp.zeros_like(g_ref)

    g = g_ref[...]
    for u, k_ref in enumerate(k_refs):
        qk = jnp.dot(q32_ref[...], k_ref[0], preferred_element_type=F32, precision=HIGHEST)
        g = jnp.where(lane == j * pps + u, jnp.sum(qk, axis=1, keepdims=True), g)
    g_ref[...] = g

    @pl.when(j == pl.num_programs(1) - 1)
    def _():
        blk = g
        for u in range(1, pages_per_block):
            blk = blk + pltpu.roll(g, npg - u, 1)
        gs = jnp.where(lane % pages_per_block == 0, blk * (1.0 / MOBA_BLOCK), NEG)
        sel = jnp.zeros((rows, npg), F32)
        for _ in range(MOBA_TOPK):
            mx = jnp.max(gs, axis=1, keepdims=True)
            idx = jnp.min(jnp.where(gs == mx, lane, npg), axis=1, keepdims=True)
            hit = lane == idx
            sel = jnp.where(hit, 1.0, sel)
            gs = jnp.where(hit, MASKED, gs)
        out = sel
        for u in range(1, pages_per_block):
            out = out + pltpu.roll(sel, u, 1)
        sel_ref[0] = out


def moba_select_t_call(pt, q, cache_kvt):
    n, nt, _ = q.shape
    npages = pt.shape[1]
    pps = PPS if npages % PPS == 0 else 1
    rows = nt * N_HEADS
    grid_spec = pltpu.PrefetchScalarGridSpec(
        num_scalar_prefetch=1,
        grid=(n, npages // pps),
        in_specs=[pl.BlockSpec((1, nt, BRANCH_W), lambda b, j, pt: (b, 0, 0))] + _page_specs(BRANCH_W, 0, pps),
        out_specs=pl.BlockSpec((1, rows, npages), lambda b, j, pt: (b, 0, 0)),
        scratch_shapes=[pltpu.VMEM((rows, BRANCH_W), F32), pltpu.VMEM((rows, npages), F32)],
    )
    return pl.pallas_call(
        functools.partial(_dmoba_sel_t_kernel, nt=nt, pps=pps),
        grid_spec=grid_spec,
        out_shape=jax.ShapeDtypeStruct((n, rows, npages), F32),
        compiler_params=_cparams("parallel", "arbitrary"),
        name="moba_select",
    )(pt, q, *([cache_kvt] * pps))


def _dmoba_t_kernel(pt_ref, q_ref, sel_ref, *refs, nt, pps):
    del pt_ref
    kv_refs = refs[:pps]
    nkv_ref, o_ref, qbd_ref, m_ref, l_ref, acc_ref = refs[pps:]
    j = pl.program_id(1)
    nsteps = pl.num_programs(1)
    rows = nt * N_HEADS
    slope = _slope_col(rows)

    @pl.when(j == 0)
    def _():
        qbd_ref[...] = (_block_diag_queries(q_ref, nt) * SCALE).astype(BF16)
        _softmax_reset(m_ref, l_ref, acc_ref)

    def block(kvs, rel, mask):
        parts = [jnp.dot(qbd_ref[...], kv_ref[0, 0:BRANCH_W, :].astype(BF16), preferred_element_type=F32)
                 for kv_ref in kvs]
        s = parts[0] if len(parts) == 1 else jnp.concatenate(parts, axis=1)
        s = jnp.where(mask, s + slope * rel, MASKED)
        _softmax_step_t(s, [kv_ref[0, BRANCH_W:2 * BRANCH_W, :].astype(BF16) for kv_ref in kvs], m_ref, l_ref, acc_ref)

    npg = sel_ref.shape[2]
    prow = lax.broadcasted_iota(jnp.int32, (npg, pps * PAGE), 0)
    pcol = lax.broadcasted_iota(jnp.int32, (npg, pps * PAGE), 1) // PAGE
    keep = jnp.dot(sel_ref[0].astype(BF16), jnp.where(prow == pcol + j * pps, 1.0, 0.0).astype(BF16),
                   preferred_element_type=F32) > 0.5
    klane = lax.broadcasted_iota(jnp.int32, (1, pps * PAGE), 1)
    block(kv_refs, (klane + (j - nsteps) * (pps * PAGE)).astype(F32), keep)

    @pl.when(j == nsteps - 1)
    def _():
        block([nkv_ref], lax.broadcasted_iota(jnp.int32, (1, PAGE), 1).astype(F32), _new_token_mask(rows, nt))
        o_ref[0] = _extract_heads(acc_ref[...] * (1.0 / l_ref[...]), nt)


def moba_decode_t_call(pt, q, sel, cache_kvt, new_kvt):
    n, nt, _ = q.shape
    npages = pt.shape[1]
    pps = PPS if npages % PPS == 0 else 1
    rows = nt * N_HEADS
    grid_spec = pltpu.PrefetchScalarGridSpec(
        num_scalar_prefetch=1,
        grid=(n, npages // pps),
        in_specs=[pl.BlockSpec((1, nt, BRANCH_W), lambda b, j, pt: (b, 0, 0)),
                  pl.BlockSpec((1, rows, npages), lambda b, j, pt: (b, 0, 0))]
        + _page_specs(2 * BRANCH_W, 0, pps)
        + [pl.BlockSpec((1, 2 * BRANCH_W, PAGE), lambda b, j, pt: (b, 0, 0))],
        out_specs=pl.BlockSpec((1, nt, BRANCH_W), lambda b, j, pt: (b, 0, 0)),
        scratch_shapes=[pltpu.VMEM((rows, BRANCH_W), BF16), pltpu.VMEM((rows, 1), F32), pltpu.VMEM((rows, 1), F32),
                        pltpu.VMEM((rows, BRANCH_W), F32)],
    )
    return pl.pallas_call(
        functools.partial(_dmoba_t_kernel, nt=nt, pps=pps),
        grid_spec=grid_spec,
        out_shape=jax.ShapeDtypeStruct((n, nt, BRANCH_W), F32),
        compiler_params=_cparams("parallel", "arbitrary"),
        name="moba_decode",
    )(pt, q, sel, *([cache_kvt] * pps), new_kvt)


def _compress_t_kernel(pt_ref, *refs, rows, pps):
    del pt_ref
    a_refs = refs[:pps]
    pet_ref, peb_ref, wt_ref, wb_ref, w2k_ref, w2v_ref, w2vt_ref, kc_ref, vc_ref, vct_ref, rows_ref = refs[pps:]
    j = pl.program_id(1)
    pr = lax.broadcasted_iota(jnp.int32, (PAGE, PAGE), 0)
    pc = lax.broadcasted_iota(jnp.int32, (PAGE, PAGE), 1)
    perm = jnp.where(pc == CMP_S * (pr % PAGE_ROWS) + pr // PAGE_ROWS, 1.0, 0.0).astype(BF16)
    w2 = 2 * KV_W
    for u, a_ref in enumerate(a_refs):
        p3 = _dot_nt(perm, _split3(a_ref[0]))
        pg = p3[:, 0:w2] + p3[:, w2:2 * w2] + p3[:, 2 * w2:3 * w2]
        r0 = pl.multiple_of((j * pps + u) * PAGE_ROWS, PAGE_ROWS)
        for i in range(CMP_S):
            rows_ref[pl.ds(r0, PAGE_ROWS), i * w2:(i + 1) * w2] = pg[i * PAGE_ROWS:(i + 1) * PAGE_ROWS, :]

    @pl.when(j == pl.num_programs(1) - 1)
    def _():
        chunk = min(256, rows)
        prev_top = jnp.zeros((1, 4 * CMP_HID), F32)
        for c in range(rows // chunk):
            a = rows_ref[c * chunk:(c + 1) * chunk, :]
            ht = jnp.dot((a + pet_ref[...]).astype(BF16), wt_ref[...], preferred_element_type=F32)
            hb = jnp.dot((a + peb_ref[...]).astype(BF16), wb_ref[...], preferred_element_type=F32)
            row = lax.broadcasted_iota(jnp.int32, ht.shape, 0)
            shifted = jnp.where(row == 0, prev_top, pltpu.roll(ht, 1, 0))
            prev_top = ht[chunk - 1:chunk, :]
            g = _gelu_tanh(shifted + hb).astype(BF16)
            kc_ref[0, c * chunk:(c + 1) * chunk, :] = jnp.dot(
                g[:, :2 * CMP_HID], w2k_ref[...], preferred_element_type=F32).astype(BF16)
            vc_ref[0, c * chunk:(c + 1) * chunk, :] = jnp.dot(
                g[:, 2 * CMP_HID:], w2v_ref[...], preferred_element_type=F32).astype(BF16)
            vct_ref[0, :, c * chunk:(c + 1) * chunk] = _dot_nt(w2vt_ref[...], g[:, 2 * CMP_HID:]).astype(BF16)


def compress_t_call(pages_t, page_idx, pe_top, pe_bot, wt, wb, w2k, w2v, w2vt, n, npages):
    rows = npages * PAGE_ROWS
    pps = PPS if npages % PPS == 0 else 1
    const = lambda b, j, pt: (0, 0)
    grid_spec = pltpu.PrefetchScalarGridSpec(
        num_scalar_prefetch=1,
        grid=(n, npages // pps),
        in_specs=_page_specs(2 * KV_W, 0, pps)
        + [pl.BlockSpec((1, CMP_ROW), const),
           pl.BlockSpec((1, CMP_ROW), const),
           pl.BlockSpec((CMP_ROW, 4 * CMP_HID), const, pipeline_mode=pl.Buffered(1)),
           pl.BlockSpec((CMP_ROW, 4 * CMP_HID), const, pipeline_mode=pl.Buffered(1)),
           pl.BlockSpec((2 * CMP_HID, KV_W), const),
           pl.BlockSpec((2 * CMP_HID, KV_W), const),
           pl.BlockSpec((KV_W, 2 * CMP_HID), const)],
        out_specs=[pl.BlockSpec((1, rows, KV_W), lambda b, j, pt: (b, 0, 0)),
                   pl.BlockSpec((1, rows, KV_W), lambda b, j, pt: (b, 0, 0)),
                   pl.BlockSpec((1, KV_W, rows), lambda b, j, pt: (b, 0, 0))],
        scratch_shapes=[pltpu.VMEM((rows, CMP_ROW), F32)],
    )
    return pl.pallas_call(
        functools.partial(_compress_t_kernel, rows=rows, pps=pps),
        grid_spec=grid_spec,
        out_shape=[jax.ShapeDtypeStruct((n, rows, KV_W), BF16),
                   jax.ShapeDtypeStruct((n, rows, KV_W), BF16),
                   jax.ShapeDtypeStruct((n, KV_W, rows), BF16)],
        compiler_params=_cparams("parallel", "arbitrary"),
        name="nsa_compress",
    )(page_idx, *([pages_t] * pps), pe_top, pe_bot, wt, wb, w2k, w2v, w2vt)


def _dnsa_t_kernel(pt_ref, q_ref, gate_ref, kc_ref, vc_ref, *refs, nt, nsb_pad, pps):
    del pt_ref
    kv_refs = refs[:pps]
    nkv_ref, win_ref, nwin_ref, o_ref, qbd_ref, sel_ref, ocmp_ref, m_ref, l_ref, acc_ref = refs[pps:]
    j = pl.program_id(1)
    nsteps = pl.num_programs(1)
    rows = nt * N_HEADS
    ncmp = kc_ref.shape[1]
    q0 = nsteps * (pps * PAGE)
    slope = _slope_col(rows)
    klane = lax.broadcasted_iota(jnp.int32, (1, PAGE), 1)
    blocks_per_page = PAGE // SLC_BLOCK

    @pl.when(j == 0)
    def _():
        pr = lax.broadcasted_iota(jnp.int32, (BRANCH_W, KV_W), 0)
        pc = lax.broadcasted_iota(jnp.int32, (BRANCH_W, KV_W), 1)
        place = jnp.where((pr % HEAD_DIM == pc % HEAD_DIM) & (pc // HEAD_DIM == pr // (NSA_R * HEAD_DIM)), 1.0, 0.0)
        qg = jnp.dot(_block_diag_queries(q_ref, nt), place, preferred_element_type=F32, precision=HIGHEST)
        qbd = (qg * SCALE).astype(BF16)
        qbd_ref[...] = qbd
        c = lax.broadcasted_iota(jnp.int32, (1, ncmp), 1)
        s = _dot_nt(qbd, kc_ref[0]) + slope * (CMP_S * c + (CMP_S - 1) - q0).astype(F32)
        valid = c >= 1
        s = jnp.where(valid, s, MASKED)
        p = jnp.where(valid, jnp.exp(s - jnp.max(s, axis=1, keepdims=True)), 0.0)
        p = p * (1.0 / jnp.sum(p, axis=1, keepdims=True))
        ocmp_ref[...] = gate_ref[0, :, 0:1] * jnp.dot(p.astype(BF16), vc_ref[0], preferred_element_type=F32)
        gr = lax.broadcasted_iota(jnp.int32, (nt * NSA_G, rows), 0)
        gc = lax.broadcasted_iota(jnp.int32, (nt * NSA_G, rows), 1) // NSA_R
        imp = jnp.dot(jnp.where(gr == gc, 1.0, 0.0), p, preferred_element_type=F32, precision=HIGHEST)
        ratio = SLC_BLOCK // CMP_S
        mr = lax.broadcasted_iota(jnp.int32, (ncmp, nsb_pad), 0)
        mb = lax.broadcasted_iota(jnp.int32, (ncmp, nsb_pad), 1)
        gather_m = jnp.where((mr >= ratio * mb) & (mr <= ratio * mb + ratio) & (mr >= 1), 1.0, 0.0)
        p_slc = jnp.dot(imp, gather_m, preferred_element_type=F32, precision=HIGHEST)
        bj = lax.broadcasted_iota(jnp.int32, p_slc.shape, 1)
        own = q0 // SLC_BLOCK
        forced = (bj == 0) | (bj >= own - 1)
        allowed = bj <= own
        score = jnp.where(allowed, jnp.where(forced, p_slc + FORCE_SCORE, p_slc), -1.0)
        sel = jnp.zeros(p_slc.shape, F32)
        for _ in range(SLC_N):
            mx = jnp.max(score, axis=1, keepdims=True)
            idx = jnp.min(jnp.where(score == mx, bj, nsb_pad), axis=1, keepdims=True)
            hit = bj == idx
            sel = jnp.where(hit, 1.0, sel)
            score = jnp.where(hit, -2.0, score)
        sel = jnp.where(allowed, sel, 0.0)
        er = lax.broadcasted_iota(jnp.int32, (rows, nt * NSA_G), 0) // NSA_R
        ec = lax.broadcasted_iota(jnp.int32, (rows, nt * NSA_G), 1)
        sel_ref[...] = jnp.dot(jnp.where(er == ec, 1.0, 0.0), sel, preferred_element_type=F32).astype(BF16)
        _softmax_reset(m_ref, l_ref, acc_ref)

    def block(kvs, rel, mask):
        parts = [jnp.dot(qbd_ref[...], kv_ref[0, 0:KV_W, :].astype(BF16), preferred_element_type=F32) for kv_ref in kvs]
        s = (parts[0] if len(parts) == 1 else jnp.concatenate(parts, axis=1)) + slope * rel
        if mask is not None:
            s = jnp.where(mask, s, MASKED)
        _softmax_step_t(s, [kv_ref[0, KV_W:2 * KV_W, :].astype(BF16) for kv_ref in kvs], m_ref, l_ref, acc_ref)

    brow = lax.broadcasted_iota(jnp.int32, (nsb_pad, pps * PAGE), 0)
    bcol = lax.broadcasted_iota(jnp.int32, (nsb_pad, pps * PAGE), 1) // SLC_BLOCK
    expand = jnp.where(brow == bcol + j * (pps * blocks_per_page), 1.0, 0.0).astype(BF16)
    keep = jnp.dot(sel_ref[...], expand, preferred_element_type=F32) > 0.5
    kl = lax.broadcasted_iota(jnp.int32, (1, pps * PAGE), 1)
    block(kv_refs, (kl + (j - nsteps) * (pps * PAGE)).astype(F32), keep)

    @pl.when(j == nsteps - 1)
    def _():
        new_mask = _new_token_mask(rows, nt)
        block([nkv_ref], klane.astype(F32), new_mask)
        o_slc = acc_ref[...] * (1.0 / l_ref[...])
        _softmax_reset(m_ref, l_ref, acc_ref)
        wlen = win_ref.shape[2]
        wi = lax.broadcasted_iota(jnp.int32, (rows, wlen), 1)
        wt = lax.broadcasted_iota(jnp.int32, (rows, wlen), 0) // N_HEADS
        wrel = (lax.broadcasted_iota(jnp.int32, (1, wlen), 1) - wlen).astype(F32)
        block([win_ref], wrel, wi > wt + (wlen - WINDOW))
        block([nwin_ref], klane.astype(F32), new_mask)
        o_win = acc_ref[...] * (1.0 / l_ref[...])
        o = ocmp_ref[...] + gate_ref[0, :, 1:2] * o_slc + gate_ref[0, :, 2:3] * o_win
        ur = lax.broadcasted_iota(jnp.int32, (KV_W, BRANCH_W), 0)
        uc = lax.broadcasted_iota(jnp.int32, (KV_W, BRANCH_W), 1)
        unplace = jnp.where((ur % HEAD_DIM == uc % HEAD_DIM) & (ur // HEAD_DIM == uc // (NSA_R * HEAD_DIM)), 1.0, 0.0)
        o_ref[0] = _extract_heads(jnp.dot(o, unplace, preferred_element_type=F32, precision=HIGHEST), nt)


def nsa_decode_t_call(pt, q, gate, kcmp, vcmp, cache_kvt, new_kvt, win_t, new_win_t):
    n, nt, _ = q.shape
    npages = pt.shape[1]
    pps = PPS if npages % PPS == 0 else 1
    rows = nt * N_HEADS
    ncmp = kcmp.shape[1]
    nsb = (npages * PAGE) // SLC_BLOCK + 1
    nsb_pad = -(-nsb // 128) * 128
    wlen = win_t.shape[2]
    grid_spec = pltpu.PrefetchScalarGridSpec(
        num_scalar_prefetch=1,
        grid=(n, npages // pps),
        in_specs=[pl.BlockSpec((1, nt, BRANCH_W), lambda b, j, pt: (b, 0, 0)),
                  pl.BlockSpec((1, rows, 3), lambda b, j, pt: (b, 0, 0)),
                  pl.BlockSpec((1, ncmp, KV_W), lambda b, j, pt: (b, 0, 0)),
                  pl.BlockSpec((1, ncmp, KV_W), lambda b, j, pt: (b, 0, 0))]
        + _page_specs(2 * KV_W, 1, pps)
        + [pl.BlockSpec((1, 2 * KV_W, PAGE), lambda b, j, pt: (b, 0, 0)),
           pl.BlockSpec((1, 2 * KV_W, wlen), lambda b, j, pt: (b, 0, 0)),
           pl.BlockSpec((1, 2 * KV_W, PAGE), lambda b, j, pt: (b, 0, 0))],
        out_specs=pl.BlockSpec((1, nt, BRANCH_W), lambda b, j, pt: (b, 0, 0)),
        scratch_shapes=[pltpu.VMEM((rows, KV_W), BF16), pltpu.VMEM((rows, nsb_pad), BF16),
                        pltpu.VMEM((rows, KV_W), F32), pltpu.VMEM((rows, 1), F32), pltpu.VMEM((rows, 1), F32),
                        pltpu.VMEM((rows, KV_W), F32)],
    )
    return pl.pallas_call(
        functools.partial(_dnsa_t_kernel, nt=nt, nsb_pad=nsb_pad, pps=pps),
        grid_spec=grid_spec,
        out_shape=jax.ShapeDtypeStruct((n, nt, BRANCH_W), F32),
        compiler_params=_cparams("parallel", "arbitrary"),
        name="nsa_decode",
    )(pt, q, gate, kcmp, vcmp, *([cache_kvt] * pps), new_kvt, win_t, new_win_t)


def _compress_weights(cmp_w1, cmp_w2, cmp_pe):
    w1 = cmp_w1.reshape(2, CMP_L, HEAD_DIM, CMP_HID)
    eye_w = jnp.eye(2, dtype=F32)
    eye_g = jnp.eye(NSA_G, dtype=F32)
    big = jnp.einsum('widh,sw,gk->isgdwkh', w1, eye_w, eye_g)
    big = big.reshape(CMP_L, 2 * KV_W, 4 * CMP_HID)
    wt = big[:CMP_S].reshape(CMP_ROW, 4 * CMP_HID).astype(BF16)
    wb = big[CMP_S:].reshape(CMP_ROW, 4 * CMP_HID).astype(BF16)
    w2k = jnp.einsum('hd,gk->ghkd', cmp_w2[0], eye_g).reshape(2 * CMP_HID, KV_W).astype(BF16)
    w2v = jnp.einsum('hd,gk->ghkd', cmp_w2[1], eye_g).reshape(2 * CMP_HID, KV_W).astype(BF16)
    w2vt = jnp.einsum('hd,gk->kdgh', cmp_w2[1], eye_g).reshape(KV_W, 2 * CMP_HID).astype(BF16)
    pe = jnp.broadcast_to(cmp_pe[:, :, None, :], (2, CMP_L, NSA_G, HEAD_DIM))
    pe = jnp.transpose(pe, (1, 0, 2, 3)).reshape(CMP_L, 2 * KV_W)
    pe_top = pe[:CMP_S].reshape(1, CMP_ROW)
    pe_bot = pe[CMP_S:].reshape(1, CMP_ROW)
    return pe_top, pe_bot, wt, wb, w2k, w2v, w2vt


def _layer_weights(l, w_in, w_branch, w_out, w_up, w_down):
    wl = w_in[l]
    small = jnp.concatenate(
        [wl[:, OFF_FOX_F:OFF_FOX_F + N_HEADS], wl[:, OFF_NSA_GATE:OFF_NSA_GATE + 3 * N_HEADS],
         jnp.zeros((D_MODEL, SMALL_W - 4 * N_HEADS), wl.dtype)], axis=1)
    return dict(
        conv=wl[:, OFF_CONV:OFF_FOX].astype(BF16),
        fox=wl[:, OFF_FOX:OFF_FOX_F].astype(BF16),
        moba=wl[:, OFF_MOBA:OFF_NSA].astype(BF16),
        nsa=wl[:, OFF_NSA:OFF_NSA_GATE].astype(BF16),
        small=small.astype(BF16),
        gate=wl[:, OFF_MERGE:].astype(BF16),
        branch=w_branch[l].astype(BF16),
        out=w_out[l].astype(BF16),
        up=w_up[l].astype(BF16),
        down=w_down[l].astype(BF16),
    )


def _finish_layer(x, branches, gate, w, g_mlp, g_next, next_dtype, tm):
    merged = merge_call(branches, gate, w["branch"], tm, 512)
    x1, hm = outproj_call(merged, w["out"], x, g_mlp, tm)
    return mlp_call(hm, w["up"], w["down"], x1, g_next, next_dtype, tm, 512)


def prompt_mixers(h, n, t, w, b_forget, conv_w, cmp_w, tm):
    bw = BRANCH_W
    (z_conv,) = proj_call(h, w["conv"], [("f32", 3 * bw)], [(a, a + 512, ((0, a),)) for a in (0, 512, 1024)],
                          tm, "proj_conv")
    qkv_defs = [("f32", bw), ("f32", 2 * bw), ("bf16", bw), ("bf16T", bw)]
    qkv_plan = [(0, bw, ((0, 0),)), (bw, 2 * bw, ((1, 0), (2, 0))), (2 * bw, 3 * bw, ((1, bw), (3, 0)))]
    fox_q, fox_kv, fox_kb, fox_vt = proj_call(h, w["fox"], qkv_defs, qkv_plan, tm, "proj_fox")
    moba_q, moba_kv, moba_kb, moba_vt, moba_km = proj_call(
        h, w["moba"], qkv_defs + [("blockmean", bw)],
        [qkv_plan[0], (bw, 2 * bw, ((1, 0), (2, 0), (4, 0))), qkv_plan[2]], tm, "proj_moba")
    kv = KV_W
    nsa_q, nsa_kv, nsa_win, nsa_ks, nsa_vst, nsa_kw, nsa_vwt, nsa_cmp_t = proj_call(
        h, w["nsa"],
        [("f32", bw), ("f32", 4 * kv), ("f32", 2 * kv), ("bf16", kv), ("bf16T", kv), ("bf16", kv), ("bf16T", kv),
         ("f32T", 2 * kv)],
        [(0, bw, ((0, 0),)), (bw, bw + 2 * kv, ((1, 0), (7, 0))),
         (bw + 2 * kv, bw + 3 * kv, ((1, 2 * kv), (3, 0))), (bw + 3 * kv, bw + 4 * kv, ((1, 3 * kv), (4, 0))),
         (bw + 4 * kv, bw + 5 * kv, ((2, 0), (5, 0))), (bw + 5 * kv, bw + 6 * kv, ((2, kv), (6, 0)))],
        tm, "proj_nsa")
    (z_small,) = proj_call(h, w["small"], [("f32", SMALL_W)], [(0, SMALL_W, ((0, 0),))], tm, "proj_small")

    out_a, new_conv = conv_prompt_call(z_conv, conv_w, n, t, tm)
    lg, fk = small_call(z_small, b_forget, n, t, tm)
    logf = lg[:, :N_HEADS].reshape(n, t, N_HEADS)
    gate_t = lg[:, N_HEADS:4 * N_HEADS].T

    out_b = fox_prompt_call(fox_q, fox_kb, fox_vt, fk, n, t)
    out_c = moba_prompt_call(moba_q, moba_kb, moba_vt, moba_km, n, t)

    npages = t // PAGE
    page_idx = jnp.arange(n * npages, dtype=jnp.int32).reshape(n, npages)
    kcmp, _, vcmpt = compress_t_call(nsa_cmp_t, page_idx, *cmp_w, n, npages)
    out_d = nsa_prompt_call(nsa_q, kcmp, vcmpt, nsa_ks, nsa_vst, nsa_kw, nsa_vwt, gate_t, n, t)

    wb = min(WINDOW, t)
    new_state = (new_conv,
                 fox_kv.reshape(n, t, 2, N_HEADS, HEAD_DIM),
                 logf,
                 moba_kv.reshape(n, t, 2, N_HEADS, HEAD_DIM),
                 nsa_kv.reshape(n, t, 2, 2, NSA_G, HEAD_DIM),
                 nsa_win.reshape(n, t, 2, NSA_G, HEAD_DIM)[:, t - wb:])
    return [out_a, out_b, out_c, out_d], new_state


def sample_mixers(h, n, t, q0, past, w, b_forget, conv_w, cmp_w1, cmp_w2, cmp_pe, tm):
    dt = F32
    bw = BRANCH_W
    kv = KV_W
    (z_conv,) = proj_call(h, w["conv"], [("f32", 3 * bw)], [(0, 3 * bw, ((0, 0),))], tm, "proj_conv_s")
    fox_q, fox_kv = proj_call(h, w["fox"], [("f32", bw), ("f32", 2 * bw)],
                              [(0, bw, ((0, 0),)), (bw, 3 * bw, ((1, 0),))], tm, "proj_fox_s")
    moba_q, moba_kv = proj_call(h, w["moba"], [("f32", bw), ("f32", 2 * bw)],
                                [(0, bw, ((0, 0),)), (bw, 3 * bw, ((1, 0),))], tm, "proj_moba_s")
    nsa_q, nsa_kv, nsa_win = proj_call(
        h, w["nsa"], [("f32", bw), ("f32", 4 * kv), ("f32", 2 * kv)],
        [(0, bw, ((0, 0),)), (bw, bw + 4 * kv, ((1, 0),)), (bw + 4 * kv, bw + 6 * kv, ((2, 0),))], tm, "proj_nsa_s")
    (z_small,) = proj_call(h, w["small"], [("f32", SMALL_W)], [(0, SMALL_W, ((0, 0),))], tm, "proj_small_s")

    def heads(a, nh):
        return a.reshape(n, t, nh, HEAD_DIM)

    def cat(old, new):
        return jnp.concatenate([old.astype(new.dtype), new], axis=1)

    zc = z_conv.reshape(n, t, 3 * bw)
    conv_x, conv_b, conv_c = zc[..., :bw], zc[..., bw:2 * bw], zc[..., 2 * bw:]
    u = conv_c * conv_x
    ext = jnp.concatenate([past['conv'].astype(dt), u], axis=1)
    y_conv = ext[:, 0:t] * conv_w[0]
    for j in range(1, CONV_W):
        y_conv = y_conv + ext[:, j:j + t] * conv_w[j]
    out_a = conv_b * y_conv
    new_conv = ext[:, -(CONV_W - 1):]

    fkv = fox_kv.reshape(n, t, 2, N_HEADS, HEAD_DIM)
    fq, fk, fv = heads(fox_q, N_HEADS), fkv[:, :, 0], fkv[:, :, 1]
    fox_f = z_small.reshape(n, t, SMALL_W)[..., :N_HEADS]
    logf = jax.nn.log_sigmoid(fox_f + b_forget)
    lf_all = jnp.concatenate([past['fox_logf'].astype(F32), logf], axis=1)
    out_b = fox_attention(fq, cat(past['fox_k'], fk), cat(past['fox_v'], fv), lf_all, q0)

    mkv = moba_kv.reshape(n, t, 2, N_HEADS, HEAD_DIM)
    mq, mk, mv = heads(moba_q, N_HEADS), mkv[:, :, 0], mkv[:, :, 1]
    out_c = moba_attention(mq, cat(past['moba_k'], mk), cat(past['moba_v'], mv), q0, alibi_slopes(N_HEADS))

    nq = heads(nsa_q, N_HEADS)
    nkv = nsa_kv.reshape(n, t, 2, 2, NSA_G, HEAD_DIM)
    kc, vc, ks, vs = nkv[:, :, 0, 0], nkv[:, :, 0, 1], nkv[:, :, 1, 0], nkv[:, :, 1, 1]
    nwin = nsa_win.reshape(n, t, 2, NSA_G, HEAD_DIM)
    kw, vw = nwin[:, :, 0], nwin[:, :, 1]
    ngate = jax.nn.sigmoid(z_small.reshape(n, t, SMALL_W)[..., N_HEADS:4 * N_HEADS]).reshape(n, t, NSA_G, NSA_R, 3)
    kw_all = cat(past['win_k'], kw)
    vw_all = cat(past['win_v'], vw)
    w_ofs = q0 - past['win_k'].shape[1]
    out_d = nsa_attention(nq, cat(past['nsa_kc'], kc), cat(past['nsa_vc'], vc),
                          cat(past['nsa_ks'], ks), cat(past['nsa_vs'], vs),
                          kw_all, vw_all, ngate, q0, w_ofs, alibi_slopes(N_HEADS), cmp_w1, cmp_w2, cmp_pe)
    branches = [a.reshape(n * t, bw) for a in (out_a, out_b, out_c, out_d)]
    wb = past['win_k'].shape[1]
    new_state = (new_conv, fkv, logf.astype(dt), mkv, nkv,
                 jnp.stack([kw_all, vw_all], axis=2)[:, -wb:])
    return branches, new_state


def decode_mixers(h, n, nt, pt, caches, state_conv, state_win, w, b_forget, conv_w, cmp_w, tm):
    bw = BRANCH_W
    kv = KV_W
    cache_fox, cache_lf, cache_moba, cache_nsa = caches
    (z_conv,) = proj_call(h, w["conv"], [("f32", 3 * bw)], [(0, 3 * bw, ((0, 0),))], tm, "proj_conv_s")
    fox_q, fox_kv = proj_call(h, w["fox"], [("f32", bw), ("f32", 2 * bw)],
                              [(0, bw, ((0, 0),)), (bw, 3 * bw, ((1, 0),))], tm, "proj_fox_s")
    moba_q, moba_kv = proj_call(h, w["moba"], [("f32", bw), ("f32", 2 * bw)],
                                [(0, bw, ((0, 0),)), (bw, 3 * bw, ((1, 0),))], tm, "proj_moba_s")
    nsa_q, nsa_kv, nsa_win = proj_call(
        h, w["nsa"], [("f32", bw), ("f32", 4 * kv), ("f32", 2 * kv)],
        [(0, bw, ((0, 0),)), (bw, bw + 4 * kv, ((1, 0),)), (bw + 4 * kv, bw + 6 * kv, ((2, 0),))], tm, "proj_nsa_s")
    (z_small,) = proj_call(h, w["small"], [("f32", SMALL_W)], [(0, SMALL_W, ((0, 0),))], tm, "proj_small_s")

    def new_page(a):
        at = jnp.transpose(a.reshape(n, nt, a.shape[-1]), (0, 2, 1))
        return jnp.pad(at, ((0, 0), (0, 0), (0, PAGE - nt)))

    out_a, u = conv_sample_call(z_conv, state_conv, conv_w, n, nt)
    new_conv = u.reshape(n, nt, bw)[:, nt - (CONV_W - 1):]

    lg, _ = small_call(z_small, b_forget, 1, n * nt, n * nt)
    logf = lg[:, :N_HEADS]
    gate = lg[:, N_HEADS:4 * N_HEADS].reshape(n, nt * N_HEADS, 3)

    out_b = fox_decode_t_call(pt, fox_q.reshape(n, nt, bw), cache_fox, cache_lf, new_page(fox_kv), new_page(logf))
    mq = moba_q.reshape(n, nt, bw)
    sel = moba_select_t_call(pt, mq, cache_moba)
    out_c = moba_decode_t_call(pt, mq, sel, cache_moba, new_page(moba_kv))

    npages = pt.shape[1]
    kcmp, vcmp, _ = compress_t_call(cache_nsa, pt, *cmp_w, n, npages)
    win = state_win.reshape(n, state_win.shape[1], 2 * kv)
    out_d = nsa_decode_t_call(pt, nsa_q.reshape(n, nt, bw), gate, kcmp, vcmp, cache_nsa,
                              new_page(nsa_kv[:, 2 * kv:]), jnp.transpose(win, (0, 2, 1)), new_page(nsa_win))

    wb = win.shape[1]
    win_all = jnp.concatenate([win, nsa_win.reshape(n, nt, 2 * kv)], axis=1)[:, nt:]
    new_state = (new_conv,
                 fox_kv.reshape(n, nt, 2, N_HEADS, HEAD_DIM),
                 logf.reshape(n, nt, N_HEADS),
                 moba_kv.reshape(n, nt, 2, N_HEADS, HEAD_DIM),
                 nsa_kv.reshape(n, nt, 2, 2, NSA_G, HEAD_DIM),
                 win_all.reshape(n, wb, 2, NSA_G, HEAD_DIM))
    branches = [out_a] + [o.reshape(n * nt, bw) for o in (out_b, out_c, out_d)]
    return branches, new_state


def kernel(x_prompt, x_sample, state_conv, cache_fox_kv, cache_fox_logf, cache_moba_kv, cache_nsa_kv,
           state_nsa_win, page_table, g_mix, w_in, b_forget, conv_w, cmp_w1, cmp_w2, cmp_pe,
           w_branch, w_out, g_mlp, w_up, w_down, g_final):
    depth = w_in.shape[0]
    nb, seq, _ = x_prompt.shape
    db, dseq, _ = x_sample.shape
    tm_p, tm_s = 512, db * dseq
    pool, page = cache_fox_kv.shape[1], cache_fox_kv.shape[2]
    def pages_t(c, width):
        return jnp.transpose(c.reshape(depth * pool, page, width), (0, 2, 1))

    caches = (pages_t(cache_fox_kv, 2 * BRANCH_W), pages_t(cache_fox_logf, N_HEADS),
              pages_t(cache_moba_kv, 2 * BRANCH_W), pages_t(cache_nsa_kv, 4 * KV_W))

    xp = x_prompt.reshape(nb * seq, D_MODEL)
    xs = x_sample.reshape(db * dseq, D_MODEL)
    hp = rms_norm_call(xp, g_mix[0], BF16, tm_p)
    hs = rms_norm_call(xs, g_mix[0], BF16, tm_s)
    new_p, new_s = [], []
    for l in range(depth):
        w = _layer_weights(l, w_in, w_branch, w_out, w_up, w_down)
        cmp_w = _compress_weights(cmp_w1[l], cmp_w2[l], cmp_pe[l])
        last = l == depth - 1
        g_next = g_final if last else g_mix[l + 1]
        next_dtype = F32 if last else BF16

        branches, st_p = prompt_mixers(hp, nb, seq, w, b_forget[l], conv_w[l], cmp_w, tm_p)
        gate_p = gate_call(hp, w["gate"], tm_p, 1024)
        xp, hp = _finish_layer(xp, branches, gate_p, w, g_mlp[l], g_next, next_dtype, tm_p)
        new_p.append(st_p)

        branches, st_s = decode_mixers(hs, db, dseq, page_table + l * pool, caches, state_conv[l],
                                       state_nsa_win[l], w, b_forget[l], conv_w[l], cmp_w, tm_s)
        gate_s = gate_call(hs, w["gate"], tm_s, 1024)
        xs, hs = _finish_layer(xs, branches, gate_s, w, g_mlp[l], g_next, next_dtype, tm_s)
        new_s.append(st_s)
    y_prompt = hp.reshape(nb, seq, D_MODEL)
    y_sample = hs.reshape(db, dseq, D_MODEL)
    conv_p, fox_kv_p, fox_logf_p, moba_kv_p, nsa_kv_p, win_p = [jnp.stack(a) for a in zip(*new_p)]
    conv_s, fox_kv_s, fox_logf_s, moba_kv_s, nsa_kv_s, win_s = [jnp.stack(a) for a in zip(*new_s)]
    return (y_prompt, y_sample, conv_p, conv_s, fox_kv_p, fox_kv_s, fox_logf_p, fox_logf_s,
            moba_kv_p, moba_kv_s, nsa_kv_p, nsa_kv_s, win_p, win_s)
```

```python
import functools

import jax
import jax.numpy as jnp
from jax import lax
from jax.experimental import pallas as pl
from jax.experimental.pallas import tpu as pltpu

F32 = jnp.float32
BF16 = jnp.bfloat16
HIGHEST = lax.Precision.HIGHEST

D_MODEL = 2048
HEAD_DIM = 64
N_BRANCH = 4
BRANCH_W = D_MODEL // N_BRANCH
N_HEADS = BRANCH_W // HEAD_DIM
CONV_W = 3
NSA_G = 2
NSA_R = N_HEADS // NSA_G
MOBA_BLOCK = 256
MOBA_TOPK = 3
CMP_L = 32
CMP_S = 16
CMP_HID = 4 * HEAD_DIM
SLC_BLOCK = 64
SLC_N = 16
WINDOW = 512
D_FF = 4 * D_MODEL
Q_BLOCK = 128
SPARSE_Q_BLOCK = 32
RMS_EPS = 1e-6
NEG = -1e30
MASKED = 2.0 * NEG
FORCE_SCORE = 1e4
KV_W = NSA_G * HEAD_DIM
SCALE = HEAD_DIM ** -0.5
PAIR_W = 2 * HEAD_DIM

OFF_CONV = 0
OFF_FOX = 3 * BRANCH_W
OFF_FOX_F = OFF_FOX + 3 * BRANCH_W
OFF_MOBA = OFF_FOX_F + N_HEADS
OFF_NSA = OFF_MOBA + 3 * BRANCH_W
OFF_NSA_GATE = OFF_NSA + BRANCH_W + 6 * KV_W
OFF_MERGE = OFF_NSA_GATE + 3 * N_HEADS
IN_W = OFF_MERGE + N_BRANCH * D_MODEL
SMALL_W = 128

TQ = 256
TK = 256
CMP_ROW = CMP_S * 2 * KV_W
PAGE_ROWS = 8

VMEM_LIMIT = 56 * 1024 * 1024


def _cparams(*sem):
    return pltpu.CompilerParams(dimension_semantics=sem, vmem_limit_bytes=VMEM_LIMIT)


def _rms(x, g):
    return x * lax.rsqrt(jnp.mean(x * x, axis=-1, keepdims=True) + RMS_EPS) * g


def _dot_nt(a, b, precision=None):
    return lax.dot_general(a, b, (((1,), (1,)), ((), ())), preferred_element_type=F32, precision=precision)


def _norm_kernel(x_ref, g_ref, o_ref):
    o_ref[...] = _rms(x_ref[...], g_ref[...]).astype(o_ref.dtype)


def rms_norm_call(x, g, out_dtype, tm):
    t, d = x.shape
    return pl.pallas_call(
        _norm_kernel,
        grid=(t // tm,),
        in_specs=[pl.BlockSpec((tm, d), lambda i: (i, 0)),
                  pl.BlockSpec((1, d), lambda i: (0, 0))],
        out_specs=pl.BlockSpec((tm, d), lambda i: (i, 0)),
        out_shape=jax.ShapeDtypeStruct((t, d), out_dtype),
        compiler_params=_cparams("parallel"),
        name="rms_norm",
    )(x, g.reshape(1, d))


def _proj_kernel(h_ref, w_ref, *out_refs, kinds, plan):
    h = h_ref[...]
    for c0, c1, dests in plan:
        z = jnp.dot(h, w_ref[:, c0:c1], preferred_element_type=F32)
        for idx, off in dests:
            o_ref, kind = out_refs[idx], kinds[idx]
            if kind == "bf16T":
                for r in range(z.shape[0] // TK):
                    o_ref[r, off:off + c1 - c0, :] = z[r * TK:(r + 1) * TK].T.astype(BF16)
            elif kind == "f32T":
                for r in range(z.shape[0] // 128):
                    o_ref[r, off:off + c1 - c0, :] = z[r * 128:(r + 1) * 128].T
            elif kind == "blockmean":
                for r in range(z.shape[0] // MOBA_BLOCK):
                    o_ref[r, :, off:off + c1 - c0] = jnp.mean(
                        z[r * MOBA_BLOCK:(r + 1) * MOBA_BLOCK], axis=0, keepdims=True)
            else:
                o_ref[:, off:off + c1 - c0] = z.astype(o_ref.dtype)


def proj_call(h, w, out_defs, plan, tm, name):
    t, d = h.shape
    n = w.shape[1]
    out_specs, out_shapes = [], []
    for kind, width in out_defs:
        if kind == "bf16T":
            out_specs.append(pl.BlockSpec((tm // TK, width, TK), lambda i: (i, 0, 0)))
            out_shapes.append(jax.ShapeDtypeStruct((t // TK, width, TK), BF16))
        elif kind == "f32T":
            out_specs.append(pl.BlockSpec((tm // 128, width, 128), lambda i: (i, 0, 0)))
            out_shapes.append(jax.ShapeDtypeStruct((t // 128, width, 128), F32))
        elif kind == "blockmean":
            out_specs.append(pl.BlockSpec((tm // MOBA_BLOCK, 1, width), lambda i: (i, 0, 0)))
            out_shapes.append(jax.ShapeDtypeStruct((t // MOBA_BLOCK, 1, width), F32))
        else:
            out_specs.append(pl.BlockSpec((tm, width), lambda i: (i, 0)))
            out_shapes.append(jax.ShapeDtypeStruct((t, width), BF16 if kind == "bf16" else F32))
    return pl.pallas_call(
        functools.partial(_proj_kernel, kinds=tuple(k for k, _ in out_defs), plan=tuple(plan)),
        grid=(t // tm,),
        in_specs=[pl.BlockSpec((tm, d), lambda i: (i, 0)),
                  pl.BlockSpec((d, n), lambda i: (0, 0))],
        out_specs=out_specs,
        out_shape=out_shapes,
        compiler_params=_cparams("parallel"),
        name=name,
    )(h, w)


def _gate_kernel(h_ref, w_ref, o_ref):
    z = jnp.dot(h_ref[...], w_ref[...], preferred_element_type=F32)
    o_ref[...] = jax.nn.sigmoid(z)


def gate_call(h, w, tm, tn):
    t, d = h.shape
    n = w.shape[1]
    return pl.pallas_call(
        _gate_kernel,
        grid=(t // tm, n // tn),
        in_specs=[pl.BlockSpec((tm, d), lambda i, j: (i, 0)),
                  pl.BlockSpec((d, tn), lambda i, j: (0, j))],
        out_specs=pl.BlockSpec((tm, tn), lambda i, j: (i, j)),
        out_shape=jax.ShapeDtypeStruct((t, n), F32),
        compiler_params=_cparams("parallel", "arbitrary"),
        name="merge_gate_proj",
    )(h, w)


def _merge_kernel(oa_ref, ob_ref, oc_ref, od_ref, g0_ref, g1_ref, g2_ref, g3_ref, wb_ref, o_ref):
    acc = None
    for b, (o, g) in enumerate(zip((oa_ref, ob_ref, oc_ref, od_ref), (g0_ref, g1_ref, g2_ref, g3_ref))):
        br = jnp.dot(o[...].astype(BF16), wb_ref[b], preferred_element_type=F32)
        term = g[...] * br
        acc = term if acc is None else acc + term
    o_ref[...] = acc.astype(o_ref.dtype)


def merge_call(branches, gate, w_branch, tm, tn):
    t = branches[0].shape[0]
    nj = D_MODEL // tn
    gate_specs = [pl.BlockSpec((tm, tn), functools.partial(lambda i, j, b: (i, b * nj + j), b=b))
                  for b in range(N_BRANCH)]
    return pl.pallas_call(
        _merge_kernel,
        grid=(t // tm, nj),
        in_specs=[pl.BlockSpec((tm, BRANCH_W), lambda i, j: (i, 0))] * N_BRANCH + gate_specs
        + [pl.BlockSpec((N_BRANCH, BRANCH_W, tn), lambda i, j: (0, 0, j))],
        out_specs=pl.BlockSpec((tm, tn), lambda i, j: (i, j)),
        out_shape=jax.ShapeDtypeStruct((t, D_MODEL), BF16),
        compiler_params=_cparams("parallel", "arbitrary"),
        name="branch_merge",
    )(*branches, gate, gate, gate, gate, w_branch)


def _outproj_kernel(m_ref, w_ref, x_ref, g_ref, xo_ref, hn_ref):
    xn = x_ref[...] + jnp.dot(m_ref[...], w_ref[...], preferred_element_type=F32)
    xo_ref[...] = xn
    hn_ref[...] = _rms(xn, g_ref[...]).astype(hn_ref.dtype)


def outproj_call(merged, w_out, x, g_next, tm):
    t = x.shape[0]
    return pl.pallas_call(
        _outproj_kernel,
        grid=(t // tm,),
        in_specs=[pl.BlockSpec((tm, D_MODEL), lambda i: (i, 0)),
                  pl.BlockSpec((D_MODEL, D_MODEL), lambda i: (0, 0)),
                  pl.BlockSpec((tm, D_MODEL), lambda i: (i, 0)),
                  pl.BlockSpec((1, D_MODEL), lambda i: (0, 0))],
        out_specs=[pl.BlockSpec((tm, D_MODEL), lambda i: (i, 0)),
                   pl.BlockSpec((tm, D_MODEL), lambda i: (i, 0))],
        out_shape=[jax.ShapeDtypeStruct((t, D_MODEL), F32),
                   jax.ShapeDtypeStruct((t, D_MODEL), BF16)],
        compiler_params=_cparams("parallel"),
        name="out_proj",
    )(merged, w_out, x, g_next.reshape(1, D_MODEL))


def _mlp_kernel(h_ref, wu_ref, wd_ref, x_ref, g_ref, xo_ref, hn_ref, acc_ref):
    j = pl.program_id(1)

    @pl.when(j == 0)
    def _():
        acc_ref[...] = jnp.zeros_like(acc_ref)

    a = jnp.dot(h_ref[...], wu_ref[...], preferred_element_type=F32)
    a = jnp.square(jnp.maximum(a, 0.0)).astype(BF16)
    acc_ref[...] += jnp.dot(a, wd_ref[...], preferred_element_type=F32)

    @pl.when(j == pl.num_programs(1) - 1)
    def _():
        xn = x_ref[...] + acc_ref[...]
        xo_ref[...] = xn
        hn_ref[...] = _rms(xn, g_ref[...]).astype(hn_ref.dtype)


def mlp_call(h, w_up, w_down, x, g_next, next_dtype, tm, tf):
    t = x.shape[0]
    return pl.pallas_call(
        _mlp_kernel,
        grid=(t // tm, D_FF // tf),
        in_specs=[pl.BlockSpec((tm, D_MODEL), lambda i, j: (i, 0)),
                  pl.BlockSpec((D_MODEL, tf), lambda i, j: (0, j)),
                  pl.BlockSpec((tf, D_MODEL), lambda i, j: (j, 0)),
                  pl.BlockSpec((tm, D_MODEL), lambda i, j: (i, 0)),
                  pl.BlockSpec((1, D_MODEL), lambda i, j: (0, 0))],
        out_specs=[pl.BlockSpec((tm, D_MODEL), lambda i, j: (i, 0)),
                   pl.BlockSpec((tm, D_MODEL), lambda i, j: (i, 0))],
        out_shape=[jax.ShapeDtypeStruct((t, D_MODEL), F32),
                   jax.ShapeDtypeStruct((t, D_MODEL), next_dtype)],
        scratch_shapes=[pltpu.VMEM((tm, D_MODEL), F32)],
        compiler_params=_cparams("parallel", "arbitrary"),
        name="mlp",
    )(h, w_up, w_down, x, g_next.reshape(1, D_MODEL))


def _small_kernel(z_ref, b_ref, a_ref, f_ref, carry_ref):
    @pl.when(pl.program_id(1) == 0)
    def _():
        carry_ref[...] = jnp.zeros_like(carry_ref)

    z = z_ref[...]
    tm = z.shape[0]
    lane = lax.broadcasted_iota(jnp.int32, z.shape, 1)
    pre = z + b_ref[...]
    lf = jnp.minimum(pre, 0.0) - jnp.log1p(jnp.exp(-jnp.abs(pre)))
    lf = jnp.where(lane < N_HEADS, lf, 0.0)
    a_ref[...] = jnp.where(lane < N_HEADS, lf, jnp.where(lane < 4 * N_HEADS, jax.nn.sigmoid(z), 0.0))
    row = lax.broadcasted_iota(jnp.int32, (tm, tm), 0)
    col = lax.broadcasted_iota(jnp.int32, (tm, tm), 1)
    tril = jnp.where(col <= row, 1.0, 0.0)
    f = jnp.dot(tril, lf, preferred_element_type=F32, precision=HIGHEST) + carry_ref[0:1, :]
    f_ref[...] = f
    carry_ref[0:1, :] = f[tm - 1:tm, :]


def small_call(z_small, b_forget, n, t, tm):
    bias = jnp.zeros((1, SMALL_W), F32).at[0, :N_HEADS].set(b_forget)
    nt = t // tm
    return pl.pallas_call(
        _small_kernel,
        grid=(n, nt),
        in_specs=[pl.BlockSpec((tm, SMALL_W), lambda b, j: (b * nt + j, 0)),
                  pl.BlockSpec((1, SMALL_W), lambda b, j: (0, 0))],
        out_specs=[pl.BlockSpec((tm, SMALL_W), lambda b, j: (b * nt + j, 0)),
                   pl.BlockSpec((tm, SMALL_W), lambda b, j: (b * nt + j, 0))],
        out_shape=[jax.ShapeDtypeStruct((n * t, SMALL_W), F32),
                   jax.ShapeDtypeStruct((n * t, SMALL_W), F32)],
        scratch_shapes=[pltpu.VMEM((8, SMALL_W), F32)],
        compiler_params=_cparams("parallel", "arbitrary"),
        name="forget_and_gates",
    )(z_small, bias)


def _conv_prompt_kernel(z_ref, w_ref, o_ref, st_ref, prev_ref):
    @pl.when(pl.program_id(1) == 0)
    def _():
        prev_ref[...] = jnp.zeros_like(prev_ref)

    bw = BRANCH_W
    u = z_ref[:, 2 * bw:3 * bw] * z_ref[:, 0:bw]
    tm = u.shape[0]
    row = lax.broadcasted_iota(jnp.int32, u.shape, 0)
    u1 = jnp.where(row == 0, prev_ref[7:8, :], pltpu.roll(u, 1, 0))
    u2 = jnp.where(row == 0, prev_ref[6:7, :], jnp.where(row == 1, prev_ref[7:8, :], pltpu.roll(u, 2, 0)))
    y = u2 * w_ref[0:1, :] + u1 * w_ref[1:2, :] + u * w_ref[2:3, :]
    o_ref[...] = z_ref[:, bw:2 * bw] * y
    prev_ref[...] = u[tm - 8:tm]
    st_ref[0] = u[tm - 2:tm]


def conv_prompt_call(z_conv, conv_w, n, t, tm):
    nt = t // tm
    return pl.pallas_call(
        _conv_prompt_kernel,
        grid=(n, nt),
        in_specs=[pl.BlockSpec((tm, 3 * BRANCH_W), lambda b, j: (b * nt + j, 0)),
                  pl.BlockSpec((CONV_W, BRANCH_W), lambda b, j: (0, 0))],
        out_specs=[pl.BlockSpec((tm, BRANCH_W), lambda b, j: (b * nt + j, 0)),
                   pl.BlockSpec((1, CONV_W - 1, BRANCH_W), lambda b, j: (b, 0, 0))],
        out_shape=[jax.ShapeDtypeStruct((n * t, BRANCH_W), F32),
                   jax.ShapeDtypeStruct((n, CONV_W - 1, BRANCH_W), F32)],
        scratch_shapes=[pltpu.VMEM((8, BRANCH_W), F32)],
        compiler_params=_cparams("parallel", "arbitrary"),
        name="conv_prompt",
    )(z_conv, conv_w)


def _online_step(s, vt, carry):
    m, l, acc = carry
    m_new = jnp.maximum(m, jnp.max(s, axis=0, keepdims=True))
    alpha = jnp.exp(m - m_new)
    p = jnp.exp(s - m_new)
    l = alpha * l + jnp.sum(p, axis=0, keepdims=True)
    acc = alpha * acc + jnp.dot(vt, p.astype(BF16), preferred_element_type=F32)
    return m_new, l, acc


def _softmax_init(width=TQ):
    return (jnp.full((1, width), NEG, F32), jnp.zeros((1, width), F32), jnp.zeros((HEAD_DIM, width), F32))


def _query_pair(q_ref, h, half):
    hp, e = divmod(h, 2)
    qp = q_ref[:, hp * PAIR_W:(hp + 1) * PAIR_W]
    if e != half:
        qp = pltpu.roll(qp, HEAD_DIM, 1)
    lane = lax.broadcasted_iota(jnp.int32, qp.shape, 1)
    return jnp.where((lane // HEAD_DIM) == half, qp, 0.0)


def _tile_masks():
    sub = lax.broadcasted_iota(jnp.int32, (TK, TQ), 0)
    lane = lax.broadcasted_iota(jnp.int32, (TK, TQ), 1)
    return sub <= lane, sub > lane


def _key_col():
    return lax.broadcasted_iota(jnp.int32, (TK, 1), 0).astype(F32)


def _fox_kernel(q_ref, k_ref, vt_ref, fk_ref, o_ref, ot_ref):
    i = pl.program_id(1)
    causal, _ = _tile_masks()
    qcats = [jnp.concatenate([(_query_pair(q_ref, 2 * hp + e, e) * SCALE).astype(BF16) for e in (0, 1)], axis=0)
             for hp in range(N_HEADS // 2)]

    def tile(j, carry, diag):
        r0 = pl.multiple_of(j * TK, TK)
        out = []
        for hp in range(N_HEADS // 2):
            s2 = _dot_nt(k_ref[pl.ds(r0, TK), hp * PAIR_W:(hp + 1) * PAIR_W], qcats[hp])
            for e in (0, 1):
                h = 2 * hp + e
                s = s2[:, e * TQ:(e + 1) * TQ] - fk_ref[pl.ds(r0, TK), h:h + 1]
                if diag:
                    s = jnp.where(causal, s, MASKED)
                out.append(_online_step(s, vt_ref[j, h * HEAD_DIM:(h + 1) * HEAD_DIM, :], carry[h]))
        return tuple(out)

    carry = lax.fori_loop(0, i, lambda j, c: tile(j, c, False), tuple(_softmax_init() for _ in range(N_HEADS)))
    for h, (_, l, acc) in enumerate(tile(i, carry, True)):
        ot_ref[h * HEAD_DIM:(h + 1) * HEAD_DIM, :] = acc * (1.0 / l)
    o_ref[...] = ot_ref[...].T


def fox_prompt_call(q, kb, vt, fk, n, t):
    nq = t // TQ
    return pl.pallas_call(
        _fox_kernel,
        grid=(n, nq),
        in_specs=[pl.BlockSpec((TQ, BRANCH_W), lambda b, i: (b * nq + i, 0)),
                  pl.BlockSpec((t, BRANCH_W), lambda b, i: (b, 0)),
                  pl.BlockSpec((t // TK, BRANCH_W, TK), lambda b, i: (b, 0, 0)),
                  pl.BlockSpec((t, SMALL_W), lambda b, i: (b, 0))],
        out_specs=pl.BlockSpec((TQ, BRANCH_W), lambda b, i: (b * nq + i, 0)),
        out_shape=jax.ShapeDtypeStruct((n * t, BRANCH_W), F32),
        scratch_shapes=[pltpu.VMEM((BRANCH_W, TQ), F32)],
        compiler_params=_cparams("parallel", "arbitrary"),
        name="fox_prompt",
    )(q, kb, vt, fk)


def _rank_before(score, bidx, nblk):
    cnt = jnp.zeros(score.shape, F32)
    for b2 in range(nblk):
        row = score[b2:b2 + 1, :]
        beats = (row > score) | ((row == score) & (b2 < bidx))
        cnt = cnt + jnp.where(beats, 1.0, 0.0)
    return cnt


def _moba_kernel(q_ref, k_ref, vt_ref, km_ref, o_ref, ot_ref, sel_ref, *, nblk):
    i = pl.program_id(1)
    causal, _ = _tile_masks()
    kcol = _key_col()
    bidx = lax.broadcasted_iota(jnp.int32, (nblk, TQ), 0)
    for hp in range(N_HEADS // 2):
        qpads = []
        for e in (0, 1):
            q32 = _query_pair(q_ref, 2 * hp + e, e)
            qpads.append((q32 * SCALE).astype(BF16))
            gs = _dot_nt(km_ref[0, :, hp * PAIR_W:(hp + 1) * PAIR_W], q32, precision=HIGHEST)
            gs = jnp.where(bidx < i, gs, NEG)
            sel = (_rank_before(gs, bidx, nblk) < MOBA_TOPK) & (bidx < i)
            sel_ref[:, e * TQ:(e + 1) * TQ] = jnp.where(sel, 1.0, 0.0)
        qcat = jnp.concatenate(qpads, axis=0)
        slopes = [2.0 ** (-8.0 * (2 * hp + e + 1) / N_HEADS) for e in (0, 1)]

        def tile(j, carry, diag):
            r0 = pl.multiple_of(j * TK, TK)
            s2 = _dot_nt(k_ref[pl.ds(r0, TK), hp * PAIR_W:(hp + 1) * PAIR_W], qcat)
            dist = kcol - ((i - j) * TK).astype(F32)
            out = []
            for e in (0, 1):
                h = 2 * hp + e
                s = s2[:, e * TQ:(e + 1) * TQ] + slopes[e] * dist
                if diag:
                    s = jnp.where(causal, s, MASKED)
                else:
                    s = jnp.where(sel_ref[pl.ds(j, 1), e * TQ:(e + 1) * TQ] > 0.5, s, MASKED)
                out.append(_online_step(s, vt_ref[j, h * HEAD_DIM:(h + 1) * HEAD_DIM, :], carry[e]))
            return tuple(out)

        carry = lax.fori_loop(0, i, lambda j, c: tile(j, c, False), (_softmax_init(), _softmax_init()))
        for e, (_, l, acc) in enumerate(tile(i, carry, True)):
            h = 2 * hp + e
            ot_ref[h * HEAD_DIM:(h + 1) * HEAD_DIM, :] = acc * (1.0 / l)
    o_ref[...] = ot_ref[...].T


def moba_prompt_call(q, kb, vt, kmean, n, t):
    nq = t // TQ
    nblk = t // MOBA_BLOCK
    return pl.pallas_call(
        functools.partial(_moba_kernel, nblk=nblk),
        grid=(n, nq),
        in_specs=[pl.BlockSpec((TQ, BRANCH_W), lambda b, i: (b * nq + i, 0)),
                  pl.BlockSpec((t, BRANCH_W), lambda b, i: (b, 0)),
                  pl.BlockSpec((t // TK, BRANCH_W, TK), lambda b, i: (b, 0, 0)),
                  pl.BlockSpec((1, nblk, BRANCH_W), lambda b, i: (b, 0, 0))],
        out_specs=pl.BlockSpec((TQ, BRANCH_W), lambda b, i: (b * nq + i, 0)),
        out_shape=jax.ShapeDtypeStruct((n * t, BRANCH_W), F32),
        scratch_shapes=[pltpu.VMEM((BRANCH_W, TQ), F32), pltpu.VMEM((nblk, 2 * TQ), F32)],
        compiler_params=_cparams("parallel", "arbitrary"),
        name="moba_prompt",
    )(q, kb, vt, kmean.reshape(n, nblk, BRANCH_W))


def _gelu_tanh(x):
    return 0.5 * x * (1.0 + jnp.tanh(0.7978845608028654 * (x + 0.044715 * x * x * x)))


def _compress_kernel(pt_ref, a_ref, pet_ref, peb_ref, wt_ref, wb_ref, w2k_ref, w2v_ref, w2vt_ref,
                     kc_ref, vc_ref, vct_ref, rows_ref, *, rows):
    del pt_ref
    j = pl.program_id(1)
    r0 = pl.multiple_of(j * PAGE_ROWS, PAGE_ROWS)
    for i in range(CMP_S):
        rows_ref[pl.ds(r0, PAGE_ROWS), i * 2 * KV_W:(i + 1) * 2 * KV_W] = a_ref[0, :, i * 4 * KV_W:i * 4 * KV_W + 2 * KV_W]

    @pl.when(j == pl.num_programs(1) - 1)
    def _():
        chunk = min(256, rows)
        prev_top = jnp.zeros((1, 4 * CMP_HID), F32)
        for c in range(rows // chunk):
            a = rows_ref[c * chunk:(c + 1) * chunk, :]
            ht = jnp.dot((a + pet_ref[...]).astype(BF16), wt_ref[...], preferred_element_type=F32)
            hb = jnp.dot((a + peb_ref[...]).astype(BF16), wb_ref[...], preferred_element_type=F32)
            row = lax.broadcasted_iota(jnp.int32, ht.shape, 0)
            shifted = jnp.where(row == 0, prev_top, pltpu.roll(ht, 1, 0))
            prev_top = ht[chunk - 1:chunk, :]
            g = _gelu_tanh(shifted + hb).astype(BF16)
            kc_ref[0, c * chunk:(c + 1) * chunk, :] = jnp.dot(
                g[:, :2 * CMP_HID], w2k_ref[...], preferred_element_type=F32).astype(BF16)
            vc_ref[0, c * chunk:(c + 1) * chunk, :] = jnp.dot(
                g[:, 2 * CMP_HID:], w2v_ref[...], preferred_element_type=F32).astype(BF16)
            vct_ref[0, :, c * chunk:(c + 1) * chunk] = _dot_nt(w2vt_ref[...], g[:, 2 * CMP_HID:]).astype(BF16)


def compress_call(pages, page_idx, pe_top, pe_bot, wt, wb, w2k, w2v, w2vt, n, npages):
    rows = npages * PAGE_ROWS
    const = lambda b, j, pt: (0, 0)
    grid_spec = pltpu.PrefetchScalarGridSpec(
        num_scalar_prefetch=1,
        grid=(n, npages),
        in_specs=[pl.BlockSpec((1, PAGE_ROWS, CMP_S * 4 * KV_W), lambda b, j, pt: (pt[b, j], 0, 0)),
                  pl.BlockSpec((1, CMP_ROW), const),
                  pl.BlockSpec((1, CMP_ROW), const),
                  pl.BlockSpec((CMP_ROW, 4 * CMP_HID), const, pipeline_mode=pl.Buffered(1)),
                  pl.BlockSpec((CMP_ROW, 4 * CMP_HID), const, pipeline_mode=pl.Buffered(1)),
                  pl.BlockSpec((2 * CMP_HID, KV_W), const),
                  pl.BlockSpec((2 * CMP_HID, KV_W), const),
                  pl.BlockSpec((KV_W, 2 * CMP_HID), const)],
        out_specs=[pl.BlockSpec((1, rows, KV_W), lambda b, j, pt: (b, 0, 0)),
                   pl.BlockSpec((1, rows, KV_W), lambda b, j, pt: (b, 0, 0)),
                   pl.BlockSpec((1, KV_W, rows), lambda b, j, pt: (b, 0, 0))],
        scratch_shapes=[pltpu.VMEM((rows, CMP_ROW), F32)],
    )
    return pl.pallas_call(
        functools.partial(_compress_kernel, rows=rows),
        grid_spec=grid_spec,
        out_shape=[jax.ShapeDtypeStruct((n, rows, KV_W), BF16),
                   jax.ShapeDtypeStruct((n, rows, KV_W), BF16),
                   jax.ShapeDtypeStruct((n, KV_W, rows), BF16)],
        compiler_params=_cparams("parallel", "arbitrary"),
        name="nsa_compress",
    )(page_idx, pages, pe_top, pe_bot, wt, wb, w2k, w2v, w2vt)


def _nsa_kernel(q_ref, kc_ref, vct_ref, ks_ref, vst_ref, kw_ref, vwt_ref, gt_ref, o_ref,
                ot_ref, sel_ref, *, ncmp, nsb):
    i = pl.program_id(1)
    causal, below = _tile_masks()
    kcol = _key_col()
    lane_q = lax.broadcasted_iota(jnp.int32, (1, TQ), 1)
    qpos = i * TQ + lane_q
    crow = lax.broadcasted_iota(jnp.int32, (ncmp, 1), 0)
    cpos = CMP_S * crow + (CMP_S - 1)
    cvalid = (crow >= 1) & (cpos <= qpos)
    cposf = cpos.astype(F32)
    mj = lax.broadcasted_iota(jnp.int32, (nsb, ncmp), 0)
    mc = lax.broadcasted_iota(jnp.int32, (nsb, ncmp), 1)
    ratio = SLC_BLOCK // CMP_S
    gather_m = jnp.where((mc >= ratio * mj) & (mc <= ratio * mj + ratio) & (mc >= 1), 1.0, 0.0)
    bj = lax.broadcasted_iota(jnp.int32, (nsb, TQ), 0)
    own = qpos // SLC_BLOCK
    forced = (bj == 0) | (bj >= own - 1)
    allowed = bj <= own
    ecol = lax.broadcasted_iota(jnp.int32, (TK, nsb), 1)
    erow = lax.broadcasted_iota(jnp.int32, (TK, nsb), 0) // SLC_BLOCK
    gw = NSA_R * TQ
    sub4 = lax.broadcasted_iota(jnp.int32, (TK, gw), 0)
    lane4 = lax.broadcasted_iota(jnp.int32, (TK, gw), 1) % TQ
    causal4, below4 = sub4 <= lane4, sub4 > lane4
    cvalid4 = (crow >= 1) & (cpos <= i * TQ + lax.broadcasted_iota(jnp.int32, (1, gw), 1) % TQ)

    for g in range(NSA_G):
        heads = [g * NSA_R + r for r in range(NSA_R)]
        q4 = jnp.concatenate([(_query_pair(q_ref, h, g) * SCALE).astype(BF16) for h in heads], axis=0)
        slope_row = jnp.concatenate([jnp.full((1, TQ), 2.0 ** (-8.0 * (h + 1) / N_HEADS), F32) for h in heads], axis=1)
        gates = [jnp.concatenate([gt_ref[3 * h + c:3 * h + c + 1, :] for h in heads], axis=1) for c in range(3)]
        v_rows = slice(g * HEAD_DIM, (g + 1) * HEAD_DIM)
        s = _dot_nt(kc_ref[0], q4) + slope_row * (cposf - (i * TQ).astype(F32))
        s = jnp.where(cvalid4, s, MASKED)
        p = jnp.where(cvalid4, jnp.exp(s - jnp.max(s, axis=0, keepdims=True)), 0.0)
        l = jnp.sum(p, axis=0, keepdims=True)
        p = p * (1.0 / jnp.where(l > 0.0, l, 1.0))
        imp = p[:, 0:TQ]
        for r in range(1, NSA_R):
            imp = imp + p[:, r * TQ:(r + 1) * TQ]
        o_all = gates[0] * jnp.dot(vct_ref[0, v_rows, :], p.astype(BF16), preferred_element_type=F32)
        p_slc = jnp.dot(gather_m, imp, preferred_element_type=F32, precision=HIGHEST)
        score = jnp.where(allowed, jnp.where(forced, p_slc + FORCE_SCORE, p_slc), -1.0)
        sel = jnp.where((_rank_before(score, bj, nsb) < SLC_N) & allowed, 1.0, 0.0)
        sel_ref[...] = jnp.concatenate([sel] * NSA_R, axis=1).astype(BF16)

        def alibi(j):
            return slope_row * (kcol - ((i - j) * TK).astype(F32))

        def slc_tile(j, carry, diag):
            r0 = pl.multiple_of(j * TK, TK)
            expand = jnp.where(ecol == erow + j * (TK // SLC_BLOCK), 1.0, 0.0).astype(BF16)
            keep = jnp.dot(expand, sel_ref[...], preferred_element_type=F32) > 0.5
            if diag:
                keep = keep & causal4
            s = jnp.where(keep, _dot_nt(ks_ref[pl.ds(r0, TK), :], q4) + alibi(j), MASKED)
            return _online_step(s, vst_ref[j, v_rows, :], carry)

        carry = lax.fori_loop(0, i, lambda j, c: slc_tile(j, c, False), _softmax_init(gw))
        _, l, acc = slc_tile(i, carry, True)
        o_all = o_all + gates[1] * (acc * (1.0 / l))

        def win_tile(j, carry, mask):
            r0 = pl.multiple_of(j * TK, TK)
            s = _dot_nt(kw_ref[pl.ds(r0, TK), :], q4) + alibi(j)
            if mask is not None:
                s = jnp.where(mask, s, MASKED)
            return _online_step(s, vwt_ref[j, v_rows, :], carry)

        carry = _softmax_init(gw)
        carry = lax.cond(i >= 2, lambda c: win_tile(i - 2, c, below4), lambda c: c, carry)
        carry = lax.cond(i >= 1, lambda c: win_tile(i - 1, c, None), lambda c: c, carry)
        _, l, acc = win_tile(i, carry, causal4)
        o_all = o_all + gates[2] * (acc * (1.0 / l))
        for r, h in enumerate(heads):
            ot_ref[h * HEAD_DIM:(h + 1) * HEAD_DIM, :] = o_all[:, r * TQ:(r + 1) * TQ]
    o_ref[...] = ot_ref[...].T


def nsa_prompt_call(q, kcmp, vcmpt, ks, vst, kw, vwt, gate_t, n, t):
    nq = t // TQ
    ncmp = kcmp.shape[1]
    nsb = t // SLC_BLOCK
    return pl.pallas_call(
        functools.partial(_nsa_kernel, ncmp=ncmp, nsb=nsb),
        grid=(n, nq),
        in_specs=[pl.BlockSpec((TQ, BRANCH_W), lambda b, i: (b * nq + i, 0)),
                  pl.BlockSpec((1, ncmp, KV_W), lambda b, i: (b, 0, 0)),
                  pl.BlockSpec((1, KV_W, ncmp), lambda b, i: (b, 0, 0)),
                  pl.BlockSpec((t, KV_W), lambda b, i: (b, 0)),
                  pl.BlockSpec((t // TK, KV_W, TK), lambda b, i: (b, 0, 0)),
                  pl.BlockSpec((t, KV_W), lambda b, i: (b, 0)),
                  pl.BlockSpec((t // TK, KV_W, TK), lambda b, i: (b, 0, 0)),
                  pl.BlockSpec((3 * N_HEADS, TQ), lambda b, i: (0, b * nq + i))],
        out_specs=pl.BlockSpec((TQ, BRANCH_W), lambda b, i: (b * nq + i, 0)),
        out_shape=jax.ShapeDtypeStruct((n * t, BRANCH_W), F32),
        scratch_shapes=[pltpu.VMEM((BRANCH_W, TQ), F32), pltpu.VMEM((nsb, NSA_R * TQ), BF16)],
        compiler_params=_cparams("parallel", "arbitrary"),
        name="nsa_prompt",
    )(q, kcmp, vcmpt, ks, vst, kw, vwt, gate_t)


def alibi_slopes(n):
    return jnp.exp2(-8.0 * jnp.arange(1, n + 1, dtype=jnp.float32) / n)


def masked_softmax(s, mask, axis=-1):
    p = jax.nn.softmax(jnp.where(mask, s, NEG), axis=axis)
    return p * mask


def sweep_queries(fn, block, *qs):
    n, t = qs[0].shape[:2]
    qb = block if t % block == 0 else t
    nb = t // qb
    xs = tuple(jnp.moveaxis(a.reshape((n, nb, qb) + a.shape[2:]), 1, 0) for a in qs)
    out = lax.map(lambda args: fn(args[0] * qb, *args[1:]), (jnp.arange(nb, dtype=jnp.int32),) + xs)
    return jnp.moveaxis(out, 0, 1).reshape((n, t) + out.shape[3:])


def gather_pages(pool, page_table):
    g = pool[page_table]
    return g.reshape((g.shape[0], g.shape[1] * g.shape[2]) + g.shape[3:])


def fox_attention(q, k, v, logf, q0):
    d = q.shape[-1]
    L = k.shape[1]
    scale = d ** -0.5
    F = jnp.cumsum(logf.astype(jnp.float32), axis=1)
    Fk = jnp.moveaxis(F, 1, 2)
    Fq = F[:, q0:]
    kpos = jnp.arange(L)

    def block(start, qb, fq):
        qpos = q0 + start + jnp.arange(qb.shape[1])
        s = jnp.einsum('nqhd,nkhd->nhqk', qb, k).astype(jnp.float32) * scale
        s = s + jnp.moveaxis(fq, 1, 2)[..., None] - Fk[:, :, None, :]
        p = masked_softmax(s, kpos[None, :] <= qpos[:, None])
        return jnp.einsum('nhqk,nkhd->nqhd', p.astype(q.dtype), v)

    return sweep_queries(block, Q_BLOCK, q, Fq)


def moba_attention(q, k, v, q0, slopes):
    n, tq, nh, d = q.shape
    L = k.shape[1]
    scale = d ** -0.5
    nbk = -(-L // MOBA_BLOCK)
    pad = nbk * MOBA_BLOCK - L

    def to_blocks(a):
        a = jnp.pad(a, ((0, 0), (0, pad), (0, 0), (0, 0)))
        return a.reshape(n, nbk, MOBA_BLOCK, nh, d).transpose(0, 3, 1, 2, 4)

    kb, vb = to_blocks(k), to_blocks(v)
    kmean = jnp.mean(kb.astype(jnp.float32), axis=3)
    topk = min(MOBA_TOPK, nbk)
    bidx = jnp.arange(nbk)
    ni = jnp.arange(n)[:, None, None, None]
    hi = jnp.arange(nh)[None, :, None, None]
    sl = slopes[None, :, None, None, None]

    def block(start, qb):
        m = qb.shape[1]
        qpos = q0 + start + jnp.arange(m)
        own = qpos // MOBA_BLOCK
        gs = jnp.einsum('nqhd,nhbd->nhqb', qb.astype(jnp.float32), kmean)
        gs = jnp.where(bidx[None, :] < own[:, None], gs, NEG)
        _, top = lax.top_k(gs, topk)
        sel_ok = top < own[:, None]
        idx = jnp.concatenate([top, jnp.broadcast_to(own[:, None], (n, nh, m, 1))], axis=-1)
        ok = jnp.concatenate([sel_ok, jnp.ones((n, nh, m, 1), bool)], axis=-1)
        kg = kb[ni, hi, idx]
        vg = vb[ni, hi, idx]
        kpos = idx[..., None] * MOBA_BLOCK + jnp.arange(MOBA_BLOCK)
        s = jnp.einsum('nqhd,nhqsjd->nhqsj', qb, kg).astype(jnp.float32) * scale
        s = s - sl * (qpos[:, None, None] - kpos).astype(jnp.float32)
        mask = ok[..., None] & (kpos <= qpos[:, None, None])
        p = masked_softmax(s, mask, axis=(-2, -1))
        return jnp.einsum('nhqsj,nhqsjd->nqhd', p.astype(q.dtype), vg)

    return sweep_queries(block, SPARSE_Q_BLOCK, q)


def compress_tokens(a, w1, w2, pe):
    n, L, g, d = a.shape
    nc = (L - CMP_L) // CMP_S + 1
    idx = jnp.arange(nc)[:, None] * CMP_S + jnp.arange(CMP_L)[None, :]
    blocks = a[:, idx] + pe[:, None, :].astype(a.dtype)
    flat = jnp.swapaxes(blocks, 2, 3).reshape(n, nc, g, CMP_L * d)
    return jax.nn.gelu(flat @ w1) @ w2


def nsa_attention(q, kc, vc, ks, vs, kw, vw, gate, q0, w_ofs, slopes, cmp_w1, cmp_w2, cmp_pe):
    n, tq, nh, d = q.shape
    L = kc.shape[1]
    dt = q.dtype
    scale = d ** -0.5
    qg = q.reshape(n, tq, NSA_G, NSA_R, d)
    k_cmp = compress_tokens(kc, cmp_w1[0], cmp_w2[0], cmp_pe[0])
    v_cmp = compress_tokens(vc, cmp_w1[1], cmp_w2[1], cmp_pe[1])
    nc = k_cmp.shape[1]
    cpos = jnp.arange(nc) * CMP_S + (CMP_L - 1)
    nsb = -(-L // SLC_BLOCK)
    padl = nsb * SLC_BLOCK - L

    def to_blocks(a):
        a = jnp.pad(a, ((0, 0), (0, padl), (0, 0), (0, 0)))
        return a.reshape(n, nsb, SLC_BLOCK, NSA_G, d).transpose(0, 3, 1, 2, 4)

    ks_b, vs_b = to_blocks(ks), to_blocks(vs)
    nsel = min(SLC_N, nsb)
    front = CMP_L // CMP_S - 1
    ratio = SLC_BLOCK // CMP_S
    width = ratio + front
    back = ratio * nsb + width - front - nc
    kw_p = jnp.pad(kw, ((0, 0), (WINDOW, 0), (0, 0), (0, 0)))
    vw_p = jnp.pad(vw, ((0, 0), (WINDOW, 0), (0, 0), (0, 0)))
    sl = slopes.reshape(NSA_G, NSA_R)[None, :, :, None, None]
    ni = jnp.arange(n)[:, None, None, None]
    gi = jnp.arange(NSA_G)[None, :, None, None]
    bj = jnp.arange(nsb)

    def block(start, qb, gb):
        m = qb.shape[1]
        qpos = q0 + start + jnp.arange(m)
        s = jnp.einsum('nqgrd,ncgd->ngrqc', qb, k_cmp).astype(jnp.float32) * scale
        s = s - sl * (qpos[:, None] - cpos[None, :]).astype(jnp.float32)
        p_cmp = masked_softmax(s, cpos[None, :] <= qpos[:, None])
        o_cmp = jnp.einsum('ngrqc,ncgd->nqgrd', p_cmp.astype(dt), v_cmp)
        imp = jnp.pad(p_cmp.sum(axis=2), ((0, 0), (0, 0), (0, 0), (front, back)))
        p_slc = imp[..., 0:ratio * nsb:ratio]
        for u in range(1, width):
            p_slc = p_slc + imp[..., u:u + ratio * nsb:ratio]
        own = qpos // SLC_BLOCK
        forced = (bj[None, :] == 0) | (bj[None, :] >= own[:, None] - 1)
        allowed = bj[None, :] <= own[:, None]
        score = jnp.where(allowed, jnp.where(forced, p_slc + FORCE_SCORE, p_slc), -1.0)
        _, top = lax.top_k(score, nsel)
        ok = top <= own[:, None]
        kg = ks_b[ni, gi, top]
        vg = vs_b[ni, gi, top]
        kpos = top[..., None] * SLC_BLOCK + jnp.arange(SLC_BLOCK)
        dist = (qpos[:, None, None] - kpos)[:, :, None].astype(jnp.float32)
        s2 = jnp.einsum('nqgrd,ngqsjd->ngrqsj', qb, kg).astype(jnp.float32) * scale - sl[..., None] * dist
        mask2 = (ok[..., None] & (kpos <= qpos[:, None, None]))[:, :, None]
        p2 = masked_softmax(s2, mask2, axis=(-2, -1))
        o_slc = jnp.einsum('ngrqsj,ngqsjd->nqgrd', p2.astype(dt), vg)
        off = q0 + start - w_ofs
        kwin = lax.dynamic_slice_in_dim(kw_p, off, WINDOW + m, axis=1)
        vwin = lax.dynamic_slice_in_dim(vw_p, off, WINDOW + m, axis=1)
        wpos = q0 + start - WINDOW + jnp.arange(WINDOW + m)
        s3 = jnp.einsum('nqgrd,nkgd->ngrqk', qb, kwin).astype(jnp.float32) * scale
        s3 = s3 - sl * (qpos[:, None] - wpos[None, :]).astype(jnp.float32)
        wmask = (wpos[None, :] <= qpos[:, None]) & (wpos[None, :] > qpos[:, None] - WINDOW) & (wpos[None, :] >= 0)
        p3 = masked_softmax(s3, wmask)
        o_win = jnp.einsum('ngrqk,nkgd->nqgrd', p3.astype(dt), vwin)
        return gb[..., 0:1] * o_cmp + gb[..., 1:2] * o_slc + gb[..., 2:3] * o_win

    return sweep_queries(block, SPARSE_Q_BLOCK, qg, gate)


PAGE = 128


def _head_diag_mask(width):
    sub = lax.broadcasted_iota(jnp.int32, (N_HEADS, width), 0)
    lane = lax.broadcasted_iota(jnp.int32, (N_HEADS, width), 1)
    return sub == lane // HEAD_DIM


def _block_diag_queries(q_ref, nt):
    diag = _head_diag_mask(BRANCH_W)
    return jnp.concatenate([jnp.where(diag, q_ref[0, t:t + 1, :], 0.0) for t in range(nt)], axis=0)


def _extract_heads(o, nt):
    diag = _head_diag_mask(BRANCH_W)
    return jnp.concatenate(
        [jnp.sum(jnp.where(diag, o[t * N_HEADS:(t + 1) * N_HEADS], 0.0), axis=0, keepdims=True) for t in range(nt)],
        axis=0)


def _slope_col(rows):
    h = lax.broadcasted_iota(jnp.int32, (rows, 1), 0) % N_HEADS
    col = jnp.zeros((rows, 1), F32)
    for k in range(N_HEADS):
        col = jnp.where(h == k, 2.0 ** (-8.0 * (k + 1) / N_HEADS), col)
    return col


def _row_softmax_step(s, v, m_ref, l_ref, acc_ref):
    m_old = m_ref[...]
    m_new = jnp.maximum(m_old, jnp.max(s, axis=1, keepdims=True))
    alpha = jnp.exp(m_old - m_new)
    p = jnp.exp(s - m_new)
    l_ref[...] = alpha * l_ref[...] + jnp.sum(p, axis=1, keepdims=True)
    acc_ref[...] = alpha * acc_ref[...] + jnp.dot(p.astype(BF16), v, preferred_element_type=F32)
    m_ref[...] = m_new


def _softmax_reset(m_ref, l_ref, acc_ref):
    m_ref[...] = jnp.full(m_ref.shape, NEG, F32)
    l_ref[...] = jnp.zeros(l_ref.shape, F32)
    acc_ref[...] = jnp.zeros(acc_ref.shape, F32)


def _new_token_mask(rows, nt):
    k = lax.broadcasted_iota(jnp.int32, (rows, PAGE), 1)
    t = lax.broadcasted_iota(jnp.int32, (rows, PAGE), 0) // N_HEADS
    return (k <= t) & (k < nt)


def _dfox_kernel(pt_ref, q_ref, pg_ref, lf_ref, npg_ref, nlf_ref, o_ref, qbd_ref, m_ref, l_ref, acc_ref, cf_ref, *, nt):
    del pt_ref
    j = pl.program_id(1)
    rows = nt * N_HEADS

    @pl.when(j == 0)
    def _():
        qbd_ref[...] = (_block_diag_queries(q_ref, nt) * SCALE).astype(BF16)
        _softmax_reset(m_ref, l_ref, acc_ref)
        cf_ref[...] = jnp.zeros_like(cf_ref)

    def step(kv_ref, lf, mask):
        k = kv_ref[0, :, 0:BRANCH_W].astype(BF16)
        v = kv_ref[0, :, BRANCH_W:2 * BRANCH_W].astype(BF16)
        rr = lax.broadcasted_iota(jnp.int32, (rows, N_HEADS), 0) % N_HEADS
        rc = lax.broadcasted_iota(jnp.int32, (rows, N_HEADS), 1)
        lfe = _dot_nt(jnp.where(rr == rc, 1.0, 0.0), lf, precision=HIGHEST)
        a = lax.broadcasted_iota(jnp.int32, (PAGE, PAGE), 0)
        b = lax.broadcasted_iota(jnp.int32, (PAGE, PAGE), 1)
        fk = jnp.dot(lfe, jnp.where(a <= b, 1.0, 0.0), preferred_element_type=F32, precision=HIGHEST) + cf_ref[...]
        cf_ref[...] = fk[:, PAGE - 1:PAGE]
        s = _dot_nt(qbd_ref[...], k) - fk
        if mask is not None:
            s = jnp.where(mask, s, MASKED)
        _row_softmax_step(s, v, m_ref, l_ref, acc_ref)

    step(pg_ref, lf_ref[0], None)

    @pl.when(j == pl.num_programs(1) - 1)
    def _():
        step(npg_ref, nlf_ref[0], _new_token_mask(rows, nt))
        o_ref[0] = _extract_heads(acc_ref[...] * (1.0 / l_ref[...]), nt)


def fox_decode_call(pt, q, cache_kv, cache_lf, new_kv, new_lf):
    n, nt, _ = q.shape
    npages = pt.shape[1]
    rows = nt * N_HEADS
    grid_spec = pltpu.PrefetchScalarGridSpec(
        num_scalar_prefetch=1,
        grid=(n, npages),
        in_specs=[pl.BlockSpec((1, nt, BRANCH_W), lambda b, j, pt: (b, 0, 0)),
                  pl.BlockSpec((1, PAGE, 2 * BRANCH_W), lambda b, j, pt: (pt[b, j], 0, 0)),
                  pl.BlockSpec((1, PAGE, N_HEADS), lambda b, j, pt: (pt[b, j], 0, 0)),
                  pl.BlockSpec((1, PAGE, 2 * BRANCH_W), lambda b, j, pt: (b, 0, 0)),
                  pl.BlockSpec((1, PAGE, N_HEADS), lambda b, j, pt: (b, 0, 0))],
        out_specs=pl.BlockSpec((1, nt, BRANCH_W), lambda b, j, pt: (b, 0, 0)),
        scratch_shapes=[pltpu.VMEM((rows, BRANCH_W), BF16), pltpu.VMEM((rows, 1), F32), pltpu.VMEM((rows, 1), F32),
                        pltpu.VMEM((rows, BRANCH_W), F32), pltpu.VMEM((rows, 1), F32)],
    )
    return pl.pallas_call(
        functools.partial(_dfox_kernel, nt=nt),
        grid_spec=grid_spec,
        out_shape=jax.ShapeDtypeStruct((n, nt, BRANCH_W), F32),
        compiler_params=_cparams("parallel", "arbitrary"),
        name="fox_decode",
    )(pt, q, cache_kv, cache_lf, new_kv, new_lf)


def _dmoba_sel_kernel(pt_ref, q_ref, kp_ref, sel_ref, q32_ref, g_ref, *, nt):
    del pt_ref
    j = pl.program_id(1)
    rows = nt * N_HEADS
    pages_per_block = MOBA_BLOCK // PAGE

    @pl.when(j == 0)
    def _():
        q32_ref[...] = _block_diag_queries(q_ref, nt)
        g_ref[...] = jnp.zeros_like(g_ref)

    ksum = jnp.sum(kp_ref[0], axis=0, keepdims=True)
    prow = lax.broadcasted_iota(jnp.int32, (g_ref.shape[1], BRANCH_W), 0)
    g_ref[...] += _dot_nt(q32_ref[...], jnp.where(prow == j, ksum, 0.0), precision=HIGHEST)

    @pl.when(j == pl.num_programs(1) - 1)
    def _():
        npg = g_ref.shape[1]
        g = g_ref[...]
        lane = lax.broadcasted_iota(jnp.int32, (rows, npg), 1)
        blk = g
        for u in range(1, pages_per_block):
            blk = blk + pltpu.roll(g, npg - u, 1)
        gs = jnp.where(lane % pages_per_block == 0, blk * (1.0 / MOBA_BLOCK), NEG)
        sel = jnp.zeros((rows, npg), F32)
        for _ in range(MOBA_TOPK):
            mx = jnp.max(gs, axis=1, keepdims=True)
            idx = jnp.min(jnp.where(gs == mx, lane, npg), axis=1, keepdims=True)
            hit = lane == idx
            sel = jnp.where(hit, 1.0, sel)
            gs = jnp.where(hit, MASKED, gs)
        out = sel
        for u in range(1, pages_per_block):
            out = out + pltpu.roll(sel, u, 1)
        sel_ref[0] = out


def moba_select_call(pt, q, cache_kv):
    n, nt, _ = q.shape
    npages = pt.shape[1]
    rows = nt * N_HEADS
    grid_spec = pltpu.PrefetchScalarGridSpec(
        num_scalar_prefetch=1,
        grid=(n, npages),
        in_specs=[pl.BlockSpec((1, nt, BRANCH_W), lambda b, j, pt: (b, 0, 0)),
                  pl.BlockSpec((1, PAGE, BRANCH_W), lambda b, j, pt: (pt[b, j], 0, 0))],
        out_specs=pl.BlockSpec((1, rows, npages), lambda b, j, pt: (b, 0, 0)),
        scratch_shapes=[pltpu.VMEM((rows, BRANCH_W), F32), pltpu.VMEM((rows, npages), F32)],
    )
    return pl.pallas_call(
        functools.partial(_dmoba_sel_kernel, nt=nt),
        grid_spec=grid_spec,
        out_shape=jax.ShapeDtypeStruct((n, rows, npages), F32),
        compiler_params=_cparams("parallel", "arbitrary"),
        name="moba_select",
    )(pt, q, cache_kv)


def _dmoba_kernel(pt_ref, q_ref, sel_ref, pg_ref, npg_ref, o_ref, qbd_ref, m_ref, l_ref, acc_ref, *, nt):
    del pt_ref
    j = pl.program_id(1)
    npages = pl.num_programs(1)
    rows = nt * N_HEADS
    slope = _slope_col(rows)
    klane = lax.broadcasted_iota(jnp.int32, (1, PAGE), 1)

    @pl.when(j == 0)
    def _():
        qbd_ref[...] = (_block_diag_queries(q_ref, nt) * SCALE).astype(BF16)
        _softmax_reset(m_ref, l_ref, acc_ref)

    def step(kv_ref, rel, mask):
        k = kv_ref[0, :, 0:BRANCH_W].astype(BF16)
        v = kv_ref[0, :, BRANCH_W:2 * BRANCH_W].astype(BF16)
        s = _dot_nt(qbd_ref[...], k) + slope * rel
        _row_softmax_step(jnp.where(mask, s, MASKED), v, m_ref, l_ref, acc_ref)

    prow = lax.broadcasted_iota(jnp.int32, (sel_ref.shape[2], PAGE), 0)
    keep = jnp.dot(sel_ref[0].astype(BF16), jnp.where(prow == j, 1.0, 0.0).astype(BF16),
                   preferred_element_type=F32) > 0.5
    step(pg_ref, (klane + (j - npages) * PAGE).astype(F32), keep)

    @pl.when(j == npages - 1)
    def _():
        step(npg_ref, klane.astype(F32), _new_token_mask(rows, nt))
        o_ref[0] = _extract_heads(acc_ref[...] * (1.0 / l_ref[...]), nt)


def moba_decode_call(pt, q, sel, cache_kv, new_kv):
    n, nt, _ = q.shape
    npages = pt.shape[1]
    rows = nt * N_HEADS
    grid_spec = pltpu.PrefetchScalarGridSpec(
        num_scalar_prefetch=1,
        grid=(n, npages),
        in_specs=[pl.BlockSpec((1, nt, BRANCH_W), lambda b, j, pt: (b, 0, 0)),
                  pl.BlockSpec((1, rows, npages), lambda b, j, pt: (b, 0, 0)),
                  pl.BlockSpec((1, PAGE, 2 * BRANCH_W), lambda b, j, pt: (pt[b, j], 0, 0)),
                  pl.BlockSpec((1, PAGE, 2 * BRANCH_W), lambda b, j, pt: (b, 0, 0))],
        out_specs=pl.BlockSpec((1, nt, BRANCH_W), lambda b, j, pt: (b, 0, 0)),
        scratch_shapes=[pltpu.VMEM((rows, BRANCH_W), BF16), pltpu.VMEM((rows, 1), F32), pltpu.VMEM((rows, 1), F32),
                        pltpu.VMEM((rows, BRANCH_W), F32)],
    )
    return pl.pallas_call(
        functools.partial(_dmoba_kernel, nt=nt),
        grid_spec=grid_spec,
        out_shape=jax.ShapeDtypeStruct((n, nt, BRANCH_W), F32),
        compiler_params=_cparams("parallel", "arbitrary"),
        name="moba_decode",
    )(pt, q, sel, cache_kv, new_kv)


def _dnsa_kernel(pt_ref, q_ref, gate_ref, kc_ref, vc_ref, pg_ref, npg_ref, win_ref, nwin_ref, o_ref,
                 qbd_ref, sel_ref, ocmp_ref, m_ref, l_ref, acc_ref, *, nt, nsb_pad):
    del pt_ref
    j = pl.program_id(1)
    npages = pl.num_programs(1)
    rows = nt * N_HEADS
    ncmp = kc_ref.shape[1]
    q0 = npages * PAGE
    slope = _slope_col(rows)
    klane = lax.broadcasted_iota(jnp.int32, (1, PAGE), 1)
    blocks_per_page = PAGE // SLC_BLOCK

    @pl.when(j == 0)
    def _():
        pr = lax.broadcasted_iota(jnp.int32, (BRANCH_W, KV_W), 0)
        pc = lax.broadcasted_iota(jnp.int32, (BRANCH_W, KV_W), 1)
        place = jnp.where((pr % HEAD_DIM == pc % HEAD_DIM) & (pc // HEAD_DIM == pr // (NSA_R * HEAD_DIM)), 1.0, 0.0)
        qg = jnp.dot(_block_diag_queries(q_ref, nt), place, preferred_element_type=F32, precision=HIGHEST)
        qbd = (qg * SCALE).astype(BF16)
        qbd_ref[...] = qbd
        c = lax.broadcasted_iota(jnp.int32, (1, ncmp), 1)
        s = _dot_nt(qbd, kc_ref[0]) + slope * (CMP_S * c + (CMP_S - 1) - q0).astype(F32)
        valid = c >= 1
        s = jnp.where(valid, s, MASKED)
        p = jnp.where(valid, jnp.exp(s - jnp.max(s, axis=1, keepdims=True)), 0.0)
        p = p * (1.0 / jnp.sum(p, axis=1, keepdims=True))
        ocmp_ref[...] = gate_ref[0, :, 0:1] * jnp.dot(p.astype(BF16), vc_ref[0], preferred_element_type=F32)
        gr = lax.broadcasted_iota(jnp.int32, (nt * NSA_G, rows), 0)
        gc = lax.broadcasted_iota(jnp.int32, (nt * NSA_G, rows), 1) // NSA_R
        imp = jnp.dot(jnp.where(gr == gc, 1.0, 0.0), p, preferred_element_type=F32, precision=HIGHEST)
        ratio = SLC_BLOCK // CMP_S
        mr = lax.broadcasted_iota(jnp.int32, (ncmp, nsb_pad), 0)
        mb = lax.broadcasted_iota(jnp.int32, (ncmp, nsb_pad), 1)
        gather_m = jnp.where((mr >= ratio * mb) & (mr <= ratio * mb + ratio) & (mr >= 1), 1.0, 0.0)
        p_slc = jnp.dot(imp, gather_m, preferred_element_type=F32, precision=HIGHEST)
        bj = lax.broadcasted_iota(jnp.int32, p_slc.shape, 1)
        own = q0 // SLC_BLOCK
        forced = (bj == 0) | (bj >= own - 1)
        allowed = bj <= own
        score = jnp.where(allowed, jnp.where(forced, p_slc + FORCE_SCORE, p_slc), -1.0)
        sel = jnp.zeros(p_slc.shape, F32)
        for _ in range(SLC_N):
            mx = jnp.max(score, axis=1, keepdims=True)
            idx = jnp.min(jnp.where(score == mx, bj, nsb_pad), axis=1, keepdims=True)
            hit = bj == idx
            sel = jnp.where(hit, 1.0, sel)
            score = jnp.where(hit, -2.0, score)
        sel = jnp.where(allowed, sel, 0.0)
        er = lax.broadcasted_iota(jnp.int32, (rows, nt * NSA_G), 0) // NSA_R
        ec = lax.broadcasted_iota(jnp.int32, (rows, nt * NSA_G), 1)
        sel_ref[...] = jnp.dot(jnp.where(er == ec, 1.0, 0.0), sel, preferred_element_type=F32).astype(BF16)
        _softmax_reset(m_ref, l_ref, acc_ref)

    def step(kv_ref, rel, mask):
        k = kv_ref[0, :, 0:KV_W].astype(BF16)
        v = kv_ref[0, :, KV_W:2 * KV_W].astype(BF16)
        s = _dot_nt(qbd_ref[...], k) + slope * rel
        if mask is not None:
            s = jnp.where(mask, s, MASKED)
        _row_softmax_step(s, v, m_ref, l_ref, acc_ref)

    brow = lax.broadcasted_iota(jnp.int32, (nsb_pad, PAGE), 0)
    bcol = lax.broadcasted_iota(jnp.int32, (nsb_pad, PAGE), 1) // SLC_BLOCK
    expand = jnp.where(brow == bcol + j * blocks_per_page, 1.0, 0.0).astype(BF16)
    keep = jnp.dot(sel_ref[...], expand, preferred_element_type=F32) > 0.5
    step(pg_ref, (klane + (j - npages) * PAGE).astype(F32), keep)

    @pl.when(j == npages - 1)
    def _():
        new_mask = _new_token_mask(rows, nt)
        step(npg_ref, klane.astype(F32), new_mask)
        o_slc = acc_ref[...] * (1.0 / l_ref[...])
        _softmax_reset(m_ref, l_ref, acc_ref)
        wlen = win_ref.shape[1]
        wi = lax.broadcasted_iota(jnp.int32, (rows, wlen), 1)
        wt = lax.broadcasted_iota(jnp.int32, (rows, wlen), 0) // N_HEADS
        wrel = (lax.broadcasted_iota(jnp.int32, (1, wlen), 1) - wlen).astype(F32)
        step(win_ref, wrel, wi > wt + (wlen - WINDOW))
        step(nwin_ref, klane.astype(F32), new_mask)
        o_win = acc_ref[...] * (1.0 / l_ref[...])
        o = ocmp_ref[...] + gate_ref[0, :, 1:2] * o_slc + gate_ref[0, :, 2:3] * o_win
        ur = lax.broadcasted_iota(jnp.int32, (KV_W, BRANCH_W), 0)
        uc = lax.broadcasted_iota(jnp.int32, (KV_W, BRANCH_W), 1)
        unplace = jnp.where((ur % HEAD_DIM == uc % HEAD_DIM) & (ur // HEAD_DIM == uc // (NSA_R * HEAD_DIM)), 1.0, 0.0)
        o_ref[0] = _extract_heads(jnp.dot(o, unplace, preferred_element_type=F32, precision=HIGHEST), nt)


def nsa_decode_call(pt, q, gate, kcmp, vcmp, cache_kv, new_kv, win, new_win):
    n, nt, _ = q.shape
    npages = pt.shape[1]
    rows = nt * N_HEADS
    ncmp = kcmp.shape[1]
    nsb = (npages * PAGE) // SLC_BLOCK + 1
    nsb_pad = -(-nsb // 128) * 128
    wlen = win.shape[1]
    grid_spec = pltpu.PrefetchScalarGridSpec(
        num_scalar_prefetch=1,
        grid=(n, npages),
        in_specs=[pl.BlockSpec((1, nt, BRANCH_W), lambda b, j, pt: (b, 0, 0)),
                  pl.BlockSpec((1, rows, 3), lambda b, j, pt: (b, 0, 0)),
                  pl.BlockSpec((1, ncmp, KV_W), lambda b, j, pt: (b, 0, 0)),
                  pl.BlockSpec((1, ncmp, KV_W), lambda b, j, pt: (b, 0, 0)),
                  pl.BlockSpec((1, PAGE, 2 * KV_W), lambda b, j, pt: (pt[b, j], 0, 1)),
                  pl.BlockSpec((1, PAGE, 2 * KV_W), lambda b, j, pt: (b, 0, 0)),
                  pl.BlockSpec((1, wlen, 2 * KV_W), lambda b, j, pt: (b, 0, 0)),
                  pl.BlockSpec((1, PAGE, 2 * KV_W), lambda b, j, pt: (b, 0, 0))],
        out_specs=pl.BlockSpec((1, nt, BRANCH_W), lambda b, j, pt: (b, 0, 0)),
        scratch_shapes=[pltpu.VMEM((rows, KV_W), BF16), pltpu.VMEM((rows, nsb_pad), BF16),
                        pltpu.VMEM((rows, KV_W), F32), pltpu.VMEM((rows, 1), F32), pltpu.VMEM((rows, 1), F32),
                        pltpu.VMEM((rows, KV_W), F32)],
    )
    return pl.pallas_call(
        functools.partial(_dnsa_kernel, nt=nt, nsb_pad=nsb_pad),
        grid_spec=grid_spec,
        out_shape=jax.ShapeDtypeStruct((n, nt, BRANCH_W), F32),
        compiler_params=_cparams("parallel", "arbitrary"),
        name="nsa_decode",
    )(pt, q, gate, kcmp, vcmp, cache_kv, new_kv, win, new_win)


def _conv_sample_kernel(z_ref, pre1_ref, pre2_ref, w_ref, o_ref, u_ref, *, nt):
    bw = BRANCH_W
    u = z_ref[:, 2 * bw:3 * bw] * z_ref[:, 0:bw]
    t = lax.broadcasted_iota(jnp.int32, u.shape, 0) % nt
    u1 = jnp.where(t < 1, pre1_ref[...], pltpu.roll(u, 1, 0))
    u2 = jnp.where(t < 2, pre2_ref[...], pltpu.roll(u, 2, 0))
    y = u2 * w_ref[0:1, :] + u1 * w_ref[1:2, :] + u * w_ref[2:3, :]
    o_ref[...] = z_ref[:, bw:2 * bw] * y
    u_ref[...] = u


def conv_sample_call(z_conv, state, conv_w, n, nt):
    zero = jnp.zeros((n, nt, BRANCH_W), F32)
    pre1 = zero.at[:, 0].set(state[:, 1]).reshape(n * nt, BRANCH_W)
    pre2 = zero.at[:, 0].set(state[:, 0]).at[:, 1].set(state[:, 1]).reshape(n * nt, BRANCH_W)
    return pl.pallas_call(
        functools.partial(_conv_sample_kernel, nt=nt),
        out_shape=[jax.ShapeDtypeStruct((n * nt, BRANCH_W), F32), jax.ShapeDtypeStruct((n * nt, BRANCH_W), F32)],
        name="conv_sample",
    )(z_conv, pre1, pre2, conv_w)


PPS = 8


def _split3(x):
    hi = x.astype(BF16)
    r1 = x - hi.astype(F32)
    mid = r1.astype(BF16)
    lo = (r1 - mid.astype(F32)).astype(BF16)
    return jnp.concatenate([hi, mid, lo], axis=0)


def _softmax_step_t(s, vts, m_ref, l_ref, acc_ref):
    m_old = m_ref[...]
    m_new = jnp.maximum(m_old, jnp.max(s, axis=1, keepdims=True))
    alpha = jnp.exp(m_old - m_new)
    p = jnp.exp(s - m_new)
    l_ref[...] = alpha * l_ref[...] + jnp.sum(p, axis=1, keepdims=True)
    p = p.astype(BF16)
    pv, off = None, 0
    for vt in vts:
        nk = vt.shape[1]
        term = _dot_nt(p[:, off:off + nk], vt)
        pv = term if pv is None else pv + term
        off += nk
    acc_ref[...] = alpha * acc_ref[...] + pv
    m_ref[...] = m_new


def _page_specs(rows, row_block, pps):
    return [pl.BlockSpec((1, rows, PAGE), functools.partial(lambda b, j, pt, u: (pt[b, j * pps + u], row_block, 0), u=u))
            for u in range(pps)]


def _dfox_t_kernel(pt_ref, q_ref, *refs, nt, pps):
    del pt_ref
    kv_refs, lf_refs = refs[:pps], refs[pps:2 * pps]
    nkv_ref, nlf_ref, o_ref, qbd_ref, m_ref, l_ref, acc_ref, cf_ref = refs[2 * pps:]
    j = pl.program_id(1)
    rows = nt * N_HEADS

    @pl.when(j == 0)
    def _():
        qbd_ref[...] = (_block_diag_queries(q_ref, nt) * SCALE).astype(BF16)
        _softmax_reset(m_ref, l_ref, acc_ref)
        cf_ref[...] = jnp.zeros_like(cf_ref)

    def block(kvs, lfs, mask):
        a = lax.broadcasted_iota(jnp.int32, (PAGE, PAGE), 0)
        b = lax.broadcasted_iota(jnp.int32, (PAGE, PAGE), 1)
        tri = jnp.where(a <= b, 1.0, 0.0).astype(BF16)
        carry = cf_ref[...]
        parts, vts = [], []
        for kv_ref, lf_ref in zip(kvs, lfs):
            lfe = jnp.concatenate([lf_ref[0]] * nt, axis=0)
            c3 = jnp.dot(_split3(lfe), tri, preferred_element_type=F32)
            fk = c3[0:rows] + c3[rows:2 * rows] + c3[2 * rows:3 * rows] + carry
            carry = fk[:, PAGE - 1:PAGE]
            parts.append(jnp.dot(qbd_ref[...], kv_ref[0, 0:BRANCH_W, :].astype(BF16), preferred_element_type=F32) - fk)
            vts.append(kv_ref[0, BRANCH_W:2 * BRANCH_W, :].astype(BF16))
        cf_ref[...] = carry
        s = parts[0] if len(parts) == 1 else jnp.concatenate(parts, axis=1)
        if mask is not None:
            s = jnp.where(mask, s, MASKED)
        _softmax_step_t(s, vts, m_ref, l_ref, acc_ref)

    block(kv_refs, lf_refs, None)

    @pl.when(j == pl.num_programs(1) - 1)
    def _():
        block([nkv_ref], [nlf_ref], _new_token_mask(rows, nt))
        o_ref[0] = _extract_heads(acc_ref[...] * (1.0 / l_ref[...]), nt)


def fox_decode_t_call(pt, q, cache_kvt, cache_lft, new_kvt, new_lft):
    n, nt, _ = q.shape
    npages = pt.shape[1]
    pps = PPS if npages % PPS == 0 else 1
    rows = nt * N_HEADS
    grid_spec = pltpu.PrefetchScalarGridSpec(
        num_scalar_prefetch=1,
        grid=(n, npages // pps),
        in_specs=[pl.BlockSpec((1, nt, BRANCH_W), lambda b, j, pt: (b, 0, 0))]
        + _page_specs(2 * BRANCH_W, 0, pps) + _page_specs(N_HEADS, 0, pps)
        + [pl.BlockSpec((1, 2 * BRANCH_W, PAGE), lambda b, j, pt: (b, 0, 0)),
           pl.BlockSpec((1, N_HEADS, PAGE), lambda b, j, pt: (b, 0, 0))],
        out_specs=pl.BlockSpec((1, nt, BRANCH_W), lambda b, j, pt: (b, 0, 0)),
        scratch_shapes=[pltpu.VMEM((rows, BRANCH_W), BF16), pltpu.VMEM((rows, 1), F32), pltpu.VMEM((rows, 1), F32),
                        pltpu.VMEM((rows, BRANCH_W), F32), pltpu.VMEM((rows, 1), F32)],
    )
    return pl.pallas_call(
        functools.partial(_dfox_t_kernel, nt=nt, pps=pps),
        grid_spec=grid_spec,
        out_shape=jax.ShapeDtypeStruct((n, nt, BRANCH_W), F32),
        compiler_params=_cparams("parallel", "arbitrary"),
        name="fox_decode",
    )(pt, q, *([cache_kvt] * pps), *([cache_lft] * pps), new_kvt, new_lft)


def _dmoba_sel_t_kernel(pt_ref, q_ref, *refs, nt, pps):
    del pt_ref
    k_refs = refs[:pps]
    sel_ref, q32_ref, g_ref = refs[pps:]
    j = pl.program_id(1)
    rows = nt * N_HEADS
    npg = g_ref.shape[1]
    pages_per_block = MOBA_BLOCK // PAGE
    lane = lax.broadcasted_iota(jnp.int32, (rows, npg), 1)

    @pl.when(j == 0)
    def _():
        q32_ref[...] = _block_diag_queries(q_ref, nt)
        g_ref[...] = jnp.zeros_like(g_ref)

    g = g_ref[...]
    for u, k_ref in enumerate(k_refs):
        qk = jnp.dot(q32_ref[...], k_ref[0], preferred_element_type=F32, precision=HIGHEST)
        g = jnp.where(lane == j * pps + u, jnp.sum(qk, axis=1, keepdims=True), g)
    g_ref[...] = g

    @pl.when(j == pl.num_programs(1) - 1)
    def _():
        blk = g
        for u in range(1, pages_per_block):
            blk = blk + pltpu.roll(g, npg - u, 1)
        gs = jnp.where(lane % pages_per_block == 0, blk * (1.0 / MOBA_BLOCK), NEG)
        sel = jnp.zeros((rows, npg), F32)
        for _ in range(MOBA_TOPK):
            mx = jnp.max(gs, axis=1, keepdims=True)
            idx = jnp.min(jnp.where(gs == mx, lane, npg), axis=1, keepdims=True)
            hit = lane == idx
            sel = jnp.where(hit, 1.0, sel)
            gs = jnp.where(hit, MASKED, gs)
        out = sel
        for u in range(1, pages_per_block):
            out = out + pltpu.roll(sel, u, 1)
        sel_ref[0] = out


def moba_select_t_call(pt, q, cache_kvt):
    n, nt, _ = q.shape
    npages = pt.shape[1]
    pps = PPS if npages % PPS == 0 else 1
    rows = nt * N_HEADS
    grid_spec = pltpu.PrefetchScalarGridSpec(
        num_scalar_prefetch=1,
        grid=(n, npages // pps),
        in_specs=[pl.BlockSpec((1, nt, BRANCH_W), lambda b, j, pt: (b, 0, 0))] + _page_specs(BRANCH_W, 0, pps),
        out_specs=pl.BlockSpec((1, rows, npages), lambda b, j, pt: (b, 0, 0)),
        scratch_shapes=[pltpu.VMEM((rows, BRANCH_W), F32), pltpu.VMEM((rows, npages), F32)],
    )
    return pl.pallas_call(
        functools.partial(_dmoba_sel_t_kernel, nt=nt, pps=pps),
        grid_spec=grid_spec,
        out_shape=jax.ShapeDtypeStruct((n, rows, npages), F32),
        compiler_params=_cparams("parallel", "arbitrary"),
        name="moba_select",
    )(pt, q, *([cache_kvt] * pps))


def _dmoba_t_kernel(pt_ref, q_ref, sel_ref, *refs, nt, pps):
    del pt_ref
    kv_refs = refs[:pps]
    nkv_ref, o_ref, qbd_ref, m_ref, l_ref, acc_ref = refs[pps:]
    j = pl.program_id(1)
    nsteps = pl.num_programs(1)
    rows = nt * N_HEADS
    slope = _slope_col(rows)

    @pl.when(j == 0)
    def _():
        qbd_ref[...] = (_block_diag_queries(q_ref, nt) * SCALE).astype(BF16)
        _softmax_reset(m_ref, l_ref, acc_ref)

    def block(kvs, rel, mask):
        parts = [jnp.dot(qbd_ref[...], kv_ref[0, 0:BRANCH_W, :].astype(BF16), preferred_element_type=F32)
                 for kv_ref in kvs]
        s = parts[0] if len(parts) == 1 else jnp.concatenate(parts, axis=1)
        s = jnp.where(mask, s + slope * rel, MASKED)
        _softmax_step_t(s, [kv_ref[0, BRANCH_W:2 * BRANCH_W, :].astype(BF16) for kv_ref in kvs], m_ref, l_ref, acc_ref)

    npg = sel_ref.shape[2]
    prow = lax.broadcasted_iota(jnp.int32, (npg, pps * PAGE), 0)
    pcol = lax.broadcasted_iota(jnp.int32, (npg, pps * PAGE), 1) // PAGE
    keep = jnp.dot(sel_ref[0].astype(BF16), jnp.where(prow == pcol + j * pps, 1.0, 0.0).astype(BF16),
                   preferred_element_type=F32) > 0.5
    klane = lax.broadcasted_iota(jnp.int32, (1, pps * PAGE), 1)
    block(kv_refs, (klane + (j - nsteps) * (pps * PAGE)).astype(F32), keep)

    @pl.when(j == nsteps - 1)
    def _():
        block([nkv_ref], lax.broadcasted_iota(jnp.int32, (1, PAGE), 1).astype(F32), _new_token_mask(rows, nt))
        o_ref[0] = _extract_heads(acc_ref[...] * (1.0 / l_ref[...]), nt)


def moba_decode_t_call(pt, q, sel, cache_kvt, new_kvt):
    n, nt, _ = q.shape
    npages = pt.shape[1]
    pps = PPS if npages % PPS == 0 else 1
    rows = nt * N_HEADS
    grid_spec = pltpu.PrefetchScalarGridSpec(
        num_scalar_prefetch=1,
        grid=(n, npages // pps),
        in_specs=[pl.BlockSpec((1, nt, BRANCH_W), lambda b, j, pt: (b, 0, 0)),
                  pl.BlockSpec((1, rows, npages), lambda b, j, pt: (b, 0, 0))]
        + _page_specs(2 * BRANCH_W, 0, pps)
        + [pl.BlockSpec((1, 2 * BRANCH_W, PAGE), lambda b, j, pt: (b, 0, 0))],
        out_specs=pl.BlockSpec((1, nt, BRANCH_W), lambda b, j, pt: (b, 0, 0)),
        scratch_shapes=[pltpu.VMEM((rows, BRANCH_W), BF16), pltpu.VMEM((rows, 1), F32), pltpu.VMEM((rows, 1), F32),
                        pltpu.VMEM((rows, BRANCH_W), F32)],
    )
    return pl.pallas_call(
        functools.partial(_dmoba_t_kernel, nt=nt, pps=pps),
        grid_spec=grid_spec,
        out_shape=jax.ShapeDtypeStruct((n, nt, BRANCH_W), F32),
        compiler_params=_cparams("parallel", "arbitrary"),
        name="moba_decode",
    )(pt, q, sel, *([cache_kvt] * pps), new_kvt)


def _compress_t_kernel(pt_ref, *refs, rows, pps):
    del pt_ref
    a_refs = refs[:pps]
    pet_ref, peb_ref, wt_ref, wb_ref, w2k_ref, w2v_ref, w2vt_ref, kc_ref, vc_ref, vct_ref, rows_ref = refs[pps:]
    j = pl.program_id(1)
    pr = lax.broadcasted_iota(jnp.int32, (PAGE, PAGE), 0)
    pc = lax.broadcasted_iota(jnp.int32, (PAGE, PAGE), 1)
    perm = jnp.where(pc == CMP_S * (pr % PAGE_ROWS) + pr // PAGE_ROWS, 1.0, 0.0).astype(BF16)
    w2 = 2 * KV_W
    for u, a_ref in enumerate(a_refs):
        p3 = _dot_nt(perm, _split3(a_ref[0]))
        pg = p3[:, 0:w2] + p3[:, w2:2 * w2] + p3[:, 2 * w2:3 * w2]
        r0 = pl.multiple_of((j * pps + u) * PAGE_ROWS, PAGE_ROWS)
        for i in range(CMP_S):
            rows_ref[pl.ds(r0, PAGE_ROWS), i * w2:(i + 1) * w2] = pg[i * PAGE_ROWS:(i + 1) * PAGE_ROWS, :]

    @pl.when(j == pl.num_programs(1) - 1)
    def _():
        chunk = min(256, rows)
        prev_top = jnp.zeros((1, 4 * CMP_HID), F32)
        for c in range(rows // chunk):
            a = rows_ref[c * chunk:(c + 1) * chunk, :]
            ht = jnp.dot((a + pet_ref[...]).astype(BF16), wt_ref[...], preferred_element_type=F32)
            hb = jnp.dot((a + peb_ref[...]).astype(BF16), wb_ref[...], preferred_element_type=F32)
            row = lax.broadcasted_iota(jnp.int32, ht.shape, 0)
            shifted = jnp.where(row == 0, prev_top, pltpu.roll(ht, 1, 0))
            prev_top = ht[chunk - 1:chunk, :]
            g = _gelu_tanh(shifted + hb).astype(BF16)
            kc_ref[0, c * chunk:(c + 1) * chunk, :] = jnp.dot(
                g[:, :2 * CMP_HID], w2k_ref[...], preferred_element_type=F32).astype(BF16)
            vc_ref[0, c * chunk:(c + 1) * chunk, :] = jnp.dot(
                g[:, 2 * CMP_HID:], w2v_ref[...], preferred_element_type=F32).astype(BF16)
            vct_ref[0, :, c * chunk:(c + 1) * chunk] = _dot_nt(w2vt_ref[...], g[:, 2 * CMP_HID:]).astype(BF16)


def compress_t_call(pages_t, page_idx, pe_top, pe_bot, wt, wb, w2k, w2v, w2vt, n, npages):
    rows = npages * PAGE_ROWS
    pps = PPS if npages % PPS == 0 else 1
    const = lambda b, j, pt: (0, 0)
    grid_spec = pltpu.PrefetchScalarGridSpec(
        num_scalar_prefetch=1,
        grid=(n, npages // pps),
        in_specs=_page_specs(2 * KV_W, 0, pps)
        + [pl.BlockSpec((1, CMP_ROW), const),
           pl.BlockSpec((1, CMP_ROW), const),
           pl.BlockSpec((CMP_ROW, 4 * CMP_HID), const, pipeline_mode=pl.Buffered(1)),
           pl.BlockSpec((CMP_ROW, 4 * CMP_HID), const, pipeline_mode=pl.Buffered(1)),
           pl.BlockSpec((2 * CMP_HID, KV_W), const),
           pl.BlockSpec((2 * CMP_HID, KV_W), const),
           pl.BlockSpec((KV_W, 2 * CMP_HID), const)],
        out_specs=[pl.BlockSpec((1, rows, KV_W), lambda b, j, pt: (b, 0, 0)),
                   pl.BlockSpec((1, rows, KV_W), lambda b, j, pt: (b, 0, 0)),
                   pl.BlockSpec((1, KV_W, rows), lambda b, j, pt: (b, 0, 0))],
        scratch_shapes=[pltpu.VMEM((rows, CMP_ROW), F32)],
    )
    return pl.pallas_call(
        functools.partial(_compress_t_kernel, rows=rows, pps=pps),
        grid_spec=grid_spec,
        out_shape=[jax.ShapeDtypeStruct((n, rows, KV_W), BF16),
                   jax.ShapeDtypeStruct((n, rows, KV_W), BF16),
                   jax.ShapeDtypeStruct((n, KV_W, rows), BF16)],
        compiler_params=_cparams("parallel", "arbitrary"),
        name="nsa_compress",
    )(page_idx, *([pages_t] * pps), pe_top, pe_bot, wt, wb, w2k, w2v, w2vt)


def _dnsa_t_kernel(pt_ref, q_ref, gate_ref, kc_ref, vc_ref, *refs, nt, nsb_pad, pps):
    del pt_ref
    kv_refs = refs[:pps]
    nkv_ref, win_ref, nwin_ref, o_ref, qbd_ref, sel_ref, ocmp_ref, m_ref, l_ref, acc_ref = refs[pps:]
    j = pl.program_id(1)
    nsteps = pl.num_programs(1)
    rows = nt * N_HEADS
    ncmp = kc_ref.shape[1]
    q0 = nsteps * (pps * PAGE)
    slope = _slope_col(rows)
    klane = lax.broadcasted_iota(jnp.int32, (1, PAGE), 1)
    blocks_per_page = PAGE // SLC_BLOCK

    @pl.when(j == 0)
    def _():
        pr = lax.broadcasted_iota(jnp.int32, (BRANCH_W, KV_W), 0)
        pc = lax.broadcasted_iota(jnp.int32, (BRANCH_W, KV_W), 1)
        place = jnp.where((pr % HEAD_DIM == pc % HEAD_DIM) & (pc // HEAD_DIM == pr // (NSA_R * HEAD_DIM)), 1.0, 0.0)
        qg = jnp.dot(_block_diag_queries(q_ref, nt), place, preferred_element_type=F32, precision=HIGHEST)
        qbd = (qg * SCALE).astype(BF16)
        qbd_ref[...] = qbd
        c = lax.broadcasted_iota(jnp.int32, (1, ncmp), 1)
        s = _dot_nt(qbd, kc_ref[0]) + slope * (CMP_S * c + (CMP_S - 1) - q0).astype(F32)
        valid = c >= 1
        s = jnp.where(valid, s, MASKED)
        p = jnp.where(valid, jnp.exp(s - jnp.max(s, axis=1, keepdims=True)), 0.0)
        p = p * (1.0 / jnp.sum(p, axis=1, keepdims=True))
        ocmp_ref[...] = gate_ref[0, :, 0:1] * jnp.dot(p.astype(BF16), vc_ref[0], preferred_element_type=F32)
        gr = lax.broadcasted_iota(jnp.int32, (nt * NSA_G, rows), 0)
        gc = lax.broadcasted_iota(jnp.int32, (nt * NSA_G, rows), 1) // NSA_R
        imp = jnp.dot(jnp.where(gr == gc, 1.0, 0.0), p, preferred_element_type=F32, precision=HIGHEST)
        ratio = SLC_BLOCK // CMP_S
        mr = lax.broadcasted_iota(jnp.int32, (ncmp, nsb_pad), 0)
        mb = lax.broadcasted_iota(jnp.int32, (ncmp, nsb_pad), 1)
        gather_m = jnp.where((mr >= ratio * mb) & (mr <= ratio * mb + ratio) & (mr >= 1), 1.0, 0.0)
        p_slc = jnp.dot(imp, gather_m, preferred_element_type=F32, precision=HIGHEST)
        bj = lax.broadcasted_iota(jnp.int32, p_slc.shape, 1)
        own = q0 // SLC_BLOCK
        forced = (bj == 0) | (bj >= own - 1)
        allowed = bj <= own
        score = jnp.where(allowed, jnp.where(forced, p_slc + FORCE_SCORE, p_slc), -1.0)
        sel = jnp.zeros(p_slc.shape, F32)
        for _ in range(SLC_N):
            mx = jnp.max(score, axis=1, keepdims=True)
            idx = jnp.min(jnp.where(score == mx, bj, nsb_pad), axis=1, keepdims=True)
            hit = bj == idx
            sel = jnp.where(hit, 1.0, sel)
            score = jnp.where(hit, -2.0, score)
        sel = jnp.where(allowed, sel, 0.0)
        er = lax.broadcasted_iota(jnp.int32, (rows, nt * NSA_G), 0) // NSA_R
        ec = lax.broadcasted_iota(jnp.int32, (rows, nt * NSA_G), 1)
        sel_ref[...] = jnp.dot(jnp.where(er == ec, 1.0, 0.0), sel, preferred_element_type=F32).astype(BF16)
        _softmax_reset(m_ref, l_ref, acc_ref)

    def block(kvs, rel, mask):
        parts = [jnp.dot(qbd_ref[...], kv_ref[0, 0:KV_W, :].astype(BF16), preferred_element_type=F32) for kv_ref in kvs]
        s = (parts[0] if len(parts) == 1 else jnp.concatenate(parts, axis=1)) + slope * rel
        if mask is not None:
            s = jnp.where(mask, s, MASKED)
        _softmax_step_t(s, [kv_ref[0, KV_W:2 * KV_W, :].astype(BF16) for kv_ref in kvs], m_ref, l_ref, acc_ref)

    brow = lax.broadcasted_iota(jnp.int32, (nsb_pad, pps * PAGE), 0)
    bcol = lax.broadcasted_iota(jnp.int32, (nsb_pad, pps * PAGE), 1) // SLC_BLOCK
    expand = jnp.where(brow == bcol + j * (pps * blocks_per_page), 1.0, 0.0).astype(BF16)
    keep = jnp.dot(sel_ref[...], expand, preferred_element_type=F32) > 0.5
    kl = lax.broadcasted_iota(jnp.int32, (1, pps * PAGE), 1)
    block(kv_refs, (kl + (j - nsteps) * (pps * PAGE)).astype(F32), keep)

    @pl.when(j == nsteps - 1)
    def _():
        new_mask = _new_token_mask(rows, nt)
        block([nkv_ref], klane.astype(F32), new_mask)
        o_slc = acc_ref[...] * (1.0 / l_ref[...])
        _softmax_reset(m_ref, l_ref, acc_ref)
        wlen = win_ref.shape[2]
        wi = lax.broadcasted_iota(jnp.int32, (rows, wlen), 1)
        wt = lax.broadcasted_iota(jnp.int32, (rows, wlen), 0) // N_HEADS
        wrel = (lax.broadcasted_iota(jnp.int32, (1, wlen), 1) - wlen).astype(F32)
        block([win_ref], wrel, wi > wt + (wlen - WINDOW))
        block([nwin_ref], klane.astype(F32), new_mask)
        o_win = acc_ref[...] * (1.0 / l_ref[...])
        o = ocmp_ref[...] + gate_ref[0, :, 1:2] * o_slc + gate_ref[0, :, 2:3] * o_win
        ur = lax.broadcasted_iota(jnp.int32, (KV_W, BRANCH_W), 0)
        uc = lax.broadcasted_iota(jnp.int32, (KV_W, BRANCH_W), 1)
        unplace = jnp.where((ur % HEAD_DIM == uc % HEAD_DIM) & (ur // HEAD_DIM == uc // (NSA_R * HEAD_DIM)), 1.0, 0.0)
        o_ref[0] = _extract_heads(jnp.dot(o, unplace, preferred_element_type=F32, precision=HIGHEST), nt)


def nsa_decode_t_call(pt, q, gate, kcmp, vcmp, cache_kvt, new_kvt, win_t, new_win_t):
    n, nt, _ = q.shape
    npages = pt.shape[1]
    pps = PPS if npages % PPS == 0 else 1
    rows = nt * N_HEADS
    ncmp = kcmp.shape[1]
    nsb = (npages * PAGE) // SLC_BLOCK + 1
    nsb_pad = -(-nsb // 128) * 128
    wlen = win_t.shape[2]
    grid_spec = pltpu.PrefetchScalarGridSpec(
        num_scalar_prefetch=1,
        grid=(n, npages // pps),
        in_specs=[pl.BlockSpec((1, nt, BRANCH_W), lambda b, j, pt: (b, 0, 0)),
                  pl.BlockSpec((1, rows, 3), lambda b, j, pt: (b, 0, 0)),
                  pl.BlockSpec((1, ncmp, KV_W), lambda b, j, pt: (b, 0, 0)),
                  pl.BlockSpec((1, ncmp, KV_W), lambda b, j, pt: (b, 0, 0))]
        + _page_specs(2 * KV_W, 1, pps)
        + [pl.BlockSpec((1, 2 * KV_W, PAGE), lambda b, j, pt: (b, 0, 0)),
           pl.BlockSpec((1, 2 * KV_W, wlen), lambda b, j, pt: (b, 0, 0)),
           pl.BlockSpec((1, 2 * KV_W, PAGE), lambda b, j, pt: (b, 0, 0))],
        out_specs=pl.BlockSpec((1, nt, BRANCH_W), lambda b, j, pt: (b, 0, 0)),
        scratch_shapes=[pltpu.VMEM((rows, KV_W), BF16), pltpu.VMEM((rows, nsb_pad), BF16),
                        pltpu.VMEM((rows, KV_W), F32), pltpu.VMEM((rows, 1), F32), pltpu.VMEM((rows, 1), F32),
                        pltpu.VMEM((rows, KV_W), F32)],
    )
    return pl.pallas_call(
        functools.partial(_dnsa_t_kernel, nt=nt, nsb_pad=nsb_pad, pps=pps),
        grid_spec=grid_spec,
        out_shape=jax.ShapeDtypeStruct((n, nt, BRANCH_W), F32),
        compiler_params=_cparams("parallel", "arbitrary"),
        name="nsa_decode",
    )(pt, q, gate, kcmp, vcmp, *([cache_kvt] * pps), new_kvt, win_t, new_win_t)


def _compress_weights(cmp_w1, cmp_w2, cmp_pe):
    w1 = cmp_w1.reshape(2, CMP_L, HEAD_DIM, CMP_HID)
    eye_w = jnp.eye(2, dtype=F32)
    eye_g = jnp.eye(NSA_G, dtype=F32)
    big = jnp.einsum('widh,sw,gk->isgdwkh', w1, eye_w, eye_g)
    big = big.reshape(CMP_L, 2 * KV_W, 4 * CMP_HID)
    wt = big[:CMP_S].reshape(CMP_ROW, 4 * CMP_HID).astype(BF16)
    wb = big[CMP_S:].reshape(CMP_ROW, 4 * CMP_HID).astype(BF16)
    w2k = jnp.einsum('hd,gk->ghkd', cmp_w2[0], eye_g).reshape(2 * CMP_HID, KV_W).astype(BF16)
    w2v = jnp.einsum('hd,gk->ghkd', cmp_w2[1], eye_g).reshape(2 * CMP_HID, KV_W).astype(BF16)
    w2vt = jnp.einsum('hd,gk->kdgh', cmp_w2[1], eye_g).reshape(KV_W, 2 * CMP_HID).astype(BF16)
    pe = jnp.broadcast_to(cmp_pe[:, :, None, :], (2, CMP_L, NSA_G, HEAD_DIM))
    pe = jnp.transpose(pe, (1, 0, 2, 3)).reshape(CMP_L, 2 * KV_W)
    pe_top = pe[:CMP_S].reshape(1, CMP_ROW)
    pe_bot = pe[CMP_S:].reshape(1, CMP_ROW)
    return pe_top, pe_bot, wt, wb, w2k, w2v, w2vt


def _layer_weights(l, w_in, w_branch, w_out, w_up, w_down):
    wl = w_in[l]
    small = jnp.concatenate(
        [wl[:, OFF_FOX_F:OFF_FOX_F + N_HEADS], wl[:, OFF_NSA_GATE:OFF_NSA_GATE + 3 * N_HEADS],
         jnp.zeros((D_MODEL, SMALL_W - 4 * N_HEADS), wl.dtype)], axis=1)
    return dict(
        conv=wl[:, OFF_CONV:OFF_FOX].astype(BF16),
        fox=wl[:, OFF_FOX:OFF_FOX_F].astype(BF16),
        moba=wl[:, OFF_MOBA:OFF_NSA].astype(BF16),
        nsa=wl[:, OFF_NSA:OFF_NSA_GATE].astype(BF16),
        small=small.astype(BF16),
        gate=wl[:, OFF_MERGE:].astype(BF16),
        branch=w_branch[l].astype(BF16),
        out=w_out[l].astype(BF16),
        up=w_up[l].astype(BF16),
        down=w_down[l].astype(BF16),
    )


def _finish_layer(x, branches, gate, w, g_mlp, g_next, next_dtype, tm):
    merged = merge_call(branches, gate, w["branch"], tm, 512)
    x1, hm = outproj_call(merged, w["out"], x, g_mlp, tm)
    return mlp_call(hm, w["up"], w["down"], x1, g_next, next_dtype, tm, 512)


def prompt_mixers(h, n, t, w, b_forget, conv_w, cmp_w, tm):
    bw = BRANCH_W
    (z_conv,) = proj_call(h, w["conv"], [("f32", 3 * bw)], [(a, a + 512, ((0, a),)) for a in (0, 512, 1024)],
                          tm, "proj_conv")
    qkv_defs = [("f32", bw), ("f32", 2 * bw), ("bf16", bw), ("bf16T", bw)]
    qkv_plan = [(0, bw, ((0, 0),)), (bw, 2 * bw, ((1, 0), (2, 0))), (2 * bw, 3 * bw, ((1, bw), (3, 0)))]
    fox_q, fox_kv, fox_kb, fox_vt = proj_call(h, w["fox"], qkv_defs, qkv_plan, tm, "proj_fox")
    moba_q, moba_kv, moba_kb, moba_vt, moba_km = proj_call(
        h, w["moba"], qkv_defs + [("blockmean", bw)],
        [qkv_plan[0], (bw, 2 * bw, ((1, 0), (2, 0), (4, 0))), qkv_plan[2]], tm, "proj_moba")
    kv = KV_W
    nsa_q, nsa_kv, nsa_win, nsa_ks, nsa_vst, nsa_kw, nsa_vwt, nsa_cmp_t = proj_call(
        h, w["nsa"],
        [("f32", bw), ("f32", 4 * kv), ("f32", 2 * kv), ("bf16", kv), ("bf16T", kv), ("bf16", kv), ("bf16T", kv),
         ("f32T", 2 * kv)],
        [(0, bw, ((0, 0),)), (bw, bw + 2 * kv, ((1, 0), (7, 0))),
         (bw + 2 * kv, bw + 3 * kv, ((1, 2 * kv), (3, 0))), (bw + 3 * kv, bw + 4 * kv, ((1, 3 * kv), (4, 0))),
         (bw + 4 * kv, bw + 5 * kv, ((2, 0), (5, 0))), (bw + 5 * kv, bw + 6 * kv, ((2, kv), (6, 0)))],
        tm, "proj_nsa")
    (z_small,) = proj_call(h, w["small"], [("f32", SMALL_W)], [(0, SMALL_W, ((0, 0),))], tm, "proj_small")

    out_a, new_conv = conv_prompt_call(z_conv, conv_w, n, t, tm)
    lg, fk = small_call(z_small, b_forget, n, t, tm)
    logf = lg[:, :N_HEADS].reshape(n, t, N_HEADS)
    gate_t = lg[:, N_HEADS:4 * N_HEADS].T

    out_b = fox_prompt_call(fox_q, fox_kb, fox_vt, fk, n, t)
    out_c = moba_prompt_call(moba_q, moba_kb, moba_vt, moba_km, n, t)

    npages = t // PAGE
    page_idx = jnp.arange(n * npages, dtype=jnp.int32).reshape(n, npages)
    kcmp, _, vcmpt = compress_t_call(nsa_cmp_t, page_idx, *cmp_w, n, npages)
    out_d = nsa_prompt_call(nsa_q, kcmp, vcmpt, nsa_ks, nsa_vst, nsa_kw, nsa_vwt, gate_t, n, t)

    wb = min(WINDOW, t)
    new_state = (new_conv,
                 fox_kv.reshape(n, t, 2, N_HEADS, HEAD_DIM),
                 logf,
                 moba_kv.reshape(n, t, 2, N_HEADS, HEAD_DIM),
                 nsa_kv.reshape(n, t, 2, 2, NSA_G, HEAD_DIM),
                 nsa_win.reshape(n, t, 2, NSA_G, HEAD_DIM)[:, t - wb:])
    return [out_a, out_b, out_c, out_d], new_state


def sample_mixers(h, n, t, q0, past, w, b_forget, conv_w, cmp_w1, cmp_w2, cmp_pe, tm):
    dt = F32
    bw = BRANCH_W
    kv = KV_W
    (z_conv,) = proj_call(h, w["conv"], [("f32", 3 * bw)], [(0, 3 * bw, ((0, 0),))], tm, "proj_conv_s")
    fox_q, fox_kv = proj_call(h, w["fox"], [("f32", bw), ("f32", 2 * bw)],
                              [(0, bw, ((0, 0),)), (bw, 3 * bw, ((1, 0),))], tm, "proj_fox_s")
    moba_q, moba_kv = proj_call(h, w["moba"], [("f32", bw), ("f32", 2 * bw)],
                                [(0, bw, ((0, 0),)), (bw, 3 * bw, ((1, 0),))], tm, "proj_moba_s")
    nsa_q, nsa_kv, nsa_win = proj_call(
        h, w["nsa"], [("f32", bw), ("f32", 4 * kv), ("f32", 2 * kv)],
        [(0, bw, ((0, 0),)), (bw, bw + 4 * kv, ((1, 0),)), (bw + 4 * kv, bw + 6 * kv, ((2, 0),))], tm, "proj_nsa_s")
    (z_small,) = proj_call(h, w["small"], [("f32", SMALL_W)], [(0, SMALL_W, ((0, 0),))], tm, "proj_small_s")

    def heads(a, nh):
        return a.reshape(n, t, nh, HEAD_DIM)

    def cat(old, new):
        return jnp.concatenate([old.astype(new.dtype), new], axis=1)

    zc = z_conv.reshape(n, t, 3 * bw)
    conv_x, conv_b, conv_c = zc[..., :bw], zc[..., bw:2 * bw], zc[..., 2 * bw:]
    u = conv_c * conv_x
    ext = jnp.concatenate([past['conv'].astype(dt), u], axis=1)
    y_conv = ext[:, 0:t] * conv_w[0]
    for j in range(1, CONV_W):
        y_conv = y_conv + ext[:, j:j + t] * conv_w[j]
    out_a = conv_b * y_conv
    new_conv = ext[:, -(CONV_W - 1):]

    fkv = fox_kv.reshape(n, t, 2, N_HEADS, HEAD_DIM)
    fq, fk, fv = heads(fox_q, N_HEADS), fkv[:, :, 0], fkv[:, :, 1]
    fox_f = z_small.reshape(n, t, SMALL_W)[..., :N_HEADS]
    logf = jax.nn.log_sigmoid(fox_f + b_forget)
    lf_all = jnp.concatenate([past['fox_logf'].astype(F32), logf], axis=1)
    out_b = fox_attention(fq, cat(past['fox_k'], fk), cat(past['fox_v'], fv), lf_all, q0)

    mkv = moba_kv.reshape(n, t, 2, N_HEADS, HEAD_DIM)
    mq, mk, mv = heads(moba_q, N_HEADS), mkv[:, :, 0], mkv[:, :, 1]
    out_c = moba_attention(mq, cat(past['moba_k'], mk), cat(past['moba_v'], mv), q0, alibi_slopes(N_HEADS))

    nq = heads(nsa_q, N_HEADS)
    nkv = nsa_kv.reshape(n, t, 2, 2, NSA_G, HEAD_DIM)
    kc, vc, ks, vs = nkv[:, :, 0, 0], nkv[:, :, 0, 1], nkv[:, :, 1, 0], nkv[:, :, 1, 1]
    nwin = nsa_win.reshape(n, t, 2, NSA_G, HEAD_DIM)
    kw, vw = nwin[:, :, 0], nwin[:, :, 1]
    ngate = jax.nn.sigmoid(z_small.reshape(n, t, SMALL_W)[..., N_HEADS:4 * N_HEADS]).reshape(n, t, NSA_G, NSA_R, 3)
    kw_all = cat(past['win_k'], kw)
    vw_all = cat(past['win_v'], vw)
    w_ofs = q0 - past['win_k'].shape[1]
    out_d = nsa_attention(nq, cat(past['nsa_kc'], kc), cat(past['nsa_vc'], vc),
                          cat(past['nsa_ks'], ks), cat(past['nsa_vs'], vs),
                          kw_all, vw_all, ngate, q0, w_ofs, alibi_slopes(N_HEADS), cmp_w1, cmp_w2, cmp_pe)
    branches = [a.reshape(n * t, bw) for a in (out_a, out_b, out_c, out_d)]
    wb = past['win_k'].shape[1]
    new_state = (new_conv, fkv, logf.astype(dt), mkv, nkv,
                 jnp.stack([kw_all, vw_all], axis=2)[:, -wb:])
    return branches, new_state


def decode_mixers(h, n, nt, pt, caches, state_conv, state_win, w, b_forget, conv_w, cmp_w, tm):
    bw = BRANCH_W
    kv = KV_W
    cache_fox, cache_lf, cache_moba, cache_nsa = caches
    (z_conv,) = proj_call(h, w["conv"], [("f32", 3 * bw)], [(0, 3 * bw, ((0, 0),))], tm, "proj_conv_s")
    fox_q, fox_kv = proj_call(h, w["fox"], [("f32", bw), ("f32", 2 * bw)],
                              [(0, bw, ((0, 0),)), (bw, 3 * bw, ((1, 0),))], tm, "proj_fox_s")
    moba_q, moba_kv = proj_call(h, w["moba"], [("f32", bw), ("f32", 2 * bw)],
                                [(0, bw, ((0, 0),)), (bw, 3 * bw, ((1, 0),))], tm, "proj_moba_s")
    nsa_q, nsa_kv, nsa_win = proj_call(
        h, w["nsa"], [("f32", bw), ("f32", 4 * kv), ("f32", 2 * kv)],
        [(0, bw, ((0, 0),)), (bw, bw + 4 * kv, ((1, 0),)), (bw + 4 * kv, bw + 6 * kv, ((2, 0),))], tm, "proj_nsa_s")
    (z_small,) = proj_call(h, w["small"], [("f32", SMALL_W)], [(0, SMALL_W, ((0, 0),))], tm, "proj_small_s")

    def new_page(a):
        at = jnp.transpose(a.reshape(n, nt, a.shape[-1]), (0, 2, 1))
        return jnp.pad(at, ((0, 0), (0, 0), (0, PAGE - nt)))

    out_a, u = conv_sample_call(z_conv, state_conv, conv_w, n, nt)
    new_conv = u.reshape(n, nt, bw)[:, nt - (CONV_W - 1):]

    lg, _ = small_call(z_small, b_forget, 1, n * nt, n * nt)
    logf = lg[:, :N_HEADS]
    gate = lg[:, N_HEADS:4 * N_HEADS].reshape(n, nt * N_HEADS, 3)

    out_b = fox_decode_t_call(pt, fox_q.reshape(n, nt, bw), cache_fox, cache_lf, new_page(fox_kv), new_page(logf))
    mq = moba_q.reshape(n, nt, bw)
    sel = moba_select_t_call(pt, mq, cache_moba)
    out_c = moba_decode_t_call(pt, mq, sel, cache_moba, new_page(moba_kv))

    npages = pt.shape[1]
    kcmp, vcmp, _ = compress_t_call(cache_nsa, pt, *cmp_w, n, npages)
    win = state_win.reshape(n, state_win.shape[1], 2 * kv)
    out_d = nsa_decode_t_call(pt, nsa_q.reshape(n, nt, bw), gate, kcmp, vcmp, cache_nsa,
                              new_page(nsa_kv[:, 2 * kv:]), jnp.transpose(win, (0, 2, 1)), new_page(nsa_win))

    wb = win.shape[1]
    win_all = jnp.concatenate([win, nsa_win.reshape(n, nt, 2 * kv)], axis=1)[:, nt:]
    new_state = (new_conv,
                 fox_kv.reshape(n, nt, 2, N_HEADS, HEAD_DIM),
                 logf.reshape(n, nt, N_HEADS),
                 moba_kv.reshape(n, nt, 2, N_HEADS, HEAD_DIM),
                 nsa_kv.reshape(n, nt, 2, 2, NSA_G, HEAD_DIM),
                 win_all.reshape(n, wb, 2, NSA_G, HEAD_DIM))
    branches = [out_a] + [o.reshape(n * nt, bw) for o in (out_b, out_c, out_d)]
    return branches, new_state


def kernel(x_prompt, x_sample, state_conv, cache_fox_kv, cache_fox_logf, cache_moba_kv, cache_nsa_kv,
           state_nsa_win, page_table, g_mix, w_in, b_forget, conv_w, cmp_w1, cmp_w2, cmp_pe,
           w_branch, w_out, g_mlp, w_up, w_down, g_final):
    depth = w_in.shape[0]
    nb, seq, _ = x_prompt.shape
    db, dseq, _ = x_sample.shape
    tm_p, tm_s = 512, db * dseq
    pool, page = cache_fox_kv.shape[1], cache_fox_kv.shape[2]
    def pages_t(c, width):
        return jnp.transpose(c.reshape(depth * pool, page, width), (0, 2, 1))

    caches = (pages_t(cache_fox_kv, 2 * BRANCH_W), pages_t(cache_fox_logf, N_HEADS),
              pages_t(cache_moba_kv, 2 * BRANCH_W), pages_t(cache_nsa_kv, 4 * KV_W))

    xp = x_prompt.reshape(nb * seq, D_MODEL)
    xs = x_sample.reshape(db * dseq, D_MODEL)
    hp = rms_norm_call(xp, g_mix[0], BF16, tm_p)
    hs = rms_norm_call(xs, g_mix[0], BF16, tm_s)
    new_p, new_s = [], []
    for l in range(depth):
        w = _layer_weights(l, w_in, w_branch, w_out, w_up, w_down)
        cmp_w = _compress_weights(cmp_w1[l], cmp_w2[l], cmp_pe[l])
        last = l == depth - 1
        g_next = g_final if last else g_mix[l + 1]
        next_dtype = F32 if last else BF16

        branches, st_p = prompt_mixers(hp, nb, seq, w, b_forget[l], conv_w[l], cmp_w, tm_p)
        gate_p = gate_call(hp, w["gate"], tm_p, 1024)
        xp, hp = _finish_layer(xp, branches, gate_p, w, g_mlp[l], g_next, next_dtype, tm_p)
        new_p.append(st_p)

        branches, st_s = decode_mixers(hs, db, dseq, page_table + l * pool, caches, state_conv[l],
                                       state_nsa_win[l], w, b_forget[l], conv_w[l], cmp_w, tm_s)
        gate_s = gate_call(hs, w["gate"], tm_s, 1024)
        xs, hs = _finish_layer(xs, branches, gate_s, w, g_mlp[l], g_next, next_dtype, tm_s)
        new_s.append(st_s)
    y_prompt = hp.reshape(nb, seq, D_MODEL)
    y_sample = hs.reshape(db, dseq, D_MODEL)
    conv_p, fox_kv_p, fox_logf_p, moba_kv_p, nsa_kv_p, win_p = [jnp.stack(a) for a in zip(*new_p)]
    conv_s, fox_kv_s, fox_logf_s, moba_kv_s, nsa_kv_s, win_s = [jnp.stack(a) for a in zip(*new_s)]
    return (y_prompt, y_sample, conv_p, conv_s, fox_kv_p, fox_kv_s, fox_logf_p, fox_logf_s,
            moba_kv_p, moba_kv_s, nsa_kv_p, nsa_kv_s, win_p, win_s)
```

```python
import functools

import jax
import jax.numpy as jnp
from jax import lax
from jax.experimental import pallas as pl
from jax.experimental.pallas import tpu as pltpu

F32 = jnp.float32
BF16 = jnp.bfloat16
HIGHEST = lax.Precision.HIGHEST

D_MODEL = 2048
HEAD_DIM = 64
N_BRANCH = 4
BRANCH_W = D_MODEL // N_BRANCH
N_HEADS = BRANCH_W // HEAD_DIM
CONV_W = 3
NSA_G = 2
NSA_R = N_HEADS // NSA_G
MOBA_BLOCK = 256
MOBA_TOPK = 3
CMP_L = 32
CMP_S = 16
CMP_HID = 4 * HEAD_DIM
SLC_BLOCK = 64
SLC_N = 16
WINDOW = 512
D_FF = 4 * D_MODEL
Q_BLOCK = 128
SPARSE_Q_BLOCK = 32
RMS_EPS = 1e-6
NEG = -1e30
MASKED = 2.0 * NEG
FORCE_SCORE = 1e4
KV_W = NSA_G * HEAD_DIM
SCALE = HEAD_DIM ** -0.5
PAIR_W = 2 * HEAD_DIM

OFF_CONV = 0
OFF_FOX = 3 * BRANCH_W
OFF_FOX_F = OFF_FOX + 3 * BRANCH_W
OFF_MOBA = OFF_FOX_F + N_HEADS
OFF_NSA = OFF_MOBA + 3 * BRANCH_W
OFF_NSA_GATE = OFF_NSA + BRANCH_W + 6 * KV_W
OFF_MERGE = OFF_NSA_GATE + 3 * N_HEADS
IN_W = OFF_MERGE + N_BRANCH * D_MODEL
SMALL_W = 128

TQ = 256
TK = 256
CMP_ROW = CMP_S * 2 * KV_W
PAGE_ROWS = 8

VMEM_LIMIT = 56 * 1024 * 1024


def _cparams(*sem):
    return pltpu.CompilerParams(dimension_semantics=sem, vmem_limit_bytes=VMEM_LIMIT)


def _rms(x, g):
    return x * lax.rsqrt(jnp.mean(x * x, axis=-1, keepdims=True) + RMS_EPS) * g


def _dot_nt(a, b, precision=None):
    return lax.dot_general(a, b, (((1,), (1,)), ((), ())), preferred_element_type=F32, precision=precision)


def _norm_kernel(x_ref, g_ref, o_ref):
    o_ref[...] = _rms(x_ref[...], g_ref[...]).astype(o_ref.dtype)


def rms_norm_call(x, g, out_dtype, tm):
    t, d = x.shape
    return pl.pallas_call(
        _norm_kernel,
        grid=(t // tm,),
        in_specs=[pl.BlockSpec((tm, d), lambda i: (i, 0)),
                  pl.BlockSpec((1, d), lambda i: (0, 0))],
        out_specs=pl.BlockSpec((tm, d), lambda i: (i, 0)),
        out_shape=jax.ShapeDtypeStruct((t, d), out_dtype),
        compiler_params=_cparams("parallel"),
        name="rms_norm",
    )(x, g.reshape(1, d))


def _proj_kernel(h_ref, w_ref, *out_refs, kinds, plan):
    h = h_ref[...]
    for c0, c1, dests in plan:
        z = jnp.dot(h, w_ref[:, c0:c1], preferred_element_type=F32)
        for idx, off in dests:
            o_ref, kind = out_refs[idx], kinds[idx]
            if kind == "bf16T":
                for r in range(z.shape[0] // TK):
                    o_ref[r, off:off + c1 - c0, :] = z[r * TK:(r + 1) * TK].T.astype(BF16)
            elif kind == "f32T":
                for r in range(z.shape[0] // 128):
                    o_ref[r, off:off + c1 - c0, :] = z[r * 128:(r + 1) * 128].T
            elif kind == "blockmean":
                for r in range(z.shape[0] // MOBA_BLOCK):
                    o_ref[r, :, off:off + c1 - c0] = jnp.mean(
                        z[r * MOBA_BLOCK:(r + 1) * MOBA_BLOCK], axis=0, keepdims=True)
            else:
                o_ref[:, off:off + c1 - c0] = z.astype(o_ref.dtype)


def proj_call(h, w, out_defs, plan, tm, name):
    t, d = h.shape
    n = w.shape[1]
    out_specs, out_shapes = [], []
    for kind, width in out_defs:
        if kind == "bf16T":
            out_specs.append(pl.BlockSpec((tm // TK, width, TK), lambda i: (i, 0, 0)))
            out_shapes.append(jax.ShapeDtypeStruct((t // TK, width, TK), BF16))
        elif kind == "f32T":
            out_specs.append(pl.BlockSpec((tm // 128, width, 128), lambda i: (i, 0, 0)))
            out_shapes.append(jax.ShapeDtypeStruct((t // 128, width, 128), F32))
        elif kind == "blockmean":
            out_specs.append(pl.BlockSpec((tm // MOBA_BLOCK, 1, width), lambda i: (i, 0, 0)))
            out_shapes.append(jax.ShapeDtypeStruct((t // MOBA_BLOCK, 1, width), F32))
        else:
            out_specs.append(pl.BlockSpec((tm, width), lambda i: (i, 0)))
            out_shapes.append(jax.ShapeDtypeStruct((t, width), BF16 if kind == "bf16" else F32))
    return pl.pallas_call(
        functools.partial(_proj_kernel, kinds=tuple(k for k, _ in out_defs), plan=tuple(plan)),
        grid=(t // tm,),
        in_specs=[pl.BlockSpec((tm, d), lambda i: (i, 0)),
                  pl.BlockSpec((d, n), lambda i: (0, 0))],
        out_specs=out_specs,
        out_shape=out_shapes,
        compiler_params=_cparams("parallel"),
        name=name,
    )(h, w)


def _gate_kernel(h_ref, w_ref, o_ref):
    z = jnp.dot(h_ref[...], w_ref[...], preferred_element_type=F32)
    o_ref[...] = jax.nn.sigmoid(z)


def gate_call(h, w, tm, tn):
    t, d = h.shape
    n = w.shape[1]
    return pl.pallas_call(
        _gate_kernel,
        grid=(t // tm, n // tn),
        in_specs=[pl.BlockSpec((tm, d), lambda i, j: (i, 0)),
                  pl.BlockSpec((d, tn), lambda i, j: (0, j))],
        out_specs=pl.BlockSpec((tm, tn), lambda i, j: (i, j)),
        out_shape=jax.ShapeDtypeStruct((t, n), F32),
        compiler_params=_cparams("parallel", "arbitrary"),
        name="merge_gate_proj",
    )(h, w)


def _merge_kernel(oa_ref, ob_ref, oc_ref, od_ref, g0_ref, g1_ref, g2_ref, g3_ref, wb_ref, o_ref):
    acc = None
    for b, (o, g) in enumerate(zip((oa_ref, ob_ref, oc_ref, od_ref), (g0_ref, g1_ref, g2_ref, g3_ref))):
        br = jnp.dot(o[...].astype(BF16), wb_ref[b], preferred_element_type=F32)
        term = g[...] * br
        acc = term if acc is None else acc + term
    o_ref[...] = acc.astype(o_ref.dtype)


def merge_call(branches, gate, w_branch, tm, tn):
    t = branches[0].shape[0]
    nj = D_MODEL // tn
    gate_specs = [pl.BlockSpec((tm, tn), functools.partial(lambda i, j, b: (i, b * nj + j), b=b))
                  for b in range(N_BRANCH)]
    return pl.pallas_call(
        _merge_kernel,
        grid=(t // tm, nj),
        in_specs=[pl.BlockSpec((tm, BRANCH_W), lambda i, j: (i, 0))] * N_BRANCH + gate_specs
        + [pl.BlockSpec((N_BRANCH, BRANCH_W, tn), lambda i, j: (0, 0, j))],
        out_specs=pl.BlockSpec((tm, tn), lambda i, j: (i, j)),
        out_shape=jax.ShapeDtypeStruct((t, D_MODEL), BF16),
        compiler_params=_cparams("parallel", "arbitrary"),
        name="branch_merge",
    )(*branches, gate, gate, gate, gate, w_branch)


def _outproj_kernel(m_ref, w_ref, x_ref, g_ref, xo_ref, hn_ref):
    xn = x_ref[...] + jnp.dot(m_ref[...], w_ref[...], preferred_element_type=F32)
    xo_ref[...] = xn
    hn_ref[...] = _rms(xn, g_ref[...]).astype(hn_ref.dtype)


def outproj_call(merged, w_out, x, g_next, tm):
    t = x.shape[0]
    return pl.pallas_call(
        _outproj_kernel,
        grid=(t // tm,),
        in_specs=[pl.BlockSpec((tm, D_MODEL), lambda i: (i, 0)),
                  pl.BlockSpec((D_MODEL, D_MODEL), lambda i: (0, 0)),
                  pl.BlockSpec((tm, D_MODEL), lambda i: (i, 0)),
                  pl.BlockSpec((1, D_MODEL), lambda i: (0, 0))],
        out_specs=[pl.BlockSpec((tm, D_MODEL), lambda i: (i, 0)),
                   pl.BlockSpec((tm, D_MODEL), lambda i: (i, 0))],
        out_shape=[jax.ShapeDtypeStruct((t, D_MODEL), F32),
                   jax.ShapeDtypeStruct((t, D_MODEL), BF16)],
        compiler_params=_cparams("parallel"),
        name="out_proj",
    )(merged, w_out, x, g_next.reshape(1, D_MODEL))


def _mlp_kernel(h_ref, wu_ref, wd_ref, x_ref, g_ref, xo_ref, hn_ref, acc_ref):
    j = pl.program_id(1)

    @pl.when(j == 0)
    def _():
        acc_ref[...] = jnp.zeros_like(acc_ref)

    a = jnp.dot(h_ref[...], wu_ref[...], preferred_element_type=F32)
    a = jnp.square(jnp.maximum(a, 0.0)).astype(BF16)
    acc_ref[...] += jnp.dot(a, wd_ref[...], preferred_element_type=F32)

    @pl.when(j == pl.num_programs(1) - 1)
    def _():
        xn = x_ref[...] + acc_ref[...]
        xo_ref[...] = xn
        hn_ref[...] = _rms(xn, g_ref[...]).astype(hn_ref.dtype)


def mlp_call(h, w_up, w_down, x, g_next, next_dtype, tm, tf):
    t = x.shape[0]
    return pl.pallas_call(
        _mlp_kernel,
        grid=(t // tm, D_FF // tf),
        in_specs=[pl.BlockSpec((tm, D_MODEL), lambda i, j: (i, 0)),
                  pl.BlockSpec((D_MODEL, tf), lambda i, j: (0, j)),
                  pl.BlockSpec((tf, D_MODEL), lambda i, j: (j, 0)),
                  pl.BlockSpec((tm, D_MODEL), lambda i, j: (i, 0)),
                  pl.BlockSpec((1, D_MODEL), lambda i, j: (0, 0))],
        out_specs=[pl.BlockSpec((tm, D_MODEL), lambda i, j: (i, 0)),
                   pl.BlockSpec((tm, D_MODEL), lambda i, j: (i, 0))],
        out_shape=[jax.ShapeDtypeStruct((t, D_MODEL), F32),
                   jax.ShapeDtypeStruct((t, D_MODEL), next_dtype)],
        scratch_shapes=[pltpu.VMEM((tm, D_MODEL), F32)],
        compiler_params=_cparams("parallel", "arbitrary"),
        name="mlp",
    )(h, w_up, w_down, x, g_next.reshape(1, D_MODEL))


def _small_kernel(z_ref, b_ref, a_ref, f_ref, carry_ref):
    @pl.when(pl.program_id(1) == 0)
    def _():
        carry_ref[...] = jnp.zeros_like(carry_ref)

    z = z_ref[...]
    tm = z.shape[0]
    lane = lax.broadcasted_iota(jnp.int32, z.shape, 1)
    pre = z + b_ref[...]
    lf = jnp.minimum(pre, 0.0) - jnp.log1p(jnp.exp(-jnp.abs(pre)))
    lf = jnp.where(lane < N_HEADS, lf, 0.0)
    a_ref[...] = jnp.where(lane < N_HEADS, lf, jnp.where(lane < 4 * N_HEADS, jax.nn.sigmoid(z), 0.0))
    row = lax.broadcasted_iota(jnp.int32, (tm, tm), 0)
    col = lax.broadcasted_iota(jnp.int32, (tm, tm), 1)
    tril = jnp.where(col <= row, 1.0, 0.0)
    f = jnp.dot(tril, lf, preferred_element_type=F32, precision=HIGHEST) + carry_ref[0:1, :]
    f_ref[...] = f
    carry_ref[0:1, :] = f[tm - 1:tm, :]


def small_call(z_small, b_forget, n, t, tm):
    bias = jnp.zeros((1, SMALL_W), F32).at[0, :N_HEADS].set(b_forget)
    nt = t // tm
    return pl.pallas_call(
        _small_kernel,
        grid=(n, nt),
        in_specs=[pl.BlockSpec((tm, SMALL_W), lambda b, j: (b * nt + j, 0)),
                  pl.BlockSpec((1, SMALL_W), lambda b, j: (0, 0))],
        out_specs=[pl.BlockSpec((tm, SMALL_W), lambda b, j: (b * nt + j, 0)),
                   pl.BlockSpec((tm, SMALL_W), lambda b, j: (b * nt + j, 0))],
        out_shape=[jax.ShapeDtypeStruct((n * t, SMALL_W), F32),
                   jax.ShapeDtypeStruct((n * t, SMALL_W), F32)],
        scratch_shapes=[pltpu.VMEM((8, SMALL_W), F32)],
        compiler_params=_cparams("parallel", "arbitrary"),
        name="forget_and_gates",
    )(z_small, bias)


def _conv_prompt_kernel(z_ref, w_ref, o_ref, st_ref, prev_ref):
    @pl.when(pl.program_id(1) == 0)
    def _():
        prev_ref[...] = jnp.zeros_like(prev_ref)

    bw = BRANCH_W
    u = z_ref[:, 2 * bw:3 * bw] * z_ref[:, 0:bw]
    tm = u.shape[0]
    row = lax.broadcasted_iota(jnp.int32, u.shape, 0)
    u1 = jnp.where(row == 0, prev_ref[7:8, :], pltpu.roll(u, 1, 0))
    u2 = jnp.where(row == 0, prev_ref[6:7, :], jnp.where(row == 1, prev_ref[7:8, :], pltpu.roll(u, 2, 0)))
    y = u2 * w_ref[0:1, :] + u1 * w_ref[1:2, :] + u * w_ref[2:3, :]
    o_ref[...] = z_ref[:, bw:2 * bw] * y
    prev_ref[...] = u[tm - 8:tm]
    st_ref[0] = u[tm - 2:tm]


def conv_prompt_call(z_conv, conv_w, n, t, tm):
    nt = t // tm
    return pl.pallas_call(
        _conv_prompt_kernel,
        grid=(n, nt),
        in_specs=[pl.BlockSpec((tm, 3 * BRANCH_W), lambda b, j: (b * nt + j, 0)),
                  pl.BlockSpec((CONV_W, BRANCH_W), lambda b, j: (0, 0))],
        out_specs=[pl.BlockSpec((tm, BRANCH_W), lambda b, j: (b * nt + j, 0)),
                   pl.BlockSpec((1, CONV_W - 1, BRANCH_W), lambda b, j: (b, 0, 0))],
        out_shape=[jax.ShapeDtypeStruct((n * t, BRANCH_W), F32),
                   jax.ShapeDtypeStruct((n, CONV_W - 1, BRANCH_W), F32)],
        scratch_shapes=[pltpu.VMEM((8, BRANCH_W), F32)],
        compiler_params=_cparams("parallel", "arbitrary"),
        name="conv_prompt",
    )(z_conv, conv_w)


def _online_step(s, vt, carry):
    m, l, acc = carry
    m_new = jnp.maximum(m, jnp.max(s, axis=0, keepdims=True))
    alpha = jnp.exp(m - m_new)
    p = jnp.exp(s - m_new)
    l = alpha * l + jnp.sum(p, axis=0, keepdims=True)
    acc = alpha * acc + jnp.dot(vt, p.astype(BF16), preferred_element_type=F32)
    return m_new, l, acc


def _softmax_init(width=TQ):
    return (jnp.full((1, width), NEG, F32), jnp.zeros((1, width), F32), jnp.zeros((HEAD_DIM, width), F32))


def _query_pair(q_ref, h, half):
    hp, e = divmod(h, 2)
    qp = q_ref[:, hp * PAIR_W:(hp + 1) * PAIR_W]
    if e != half:
        qp = pltpu.roll(qp, HEAD_DIM, 1)
    lane = lax.broadcasted_iota(jnp.int32, qp.shape, 1)
    return jnp.where((lane // HEAD_DIM) == half, qp, 0.0)


def _tile_masks():
    sub = lax.broadcasted_iota(jnp.int32, (TK, TQ), 0)
    lane = lax.broadcasted_iota(jnp.int32, (TK, TQ), 1)
    return sub <= lane, sub > lane


def _key_col():
    return lax.broadcasted_iota(jnp.int32, (TK, 1), 0).astype(F32)


def _fox_kernel(q_ref, k_ref, vt_ref, fk_ref, o_ref, ot_ref):
    i = pl.program_id(1)
    causal, _ = _tile_masks()
    qcats = [jnp.concatenate([(_query_pair(q_ref, 2 * hp + e, e) * SCALE).astype(BF16) for e in (0, 1)], axis=0)
             for hp in range(N_HEADS // 2)]

    def tile(j, carry, diag):
        r0 = pl.multiple_of(j * TK, TK)
        out = []
        for hp in range(N_HEADS // 2):
            s2 = _dot_nt(k_ref[pl.ds(r0, TK), hp * PAIR_W:(hp + 1) * PAIR_W], qcats[hp])
            for e in (0, 1):
                h = 2 * hp + e
                s = s2[:, e * TQ:(e + 1) * TQ] - fk_ref[pl.ds(r0, TK), h:h + 1]
                if diag:
                    s = jnp.where(causal, s, MASKED)
                out.append(_online_step(s, vt_ref[j, h * HEAD_DIM:(h + 1) * HEAD_DIM, :], carry[h]))
        return tuple(out)

    carry = lax.fori_loop(0, i, lambda j, c: tile(j, c, False), tuple(_softmax_init() for _ in range(N_HEADS)))
    for h, (_, l, acc) in enumerate(tile(i, carry, True)):
        ot_ref[h * HEAD_DIM:(h + 1) * HEAD_DIM, :] = acc * (1.0 / l)
    o_ref[...] = ot_ref[...].T


def fox_prompt_call(q, kb, vt, fk, n, t):
    nq = t // TQ
    return pl.pallas_call(
        _fox_kernel,
        grid=(n, nq),
        in_specs=[pl.BlockSpec((TQ, BRANCH_W), lambda b, i: (b * nq + i, 0)),
                  pl.BlockSpec((t, BRANCH_W), lambda b, i: (b, 0)),
                  pl.BlockSpec((t // TK, BRANCH_W, TK), lambda b, i: (b, 0, 0)),
                  pl.BlockSpec((t, SMALL_W), lambda b, i: (b, 0))],
        out_specs=pl.BlockSpec((TQ, BRANCH_W), lambda b, i: (b * nq + i, 0)),
        out_shape=jax.ShapeDtypeStruct((n * t, BRANCH_W), F32),
        scratch_shapes=[pltpu.VMEM((BRANCH_W, TQ), F32)],
        compiler_params=_cparams("parallel", "arbitrary"),
        name="fox_prompt",
    )(q, kb, vt, fk)


def _rank_before(score, bidx, nblk):
    cnt = jnp.zeros(score.shape, F32)
    for b2 in range(nblk):
        row = score[b2:b2 + 1, :]
        beats = (row > score) | ((row == score) & (b2 < bidx))
        cnt = cnt + jnp.where(beats, 1.0, 0.0)
    return cnt


def _moba_kernel(q_ref, k_ref, vt_ref, km_ref, o_ref, ot_ref, sel_ref, *, nblk):
    i = pl.program_id(1)
    causal, _ = _tile_masks()
    kcol = _key_col()
    bidx = lax.broadcasted_iota(jnp.int32, (nblk, TQ), 0)
    qcats = []
    for hp in range(N_HEADS // 2):
        qpads = []
        for e in (0, 1):
            h = 2 * hp + e
            q32 = _query_pair(q_ref, h, e)
            qpads.append((q32 * SCALE).astype(BF16))
            gs = _dot_nt(km_ref[0, :, hp * PAIR_W:(hp + 1) * PAIR_W], q32, precision=HIGHEST)
            gs = jnp.where(bidx < i, gs, NEG)
            sel = (_rank_before(gs, bidx, nblk) < MOBA_TOPK) & (bidx < i)
            sel_ref[:, h * TQ:(h + 1) * TQ] = jnp.where(sel, 1.0, 0.0)
        qcats.append(jnp.concatenate(qpads, axis=0))

    def tile(j, carry, diag):
        r0 = pl.multiple_of(j * TK, TK)
        dist = kcol - ((i - j) * TK).astype(F32)
        out = []
        for hp in range(N_HEADS // 2):
            s2 = _dot_nt(k_ref[pl.ds(r0, TK), hp * PAIR_W:(hp + 1) * PAIR_W], qcats[hp])
            for e in (0, 1):
                h = 2 * hp + e
                s = s2[:, e * TQ:(e + 1) * TQ] + 2.0 ** (-8.0 * (h + 1) / N_HEADS) * dist
                if diag:
                    s = jnp.where(causal, s, MASKED)
                else:
                    s = jnp.where(sel_ref[pl.ds(j, 1), h * TQ:(h + 1) * TQ] > 0.5, s, MASKED)
                out.append(_online_step(s, vt_ref[j, h * HEAD_DIM:(h + 1) * HEAD_DIM, :], carry[h]))
        return tuple(out)

    carry = lax.fori_loop(0, i, lambda j, c: tile(j, c, False), tuple(_softmax_init() for _ in range(N_HEADS)))
    for h, (_, l, acc) in enumerate(tile(i, carry, True)):
        ot_ref[h * HEAD_DIM:(h + 1) * HEAD_DIM, :] = acc * (1.0 / l)
    o_ref[...] = ot_ref[...].T


def moba_prompt_call(q, kb, vt, kmean, n, t):
    nq = t // TQ
    nblk = t // MOBA_BLOCK
    return pl.pallas_call(
        functools.partial(_moba_kernel, nblk=nblk),
        grid=(n, nq),
        in_specs=[pl.BlockSpec((TQ, BRANCH_W), lambda b, i: (b * nq + i, 0)),
                  pl.BlockSpec((t, BRANCH_W), lambda b, i: (b, 0)),
                  pl.BlockSpec((t // TK, BRANCH_W, TK), lambda b, i: (b, 0, 0)),
                  pl.BlockSpec((1, nblk, BRANCH_W), lambda b, i: (b, 0, 0))],
        out_specs=pl.BlockSpec((TQ, BRANCH_W), lambda b, i: (b * nq + i, 0)),
        out_shape=jax.ShapeDtypeStruct((n * t, BRANCH_W), F32),
        scratch_shapes=[pltpu.VMEM((BRANCH_W, TQ), F32), pltpu.VMEM((nblk, N_HEADS * TQ), F32)],
        compiler_params=_cparams("parallel", "arbitrary"),
        name="moba_prompt",
    )(q, kb, vt, kmean.reshape(n, nblk, BRANCH_W))


def _gelu_tanh(x):
    return 0.5 * x * (1.0 + jnp.tanh(0.7978845608028654 * (x + 0.044715 * x * x * x)))


def _compress_kernel(pt_ref, a_ref, pet_ref, peb_ref, wt_ref, wb_ref, w2k_ref, w2v_ref, w2vt_ref,
                     kc_ref, vc_ref, vct_ref, rows_ref, *, rows):
    del pt_ref
    j = pl.program_id(1)
    r0 = pl.multiple_of(j * PAGE_ROWS, PAGE_ROWS)
    for i in range(CMP_S):
        rows_ref[pl.ds(r0, PAGE_ROWS), i * 2 * KV_W:(i + 1) * 2 * KV_W] = a_ref[0, :, i * 4 * KV_W:i * 4 * KV_W + 2 * KV_W]

    @pl.when(j == pl.num_programs(1) - 1)
    def _():
        chunk = min(256, rows)
        prev_top = jnp.zeros((1, 4 * CMP_HID), F32)
        for c in range(rows // chunk):
            a = rows_ref[c * chunk:(c + 1) * chunk, :]
            ht = jnp.dot((a + pet_ref[...]).astype(BF16), wt_ref[...], preferred_element_type=F32)
            hb = jnp.dot((a + peb_ref[...]).astype(BF16), wb_ref[...], preferred_element_type=F32)
            row = lax.broadcasted_iota(jnp.int32, ht.shape, 0)
            shifted = jnp.where(row == 0, prev_top, pltpu.roll(ht, 1, 0))
            prev_top = ht[chunk - 1:chunk, :]
            g = _gelu_tanh(shifted + hb).astype(BF16)
            kc_ref[0, c * chunk:(c + 1) * chunk, :] = jnp.dot(
                g[:, :2 * CMP_HID], w2k_ref[...], preferred_element_type=F32).astype(BF16)
            vc_ref[0, c * chunk:(c + 1) * chunk, :] = jnp.dot(
                g[:, 2 * CMP_HID:], w2v_ref[...], preferred_element_type=F32).astype(BF16)
            vct_ref[0, :, c * chunk:(c + 1) * chunk] = _dot_nt(w2vt_ref[...], g[:, 2 * CMP_HID:]).astype(BF16)


def compress_call(pages, page_idx, pe_top, pe_bot, wt, wb, w2k, w2v, w2vt, n, npages):
    rows = npages * PAGE_ROWS
    const = lambda b, j, pt: (0, 0)
    grid_spec = pltpu.PrefetchScalarGridSpec(
        num_scalar_prefetch=1,
        grid=(n, npages),
        in_specs=[pl.BlockSpec((1, PAGE_ROWS, CMP_S * 4 * KV_W), lambda b, j, pt: (pt[b, j], 0, 0)),
                  pl.BlockSpec((1, CMP_ROW), const),
                  pl.BlockSpec((1, CMP_ROW), const),
                  pl.BlockSpec((CMP_ROW, 4 * CMP_HID), const, pipeline_mode=pl.Buffered(1)),
                  pl.BlockSpec((CMP_ROW, 4 * CMP_HID), const, pipeline_mode=pl.Buffered(1)),
                  pl.BlockSpec((2 * CMP_HID, KV_W), const),
                  pl.BlockSpec((2 * CMP_HID, KV_W), const),
                  pl.BlockSpec((KV_W, 2 * CMP_HID), const)],
        out_specs=[pl.BlockSpec((1, rows, KV_W), lambda b, j, pt: (b, 0, 0)),
                   pl.BlockSpec((1, rows, KV_W), lambda b, j, pt: (b, 0, 0)),
                   pl.BlockSpec((1, KV_W, rows), lambda b, j, pt: (b, 0, 0))],
        scratch_shapes=[pltpu.VMEM((rows, CMP_ROW), F32)],
    )
    return pl.pallas_call(
        functools.partial(_compress_kernel, rows=rows),
        grid_spec=grid_spec,
        out_shape=[jax.ShapeDtypeStruct((n, rows, KV_W), BF16),
                   jax.ShapeDtypeStruct((n, rows, KV_W), BF16),
                   jax.ShapeDtypeStruct((n, KV_W, rows), BF16)],
        compiler_params=_cparams("parallel", "arbitrary"),
        name="nsa_compress",
    )(page_idx, pages, pe_top, pe_bot, wt, wb, w2k, w2v, w2vt)


def _nsa_kernel(q_ref, kc_ref, vct_ref, ks_ref, vst_ref, kw_ref, vwt_ref, gt_ref, o_ref,
                ot_ref, sel_ref, *, ncmp, nsb):
    i = pl.program_id(1)
    causal, below = _tile_masks()
    kcol = _key_col()
    lane_q = lax.broadcasted_iota(jnp.int32, (1, TQ), 1)
    qpos = i * TQ + lane_q
    crow = lax.broadcasted_iota(jnp.int32, (ncmp, 1), 0)
    cpos = CMP_S * crow + (CMP_S - 1)
    cvalid = (crow >= 1) & (cpos <= qpos)
    cposf = cpos.astype(F32)
    mj = lax.broadcasted_iota(jnp.int32, (nsb, ncmp), 0)
    mc = lax.broadcasted_iota(jnp.int32, (nsb, ncmp), 1)
    ratio = SLC_BLOCK // CMP_S
    gather_m = jnp.where((mc >= ratio * mj) & (mc <= ratio * mj + ratio) & (mc >= 1), 1.0, 0.0)
    bj = lax.broadcasted_iota(jnp.int32, (nsb, TQ), 0)
    own = qpos // SLC_BLOCK
    forced = (bj == 0) | (bj >= own - 1)
    allowed = bj <= own
    ecol = lax.broadcasted_iota(jnp.int32, (TK, nsb), 1)
    erow = lax.broadcasted_iota(jnp.int32, (TK, nsb), 0) // SLC_BLOCK
    gw = NSA_R * TQ
    sub4 = lax.broadcasted_iota(jnp.int32, (TK, gw), 0)
    lane4 = lax.broadcasted_iota(jnp.int32, (TK, gw), 1) % TQ
    causal4, below4 = sub4 <= lane4, sub4 > lane4
    cvalid4 = (crow >= 1) & (cpos <= i * TQ + lax.broadcasted_iota(jnp.int32, (1, gw), 1) % TQ)

    for g in range(NSA_G):
        heads = [g * NSA_R + r for r in range(NSA_R)]
        q4 = jnp.concatenate([(_query_pair(q_ref, h, g) * SCALE).astype(BF16) for h in heads], axis=0)
        slope_row = jnp.concatenate([jnp.full((1, TQ), 2.0 ** (-8.0 * (h + 1) / N_HEADS), F32) for h in heads], axis=1)
        gates = [jnp.concatenate([gt_ref[3 * h + c:3 * h + c + 1, :] for h in heads], axis=1) for c in range(3)]
        v_rows = slice(g * HEAD_DIM, (g + 1) * HEAD_DIM)
        s = _dot_nt(kc_ref[0], q4) + slope_row * (cposf - (i * TQ).astype(F32))
        s = jnp.where(cvalid4, s, MASKED)
        p = jnp.where(cvalid4, jnp.exp(s - jnp.max(s, axis=0, keepdims=True)), 0.0)
        l = jnp.sum(p, axis=0, keepdims=True)
        p = p * (1.0 / jnp.where(l > 0.0, l, 1.0))
        imp = p[:, 0:TQ]
        for r in range(1, NSA_R):
            imp = imp + p[:, r * TQ:(r + 1) * TQ]
        o_all = gates[0] * jnp.dot(vct_ref[0, v_rows, :], p.astype(BF16), preferred_element_type=F32)
        p_slc = jnp.dot(gather_m, imp, preferred_element_type=F32, precision=HIGHEST)
        score = jnp.where(allowed, jnp.where(forced, p_slc + FORCE_SCORE, p_slc), -1.0)
        sel = jnp.where((_rank_before(score, bj, nsb) < SLC_N) & allowed, 1.0, 0.0)
        sel_ref[...] = jnp.concatenate([sel] * NSA_R, axis=1).astype(BF16)

        def alibi(j):
            return slope_row * (kcol - ((i - j) * TK).astype(F32))

        def slc_tile(j, carry, diag):
            r0 = pl.multiple_of(j * TK, TK)
            expand = jnp.where(ecol == erow + j * (TK // SLC_BLOCK), 1.0, 0.0).astype(BF16)
            keep = jnp.dot(expand, sel_ref[...], preferred_element_type=F32) > 0.5
            if diag:
                keep = keep & causal4
            s = jnp.where(keep, _dot_nt(ks_ref[pl.ds(r0, TK), :], q4) + alibi(j), MASKED)
            return _online_step(s, vst_ref[j, v_rows, :], carry)

        carry = lax.fori_loop(0, i, lambda j, c: slc_tile(j, c, False), _softmax_init(gw))
        _, l, acc = slc_tile(i, carry, True)
        o_all = o_all + gates[1] * (acc * (1.0 / l))

        def win_tile(j, carry, mask):
            r0 = pl.multiple_of(j * TK, TK)
            s = _dot_nt(kw_ref[pl.ds(r0, TK), :], q4) + alibi(j)
            if mask is not None:
                s = jnp.where(mask, s, MASKED)
            return _online_step(s, vwt_ref[j, v_rows, :], carry)

        carry = _softmax_init(gw)
        carry = lax.cond(i >= 2, lambda c: win_tile(i - 2, c, below4), lambda c: c, carry)
        carry = lax.cond(i >= 1, lambda c: win_tile(i - 1, c, None), lambda c: c, carry)
        _, l, acc = win_tile(i, carry, causal4)
        o_all = o_all + gates[2] * (acc * (1.0 / l))
        for r, h in enumerate(heads):
            ot_ref[h * HEAD_DIM:(h + 1) * HEAD_DIM, :] = o_all[:, r * TQ:(r + 1) * TQ]
    o_ref[...] = ot_ref[...].T


def nsa_prompt_call(q, kcmp, vcmpt, ks, vst, kw, vwt, gate_t, n, t):
    nq = t // TQ
    ncmp = kcmp.shape[1]
    nsb = t // SLC_BLOCK
    return pl.pallas_call(
        functools.partial(_nsa_kernel, ncmp=ncmp, nsb=nsb),
        grid=(n, nq),
        in_specs=[pl.BlockSpec((TQ, BRANCH_W), lambda b, i: (b * nq + i, 0)),
                  pl.BlockSpec((1, ncmp, KV_W), lambda b, i: (b, 0, 0)),
                  pl.BlockSpec((1, KV_W, ncmp), lambda b, i: (b, 0, 0)),
                  pl.BlockSpec((t, KV_W), lambda b, i: (b, 0)),
                  pl.BlockSpec((t // TK, KV_W, TK), lambda b, i: (b, 0, 0)),
                  pl.BlockSpec((t, KV_W), lambda b, i: (b, 0)),
                  pl.BlockSpec((t // TK, KV_W, TK), lambda b, i: (b, 0, 0)),
                  pl.BlockSpec((3 * N_HEADS, TQ), lambda b, i: (0, b * nq + i))],
        out_specs=pl.BlockSpec((TQ, BRANCH_W), lambda b, i: (b * nq + i, 0)),
        out_shape=jax.ShapeDtypeStruct((n * t, BRANCH_W), F32),
        scratch_shapes=[pltpu.VMEM((BRANCH_W, TQ), F32), pltpu.VMEM((nsb, NSA_R * TQ), BF16)],
        compiler_params=_cparams("parallel", "arbitrary"),
        name="nsa_prompt",
    )(q, kcmp, vcmpt, ks, vst, kw, vwt, gate_t)


def alibi_slopes(n):
    return jnp.exp2(-8.0 * jnp.arange(1, n + 1, dtype=jnp.float32) / n)


def masked_softmax(s, mask, axis=-1):
    p = jax.nn.softmax(jnp.where(mask, s, NEG), axis=axis)
    return p * mask


def sweep_queries(fn, block, *qs):
    n, t = qs[0].shape[:2]
    qb = block if t % block == 0 else t
    nb = t // qb
    xs = tuple(jnp.moveaxis(a.reshape((n, nb, qb) + a.shape[2:]), 1, 0) for a in qs)
    out = lax.map(lambda args: fn(args[0] * qb, *args[1:]), (jnp.arange(nb, dtype=jnp.int32),) + xs)
    return jnp.moveaxis(out, 0, 1).reshape((n, t) + out.shape[3:])


def gather_pages(pool, page_table):
    g = pool[page_table]
    return g.reshape((g.shape[0], g.shape[1] * g.shape[2]) + g.shape[3:])


def fox_attention(q, k, v, logf, q0):
    d = q.shape[-1]
    L = k.shape[1]
    scale = d ** -0.5
    F = jnp.cumsum(logf.astype(jnp.float32), axis=1)
    Fk = jnp.moveaxis(F, 1, 2)
    Fq = F[:, q0:]
    kpos = jnp.arange(L)

    def block(start, qb, fq):
        qpos = q0 + start + jnp.arange(qb.shape[1])
        s = jnp.einsum('nqhd,nkhd->nhqk', qb, k).astype(jnp.float32) * scale
        s = s + jnp.moveaxis(fq, 1, 2)[..., None] - Fk[:, :, None, :]
        p = masked_softmax(s, kpos[None, :] <= qpos[:, None])
        return jnp.einsum('nhqk,nkhd->nqhd', p.astype(q.dtype), v)

    return sweep_queries(block, Q_BLOCK, q, Fq)


def moba_attention(q, k, v, q0, slopes):
    n, tq, nh, d = q.shape
    L = k.shape[1]
    scale = d ** -0.5
    nbk = -(-L // MOBA_BLOCK)
    pad = nbk * MOBA_BLOCK - L

    def to_blocks(a):
        a = jnp.pad(a, ((0, 0), (0, pad), (0, 0), (0, 0)))
        return a.reshape(n, nbk, MOBA_BLOCK, nh, d).transpose(0, 3, 1, 2, 4)

    kb, vb = to_blocks(k), to_blocks(v)
    kmean = jnp.mean(kb.astype(jnp.float32), axis=3)
    topk = min(MOBA_TOPK, nbk)
    bidx = jnp.arange(nbk)
    ni = jnp.arange(n)[:, None, None, None]
    hi = jnp.arange(nh)[None, :, None, None]
    sl = slopes[None, :, None, None, None]

    def block(start, qb):
        m = qb.shape[1]
        qpos = q0 + start + jnp.arange(m)
        own = qpos // MOBA_BLOCK
        gs = jnp.einsum('nqhd,nhbd->nhqb', qb.astype(jnp.float32), kmean)
        gs = jnp.where(bidx[None, :] < own[:, None], gs, NEG)
        _, top = lax.top_k(gs, topk)
        sel_ok = top < own[:, None]
        idx = jnp.concatenate([top, jnp.broadcast_to(own[:, None], (n, nh, m, 1))], axis=-1)
        ok = jnp.concatenate([sel_ok, jnp.ones((n, nh, m, 1), bool)], axis=-1)
        kg = kb[ni, hi, idx]
        vg = vb[ni, hi, idx]
        kpos = idx[..., None] * MOBA_BLOCK + jnp.arange(MOBA_BLOCK)
        s = jnp.einsum('nqhd,nhqsjd->nhqsj', qb, kg).astype(jnp.float32) * scale
        s = s - sl * (qpos[:, None, None] - kpos).astype(jnp.float32)
        mask = ok[..., None] & (kpos <= qpos[:, None, None])
        p = masked_softmax(s, mask, axis=(-2, -1))
        return jnp.einsum('nhqsj,nhqsjd->nqhd', p.astype(q.dtype), vg)

    return sweep_queries(block, SPARSE_Q_BLOCK, q)


def compress_tokens(a, w1, w2, pe):
    n, L, g, d = a.shape
    nc = (L - CMP_L) // CMP_S + 1
    idx = jnp.arange(nc)[:, None] * CMP_S + jnp.arange(CMP_L)[None, :]
    blocks = a[:, idx] + pe[:, None, :].astype(a.dtype)
    flat = jnp.swapaxes(blocks, 2, 3).reshape(n, nc, g, CMP_L * d)
    return jax.nn.gelu(flat @ w1) @ w2


def nsa_attention(q, kc, vc, ks, vs, kw, vw, gate, q0, w_ofs, slopes, cmp_w1, cmp_w2, cmp_pe):
    n, tq, nh, d = q.shape
    L = kc.shape[1]
    dt = q.dtype
    scale = d ** -0.5
    qg = q.reshape(n, tq, NSA_G, NSA_R, d)
    k_cmp = compress_tokens(kc, cmp_w1[0], cmp_w2[0], cmp_pe[0])
    v_cmp = compress_tokens(vc, cmp_w1[1], cmp_w2[1], cmp_pe[1])
    nc = k_cmp.shape[1]
    cpos = jnp.arange(nc) * CMP_S + (CMP_L - 1)
    nsb = -(-L // SLC_BLOCK)
    padl = nsb * SLC_BLOCK - L

    def to_blocks(a):
        a = jnp.pad(a, ((0, 0), (0, padl), (0, 0), (0, 0)))
        return a.reshape(n, nsb, SLC_BLOCK, NSA_G, d).transpose(0, 3, 1, 2, 4)

    ks_b, vs_b = to_blocks(ks), to_blocks(vs)
    nsel = min(SLC_N, nsb)
    front = CMP_L // CMP_S - 1
    ratio = SLC_BLOCK // CMP_S
    width = ratio + front
    back = ratio * nsb + width - front - nc
    kw_p = jnp.pad(kw, ((0, 0), (WINDOW, 0), (0, 0), (0, 0)))
    vw_p = jnp.pad(vw, ((0, 0), (WINDOW, 0), (0, 0), (0, 0)))
    sl = slopes.reshape(NSA_G, NSA_R)[None, :, :, None, None]
    ni = jnp.arange(n)[:, None, None, None]
    gi = jnp.arange(NSA_G)[None, :, None, None]
    bj = jnp.arange(nsb)

    def block(start, qb, gb):
        m = qb.shape[1]
        qpos = q0 + start + jnp.arange(m)
        s = jnp.einsum('nqgrd,ncgd->ngrqc', qb, k_cmp).astype(jnp.float32) * scale
        s = s - sl * (qpos[:, None] - cpos[None, :]).astype(jnp.float32)
        p_cmp = masked_softmax(s, cpos[None, :] <= qpos[:, None])
        o_cmp = jnp.einsum('ngrqc,ncgd->nqgrd', p_cmp.astype(dt), v_cmp)
        imp = jnp.pad(p_cmp.sum(axis=2), ((0, 0), (0, 0), (0, 0), (front, back)))
        p_slc = imp[..., 0:ratio * nsb:ratio]
        for u in range(1, width):
            p_slc = p_slc + imp[..., u:u + ratio * nsb:ratio]
        own = qpos // SLC_BLOCK
        forced = (bj[None, :] == 0) | (bj[None, :] >= own[:, None] - 1)
        allowed = bj[None, :] <= own[:, None]
        score = jnp.where(allowed, jnp.where(forced, p_slc + FORCE_SCORE, p_slc), -1.0)
        _, top = lax.top_k(score, nsel)
        ok = top <= own[:, None]
        kg = ks_b[ni, gi, top]
        vg = vs_b[ni, gi, top]
        kpos = top[..., None] * SLC_BLOCK + jnp.arange(SLC_BLOCK)
        dist = (qpos[:, None, None] - kpos)[:, :, None].astype(jnp.float32)
        s2 = jnp.einsum('nqgrd,ngqsjd->ngrqsj', qb, kg).astype(jnp.float32) * scale - sl[..., None] * dist
        mask2 = (ok[..., None] & (kpos <= qpos[:, None, None]))[:, :, None]
        p2 = masked_softmax(s2, mask2, axis=(-2, -1))
        o_slc = jnp.einsum('ngrqsj,ngqsjd->nqgrd', p2.astype(dt), vg)
        off = q0 + start - w_ofs
        kwin = lax.dynamic_slice_in_dim(kw_p, off, WINDOW + m, axis=1)
        vwin = lax.dynamic_slice_in_dim(vw_p, off, WINDOW + m, axis=1)
        wpos = q0 + start - WINDOW + jnp.arange(WINDOW + m)
        s3 = jnp.einsum('nqgrd,nkgd->ngrqk', qb, kwin).astype(jnp.float32) * scale
        s3 = s3 - sl * (qpos[:, None] - wpos[None, :]).astype(jnp.float32)
        wmask = (wpos[None, :] <= qpos[:, None]) & (wpos[None, :] > qpos[:, None] - WINDOW) & (wpos[None, :] >= 0)
        p3 = masked_softmax(s3, wmask)
        o_win = jnp.einsum('ngrqk,nkgd->nqgrd', p3.astype(dt), vwin)
        return gb[..., 0:1] * o_cmp + gb[..., 1:2] * o_slc + gb[..., 2:3] * o_win

    return sweep_queries(block, SPARSE_Q_BLOCK, qg, gate)


PAGE = 128


def _head_diag_mask(width):
    sub = lax.broadcasted_iota(jnp.int32, (N_HEADS, width), 0)
    lane = lax.broadcasted_iota(jnp.int32, (N_HEADS, width), 1)
    return sub == lane // HEAD_DIM


def _block_diag_queries(q_ref, nt):
    diag = _head_diag_mask(BRANCH_W)
    return jnp.concatenate([jnp.where(diag, q_ref[0, t:t + 1, :], 0.0) for t in range(nt)], axis=0)


def _extract_heads(o, nt):
    diag = _head_diag_mask(BRANCH_W)
    return jnp.concatenate(
        [jnp.sum(jnp.where(diag, o[t * N_HEADS:(t + 1) * N_HEADS], 0.0), axis=0, keepdims=True) for t in range(nt)],
        axis=0)


def _slope_col(rows):
    h = lax.broadcasted_iota(jnp.int32, (rows, 1), 0) % N_HEADS
    col = jnp.zeros((rows, 1), F32)
    for k in range(N_HEADS):
        col = jnp.where(h == k, 2.0 ** (-8.0 * (k + 1) / N_HEADS), col)
    return col


def _row_softmax_step(s, v, m_ref, l_ref, acc_ref):
    m_old = m_ref[...]
    m_new = jnp.maximum(m_old, jnp.max(s, axis=1, keepdims=True))
    alpha = jnp.exp(m_old - m_new)
    p = jnp.exp(s - m_new)
    l_ref[...] = alpha * l_ref[...] + jnp.sum(p, axis=1, keepdims=True)
    acc_ref[...] = alpha * acc_ref[...] + jnp.dot(p.astype(BF16), v, preferred_element_type=F32)
    m_ref[...] = m_new


def _softmax_reset(m_ref, l_ref, acc_ref):
    m_ref[...] = jnp.full(m_ref.shape, NEG, F32)
    l_ref[...] = jnp.zeros(l_ref.shape, F32)
    acc_ref[...] = jnp.zeros(acc_ref.shape, F32)


def _new_token_mask(rows, nt):
    k = lax.broadcasted_iota(jnp.int32, (rows, PAGE), 1)
    t = lax.broadcasted_iota(jnp.int32, (rows, PAGE), 0) // N_HEADS
    return (k <= t) & (k < nt)


def _dfox_kernel(pt_ref, q_ref, pg_ref, lf_ref, npg_ref, nlf_ref, o_ref, qbd_ref, m_ref, l_ref, acc_ref, cf_ref, *, nt):
    del pt_ref
    j = pl.program_id(1)
    rows = nt * N_HEADS

    @pl.when(j == 0)
    def _():
        qbd_ref[...] = (_block_diag_queries(q_ref, nt) * SCALE).astype(BF16)
        _softmax_reset(m_ref, l_ref, acc_ref)
        cf_ref[...] = jnp.zeros_like(cf_ref)

    def step(kv_ref, lf, mask):
        k = kv_ref[0, :, 0:BRANCH_W].astype(BF16)
        v = kv_ref[0, :, BRANCH_W:2 * BRANCH_W].astype(BF16)
        rr = lax.broadcasted_iota(jnp.int32, (rows, N_HEADS), 0) % N_HEADS
        rc = lax.broadcasted_iota(jnp.int32, (rows, N_HEADS), 1)
        lfe = _dot_nt(jnp.where(rr == rc, 1.0, 0.0), lf, precision=HIGHEST)
        a = lax.broadcasted_iota(jnp.int32, (PAGE, PAGE), 0)
        b = lax.broadcasted_iota(jnp.int32, (PAGE, PAGE), 1)
        fk = jnp.dot(lfe, jnp.where(a <= b, 1.0, 0.0), preferred_element_type=F32, precision=HIGHEST) + cf_ref[...]
        cf_ref[...] = fk[:, PAGE - 1:PAGE]
        s = _dot_nt(qbd_ref[...], k) - fk
        if mask is not None:
            s = jnp.where(mask, s, MASKED)
        _row_softmax_step(s, v, m_ref, l_ref, acc_ref)

    step(pg_ref, lf_ref[0], None)

    @pl.when(j == pl.num_programs(1) - 1)
    def _():
        step(npg_ref, nlf_ref[0], _new_token_mask(rows, nt))
        o_ref[0] = _extract_heads(acc_ref[...] * (1.0 / l_ref[...]), nt)


def fox_decode_call(pt, q, cache_kv, cache_lf, new_kv, new_lf):
    n, nt, _ = q.shape
    npages = pt.shape[1]
    rows = nt * N_HEADS
    grid_spec = pltpu.PrefetchScalarGridSpec(
        num_scalar_prefetch=1,
        grid=(n, npages),
        in_specs=[pl.BlockSpec((1, nt, BRANCH_W), lambda b, j, pt: (b, 0, 0)),
                  pl.BlockSpec((1, PAGE, 2 * BRANCH_W), lambda b, j, pt: (pt[b, j], 0, 0)),
                  pl.BlockSpec((1, PAGE, N_HEADS), lambda b, j, pt: (pt[b, j], 0, 0)),
                  pl.BlockSpec((1, PAGE, 2 * BRANCH_W), lambda b, j, pt: (b, 0, 0)),
                  pl.BlockSpec((1, PAGE, N_HEADS), lambda b, j, pt: (b, 0, 0))],
        out_specs=pl.BlockSpec((1, nt, BRANCH_W), lambda b, j, pt: (b, 0, 0)),
        scratch_shapes=[pltpu.VMEM((rows, BRANCH_W), BF16), pltpu.VMEM((rows, 1), F32), pltpu.VMEM((rows, 1), F32),
                        pltpu.VMEM((rows, BRANCH_W), F32), pltpu.VMEM((rows, 1), F32)],
    )
    return pl.pallas_call(
        functools.partial(_dfox_kernel, nt=nt),
        grid_spec=grid_spec,
        out_shape=jax.ShapeDtypeStruct((n, nt, BRANCH_W), F32),
        compiler_params=_cparams("parallel", "arbitrary"),
        name="fox_decode",
    )(pt, q, cache_kv, cache_lf, new_kv, new_lf)


def _dmoba_sel_kernel(pt_ref, q_ref, kp_ref, sel_ref, q32_ref, g_ref, *, nt):
    del pt_ref
    j = pl.program_id(1)
    rows = nt * N_HEADS
    pages_per_block = MOBA_BLOCK // PAGE

    @pl.when(j == 0)
    def _():
        q32_ref[...] = _block_diag_queries(q_ref, nt)
        g_ref[...] = jnp.zeros_like(g_ref)

    ksum = jnp.sum(kp_ref[0], axis=0, keepdims=True)
    prow = lax.broadcasted_iota(jnp.int32, (g_ref.shape[1], BRANCH_W), 0)
    g_ref[...] += _dot_nt(q32_ref[...], jnp.where(prow == j, ksum, 0.0), precision=HIGHEST)

    @pl.when(j == pl.num_programs(1) - 1)
    def _():
        npg = g_ref.shape[1]
        g = g_ref[...]
        lane = lax.broadcasted_iota(jnp.int32, (rows, npg), 1)
        blk = g
        for u in range(1, pages_per_block):
            blk = blk + pltpu.roll(g, npg - u, 1)
        gs = jnp.where(lane % pages_per_block == 0, blk * (1.0 / MOBA_BLOCK), NEG)
        sel = jnp.zeros((rows, npg), F32)
        for _ in range(MOBA_TOPK):
            mx = jnp.max(gs, axis=1, keepdims=True)
            idx = jnp.min(jnp.where(gs == mx, lane, npg), axis=1, keepdims=True)
            hit = lane == idx
            sel = jnp.where(hit, 1.0, sel)
            gs = jnp.where(hit, MASKED, gs)
        out = sel
        for u in range(1, pages_per_block):
            out = out + pltpu.roll(sel, u, 1)
        sel_ref[0] = out


def moba_select_call(pt, q, cache_kv):
    n, nt, _ = q.shape
    npages = pt.shape[1]
    rows = nt * N_HEADS
    grid_spec = pltpu.PrefetchScalarGridSpec(
        num_scalar_prefetch=1,
        grid=(n, npages),
        in_specs=[pl.BlockSpec((1, nt, BRANCH_W), lambda b, j, pt: (b, 0, 0)),
                  pl.BlockSpec((1, PAGE, BRANCH_W), lambda b, j, pt: (pt[b, j], 0, 0))],
        out_specs=pl.BlockSpec((1, rows, npages), lambda b, j, pt: (b, 0, 0)),
        scratch_shapes=[pltpu.VMEM((rows, BRANCH_W), F32), pltpu.VMEM((rows, npages), F32)],
    )
    return pl.pallas_call(
        functools.partial(_dmoba_sel_kernel, nt=nt),
        grid_spec=grid_spec,
        out_shape=jax.ShapeDtypeStruct((n, rows, npages), F32),
        compiler_params=_cparams("parallel", "arbitrary"),
        name="moba_select",
    )(pt, q, cache_kv)


def _dmoba_kernel(pt_ref, q_ref, sel_ref, pg_ref, npg_ref, o_ref, qbd_ref, m_ref, l_ref, acc_ref, *, nt):
    del pt_ref
    j = pl.program_id(1)
    npages = pl.num_programs(1)
    rows = nt * N_HEADS
    slope = _slope_col(rows)
    klane = lax.broadcasted_iota(jnp.int32, (1, PAGE), 1)

    @pl.when(j == 0)
    def _():
        qbd_ref[...] = (_block_diag_queries(q_ref, nt) * SCALE).astype(BF16)
        _softmax_reset(m_ref, l_ref, acc_ref)

    def step(kv_ref, rel, mask):
        k = kv_ref[0, :, 0:BRANCH_W].astype(BF16)
        v = kv_ref[0, :, BRANCH_W:2 * BRANCH_W].astype(BF16)
        s = _dot_nt(qbd_ref[...], k) + slope * rel
        _row_softmax_step(jnp.where(mask, s, MASKED), v, m_ref, l_ref, acc_ref)

    prow = lax.broadcasted_iota(jnp.int32, (sel_ref.shape[2], PAGE), 0)
    keep = jnp.dot(sel_ref[0].astype(BF16), jnp.where(prow == j, 1.0, 0.0).astype(BF16),
                   preferred_element_type=F32) > 0.5
    step(pg_ref, (klane + (j - npages) * PAGE).astype(F32), keep)

    @pl.when(j == npages - 1)
    def _():
        step(npg_ref, klane.astype(F32), _new_token_mask(rows, nt))
        o_ref[0] = _extract_heads(acc_ref[...] * (1.0 / l_ref[...]), nt)


def moba_decode_call(pt, q, sel, cache_kv, new_kv):
    n, nt, _ = q.shape
    npages = pt.shape[1]
    rows = nt * N_HEADS
    grid_spec = pltpu.PrefetchScalarGridSpec(
        num_scalar_prefetch=1,
        grid=(n, npages),
        in_specs=[pl.BlockSpec((1, nt, BRANCH_W), lambda b, j, pt: (b, 0, 0)),
                  pl.BlockSpec((1, rows, npages), lambda b, j, pt: (b, 0, 0)),
                  pl.BlockSpec((1, PAGE, 2 * BRANCH_W), lambda b, j, pt: (pt[b, j], 0, 0)),
                  pl.BlockSpec((1, PAGE, 2 * BRANCH_W), lambda b, j, pt: (b, 0, 0))],
        out_specs=pl.BlockSpec((1, nt, BRANCH_W), lambda b, j, pt: (b, 0, 0)),
        scratch_shapes=[pltpu.VMEM((rows, BRANCH_W), BF16), pltpu.VMEM((rows, 1), F32), pltpu.VMEM((rows, 1), F32),
                        pltpu.VMEM((rows, BRANCH_W), F32)],
    )
    return pl.pallas_call(
        functools.partial(_dmoba_kernel, nt=nt),
        grid_spec=grid_spec,
        out_shape=jax.ShapeDtypeStruct((n, nt, BRANCH_W), F32),
        compiler_params=_cparams("parallel", "arbitrary"),
        name="moba_decode",
    )(pt, q, sel, cache_kv, new_kv)


def _dnsa_kernel(pt_ref, q_ref, gate_ref, kc_ref, vc_ref, pg_ref, npg_ref, win_ref, nwin_ref, o_ref,
                 qbd_ref, sel_ref, ocmp_ref, m_ref, l_ref, acc_ref, *, nt, nsb_pad):
    del pt_ref
    j = pl.program_id(1)
    npages = pl.num_programs(1)
    rows = nt * N_HEADS
    ncmp = kc_ref.shape[1]
    q0 = npages * PAGE
    slope = _slope_col(rows)
    klane = lax.broadcasted_iota(jnp.int32, (1, PAGE), 1)
    blocks_per_page = PAGE // SLC_BLOCK

    @pl.when(j == 0)
    def _():
        pr = lax.broadcasted_iota(jnp.int32, (BRANCH_W, KV_W), 0)
        pc = lax.broadcasted_iota(jnp.int32, (BRANCH_W, KV_W), 1)
        place = jnp.where((pr % HEAD_DIM == pc % HEAD_DIM) & (pc // HEAD_DIM == pr // (NSA_R * HEAD_DIM)), 1.0, 0.0)
        qg = jnp.dot(_block_diag_queries(q_ref, nt), place, preferred_element_type=F32, precision=HIGHEST)
        qbd = (qg * SCALE).astype(BF16)
        qbd_ref[...] = qbd
        c = lax.broadcasted_iota(jnp.int32, (1, ncmp), 1)
        s = _dot_nt(qbd, kc_ref[0]) + slope * (CMP_S * c + (CMP_S - 1) - q0).astype(F32)
        valid = c >= 1
        s = jnp.where(valid, s, MASKED)
        p = jnp.where(valid, jnp.exp(s - jnp.max(s, axis=1, keepdims=True)), 0.0)
        p = p * (1.0 / jnp.sum(p, axis=1, keepdims=True))
        ocmp_ref[...] = gate_ref[0, :, 0:1] * jnp.dot(p.astype(BF16), vc_ref[0], preferred_element_type=F32)
        gr = lax.broadcasted_iota(jnp.int32, (nt * NSA_G, rows), 0)
        gc = lax.broadcasted_iota(jnp.int32, (nt * NSA_G, rows), 1) // NSA_R
        imp = jnp.dot(jnp.where(gr == gc, 1.0, 0.0), p, preferred_element_type=F32, precision=HIGHEST)
        ratio = SLC_BLOCK // CMP_S
        mr = lax.broadcasted_iota(jnp.int32, (ncmp, nsb_pad), 0)
        mb = lax.broadcasted_iota(jnp.int32, (ncmp, nsb_pad), 1)
        gather_m = jnp.where((mr >= ratio * mb) & (mr <= ratio * mb + ratio) & (mr >= 1), 1.0, 0.0)
        p_slc = jnp.dot(imp, gather_m, preferred_element_type=F32, precision=HIGHEST)
        bj = lax.broadcasted_iota(jnp.int32, p_slc.shape, 1)
        own = q0 // SLC_BLOCK
        forced = (bj == 0) | (bj >= own - 1)
        allowed = bj <= own
        score = jnp.where(allowed, jnp.where(forced, p_slc + FORCE_SCORE, p_slc), -1.0)
        sel = jnp.zeros(p_slc.shape, F32)
        for _ in range(SLC_N):
            mx = jnp.max(score, axis=1, keepdims=True)
            idx = jnp.min(jnp.where(score == mx, bj, nsb_pad), axis=1, keepdims=True)
            hit = bj == idx
            sel = jnp.where(hit, 1.0, sel)
            score = jnp.where(hit, -2.0, score)
        sel = jnp.where(allowed, sel, 0.0)
        er = lax.broadcasted_iota(jnp.int32, (rows, nt * NSA_G), 0) // NSA_R
        ec = lax.broadcasted_iota(jnp.int32, (rows, nt * NSA_G), 1)
        sel_ref[...] = jnp.dot(jnp.where(er == ec, 1.0, 0.0), sel, preferred_element_type=F32).astype(BF16)
        _softmax_reset(m_ref, l_ref, acc_ref)

    def step(kv_ref, rel, mask):
        k = kv_ref[0, :, 0:KV_W].astype(BF16)
        v = kv_ref[0, :, KV_W:2 * KV_W].astype(BF16)
        s = _dot_nt(qbd_ref[...], k) + slope * rel
        if mask is not None:
            s = jnp.where(mask, s, MASKED)
        _row_softmax_step(s, v, m_ref, l_ref, acc_ref)

    brow = lax.broadcasted_iota(jnp.int32, (nsb_pad, PAGE), 0)
    bcol = lax.broadcasted_iota(jnp.int32, (nsb_pad, PAGE), 1) // SLC_BLOCK
    expand = jnp.where(brow == bcol + j * blocks_per_page, 1.0, 0.0).astype(BF16)
    keep = jnp.dot(sel_ref[...], expand, preferred_element_type=F32) > 0.5
    step(pg_ref, (klane + (j - npages) * PAGE).astype(F32), keep)

    @pl.when(j == npages - 1)
    def _():
        new_mask = _new_token_mask(rows, nt)
        step(npg_ref, klane.astype(F32), new_mask)
        o_slc = acc_ref[...] * (1.0 / l_ref[...])
        _softmax_reset(m_ref, l_ref, acc_ref)
        wlen = win_ref.shape[1]
        wi = lax.broadcasted_iota(jnp.int32, (rows, wlen), 1)
        wt = lax.broadcasted_iota(jnp.int32, (rows, wlen), 0) // N_HEADS
        wrel = (lax.broadcasted_iota(jnp.int32, (1, wlen), 1) - wlen).astype(F32)
        step(win_ref, wrel, wi > wt + (wlen - WINDOW))
        step(nwin_ref, klane.astype(F32), new_mask)
        o_win = acc_ref[...] * (1.0 / l_ref[...])
        o = ocmp_ref[...] + gate_ref[0, :, 1:2] * o_slc + gate_ref[0, :, 2:3] * o_win
        ur = lax.broadcasted_iota(jnp.int32, (KV_W, BRANCH_W), 0)
        uc = lax.broadcasted_iota(jnp.int32, (KV_W, BRANCH_W), 1)
        unplace = jnp.where((ur % HEAD_DIM == uc % HEAD_DIM) & (ur // HEAD_DIM == uc // (NSA_R * HEAD_DIM)), 1.0, 0.0)
        o_ref[0] = _extract_heads(jnp.dot(o, unplace, preferred_element_type=F32, precision=HIGHEST), nt)


def nsa_decode_call(pt, q, gate, kcmp, vcmp, cache_kv, new_kv, win, new_win):
    n, nt, _ = q.shape
    npages = pt.shape[1]
    rows = nt * N_HEADS
    ncmp = kcmp.shape[1]
    nsb = (npages * PAGE) // SLC_BLOCK + 1
    nsb_pad = -(-nsb // 128) * 128
    wlen = win.shape[1]
    grid_spec = pltpu.PrefetchScalarGridSpec(
        num_scalar_prefetch=1,
        grid=(n, npages),
        in_specs=[pl.BlockSpec((1, nt, BRANCH_W), lambda b, j, pt: (b, 0, 0)),
                  pl.BlockSpec((1, rows, 3), lambda b, j, pt: (b, 0, 0)),
                  pl.BlockSpec((1, ncmp, KV_W), lambda b, j, pt: (b, 0, 0)),
                  pl.BlockSpec((1, ncmp, KV_W), lambda b, j, pt: (b, 0, 0)),
                  pl.BlockSpec((1, PAGE, 2 * KV_W), lambda b, j, pt: (pt[b, j], 0, 1)),
                  pl.BlockSpec((1, PAGE, 2 * KV_W), lambda b, j, pt: (b, 0, 0)),
                  pl.BlockSpec((1, wlen, 2 * KV_W), lambda b, j, pt: (b, 0, 0)),
                  pl.BlockSpec((1, PAGE, 2 * KV_W), lambda b, j, pt: (b, 0, 0))],
        out_specs=pl.BlockSpec((1, nt, BRANCH_W), lambda b, j, pt: (b, 0, 0)),
        scratch_shapes=[pltpu.VMEM((rows, KV_W), BF16), pltpu.VMEM((rows, nsb_pad), BF16),
                        pltpu.VMEM((rows, KV_W), F32), pltpu.VMEM((rows, 1), F32), pltpu.VMEM((rows, 1), F32),
                        pltpu.VMEM((rows, KV_W), F32)],
    )
    return pl.pallas_call(
        functools.partial(_dnsa_kernel, nt=nt, nsb_pad=nsb_pad),
        grid_spec=grid_spec,
        out_shape=jax.ShapeDtypeStruct((n, nt, BRANCH_W), F32),
        compiler_params=_cparams("parallel", "arbitrary"),
        name="nsa_decode",
    )(pt, q, gate, kcmp, vcmp, cache_kv, new_kv, win, new_win)


def _conv_sample_kernel(z_ref, pre1_ref, pre2_ref, w_ref, o_ref, u_ref, *, nt):
    bw = BRANCH_W
    u = z_ref[:, 2 * bw:3 * bw] * z_ref[:, 0:bw]
    t = lax.broadcasted_iota(jnp.int32, u.shape, 0) % nt
    u1 = jnp.where(t < 1, pre1_ref[...], pltpu.roll(u, 1, 0))
    u2 = jnp.where(t < 2, pre2_ref[...], pltpu.roll(u, 2, 0))
    y = u2 * w_ref[0:1, :] + u1 * w_ref[1:2, :] + u * w_ref[2:3, :]
    o_ref[...] = z_ref[:, bw:2 * bw] * y
    u_ref[...] = u


def conv_sample_call(z_conv, state, conv_w, n, nt):
    zero = jnp.zeros((n, nt, BRANCH_W), F32)
    pre1 = zero.at[:, 0].set(state[:, 1]).reshape(n * nt, BRANCH_W)
    pre2 = zero.at[:, 0].set(state[:, 0]).at[:, 1].set(state[:, 1]).reshape(n * nt, BRANCH_W)
    return pl.pallas_call(
        functools.partial(_conv_sample_kernel, nt=nt),
        out_shape=[jax.ShapeDtypeStruct((n * nt, BRANCH_W), F32), jax.ShapeDtypeStruct((n * nt, BRANCH_W), F32)],
        name="conv_sample",
    )(z_conv, pre1, pre2, conv_w)


PPS = 16


def _split3(x):
    hi = x.astype(BF16)
    r1 = x - hi.astype(F32)
    mid = r1.astype(BF16)
    lo = (r1 - mid.astype(F32)).astype(BF16)
    return jnp.concatenate([hi, mid, lo], axis=0)


def _softmax_step_t(s, vts, m_ref, l_ref, acc_ref):
    m_old = m_ref[...]
    m_new = jnp.maximum(m_old, jnp.max(s, axis=1, keepdims=True))
    alpha = jnp.exp(m_old - m_new)
    p = jnp.exp(s - m_new)
    l_ref[...] = alpha * l_ref[...] + jnp.sum(p, axis=1, keepdims=True)
    p = p.astype(BF16)
    pv, off = None, 0
    for vt in vts:
        nk = vt.shape[1]
        term = _dot_nt(p[:, off:off + nk], vt)
        pv = term if pv is None else pv + term
        off += nk
    acc_ref[...] = alpha * acc_ref[...] + pv
    m_ref[...] = m_new


def _page_specs(rows, row_block, pps):
    return [pl.BlockSpec((1, rows, PAGE), functools.partial(lambda b, j, pt, u: (pt[b, j * pps + u], row_block, 0), u=u))
            for u in range(pps)]


def _dfox_t_kernel(pt_ref, q_ref, *refs, nt, pps):
    del pt_ref
    kv_refs, lf_refs = refs[:pps], refs[pps:2 * pps]
    nkv_ref, nlf_ref, o_ref, qbd_ref, m_ref, l_ref, acc_ref, cf_ref = refs[2 * pps:]
    j = pl.program_id(1)
    rows = nt * N_HEADS

    @pl.when(j == 0)
    def _():
        qbd_ref[...] = (_block_diag_queries(q_ref, nt) * SCALE).astype(BF16)
        _softmax_reset(m_ref, l_ref, acc_ref)
        cf_ref[...] = jnp.zeros_like(cf_ref)

    def block(kvs, lfs, mask):
        a = lax.broadcasted_iota(jnp.int32, (PAGE, PAGE), 0)
        b = lax.broadcasted_iota(jnp.int32, (PAGE, PAGE), 1)
        tri = jnp.where(a <= b, 1.0, 0.0).astype(BF16)
        carry = cf_ref[...]
        parts, vts = [], []
        for kv_ref, lf_ref in zip(kvs, lfs):
            lfe = jnp.concatenate([lf_ref[0]] * nt, axis=0)
            c3 = jnp.dot(_split3(lfe), tri, preferred_element_type=F32)
            fk = c3[0:rows] + c3[rows:2 * rows] + c3[2 * rows:3 * rows] + carry
            carry = fk[:, PAGE - 1:PAGE]
            parts.append(jnp.dot(qbd_ref[...], kv_ref[0, 0:BRANCH_W, :].astype(BF16), preferred_element_type=F32) - fk)
            vts.append(kv_ref[0, BRANCH_W:2 * BRANCH_W, :].astype(BF16))
        cf_ref[...] = carry
        s = parts[0] if len(parts) == 1 else jnp.concatenate(parts, axis=1)
        if mask is not None:
            s = jnp.where(mask, s, MASKED)
        _softmax_step_t(s, vts, m_ref, l_ref, acc_ref)

    block(kv_refs, lf_refs, None)

    @pl.when(j == pl.num_programs(1) - 1)
    def _():
        block([nkv_ref], [nlf_ref], _new_token_mask(rows, nt))
        o_ref[0] = _extract_heads(acc_ref[...] * (1.0 / l_ref[...]), nt)


def fox_decode_t_call(pt, q, cache_kvt, cache_lft, new_kvt, new_lft):
    n, nt, _ = q.shape
    npages = pt.shape[1]
    pps = PPS if npages % PPS == 0 else 1
    rows = nt * N_HEADS
    grid_spec = pltpu.PrefetchScalarGridSpec(
        num_scalar_prefetch=1,
        grid=(n, npages // pps),
        in_specs=[pl.BlockSpec((1, nt, BRANCH_W), lambda b, j, pt: (b, 0, 0))]
        + _page_specs(2 * BRANCH_W, 0, pps) + _page_specs(N_HEADS, 0, pps)
        + [pl.BlockSpec((1, 2 * BRANCH_W, PAGE), lambda b, j, pt: (b, 0, 0)),
           pl.BlockSpec((1, N_HEADS, PAGE), lambda b, j, pt: (b, 0, 0))],
        out_specs=pl.BlockSpec((1, nt, BRANCH_W), lambda b, j, pt: (b, 0, 0)),
        scratch_shapes=[pltpu.VMEM((rows, BRANCH_W), BF16), pltpu.VMEM((rows, 1), F32), pltpu.VMEM((rows, 1), F32),
                        pltpu.VMEM((rows, BRANCH_W), F32), pltpu.VMEM((rows, 1), F32)],
    )
    return pl.pallas_call(
        functools.partial(_dfox_t_kernel, nt=nt, pps=pps),
        grid_spec=grid_spec,
        out_shape=jax.ShapeDtypeStruct((n, nt, BRANCH_W), F32),
        compiler_params=_cparams("parallel", "arbitrary"),
        name="fox_decode",
    )(pt, q, *([cache_kvt] * pps), *([cache_lft] * pps), new_kvt, new_lft)


def _dmoba_sel_t_kernel(pt_ref, q_ref, *refs, nt, pps):
    del pt_ref
    k_refs = refs[:pps]
    sel_ref, q3_ref, g_ref = refs[pps:]
    j = pl.program_id(1)
    rows = nt * N_HEADS
    npg = g_ref.shape[1]
    pages_per_block = MOBA_BLOCK // PAGE
    lane = lax.broadcasted_iota(jnp.int32, (rows, npg), 1)

    @pl.when(j == 0)
    def _():
        q3_ref[...] = _split3(_block_diag_queries(q_ref, nt))
        g_ref[...] = jnp.zeros_like(g_ref)

    g = g_ref[...]
    for u, k_ref in enumerate(k_refs):
        qk3 = jnp.dot(q3_ref[...], k_ref[0].astype(BF16), preferred_element_type=F32)
        qk = qk3[0:rows] + qk3[rows:2 * rows] + qk3[2 * rows:3 * rows]
        g = jnp.where(lane == j * pps + u, jnp.sum(qk, axis=1, keepdims=True), g)
    g_ref[...] = g

    @pl.when(j == pl.num_programs(1) - 1)
    def _():
        blk = g
        for u in range(1, pages_per_block):
            blk = blk + pltpu.roll(g, npg - u, 1)
        gs = jnp.where(lane % pages_per_block == 0, blk * (1.0 / MOBA_BLOCK), NEG)
        sel = jnp.zeros((rows, npg), F32)
        for _ in range(MOBA_TOPK):
            mx = jnp.max(gs, axis=1, keepdims=True)
            idx = jnp.min(jnp.where(gs == mx, lane, npg), axis=1, keepdims=True)
            hit = lane == idx
            sel = jnp.where(hit, 1.0, sel)
            gs = jnp.where(hit, MASKED, gs)
        out = sel
        for u in range(1, pages_per_block):
            out = out + pltpu.roll(sel, u, 1)
        sel_ref[0] = out


def moba_select_t_call(pt, q, cache_kvt):
    n, nt, _ = q.shape
    npages = pt.shape[1]
    pps = PPS if npages % PPS == 0 else 1
    rows = nt * N_HEADS
    grid_spec = pltpu.PrefetchScalarGridSpec(
        num_scalar_prefetch=1,
        grid=(n, npages // pps),
        in_specs=[pl.BlockSpec((1, nt, BRANCH_W), lambda b, j, pt: (b, 0, 0))] + _page_specs(BRANCH_W, 0, pps),
        out_specs=pl.BlockSpec((1, rows, npages), lambda b, j, pt: (b, 0, 0)),
        scratch_shapes=[pltpu.VMEM((3 * rows, BRANCH_W), BF16), pltpu.VMEM((rows, npages), F32)],
    )
    return pl.pallas_call(
        functools.partial(_dmoba_sel_t_kernel, nt=nt, pps=pps),
        grid_spec=grid_spec,
        out_shape=jax.ShapeDtypeStruct((n, rows, npages), F32),
        compiler_params=_cparams("parallel", "arbitrary"),
        name="moba_select",
    )(pt, q, *([cache_kvt] * pps))


def _dmoba_t_kernel(pt_ref, q_ref, sel_ref, *refs, nt, pps):
    del pt_ref
    kv_refs = refs[:pps]
    nkv_ref, o_ref, qbd_ref, m_ref, l_ref, acc_ref = refs[pps:]
    j = pl.program_id(1)
    nsteps = pl.num_programs(1)
    rows = nt * N_HEADS
    slope = _slope_col(rows)

    @pl.when(j == 0)
    def _():
        qbd_ref[...] = (_block_diag_queries(q_ref, nt) * SCALE).astype(BF16)
        _softmax_reset(m_ref, l_ref, acc_ref)

    def block(kvs, rel, mask):
        parts = [jnp.dot(qbd_ref[...], kv_ref[0, 0:BRANCH_W, :].astype(BF16), preferred_element_type=F32)
                 for kv_ref in kvs]
        s = parts[0] if len(parts) == 1 else jnp.concatenate(parts, axis=1)
        s = jnp.where(mask, s + slope * rel, MASKED)
        _softmax_step_t(s, [kv_ref[0, BRANCH_W:2 * BRANCH_W, :].astype(BF16) for kv_ref in kvs], m_ref, l_ref, acc_ref)

    npg = sel_ref.shape[2]
    prow = lax.broadcasted_iota(jnp.int32, (npg, pps * PAGE), 0)
    pcol = lax.broadcasted_iota(jnp.int32, (npg, pps * PAGE), 1) // PAGE
    keep = jnp.dot(sel_ref[0].astype(BF16), jnp.where(prow == pcol + j * pps, 1.0, 0.0).astype(BF16),
                   preferred_element_type=F32) > 0.5
    klane = lax.broadcasted_iota(jnp.int32, (1, pps * PAGE), 1)
    block(kv_refs, (klane + (j - nsteps) * (pps * PAGE)).astype(F32), keep)

    @pl.when(j == nsteps - 1)
    def _():
        block([nkv_ref], lax.broadcasted_iota(jnp.int32, (1, PAGE), 1).astype(F32), _new_token_mask(rows, nt))
        o_ref[0] = _extract_heads(acc_ref[...] * (1.0 / l_ref[...]), nt)


def moba_decode_t_call(pt, q, sel, cache_kvt, new_kvt):
    n, nt, _ = q.shape
    npages = pt.shape[1]
    pps = PPS if npages % PPS == 0 else 1
    rows = nt * N_HEADS
    grid_spec = pltpu.PrefetchScalarGridSpec(
        num_scalar_prefetch=1,
        grid=(n, npages // pps),
        in_specs=[pl.BlockSpec((1, nt, BRANCH_W), lambda b, j, pt: (b, 0, 0)),
                  pl.BlockSpec((1, rows, npages), lambda b, j, pt: (b, 0, 0))]
        + _page_specs(2 * BRANCH_W, 0, pps)
        + [pl.BlockSpec((1, 2 * BRANCH_W, PAGE), lambda b, j, pt: (b, 0, 0))],
        out_specs=pl.BlockSpec((1, nt, BRANCH_W), lambda b, j, pt: (b, 0, 0)),
        scratch_shapes=[pltpu.VMEM((rows, BRANCH_W), BF16), pltpu.VMEM((rows, 1), F32), pltpu.VMEM((rows, 1), F32),
                        pltpu.VMEM((rows, BRANCH_W), F32)],
    )
    return pl.pallas_call(
        functools.partial(_dmoba_t_kernel, nt=nt, pps=pps),
        grid_spec=grid_spec,
        out_shape=jax.ShapeDtypeStruct((n, nt, BRANCH_W), F32),
        compiler_params=_cparams("parallel", "arbitrary"),
        name="moba_decode",
    )(pt, q, sel, *([cache_kvt] * pps), new_kvt)


def _compress_t_kernel(pt_ref, *refs, rows, pps):
    del pt_ref
    a_refs = refs[:pps]
    pet_ref, peb_ref, wt_ref, wb_ref, w2k_ref, w2v_ref, w2vt_ref, kc_ref, vc_ref, vct_ref, rows_ref = refs[pps:]
    j = pl.program_id(1)
    pr = lax.broadcasted_iota(jnp.int32, (PAGE, PAGE), 0)
    pc = lax.broadcasted_iota(jnp.int32, (PAGE, PAGE), 1)
    perm = jnp.where(pc == CMP_S * (pr % PAGE_ROWS) + pr // PAGE_ROWS, 1.0, 0.0).astype(BF16)
    w2 = 2 * KV_W
    for u, a_ref in enumerate(a_refs):
        p3 = _dot_nt(perm, _split3(a_ref[0]))
        pg = p3[:, 0:w2] + p3[:, w2:2 * w2] + p3[:, 2 * w2:3 * w2]
        r0 = pl.multiple_of((j * pps + u) * PAGE_ROWS, PAGE_ROWS)
        for i in range(CMP_S):
            rows_ref[pl.ds(r0, PAGE_ROWS), i * w2:(i + 1) * w2] = pg[i * PAGE_ROWS:(i + 1) * PAGE_ROWS, :]

    @pl.when(j == pl.num_programs(1) - 1)
    def _():
        chunk = min(256, rows)
        prev_top = jnp.zeros((1, 4 * CMP_HID), F32)
        for c in range(rows // chunk):
            a = rows_ref[c * chunk:(c + 1) * chunk, :]
            ht = jnp.dot((a + pet_ref[...]).astype(BF16), wt_ref[...], preferred_element_type=F32)
            hb = jnp.dot((a + peb_ref[...]).astype(BF16), wb_ref[...], preferred_element_type=F32)
            row = lax.broadcasted_iota(jnp.int32, ht.shape, 0)
            shifted = jnp.where(row == 0, prev_top, pltpu.roll(ht, 1, 0))
            prev_top = ht[chunk - 1:chunk, :]
            g = _gelu_tanh(shifted + hb).astype(BF16)
            kc_ref[0, c * chunk:(c + 1) * chunk, :] = jnp.dot(
                g[:, :2 * CMP_HID], w2k_ref[...], preferred_element_type=F32).astype(BF16)
            vc_ref[0, c * chunk:(c + 1) * chunk, :] = jnp.dot(
                g[:, 2 * CMP_HID:], w2v_ref[...], preferred_element_type=F32).astype(BF16)
            vct_ref[0, :, c * chunk:(c + 1) * chunk] = _dot_nt(w2vt_ref[...], g[:, 2 * CMP_HID:]).astype(BF16)


def compress_t_call(pages_t, page_idx, pe_top, pe_bot, wt, wb, w2k, w2v, w2vt, n, npages):
    rows = npages * PAGE_ROWS
    pps = PPS if npages % PPS == 0 else 1
    const = lambda b, j, pt: (0, 0)
    grid_spec = pltpu.PrefetchScalarGridSpec(
        num_scalar_prefetch=1,
        grid=(n, npages // pps),
        in_specs=_page_specs(2 * KV_W, 0, pps)
        + [pl.BlockSpec((1, CMP_ROW), const),
           pl.BlockSpec((1, CMP_ROW), const),
           pl.BlockSpec((CMP_ROW, 4 * CMP_HID), const, pipeline_mode=pl.Buffered(1)),
           pl.BlockSpec((CMP_ROW, 4 * CMP_HID), const, pipeline_mode=pl.Buffered(1)),
           pl.BlockSpec((2 * CMP_HID, KV_W), const),
           pl.BlockSpec((2 * CMP_HID, KV_W), const),
           pl.BlockSpec((KV_W, 2 * CMP_HID), const)],
        out_specs=[pl.BlockSpec((1, rows, KV_W), lambda b, j, pt: (b, 0, 0)),
                   pl.BlockSpec((1, rows, KV_W), lambda b, j, pt: (b, 0, 0)),
                   pl.BlockSpec((1, KV_W, rows), lambda b, j, pt: (b, 0, 0))],
        scratch_shapes=[pltpu.VMEM((rows, CMP_ROW), F32)],
    )
    return pl.pallas_call(
        functools.partial(_compress_t_kernel, rows=rows, pps=pps),
        grid_spec=grid_spec,
        out_shape=[jax.ShapeDtypeStruct((n, rows, KV_W), BF16),
                   jax.ShapeDtypeStruct((n, rows, KV_W), BF16),
                   jax.ShapeDtypeStruct((n, KV_W, rows), BF16)],
        compiler_params=_cparams("parallel", "arbitrary"),
        name="nsa_compress",
    )(page_idx, *([pages_t] * pps), pe_top, pe_bot, wt, wb, w2k, w2v, w2vt)


def _dnsa_t_kernel(pt_ref, q_ref, gate_ref, kc_ref, vc_ref, *refs, nt, nsb_pad, pps):
    del pt_ref
    kv_refs = refs[:pps]
    nkv_ref, win_ref, nwin_ref, o_ref, qbd_ref, sel_ref, ocmp_ref, m_ref, l_ref, acc_ref = refs[pps:]
    j = pl.program_id(1)
    nsteps = pl.num_programs(1)
    rows = nt * N_HEADS
    ncmp = kc_ref.shape[1]
    q0 = nsteps * (pps * PAGE)
    slope = _slope_col(rows)
    klane = lax.broadcasted_iota(jnp.int32, (1, PAGE), 1)
    blocks_per_page = PAGE // SLC_BLOCK

    @pl.when(j == 0)
    def _():
        pr = lax.broadcasted_iota(jnp.int32, (BRANCH_W, KV_W), 0)
        pc = lax.broadcasted_iota(jnp.int32, (BRANCH_W, KV_W), 1)
        place = jnp.where((pr % HEAD_DIM == pc % HEAD_DIM) & (pc // HEAD_DIM == pr // (NSA_R * HEAD_DIM)), 1.0, 0.0)
        qg = jnp.dot(_block_diag_queries(q_ref, nt), place, preferred_element_type=F32, precision=HIGHEST)
        qbd = (qg * SCALE).astype(BF16)
        qbd_ref[...] = qbd
        c = lax.broadcasted_iota(jnp.int32, (1, ncmp), 1)
        s = _dot_nt(qbd, kc_ref[0]) + slope * (CMP_S * c + (CMP_S - 1) - q0).astype(F32)
        valid = c >= 1
        s = jnp.where(valid, s, MASKED)
        p = jnp.where(valid, jnp.exp(s - jnp.max(s, axis=1, keepdims=True)), 0.0)
        p = p * (1.0 / jnp.sum(p, axis=1, keepdims=True))
        ocmp_ref[...] = gate_ref[0, :, 0:1] * jnp.dot(p.astype(BF16), vc_ref[0], preferred_element_type=F32)
        gr = lax.broadcasted_iota(jnp.int32, (nt * NSA_G, rows), 0)
        gc = lax.broadcasted_iota(jnp.int32, (nt * NSA_G, rows), 1) // NSA_R
        imp = jnp.dot(jnp.where(gr == gc, 1.0, 0.0), p, preferred_element_type=F32, precision=HIGHEST)
        ratio = SLC_BLOCK // CMP_S
        mr = lax.broadcasted_iota(jnp.int32, (ncmp, nsb_pad), 0)
        mb = lax.broadcasted_iota(jnp.int32, (ncmp, nsb_pad), 1)
        gather_m = jnp.where((mr >= ratio * mb) & (mr <= ratio * mb + ratio) & (mr >= 1), 1.0, 0.0)
        p_slc = jnp.dot(imp, gather_m, preferred_element_type=F32, precision=HIGHEST)
        bj = lax.broadcasted_iota(jnp.int32, p_slc.shape, 1)
        own = q0 // SLC_BLOCK
        forced = (bj == 0) | (bj >= own - 1)
        allowed = bj <= own
        score = jnp.where(allowed, jnp.where(forced, p_slc + FORCE_SCORE, p_slc), -1.0)
        sel = jnp.zeros(p_slc.shape, F32)
        for _ in range(SLC_N):
            mx = jnp.max(score, axis=1, keepdims=True)
            idx = jnp.min(jnp.where(score == mx, bj, nsb_pad), axis=1, keepdims=True)
            hit = bj == idx
            sel = jnp.where(hit, 1.0, sel)
            score = jnp.where(hit, -2.0, score)
        sel = jnp.where(allowed, sel, 0.0)
        er = lax.broadcasted_iota(jnp.int32, (rows, nt * NSA_G), 0) // NSA_R
        ec = lax.broadcasted_iota(jnp.int32, (rows, nt * NSA_G), 1)
        sel_ref[...] = jnp.dot(jnp.where(er == ec, 1.0, 0.0), sel, preferred_element_type=F32).astype(BF16)
        _softmax_reset(m_ref, l_ref, acc_ref)

    def block(kvs, rel, mask):
        parts = [jnp.dot(qbd_ref[...], kv_ref[0, 0:KV_W, :].astype(BF16), preferred_element_type=F32) for kv_ref in kvs]
        s = (parts[0] if len(parts) == 1 else jnp.concatenate(parts, axis=1)) + slope * rel
        if mask is not None:
            s = jnp.where(mask, s, MASKED)
        _softmax_step_t(s, [kv_ref[0, KV_W:2 * KV_W, :].astype(BF16) for kv_ref in kvs], m_ref, l_ref, acc_ref)

    brow = lax.broadcasted_iota(jnp.int32, (nsb_pad, pps * PAGE), 0)
    bcol = lax.broadcasted_iota(jnp.int32, (nsb_pad, pps * PAGE), 1) // SLC_BLOCK
    expand = jnp.where(brow == bcol + j * (pps * blocks_per_page), 1.0, 0.0).astype(BF16)
    keep = jnp.dot(sel_ref[...], expand, preferred_element_type=F32) > 0.5
    kl = lax.broadcasted_iota(jnp.int32, (1, pps * PAGE), 1)
    block(kv_refs, (kl + (j - nsteps) * (pps * PAGE)).astype(F32), keep)

    @pl.when(j == nsteps - 1)
    def _():
        new_mask = _new_token_mask(rows, nt)
        block([nkv_ref], klane.astype(F32), new_mask)
        o_slc = acc_ref[...] * (1.0 / l_ref[...])
        _softmax_reset(m_ref, l_ref, acc_ref)
        wlen = win_ref.shape[2]
        wi = lax.broadcasted_iota(jnp.int32, (rows, wlen), 1)
        wt = lax.broadcasted_iota(jnp.int32, (rows, wlen), 0) // N_HEADS
        wrel = (lax.broadcasted_iota(jnp.int32, (1, wlen), 1) - wlen).astype(F32)
        block([win_ref], wrel, wi > wt + (wlen - WINDOW))
        block([nwin_ref], klane.astype(F32), new_mask)
        o_win = acc_ref[...] * (1.0 / l_ref[...])
        o = ocmp_ref[...] + gate_ref[0, :, 1:2] * o_slc + gate_ref[0, :, 2:3] * o_win
        ur = lax.broadcasted_iota(jnp.int32, (KV_W, BRANCH_W), 0)
        uc = lax.broadcasted_iota(jnp.int32, (KV_W, BRANCH_W), 1)
        unplace = jnp.where((ur % HEAD_DIM == uc % HEAD_DIM) & (ur // HEAD_DIM == uc // (NSA_R * HEAD_DIM)), 1.0, 0.0)
        o_ref[0] = _extract_heads(jnp.dot(o, unplace, preferred_element_type=F32, precision=HIGHEST), nt)


def nsa_decode_t_call(pt, q, gate, kcmp, vcmp, cache_kvt, new_kvt, win_t, new_win_t):
    n, nt, _ = q.shape
    npages = pt.shape[1]
    pps = PPS if npages % PPS == 0 else 1
    rows = nt * N_HEADS
    ncmp = kcmp.shape[1]
    nsb = (npages * PAGE) // SLC_BLOCK + 1
    nsb_pad = -(-nsb // 128) * 128
    wlen = win_t.shape[2]
    grid_spec = pltpu.PrefetchScalarGridSpec(
        num_scalar_prefetch=1,
        grid=(n, npages // pps),
        in_specs=[pl.BlockSpec((1, nt, BRANCH_W), lambda b, j, pt: (b, 0, 0)),
                  pl.BlockSpec((1, rows, 3), lambda b, j, pt: (b, 0, 0)),
                  pl.BlockSpec((1, ncmp, KV_W), lambda b, j, pt: (b, 0, 0)),
                  pl.BlockSpec((1, ncmp, KV_W), lambda b, j, pt: (b, 0, 0))]
        + _page_specs(2 * KV_W, 1, pps)
        + [pl.BlockSpec((1, 2 * KV_W, PAGE), lambda b, j, pt: (b, 0, 0)),
           pl.BlockSpec((1, 2 * KV_W, wlen), lambda b, j, pt: (b, 0, 0)),
           pl.BlockSpec((1, 2 * KV_W, PAGE), lambda b, j, pt: (b, 0, 0))],
        out_specs=pl.BlockSpec((1, nt, BRANCH_W), lambda b, j, pt: (b, 0, 0)),
        scratch_shapes=[pltpu.VMEM((rows, KV_W), BF16), pltpu.VMEM((rows, nsb_pad), BF16),
                        pltpu.VMEM((rows, KV_W), F32), pltpu.VMEM((rows, 1), F32), pltpu.VMEM((rows, 1), F32),
                        pltpu.VMEM((rows, KV_W), F32)],
    )
    return pl.pallas_call(
        functools.partial(_dnsa_t_kernel, nt=nt, nsb_pad=nsb_pad, pps=pps),
        grid_spec=grid_spec,
        out_shape=jax.ShapeDtypeStruct((n, nt, BRANCH_W), F32),
        compiler_params=_cparams("parallel", "arbitrary"),
        name="nsa_decode",
    )(pt, q, gate, kcmp, vcmp, *([cache_kvt] * pps), new_kvt, win_t, new_win_t)


def _compress_weights(cmp_w1, cmp_w2, cmp_pe):
    w1 = cmp_w1.reshape(2, CMP_L, HEAD_DIM, CMP_HID)
    eye_w = jnp.eye(2, dtype=F32)
    eye_g = jnp.eye(NSA_G, dtype=F32)
    big = jnp.einsum('widh,sw,gk->isgdwkh', w1, eye_w, eye_g)
    big = big.reshape(CMP_L, 2 * KV_W, 4 * CMP_HID)
    wt = big[:CMP_S].reshape(CMP_ROW, 4 * CMP_HID).astype(BF16)
    wb = big[CMP_S:].reshape(CMP_ROW, 4 * CMP_HID).astype(BF16)
    w2k = jnp.einsum('hd,gk->ghkd', cmp_w2[0], eye_g).reshape(2 * CMP_HID, KV_W).astype(BF16)
    w2v = jnp.einsum('hd,gk->ghkd', cmp_w2[1], eye_g).reshape(2 * CMP_HID, KV_W).astype(BF16)
    w2vt = jnp.einsum('hd,gk->kdgh', cmp_w2[1], eye_g).reshape(KV_W, 2 * CMP_HID).astype(BF16)
    pe = jnp.broadcast_to(cmp_pe[:, :, None, :], (2, CMP_L, NSA_G, HEAD_DIM))
    pe = jnp.transpose(pe, (1, 0, 2, 3)).reshape(CMP_L, 2 * KV_W)
    pe_top = pe[:CMP_S].reshape(1, CMP_ROW)
    pe_bot = pe[CMP_S:].reshape(1, CMP_ROW)
    return pe_top, pe_bot, wt, wb, w2k, w2v, w2vt


def _layer_weights(l, w_in, w_branch, w_out, w_up, w_down):
    wl = w_in[l]
    small = jnp.concatenate(
        [wl[:, OFF_FOX_F:OFF_FOX_F + N_HEADS], wl[:, OFF_NSA_GATE:OFF_NSA_GATE + 3 * N_HEADS],
         jnp.zeros((D_MODEL, SMALL_W - 4 * N_HEADS), wl.dtype)], axis=1)
    return dict(
        conv=wl[:, OFF_CONV:OFF_FOX].astype(BF16),
        fox=wl[:, OFF_FOX:OFF_FOX_F].astype(BF16),
        moba=wl[:, OFF_MOBA:OFF_NSA].astype(BF16),
        nsa=wl[:, OFF_NSA:OFF_NSA_GATE].astype(BF16),
        small=small.astype(BF16),
        gate=wl[:, OFF_MERGE:].astype(BF16),
        branch=w_branch[l].astype(BF16),
        out=w_out[l].astype(BF16),
        up=w_up[l].astype(BF16),
        down=w_down[l].astype(BF16),
    )


def _finish_layer(x, branches, gate, w, g_mlp, g_next, next_dtype, tm):
    merged = merge_call(branches, gate, w["branch"], tm, 512)
    x1, hm = outproj_call(merged, w["out"], x, g_mlp, tm)
    return mlp_call(hm, w["up"], w["down"], x1, g_next, next_dtype, tm, 512)


def prompt_mixers(h, n, t, w, b_forget, conv_w, cmp_w, tm):
    bw = BRANCH_W
    (z_conv,) = proj_call(h, w["conv"], [("f32", 3 * bw)], [(a, a + 512, ((0, a),)) for a in (0, 512, 1024)],
                          tm, "proj_conv")
    qkv_defs = [("f32", bw), ("f32", 2 * bw), ("bf16", bw), ("bf16T", bw)]
    qkv_plan = [(0, bw, ((0, 0),)), (bw, 2 * bw, ((1, 0), (2, 0))), (2 * bw, 3 * bw, ((1, bw), (3, 0)))]
    fox_q, fox_kv, fox_kb, fox_vt = proj_call(h, w["fox"], qkv_defs, qkv_plan, tm, "proj_fox")
    moba_q, moba_kv, moba_kb, moba_vt, moba_km = proj_call(
        h, w["moba"], qkv_defs + [("blockmean", bw)],
        [qkv_plan[0], (bw, 2 * bw, ((1, 0), (2, 0), (4, 0))), qkv_plan[2]], tm, "proj_moba")
    kv = KV_W
    nsa_q, nsa_kv, nsa_win, nsa_ks, nsa_vst, nsa_kw, nsa_vwt, nsa_cmp_t = proj_call(
        h, w["nsa"],
        [("f32", bw), ("f32", 4 * kv), ("f32", 2 * kv), ("bf16", kv), ("bf16T", kv), ("bf16", kv), ("bf16T", kv),
         ("f32T", 2 * kv)],
        [(0, bw, ((0, 0),)), (bw, bw + 2 * kv, ((1, 0), (7, 0))),
         (bw + 2 * kv, bw + 3 * kv, ((1, 2 * kv), (3, 0))), (bw + 3 * kv, bw + 4 * kv, ((1, 3 * kv), (4, 0))),
         (bw + 4 * kv, bw + 5 * kv, ((2, 0), (5, 0))), (bw + 5 * kv, bw + 6 * kv, ((2, kv), (6, 0)))],
        tm, "proj_nsa")
    (z_small,) = proj_call(h, w["small"], [("f32", SMALL_W)], [(0, SMALL_W, ((0, 0),))], tm, "proj_small")

    out_a, new_conv = conv_prompt_call(z_conv, conv_w, n, t, tm)
    lg, fk = small_call(z_small, b_forget, n, t, tm)
    logf = lg[:, :N_HEADS].reshape(n, t, N_HEADS)
    gate_t = lg[:, N_HEADS:4 * N_HEADS].T

    out_b = fox_prompt_call(fox_q, fox_kb, fox_vt, fk, n, t)
    out_c = moba_prompt_call(moba_q, moba_kb, moba_vt, moba_km, n, t)

    npages = t // PAGE
    page_idx = jnp.arange(n * npages, dtype=jnp.int32).reshape(n, npages)
    kcmp, _, vcmpt = compress_t_call(nsa_cmp_t, page_idx, *cmp_w, n, npages)
    out_d = nsa_prompt_call(nsa_q, kcmp, vcmpt, nsa_ks, nsa_vst, nsa_kw, nsa_vwt, gate_t, n, t)

    wb = min(WINDOW, t)
    new_state = (new_conv,
                 fox_kv.reshape(n, t, 2, N_HEADS, HEAD_DIM),
                 logf,
                 moba_kv.reshape(n, t, 2, N_HEADS, HEAD_DIM),
                 nsa_kv.reshape(n, t, 2, 2, NSA_G, HEAD_DIM),
                 nsa_win.reshape(n, t, 2, NSA_G, HEAD_DIM)[:, t - wb:])
    return [out_a, out_b, out_c, out_d], new_state


def sample_mixers(h, n, t, q0, past, w, b_forget, conv_w, cmp_w1, cmp_w2, cmp_pe, tm):
    dt = F32
    bw = BRANCH_W
    kv = KV_W
    (z_conv,) = proj_call(h, w["conv"], [("f32", 3 * bw)], [(0, 3 * bw, ((0, 0),))], tm, "proj_conv_s")
    fox_q, fox_kv = proj_call(h, w["fox"], [("f32", bw), ("f32", 2 * bw)],
                              [(0, bw, ((0, 0),)), (bw, 3 * bw, ((1, 0),))], tm, "proj_fox_s")
    moba_q, moba_kv = proj_call(h, w["moba"], [("f32", bw), ("f32", 2 * bw)],
                                [(0, bw, ((0, 0),)), (bw, 3 * bw, ((1, 0),))], tm, "proj_moba_s")
    nsa_q, nsa_kv, nsa_win = proj_call(
        h, w["nsa"], [("f32", bw), ("f32", 4 * kv), ("f32", 2 * kv)],
        [(0, bw, ((0, 0),)), (bw, bw + 4 * kv, ((1, 0),)), (bw + 4 * kv, bw + 6 * kv, ((2, 0),))], tm, "proj_nsa_s")
    (z_small,) = proj_call(h, w["small"], [("f32", SMALL_W)], [(0, SMALL_W, ((0, 0),))], tm, "proj_small_s")

    def heads(a, nh):
        return a.reshape(n, t, nh, HEAD_DIM)

    def cat(old, new):
        return jnp.concatenate([old.astype(new.dtype), new], axis=1)

    zc = z_conv.reshape(n, t, 3 * bw)
    conv_x, conv_b, conv_c = zc[..., :bw], zc[..., bw:2 * bw], zc[..., 2 * bw:]
    u = conv_c * conv_x
    ext = jnp.concatenate([past['conv'].astype(dt), u], axis=1)
    y_conv = ext[:, 0:t] * conv_w[0]
    for j in range(1, CONV_W):
        y_conv = y_conv + ext[:, j:j + t] * conv_w[j]
    out_a = conv_b * y_conv
    new_conv = ext[:, -(CONV_W - 1):]

    fkv = fox_kv.reshape(n, t, 2, N_HEADS, HEAD_DIM)
    fq, fk, fv = heads(fox_q, N_HEADS), fkv[:, :, 0], fkv[:, :, 1]
    fox_f = z_small.reshape(n, t, SMALL_W)[..., :N_HEADS]
    logf = jax.nn.log_sigmoid(fox_f + b_forget)
    lf_all = jnp.concatenate([past['fox_logf'].astype(F32), logf], axis=1)
    out_b = fox_attention(fq, cat(past['fox_k'], fk), cat(past['fox_v'], fv), lf_all, q0)

    mkv = moba_kv.reshape(n, t, 2, N_HEADS, HEAD_DIM)
    mq, mk, mv = heads(moba_q, N_HEADS), mkv[:, :, 0], mkv[:, :, 1]
    out_c = moba_attention(mq, cat(past['moba_k'], mk), cat(past['moba_v'], mv), q0, alibi_slopes(N_HEADS))

    nq = heads(nsa_q, N_HEADS)
    nkv = nsa_kv.reshape(n, t, 2, 2, NSA_G, HEAD_DIM)
    kc, vc, ks, vs = nkv[:, :, 0, 0], nkv[:, :, 0, 1], nkv[:, :, 1, 0], nkv[:, :, 1, 1]
    nwin = nsa_win.reshape(n, t, 2, NSA_G, HEAD_DIM)
    kw, vw = nwin[:, :, 0], nwin[:, :, 1]
    ngate = jax.nn.sigmoid(z_small.reshape(n, t, SMALL_W)[..., N_HEADS:4 * N_HEADS]).reshape(n, t, NSA_G, NSA_R, 3)
    kw_all = cat(past['win_k'], kw)
    vw_all = cat(past['win_v'], vw)
    w_ofs = q0 - past['win_k'].shape[1]
    out_d = nsa_attention(nq, cat(past['nsa_kc'], kc), cat(past['nsa_vc'], vc),
                          cat(past['nsa_ks'], ks), cat(past['nsa_vs'], vs),
                          kw_all, vw_all, ngate, q0, w_ofs, alibi_slopes(N_HEADS), cmp_w1, cmp_w2, cmp_pe)
    branches = [a.reshape(n * t, bw) for a in (out_a, out_b, out_c, out_d)]
    wb = past['win_k'].shape[1]
    new_state = (new_conv, fkv, logf.astype(dt), mkv, nkv,
                 jnp.stack([kw_all, vw_all], axis=2)[:, -wb:])
    return branches, new_state


def decode_mixers(h, n, nt, pt, caches, state_conv, state_win, w, b_forget, conv_w, cmp_w, tm):
    bw = BRANCH_W
    kv = KV_W
    cache_fox, cache_lf, cache_moba, cache_nsa = caches
    (z_conv,) = proj_call(h, w["conv"], [("f32", 3 * bw)], [(0, 3 * bw, ((0, 0),))], tm, "proj_conv_s")
    fox_q, fox_kv = proj_call(h, w["fox"], [("f32", bw), ("f32", 2 * bw)],
                              [(0, bw, ((0, 0),)), (bw, 3 * bw, ((1, 0),))], tm, "proj_fox_s")
    moba_q, moba_kv = proj_call(h, w["moba"], [("f32", bw), ("f32", 2 * bw)],
                                [(0, bw, ((0, 0),)), (bw, 3 * bw, ((1, 0),))], tm, "proj_moba_s")
    nsa_q, nsa_kv, nsa_win = proj_call(
        h, w["nsa"], [("f32", bw), ("f32", 4 * kv), ("f32", 2 * kv)],
        [(0, bw, ((0, 0),)), (bw, bw + 4 * kv, ((1, 0),)), (bw + 4 * kv, bw + 6 * kv, ((2, 0),))], tm, "proj_nsa_s")
    (z_small,) = proj_call(h, w["small"], [("f32", SMALL_W)], [(0, SMALL_W, ((0, 0),))], tm, "proj_small_s")

    def new_page(a):
        at = jnp.transpose(a.reshape(n, nt, a.shape[-1]), (0, 2, 1))
        return jnp.pad(at, ((0, 0), (0, 0), (0, PAGE - nt)))

    out_a, u = conv_sample_call(z_conv, state_conv, conv_w, n, nt)
    new_conv = u.reshape(n, nt, bw)[:, nt - (CONV_W - 1):]

    lg, _ = small_call(z_small, b_forget, 1, n * nt, n * nt)
    logf = lg[:, :N_HEADS]
    gate = lg[:, N_HEADS:4 * N_HEADS].reshape(n, nt * N_HEADS, 3)

    out_b = fox_decode_t_call(pt, fox_q.reshape(n, nt, bw), cache_fox, cache_lf, new_page(fox_kv), new_page(logf))
    mq = moba_q.reshape(n, nt, bw)
    sel = moba_select_t_call(pt, mq, cache_moba)
    out_c = moba_decode_t_call(pt, mq, sel, cache_moba, new_page(moba_kv))

    npages = pt.shape[1]
    kcmp, vcmp, _ = compress_t_call(cache_nsa, pt, *cmp_w, n, npages)
    win = state_win.reshape(n, state_win.shape[1], 2 * kv)
    out_d = nsa_decode_t_call(pt, nsa_q.reshape(n, nt, bw), gate, kcmp, vcmp, cache_nsa,
                              new_page(nsa_kv[:, 2 * kv:]), jnp.transpose(win, (0, 2, 1)), new_page(nsa_win))

    wb = win.shape[1]
    win_all = jnp.concatenate([win, nsa_win.reshape(n, nt, 2 * kv)], axis=1)[:, nt:]
    new_state = (new_conv,
                 fox_kv.reshape(n, nt, 2, N_HEADS, HEAD_DIM),
                 logf.reshape(n, nt, N_HEADS),
                 moba_kv.reshape(n, nt, 2, N_HEADS, HEAD_DIM),
                 nsa_kv.reshape(n, nt, 2, 2, NSA_G, HEAD_DIM),
                 win_all.reshape(n, wb, 2, NSA_G, HEAD_DIM))
    branches = [out_a] + [o.reshape(n * nt, bw) for o in (out_b, out_c, out_d)]
    return branches, new_state


def kernel(x_prompt, x_sample, state_conv, cache_fox_kv, cache_fox_logf, cache_moba_kv, cache_nsa_kv,
           state_nsa_win, page_table, g_mix, w_in, b_forget, conv_w, cmp_w1, cmp_w2, cmp_pe,
           w_branch, w_out, g_mlp, w_up, w_down, g_final):
    depth = w_in.shape[0]
    nb, seq, _ = x_prompt.shape
    db, dseq, _ = x_sample.shape
    tm_p, tm_s = 512, db * dseq
    pool, page = cache_fox_kv.shape[1], cache_fox_kv.shape[2]
    def pages_t(c, width):
        return jnp.transpose(c.reshape(depth * pool, page, width), (0, 2, 1))

    caches = (pages_t(cache_fox_kv, 2 * BRANCH_W), pages_t(cache_fox_logf, N_HEADS),
              pages_t(cache_moba_kv, 2 * BRANCH_W), pages_t(cache_nsa_kv, 4 * KV_W))

    xp = x_prompt.reshape(nb * seq, D_MODEL)
    xs = x_sample.reshape(db * dseq, D_MODEL)
    hp = rms_norm_call(xp, g_mix[0], BF16, tm_p)
    hs = rms_norm_call(xs, g_mix[0], BF16, tm_s)
    new_p, new_s = [], []
    for l in range(depth):
        w = _layer_weights(l, w_in, w_branch, w_out, w_up, w_down)
        cmp_w = _compress_weights(cmp_w1[l], cmp_w2[l], cmp_pe[l])
        last = l == depth - 1
        g_next = g_final if last else g_mix[l + 1]
        next_dtype = F32 if last else BF16

        branches, st_p = prompt_mixers(hp, nb, seq, w, b_forget[l], conv_w[l], cmp_w, tm_p)
        gate_p = gate_call(hp, w["gate"], tm_p, 1024)
        xp, hp = _finish_layer(xp, branches, gate_p, w, g_mlp[l], g_next, next_dtype, tm_p)
        new_p.append(st_p)

        branches, st_s = decode_mixers(hs, db, dseq, page_table + l * pool, caches, state_conv[l],
                                       state_nsa_win[l], w, b_forget[l], conv_w[l], cmp_w, tm_s)
        gate_s = gate_call(hs, w["gate"], tm_s, 1024)
        xs, hs = _finish_layer(xs, branches, gate_s, w, g_mlp[l], g_next, next_dtype, tm_s)
        new_s.append(st_s)
    y_prompt = hp.reshape(nb, seq, D_MODEL)
    y_sample = hs.reshape(db, dseq, D_MODEL)
    conv_p, fox_kv_p, fox_logf_p, moba_kv_p, nsa_kv_p, win_p = [jnp.stack(a) for a in zip(*new_p)]
    conv_s, fox_kv_s, fox_logf_s, moba_kv_s, nsa_kv_s, win_s = [jnp.stack(a) for a in zip(*new_s)]
    return (y_prompt, y_sample, conv_p, conv_s, fox_kv_p, fox_kv_s, fox_logf_p, fox_logf_s,
            moba_kv_p, moba_kv_s, nsa_kv_p, nsa_kv_s, win_p, win_s)
```

```python
import functools

import jax
import jax.numpy as jnp
from jax import lax
from jax.experimental import pallas as pl
from jax.experimental.pallas import tpu as pltpu

F32 = jnp.float32
BF16 = jnp.bfloat16
HIGHEST = lax.Precision.HIGHEST

D_MODEL = 2048
HEAD_DIM = 64
N_BRANCH = 4
BRANCH_W = D_MODEL // N_BRANCH
N_HEADS = BRANCH_W // HEAD_DIM
CONV_W = 3
NSA_G = 2
NSA_R = N_HEADS // NSA_G
MOBA_BLOCK = 256
MOBA_TOPK = 3
CMP_L = 32
CMP_S = 16
CMP_HID = 4 * HEAD_DIM
SLC_BLOCK = 64
SLC_N = 16
WINDOW = 512
D_FF = 4 * D_MODEL
Q_BLOCK = 128
SPARSE_Q_BLOCK = 32
RMS_EPS = 1e-6
NEG = -1e30
MASKED = 2.0 * NEG
FORCE_SCORE = 1e4
KV_W = NSA_G * HEAD_DIM
SCALE = HEAD_DIM ** -0.5
PAIR_W = 2 * HEAD_DIM

OFF_CONV = 0
OFF_FOX = 3 * BRANCH_W
OFF_FOX_F = OFF_FOX + 3 * BRANCH_W
OFF_MOBA = OFF_FOX_F + N_HEADS
OFF_NSA = OFF_MOBA + 3 * BRANCH_W
OFF_NSA_GATE = OFF_NSA + BRANCH_W + 6 * KV_W
OFF_MERGE = OFF_NSA_GATE + 3 * N_HEADS
IN_W = OFF_MERGE + N_BRANCH * D_MODEL
SMALL_W = 128

TQ = 256
TK = 256
CMP_ROW = CMP_S * 2 * KV_W
PAGE_ROWS = 8

VMEM_LIMIT = 56 * 1024 * 1024


def _cparams(*sem):
    return pltpu.CompilerParams(dimension_semantics=sem, vmem_limit_bytes=VMEM_LIMIT)


def _rms(x, g):
    return x * lax.rsqrt(jnp.mean(x * x, axis=-1, keepdims=True) + RMS_EPS) * g


def _dot_nt(a, b, precision=None):
    return lax.dot_general(a, b, (((1,), (1,)), ((), ())), preferred_element_type=F32, precision=precision)


def _norm_kernel(x_ref, g_ref, o_ref):
    o_ref[...] = _rms(x_ref[...], g_ref[...]).astype(o_ref.dtype)


def rms_norm_call(x, g, out_dtype, tm):
    t, d = x.shape
    return pl.pallas_call(
        _norm_kernel,
        grid=(t // tm,),
        in_specs=[pl.BlockSpec((tm, d), lambda i: (i, 0)),
                  pl.BlockSpec((1, d), lambda i: (0, 0))],
        out_specs=pl.BlockSpec((tm, d), lambda i: (i, 0)),
        out_shape=jax.ShapeDtypeStruct((t, d), out_dtype),
        compiler_params=_cparams("parallel"),
        name="rms_norm",
    )(x, g.reshape(1, d))


def _proj_kernel(h_ref, w_ref, *out_refs, kinds, plan):
    h = h_ref[...]
    for c0, c1, dests in plan:
        z = jnp.dot(h, w_ref[:, c0:c1], preferred_element_type=F32)
        for idx, off in dests:
            o_ref, kind = out_refs[idx], kinds[idx]
            if kind == "bf16T":
                for r in range(z.shape[0] // TK):
                    o_ref[r, off:off + c1 - c0, :] = z[r * TK:(r + 1) * TK].T.astype(BF16)
            elif kind == "f32T":
                for r in range(z.shape[0] // 128):
                    o_ref[r, off:off + c1 - c0, :] = z[r * 128:(r + 1) * 128].T
            elif kind == "blockmean":
                for r in range(z.shape[0] // MOBA_BLOCK):
                    o_ref[r, :, off:off + c1 - c0] = jnp.mean(
                        z[r * MOBA_BLOCK:(r + 1) * MOBA_BLOCK], axis=0, keepdims=True)
            else:
                o_ref[:, off:off + c1 - c0] = z.astype(o_ref.dtype)


def proj_call(h, w, out_defs, plan, tm, name):
    t, d = h.shape
    n = w.shape[1]
    out_specs, out_shapes = [], []
    for kind, width in out_defs:
        if kind == "bf16T":
            out_specs.append(pl.BlockSpec((tm // TK, width, TK), lambda i: (i, 0, 0)))
            out_shapes.append(jax.ShapeDtypeStruct((t // TK, width, TK), BF16))
        elif kind == "f32T":
            out_specs.append(pl.BlockSpec((tm // 128, width, 128), lambda i: (i, 0, 0)))
            out_shapes.append(jax.ShapeDtypeStruct((t // 128, width, 128), F32))
        elif kind == "blockmean":
            out_specs.append(pl.BlockSpec((tm // MOBA_BLOCK, 1, width), lambda i: (i, 0, 0)))
            out_shapes.append(jax.ShapeDtypeStruct((t // MOBA_BLOCK, 1, width), F32))
        else:
            out_specs.append(pl.BlockSpec((tm, width), lambda i: (i, 0)))
            out_shapes.append(jax.ShapeDtypeStruct((t, width), BF16 if kind == "bf16" else F32))
    return pl.pallas_call(
        functools.partial(_proj_kernel, kinds=tuple(k for k, _ in out_defs), plan=tuple(plan)),
        grid=(t // tm,),
        in_specs=[pl.BlockSpec((tm, d), lambda i: (i, 0)),
                  pl.BlockSpec((d, n), lambda i: (0, 0))],
        out_specs=out_specs,
        out_shape=out_shapes,
        compiler_params=_cparams("parallel"),
        name=name,
    )(h, w)


def _gate_kernel(h_ref, w_ref, o_ref):
    z = jnp.dot(h_ref[...], w_ref[...], preferred_element_type=F32)
    o_ref[...] = jax.nn.sigmoid(z)


def gate_call(h, w, tm, tn):
    t, d = h.shape
    n = w.shape[1]
    return pl.pallas_call(
        _gate_kernel,
        grid=(t // tm, n // tn),
        in_specs=[pl.BlockSpec((tm, d), lambda i, j: (i, 0)),
                  pl.BlockSpec((d, tn), lambda i, j: (0, j))],
        out_specs=pl.BlockSpec((tm, tn), lambda i, j: (i, j)),
        out_shape=jax.ShapeDtypeStruct((t, n), F32),
        compiler_params=_cparams("parallel", "arbitrary"),
        name="merge_gate_proj",
    )(h, w)


def _merge_kernel(h_ref, oa_ref, ob_ref, oc_ref, od_ref, g0_ref, g1_ref, g2_ref, g3_ref, wb_ref, o_ref):
    h = h_ref[...]
    acc = None
    for b, (o, wg) in enumerate(zip((oa_ref, ob_ref, oc_ref, od_ref), (g0_ref, g1_ref, g2_ref, g3_ref))):
        gate = jax.nn.sigmoid(jnp.dot(h, wg[...], preferred_element_type=F32))
        br = jnp.dot(o[...].astype(BF16), wb_ref[b], preferred_element_type=F32)
        term = gate * br
        acc = term if acc is None else acc + term
    o_ref[...] = acc.astype(o_ref.dtype)


def merge_call(h, branches, w_gate, w_branch, tm, tn):
    t, d = h.shape
    nj = D_MODEL // tn
    gate_specs = [pl.BlockSpec((d, tn), functools.partial(lambda i, j, b: (0, b * nj + j), b=b))
                  for b in range(N_BRANCH)]
    return pl.pallas_call(
        _merge_kernel,
        grid=(t // tm, nj),
        in_specs=[pl.BlockSpec((tm, d), lambda i, j: (i, 0))]
        + [pl.BlockSpec((tm, BRANCH_W), lambda i, j: (i, 0))] * N_BRANCH + gate_specs
        + [pl.BlockSpec((N_BRANCH, BRANCH_W, tn), lambda i, j: (0, 0, j))],
        out_specs=pl.BlockSpec((tm, tn), lambda i, j: (i, j)),
        out_shape=jax.ShapeDtypeStruct((t, D_MODEL), BF16),
        compiler_params=_cparams("parallel", "arbitrary"),
        name="branch_merge",
    )(h, *branches, w_gate, w_gate, w_gate, w_gate, w_branch)


def _outproj_kernel(m_ref, w_ref, x_ref, g_ref, xo_ref, hn_ref):
    xn = x_ref[...] + jnp.dot(m_ref[...], w_ref[...], preferred_element_type=F32)
    xo_ref[...] = xn
    hn_ref[...] = _rms(xn, g_ref[...]).astype(hn_ref.dtype)


def outproj_call(merged, w_out, x, g_next, tm):
    t = x.shape[0]
    return pl.pallas_call(
        _outproj_kernel,
        grid=(t // tm,),
        in_specs=[pl.BlockSpec((tm, D_MODEL), lambda i: (i, 0)),
                  pl.BlockSpec((D_MODEL, D_MODEL), lambda i: (0, 0)),
                  pl.BlockSpec((tm, D_MODEL), lambda i: (i, 0)),
                  pl.BlockSpec((1, D_MODEL), lambda i: (0, 0))],
        out_specs=[pl.BlockSpec((tm, D_MODEL), lambda i: (i, 0)),
                   pl.BlockSpec((tm, D_MODEL), lambda i: (i, 0))],
        out_shape=[jax.ShapeDtypeStruct((t, D_MODEL), F32),
                   jax.ShapeDtypeStruct((t, D_MODEL), BF16)],
        compiler_params=_cparams("parallel"),
        name="out_proj",
    )(merged, w_out, x, g_next.reshape(1, D_MODEL))


def _mlp_kernel(h_ref, wu_ref, wd_ref, x_ref, g_ref, xo_ref, hn_ref, acc_ref):
    j = pl.program_id(1)

    @pl.when(j == 0)
    def _():
        acc_ref[...] = jnp.zeros_like(acc_ref)

    a = jnp.dot(h_ref[...], wu_ref[...], preferred_element_type=F32)
    a = jnp.square(jnp.maximum(a, 0.0)).astype(BF16)
    acc_ref[...] += jnp.dot(a, wd_ref[...], preferred_element_type=F32)

    @pl.when(j == pl.num_programs(1) - 1)
    def _():
        xn = x_ref[...] + acc_ref[...]
        xo_ref[...] = xn
        hn_ref[...] = _rms(xn, g_ref[...]).astype(hn_ref.dtype)


def mlp_call(h, w_up, w_down, x, g_next, next_dtype, tm, tf):
    t = x.shape[0]
    return pl.pallas_call(
        _mlp_kernel,
        grid=(t // tm, D_FF // tf),
        in_specs=[pl.BlockSpec((tm, D_MODEL), lambda i, j: (i, 0)),
                  pl.BlockSpec((D_MODEL, tf), lambda i, j: (0, j)),
                  pl.BlockSpec((tf, D_MODEL), lambda i, j: (j, 0)),
                  pl.BlockSpec((tm, D_MODEL), lambda i, j: (i, 0)),
                  pl.BlockSpec((1, D_MODEL), lambda i, j: (0, 0))],
        out_specs=[pl.BlockSpec((tm, D_MODEL), lambda i, j: (i, 0)),
                   pl.BlockSpec((tm, D_MODEL), lambda i, j: (i, 0))],
        out_shape=[jax.ShapeDtypeStruct((t, D_MODEL), F32),
                   jax.ShapeDtypeStruct((t, D_MODEL), next_dtype)],
        scratch_shapes=[pltpu.VMEM((tm, D_MODEL), F32)],
        compiler_params=_cparams("parallel", "arbitrary"),
        name="mlp",
    )(h, w_up, w_down, x, g_next.reshape(1, D_MODEL))


def _small_kernel(z_ref, b_ref, a_ref, f_ref, carry_ref):
    @pl.when(pl.program_id(1) == 0)
    def _():
        carry_ref[...] = jnp.zeros_like(carry_ref)

    z = z_ref[...]
    tm = z.shape[0]
    lane = lax.broadcasted_iota(jnp.int32, z.shape, 1)
    pre = z + b_ref[...]
    lf = jnp.minimum(pre, 0.0) - jnp.log1p(jnp.exp(-jnp.abs(pre)))
    lf = jnp.where(lane < N_HEADS, lf, 0.0)
    a_ref[...] = jnp.where(lane < N_HEADS, lf, jnp.where(lane < 4 * N_HEADS, jax.nn.sigmoid(z), 0.0))
    row = lax.broadcasted_iota(jnp.int32, (tm, tm), 0)
    col = lax.broadcasted_iota(jnp.int32, (tm, tm), 1)
    tril = jnp.where(col <= row, 1.0, 0.0)
    f = jnp.dot(tril, lf, preferred_element_type=F32, precision=HIGHEST) + carry_ref[0:1, :]
    f_ref[...] = f
    carry_ref[0:1, :] = f[tm - 1:tm, :]


def small_call(z_small, b_forget, n, t, tm):
    bias = jnp.zeros((1, SMALL_W), F32).at[0, :N_HEADS].set(b_forget)
    nt = t // tm
    return pl.pallas_call(
        _small_kernel,
        grid=(n, nt),
        in_specs=[pl.BlockSpec((tm, SMALL_W), lambda b, j: (b * nt + j, 0)),
                  pl.BlockSpec((1, SMALL_W), lambda b, j: (0, 0))],
        out_specs=[pl.BlockSpec((tm, SMALL_W), lambda b, j: (b * nt + j, 0)),
                   pl.BlockSpec((tm, SMALL_W), lambda b, j: (b * nt + j, 0))],
        out_shape=[jax.ShapeDtypeStruct((n * t, SMALL_W), F32),
                   jax.ShapeDtypeStruct((n * t, SMALL_W), F32)],
        scratch_shapes=[pltpu.VMEM((8, SMALL_W), F32)],
        compiler_params=_cparams("parallel", "arbitrary"),
        name="forget_and_gates",
    )(z_small, bias)


def _conv_prompt_kernel(z_ref, w_ref, o_ref, st_ref, prev_ref):
    @pl.when(pl.program_id(1) == 0)
    def _():
        prev_ref[...] = jnp.zeros_like(prev_ref)

    bw = BRANCH_W
    u = z_ref[:, 2 * bw:3 * bw] * z_ref[:, 0:bw]
    tm = u.shape[0]
    row = lax.broadcasted_iota(jnp.int32, u.shape, 0)
    u1 = jnp.where(row == 0, prev_ref[7:8, :], pltpu.roll(u, 1, 0))
    u2 = jnp.where(row == 0, prev_ref[6:7, :], jnp.where(row == 1, prev_ref[7:8, :], pltpu.roll(u, 2, 0)))
    y = u2 * w_ref[0:1, :] + u1 * w_ref[1:2, :] + u * w_ref[2:3, :]
    o_ref[...] = z_ref[:, bw:2 * bw] * y
    prev_ref[...] = u[tm - 8:tm]
    st_ref[0] = u[tm - 2:tm]


def conv_prompt_call(z_conv, conv_w, n, t, tm):
    nt = t // tm
    return pl.pallas_call(
        _conv_prompt_kernel,
        grid=(n, nt),
        in_specs=[pl.BlockSpec((tm, 3 * BRANCH_W), lambda b, j: (b * nt + j, 0)),
                  pl.BlockSpec((CONV_W, BRANCH_W), lambda b, j: (0, 0))],
        out_specs=[pl.BlockSpec((tm, BRANCH_W), lambda b, j: (b * nt + j, 0)),
                   pl.BlockSpec((1, CONV_W - 1, BRANCH_W), lambda b, j: (b, 0, 0))],
        out_shape=[jax.ShapeDtypeStruct((n * t, BRANCH_W), F32),
                   jax.ShapeDtypeStruct((n, CONV_W - 1, BRANCH_W), F32)],
        scratch_shapes=[pltpu.VMEM((8, BRANCH_W), F32)],
        compiler_params=_cparams("parallel", "arbitrary"),
        name="conv_prompt",
    )(z_conv, conv_w)


def _online_step(s, vt, carry):
    m, l, acc = carry
    m_new = jnp.maximum(m, jnp.max(s, axis=0, keepdims=True))
    alpha = jnp.exp(m - m_new)
    p = jnp.exp(s - m_new)
    l = alpha * l + jnp.sum(p, axis=0, keepdims=True)
    acc = alpha * acc + jnp.dot(vt, p.astype(BF16), preferred_element_type=F32)
    return m_new, l, acc


def _softmax_init(width=TQ):
    return (jnp.full((1, width), NEG, F32), jnp.zeros((1, width), F32), jnp.zeros((HEAD_DIM, width), F32))


def _query_pair(q_ref, h, half):
    hp, e = divmod(h, 2)
    qp = q_ref[:, hp * PAIR_W:(hp + 1) * PAIR_W]
    if e != half:
        qp = pltpu.roll(qp, HEAD_DIM, 1)
    lane = lax.broadcasted_iota(jnp.int32, qp.shape, 1)
    return jnp.where((lane // HEAD_DIM) == half, qp, 0.0)


def _tile_masks():
    sub = lax.broadcasted_iota(jnp.int32, (TK, TQ), 0)
    lane = lax.broadcasted_iota(jnp.int32, (TK, TQ), 1)
    return sub <= lane, sub > lane


def _key_col():
    return lax.broadcasted_iota(jnp.int32, (TK, 1), 0).astype(F32)


def _fox_kernel(q_ref, k_ref, vt_ref, fk_ref, o_ref, ot_ref):
    i = pl.program_id(1)
    causal, _ = _tile_masks()
    qcats = [jnp.concatenate([(_query_pair(q_ref, 2 * hp + e, e) * SCALE).astype(BF16) for e in (0, 1)], axis=0)
             for hp in range(N_HEADS // 2)]

    def tile(j, carry, diag):
        r0 = pl.multiple_of(j * TK, TK)
        out = []
        for hp in range(N_HEADS // 2):
            s2 = _dot_nt(k_ref[pl.ds(r0, TK), hp * PAIR_W:(hp + 1) * PAIR_W], qcats[hp])
            for e in (0, 1):
                h = 2 * hp + e
                s = s2[:, e * TQ:(e + 1) * TQ] - fk_ref[pl.ds(r0, TK), h:h + 1]
                if diag:
                    s = jnp.where(causal, s, MASKED)
                out.append(_online_step(s, vt_ref[j, h * HEAD_DIM:(h + 1) * HEAD_DIM, :], carry[h]))
        return tuple(out)

    carry = lax.fori_loop(0, i, lambda j, c: tile(j, c, False), tuple(_softmax_init() for _ in range(N_HEADS)))
    for h, (_, l, acc) in enumerate(tile(i, carry, True)):
        ot_ref[h * HEAD_DIM:(h + 1) * HEAD_DIM, :] = acc * (1.0 / l)
    o_ref[...] = ot_ref[...].T


def fox_prompt_call(q, kb, vt, fk, n, t):
    nq = t // TQ
    return pl.pallas_call(
        _fox_kernel,
        grid=(n, nq),
        in_specs=[pl.BlockSpec((TQ, BRANCH_W), lambda b, i: (b * nq + i, 0)),
                  pl.BlockSpec((t, BRANCH_W), lambda b, i: (b, 0)),
                  pl.BlockSpec((t // TK, BRANCH_W, TK), lambda b, i: (b, 0, 0)),
                  pl.BlockSpec((t, SMALL_W), lambda b, i: (b, 0))],
        out_specs=pl.BlockSpec((TQ, BRANCH_W), lambda b, i: (b * nq + i, 0)),
        out_shape=jax.ShapeDtypeStruct((n * t, BRANCH_W), F32),
        scratch_shapes=[pltpu.VMEM((BRANCH_W, TQ), F32)],
        compiler_params=_cparams("parallel", "arbitrary"),
        name="fox_prompt",
    )(q, kb, vt, fk)


def _rank_before(score, bidx, nblk):
    cnt = jnp.zeros(score.shape, F32)
    for b2 in range(nblk):
        row = score[b2:b2 + 1, :]
        beats = (row > score) | ((row == score) & (b2 < bidx))
        cnt = cnt + jnp.where(beats, 1.0, 0.0)
    return cnt


def _moba_kernel(q_ref, k_ref, vt_ref, km_ref, o_ref, ot_ref, sel_ref, *, nblk):
    i = pl.program_id(1)
    causal, _ = _tile_masks()
    kcol = _key_col()
    bidx = lax.broadcasted_iota(jnp.int32, (nblk, TQ), 0)
    qcats = []
    for hp in range(N_HEADS // 2):
        qpads = []
        for e in (0, 1):
            h = 2 * hp + e
            q32 = _query_pair(q_ref, h, e)
            qpads.append((q32 * SCALE).astype(BF16))
            gs = _dot_nt(km_ref[0, :, hp * PAIR_W:(hp + 1) * PAIR_W], q32, precision=HIGHEST)
            gs = jnp.where(bidx < i, gs, NEG)
            sel = (_rank_before(gs, bidx, nblk) < MOBA_TOPK) & (bidx < i)
            sel_ref[:, h * TQ:(h + 1) * TQ] = jnp.where(sel, 1.0, 0.0)
        qcats.append(jnp.concatenate(qpads, axis=0))

    def tile(j, carry, diag):
        r0 = pl.multiple_of(j * TK, TK)
        dist = kcol - ((i - j) * TK).astype(F32)
        out = []
        for hp in range(N_HEADS // 2):
            s2 = _dot_nt(k_ref[pl.ds(r0, TK), hp * PAIR_W:(hp + 1) * PAIR_W], qcats[hp])
            for e in (0, 1):
                h = 2 * hp + e
                s = s2[:, e * TQ:(e + 1) * TQ] + 2.0 ** (-8.0 * (h + 1) / N_HEADS) * dist
                if diag:
                    s = jnp.where(causal, s, MASKED)
                else:
                    s = jnp.where(sel_ref[pl.ds(j, 1), h * TQ:(h + 1) * TQ] > 0.5, s, MASKED)
                out.append(_online_step(s, vt_ref[j, h * HEAD_DIM:(h + 1) * HEAD_DIM, :], carry[h]))
        return tuple(out)

    carry = lax.fori_loop(0, i, lambda j, c: tile(j, c, False), tuple(_softmax_init() for _ in range(N_HEADS)))
    for h, (_, l, acc) in enumerate(tile(i, carry, True)):
        ot_ref[h * HEAD_DIM:(h + 1) * HEAD_DIM, :] = acc * (1.0 / l)
    o_ref[...] = ot_ref[...].T


def moba_prompt_call(q, kb, vt, kmean, n, t):
    nq = t // TQ
    nblk = t // MOBA_BLOCK
    return pl.pallas_call(
        functools.partial(_moba_kernel, nblk=nblk),
        grid=(n, nq),
        in_specs=[pl.BlockSpec((TQ, BRANCH_W), lambda b, i: (b * nq + i, 0)),
                  pl.BlockSpec((t, BRANCH_W), lambda b, i: (b, 0)),
                  pl.BlockSpec((t // TK, BRANCH_W, TK), lambda b, i: (b, 0, 0)),
                  pl.BlockSpec((1, nblk, BRANCH_W), lambda b, i: (b, 0, 0))],
        out_specs=pl.BlockSpec((TQ, BRANCH_W), lambda b, i: (b * nq + i, 0)),
        out_shape=jax.ShapeDtypeStruct((n * t, BRANCH_W), F32),
        scratch_shapes=[pltpu.VMEM((BRANCH_W, TQ), F32), pltpu.VMEM((nblk, N_HEADS * TQ), F32)],
        compiler_params=_cparams("parallel", "arbitrary"),
        name="moba_prompt",
    )(q, kb, vt, kmean.reshape(n, nblk, BRANCH_W))


def _gelu_tanh(x):
    return 0.5 * x * (1.0 + jnp.tanh(0.7978845608028654 * (x + 0.044715 * x * x * x)))


def _compress_kernel(pt_ref, a_ref, pet_ref, peb_ref, wt_ref, wb_ref, w2k_ref, w2v_ref, w2vt_ref,
                     kc_ref, vc_ref, vct_ref, rows_ref, *, rows):
    del pt_ref
    j = pl.program_id(1)
    r0 = pl.multiple_of(j * PAGE_ROWS, PAGE_ROWS)
    for i in range(CMP_S):
        rows_ref[pl.ds(r0, PAGE_ROWS), i * 2 * KV_W:(i + 1) * 2 * KV_W] = a_ref[0, :, i * 4 * KV_W:i * 4 * KV_W + 2 * KV_W]

    @pl.when(j == pl.num_programs(1) - 1)
    def _():
        chunk = min(256, rows)
        prev_top = jnp.zeros((1, 4 * CMP_HID), F32)
        for c in range(rows // chunk):
            a = rows_ref[c * chunk:(c + 1) * chunk, :]
            ht = jnp.dot((a + pet_ref[...]).astype(BF16), wt_ref[...], preferred_element_type=F32)
            hb = jnp.dot((a + peb_ref[...]).astype(BF16), wb_ref[...], preferred_element_type=F32)
            row = lax.broadcasted_iota(jnp.int32, ht.shape, 0)
            shifted = jnp.where(row == 0, prev_top, pltpu.roll(ht, 1, 0))
            prev_top = ht[chunk - 1:chunk, :]
            g = _gelu_tanh(shifted + hb).astype(BF16)
            kc_ref[0, c * chunk:(c + 1) * chunk, :] = jnp.dot(
                g[:, :2 * CMP_HID], w2k_ref[...], preferred_element_type=F32).astype(BF16)
            vc_ref[0, c * chunk:(c + 1) * chunk, :] = jnp.dot(
                g[:, 2 * CMP_HID:], w2v_ref[...], preferred_element_type=F32).astype(BF16)
            vct_ref[0, :, c * chunk:(c + 1) * chunk] = _dot_nt(w2vt_ref[...], g[:, 2 * CMP_HID:]).astype(BF16)


def compress_call(pages, page_idx, pe_top, pe_bot, wt, wb, w2k, w2v, w2vt, n, npages):
    rows = npages * PAGE_ROWS
    const = lambda b, j, pt: (0, 0)
    grid_spec = pltpu.PrefetchScalarGridSpec(
        num_scalar_prefetch=1,
        grid=(n, npages),
        in_specs=[pl.BlockSpec((1, PAGE_ROWS, CMP_S * 4 * KV_W), lambda b, j, pt: (pt[b, j], 0, 0)),
                  pl.BlockSpec((1, CMP_ROW), const),
                  pl.BlockSpec((1, CMP_ROW), const),
                  pl.BlockSpec((CMP_ROW, 4 * CMP_HID), const, pipeline_mode=pl.Buffered(1)),
                  pl.BlockSpec((CMP_ROW, 4 * CMP_HID), const, pipeline_mode=pl.Buffered(1)),
                  pl.BlockSpec((2 * CMP_HID, KV_W), const),
                  pl.BlockSpec((2 * CMP_HID, KV_W), const),
                  pl.BlockSpec((KV_W, 2 * CMP_HID), const)],
        out_specs=[pl.BlockSpec((1, rows, KV_W), lambda b, j, pt: (b, 0, 0)),
                   pl.BlockSpec((1, rows, KV_W), lambda b, j, pt: (b, 0, 0)),
                   pl.BlockSpec((1, KV_W, rows), lambda b, j, pt: (b, 0, 0))],
        scratch_shapes=[pltpu.VMEM((rows, CMP_ROW), F32)],
    )
    return pl.pallas_call(
        functools.partial(_compress_kernel, rows=rows),
        grid_spec=grid_spec,
        out_shape=[jax.ShapeDtypeStruct((n, rows, KV_W), BF16),
                   jax.ShapeDtypeStruct((n, rows, KV_W), BF16),
                   jax.ShapeDtypeStruct((n, KV_W, rows), BF16)],
        compiler_params=_cparams("parallel", "arbitrary"),
        name="nsa_compress",
    )(page_idx, pages, pe_top, pe_bot, wt, wb, w2k, w2v, w2vt)


def _nsa_kernel(q_ref, kc_ref, vct_ref, ks_ref, vst_ref, kw_ref, vwt_ref, gt_ref, o_ref,
                ot_ref, sel_ref, *, ncmp, nsb):
    i = pl.program_id(1)
    causal, below = _tile_masks()
    kcol = _key_col()
    lane_q = lax.broadcasted_iota(jnp.int32, (1, TQ), 1)
    qpos = i * TQ + lane_q
    crow = lax.broadcasted_iota(jnp.int32, (ncmp, 1), 0)
    cpos = CMP_S * crow + (CMP_S - 1)
    cvalid = (crow >= 1) & (cpos <= qpos)
    cposf = cpos.astype(F32)
    mj = lax.broadcasted_iota(jnp.int32, (nsb, ncmp), 0)
    mc = lax.broadcasted_iota(jnp.int32, (nsb, ncmp), 1)
    ratio = SLC_BLOCK // CMP_S
    gather_m = jnp.where((mc >= ratio * mj) & (mc <= ratio * mj + ratio) & (mc >= 1), 1.0, 0.0)
    bj = lax.broadcasted_iota(jnp.int32, (nsb, TQ), 0)
    own = qpos // SLC_BLOCK
    forced = (bj == 0) | (bj >= own - 1)
    allowed = bj <= own
    ecol = lax.broadcasted_iota(jnp.int32, (TK, nsb), 1)
    erow = lax.broadcasted_iota(jnp.int32, (TK, nsb), 0) // SLC_BLOCK
    gw = NSA_R * TQ
    sub4 = lax.broadcasted_iota(jnp.int32, (TK, gw), 0)
    lane4 = lax.broadcasted_iota(jnp.int32, (TK, gw), 1) % TQ
    causal4, below4 = sub4 <= lane4, sub4 > lane4
    cvalid4 = (crow >= 1) & (cpos <= i * TQ + lax.broadcasted_iota(jnp.int32, (1, gw), 1) % TQ)

    for g in range(NSA_G):
        heads = [g * NSA_R + r for r in range(NSA_R)]
        q4 = jnp.concatenate([(_query_pair(q_ref, h, g) * SCALE).astype(BF16) for h in heads], axis=0)
        slope_row = jnp.concatenate([jnp.full((1, TQ), 2.0 ** (-8.0 * (h + 1) / N_HEADS), F32) for h in heads], axis=1)
        gates = [jnp.concatenate([gt_ref[3 * h + c:3 * h + c + 1, :] for h in heads], axis=1) for c in range(3)]
        v_rows = slice(g * HEAD_DIM, (g + 1) * HEAD_DIM)
        s = _dot_nt(kc_ref[0], q4) + slope_row * (cposf - (i * TQ).astype(F32))
        s = jnp.where(cvalid4, s, MASKED)
        p = jnp.where(cvalid4, jnp.exp(s - jnp.max(s, axis=0, keepdims=True)), 0.0)
        l = jnp.sum(p, axis=0, keepdims=True)
        p = p * (1.0 / jnp.where(l > 0.0, l, 1.0))
        imp = p[:, 0:TQ]
        for r in range(1, NSA_R):
            imp = imp + p[:, r * TQ:(r + 1) * TQ]
        o_all = gates[0] * jnp.dot(vct_ref[0, v_rows, :], p.astype(BF16), preferred_element_type=F32)
        p_slc = jnp.dot(gather_m, imp, preferred_element_type=F32, precision=HIGHEST)
        score = jnp.where(allowed, jnp.where(forced, p_slc + FORCE_SCORE, p_slc), -1.0)
        sel = jnp.where((_rank_before(score, bj, nsb) < SLC_N) & allowed, 1.0, 0.0)
        sel_ref[...] = jnp.concatenate([sel] * NSA_R, axis=1).astype(BF16)

        def alibi(j):
            return slope_row * (kcol - ((i - j) * TK).astype(F32))

        def slc_tile(j, carry, diag):
            r0 = pl.multiple_of(j * TK, TK)
            expand = jnp.where(ecol == erow + j * (TK // SLC_BLOCK), 1.0, 0.0).astype(BF16)
            keep = jnp.dot(expand, sel_ref[...], preferred_element_type=F32) > 0.5
            if diag:
                keep = keep & causal4
            s = jnp.where(keep, _dot_nt(ks_ref[pl.ds(r0, TK), :], q4) + alibi(j), MASKED)
            return _online_step(s, vst_ref[j, v_rows, :], carry)

        carry = lax.fori_loop(0, i, lambda j, c: slc_tile(j, c, False), _softmax_init(gw))
        _, l, acc = slc_tile(i, carry, True)
        o_all = o_all + gates[1] * (acc * (1.0 / l))

        def win_tile(j, carry, mask):
            r0 = pl.multiple_of(j * TK, TK)
            s = _dot_nt(kw_ref[pl.ds(r0, TK), :], q4) + alibi(j)
            if mask is not None:
                s = jnp.where(mask, s, MASKED)
            return _online_step(s, vwt_ref[j, v_rows, :], carry)

        carry = _softmax_init(gw)
        carry = lax.cond(i >= 2, lambda c: win_tile(i - 2, c, below4), lambda c: c, carry)
        carry = lax.cond(i >= 1, lambda c: win_tile(i - 1, c, None), lambda c: c, carry)
        _, l, acc = win_tile(i, carry, causal4)
        o_all = o_all + gates[2] * (acc * (1.0 / l))
        for r, h in enumerate(heads):
            ot_ref[h * HEAD_DIM:(h + 1) * HEAD_DIM, :] = o_all[:, r * TQ:(r + 1) * TQ]
    o_ref[...] = ot_ref[...].T


def nsa_prompt_call(q, kcmp, vcmpt, ks, vst, kw, vwt, gate_t, n, t):
    nq = t // TQ
    ncmp = kcmp.shape[1]
    nsb = t // SLC_BLOCK
    return pl.pallas_call(
        functools.partial(_nsa_kernel, ncmp=ncmp, nsb=nsb),
        grid=(n, nq),
        in_specs=[pl.BlockSpec((TQ, BRANCH_W), lambda b, i: (b * nq + i, 0)),
                  pl.BlockSpec((1, ncmp, KV_W), lambda b, i: (b, 0, 0)),
                  pl.BlockSpec((1, KV_W, ncmp), lambda b, i: (b, 0, 0)),
                  pl.BlockSpec((t, KV_W), lambda b, i: (b, 0)),
                  pl.BlockSpec((t // TK, KV_W, TK), lambda b, i: (b, 0, 0)),
                  pl.BlockSpec((t, KV_W), lambda b, i: (b, 0)),
                  pl.BlockSpec((t // TK, KV_W, TK), lambda b, i: (b, 0, 0)),
                  pl.BlockSpec((3 * N_HEADS, TQ), lambda b, i: (0, b * nq + i))],
        out_specs=pl.BlockSpec((TQ, BRANCH_W), lambda b, i: (b * nq + i, 0)),
        out_shape=jax.ShapeDtypeStruct((n * t, BRANCH_W), F32),
        scratch_shapes=[pltpu.VMEM((BRANCH_W, TQ), F32), pltpu.VMEM((nsb, NSA_R * TQ), BF16)],
        compiler_params=_cparams("parallel", "arbitrary"),
        name="nsa_prompt",
    )(q, kcmp, vcmpt, ks, vst, kw, vwt, gate_t)


def alibi_slopes(n):
    return jnp.exp2(-8.0 * jnp.arange(1, n + 1, dtype=jnp.float32) / n)


def masked_softmax(s, mask, axis=-1):
    p = jax.nn.softmax(jnp.where(mask, s, NEG), axis=axis)
    return p * mask


def sweep_queries(fn, block, *qs):
    n, t = qs[0].shape[:2]
    qb = block if t % block == 0 else t
    nb = t // qb
    xs = tuple(jnp.moveaxis(a.reshape((n, nb, qb) + a.shape[2:]), 1, 0) for a in qs)
    out = lax.map(lambda args: fn(args[0] * qb, *args[1:]), (jnp.arange(nb, dtype=jnp.int32),) + xs)
    return jnp.moveaxis(out, 0, 1).reshape((n, t) + out.shape[3:])


def gather_pages(pool, page_table):
    g = pool[page_table]
    return g.reshape((g.shape[0], g.shape[1] * g.shape[2]) + g.shape[3:])


def fox_attention(q, k, v, logf, q0):
    d = q.shape[-1]
    L = k.shape[1]
    scale = d ** -0.5
    F = jnp.cumsum(logf.astype(jnp.float32), axis=1)
    Fk = jnp.moveaxis(F, 1, 2)
    Fq = F[:, q0:]
    kpos = jnp.arange(L)

    def block(start, qb, fq):
        qpos = q0 + start + jnp.arange(qb.shape[1])
        s = jnp.einsum('nqhd,nkhd->nhqk', qb, k).astype(jnp.float32) * scale
        s = s + jnp.moveaxis(fq, 1, 2)[..., None] - Fk[:, :, None, :]
        p = masked_softmax(s, kpos[None, :] <= qpos[:, None])
        return jnp.einsum('nhqk,nkhd->nqhd', p.astype(q.dtype), v)

    return sweep_queries(block, Q_BLOCK, q, Fq)


def moba_attention(q, k, v, q0, slopes):
    n, tq, nh, d = q.shape
    L = k.shape[1]
    scale = d ** -0.5
    nbk = -(-L // MOBA_BLOCK)
    pad = nbk * MOBA_BLOCK - L

    def to_blocks(a):
        a = jnp.pad(a, ((0, 0), (0, pad), (0, 0), (0, 0)))
        return a.reshape(n, nbk, MOBA_BLOCK, nh, d).transpose(0, 3, 1, 2, 4)

    kb, vb = to_blocks(k), to_blocks(v)
    kmean = jnp.mean(kb.astype(jnp.float32), axis=3)
    topk = min(MOBA_TOPK, nbk)
    bidx = jnp.arange(nbk)
    ni = jnp.arange(n)[:, None, None, None]
    hi = jnp.arange(nh)[None, :, None, None]
    sl = slopes[None, :, None, None, None]

    def block(start, qb):
        m = qb.shape[1]
        qpos = q0 + start + jnp.arange(m)
        own = qpos // MOBA_BLOCK
        gs = jnp.einsum('nqhd,nhbd->nhqb', qb.astype(jnp.float32), kmean)
        gs = jnp.where(bidx[None, :] < own[:, None], gs, NEG)
        _, top = lax.top_k(gs, topk)
        sel_ok = top < own[:, None]
        idx = jnp.concatenate([top, jnp.broadcast_to(own[:, None], (n, nh, m, 1))], axis=-1)
        ok = jnp.concatenate([sel_ok, jnp.ones((n, nh, m, 1), bool)], axis=-1)
        kg = kb[ni, hi, idx]
        vg = vb[ni, hi, idx]
        kpos = idx[..., None] * MOBA_BLOCK + jnp.arange(MOBA_BLOCK)
        s = jnp.einsum('nqhd,nhqsjd->nhqsj', qb, kg).astype(jnp.float32) * scale
        s = s - sl * (qpos[:, None, None] - kpos).astype(jnp.float32)
        mask = ok[..., None] & (kpos <= qpos[:, None, None])
        p = masked_softmax(s, mask, axis=(-2, -1))
        return jnp.einsum('nhqsj,nhqsjd->nqhd', p.astype(q.dtype), vg)

    return sweep_queries(block, SPARSE_Q_BLOCK, q)


def compress_tokens(a, w1, w2, pe):
    n, L, g, d = a.shape
    nc = (L - CMP_L) // CMP_S + 1
    idx = jnp.arange(nc)[:, None] * CMP_S + jnp.arange(CMP_L)[None, :]
    blocks = a[:, idx] + pe[:, None, :].astype(a.dtype)
    flat = jnp.swapaxes(blocks, 2, 3).reshape(n, nc, g, CMP_L * d)
    return jax.nn.gelu(flat @ w1) @ w2


def nsa_attention(q, kc, vc, ks, vs, kw, vw, gate, q0, w_ofs, slopes, cmp_w1, cmp_w2, cmp_pe):
    n, tq, nh, d = q.shape
    L = kc.shape[1]
    dt = q.dtype
    scale = d ** -0.5
    qg = q.reshape(n, tq, NSA_G, NSA_R, d)
    k_cmp = compress_tokens(kc, cmp_w1[0], cmp_w2[0], cmp_pe[0])
    v_cmp = compress_tokens(vc, cmp_w1[1], cmp_w2[1], cmp_pe[1])
    nc = k_cmp.shape[1]
    cpos = jnp.arange(nc) * CMP_S + (CMP_L - 1)
    nsb = -(-L // SLC_BLOCK)
    padl = nsb * SLC_BLOCK - L

    def to_blocks(a):
        a = jnp.pad(a, ((0, 0), (0, padl), (0, 0), (0, 0)))
        return a.reshape(n, nsb, SLC_BLOCK, NSA_G, d).transpose(0, 3, 1, 2, 4)

    ks_b, vs_b = to_blocks(ks), to_blocks(vs)
    nsel = min(SLC_N, nsb)
    front = CMP_L // CMP_S - 1
    ratio = SLC_BLOCK // CMP_S
    width = ratio + front
    back = ratio * nsb + width - front - nc
    kw_p = jnp.pad(kw, ((0, 0), (WINDOW, 0), (0, 0), (0, 0)))
    vw_p = jnp.pad(vw, ((0, 0), (WINDOW, 0), (0, 0), (0, 0)))
    sl = slopes.reshape(NSA_G, NSA_R)[None, :, :, None, None]
    ni = jnp.arange(n)[:, None, None, None]
    gi = jnp.arange(NSA_G)[None, :, None, None]
    bj = jnp.arange(nsb)

    def block(start, qb, gb):
        m = qb.shape[1]
        qpos = q0 + start + jnp.arange(m)
        s = jnp.einsum('nqgrd,ncgd->ngrqc', qb, k_cmp).astype(jnp.float32) * scale
        s = s - sl * (qpos[:, None] - cpos[None, :]).astype(jnp.float32)
        p_cmp = masked_softmax(s, cpos[None, :] <= qpos[:, None])
        o_cmp = jnp.einsum('ngrqc,ncgd->nqgrd', p_cmp.astype(dt), v_cmp)
        imp = jnp.pad(p_cmp.sum(axis=2), ((0, 0), (0, 0), (0, 0), (front, back)))
        p_slc = imp[..., 0:ratio * nsb:ratio]
        for u in range(1, width):
            p_slc = p_slc + imp[..., u:u + ratio * nsb:ratio]
        own = qpos // SLC_BLOCK
        forced = (bj[None, :] == 0) | (bj[None, :] >= own[:, None] - 1)
        allowed = bj[None, :] <= own[:, None]
        score = jnp.where(allowed, jnp.where(forced, p_slc + FORCE_SCORE, p_slc), -1.0)
        _, top = lax.top_k(score, nsel)
        ok = top <= own[:, None]
        kg = ks_b[ni, gi, top]
        vg = vs_b[ni, gi, top]
        kpos = top[..., None] * SLC_BLOCK + jnp.arange(SLC_BLOCK)
        dist = (qpos[:, None, None] - kpos)[:, :, None].astype(jnp.float32)
        s2 = jnp.einsum('nqgrd,ngqsjd->ngrqsj', qb, kg).astype(jnp.float32) * scale - sl[..., None] * dist
        mask2 = (ok[..., None] & (kpos <= qpos[:, None, None]))[:, :, None]
        p2 = masked_softmax(s2, mask2, axis=(-2, -1))
        o_slc = jnp.einsum('ngrqsj,ngqsjd->nqgrd', p2.astype(dt), vg)
        off = q0 + start - w_ofs
        kwin = lax.dynamic_slice_in_dim(kw_p, off, WINDOW + m, axis=1)
        vwin = lax.dynamic_slice_in_dim(vw_p, off, WINDOW + m, axis=1)
        wpos = q0 + start - WINDOW + jnp.arange(WINDOW + m)
        s3 = jnp.einsum('nqgrd,nkgd->ngrqk', qb, kwin).astype(jnp.float32) * scale
        s3 = s3 - sl * (qpos[:, None] - wpos[None, :]).astype(jnp.float32)
        wmask = (wpos[None, :] <= qpos[:, None]) & (wpos[None, :] > qpos[:, None] - WINDOW) & (wpos[None, :] >= 0)
        p3 = masked_softmax(s3, wmask)
        o_win = jnp.einsum('ngrqk,nkgd->nqgrd', p3.astype(dt), vwin)
        return gb[..., 0:1] * o_cmp + gb[..., 1:2] * o_slc + gb[..., 2:3] * o_win

    return sweep_queries(block, SPARSE_Q_BLOCK, qg, gate)


PAGE = 128


def _head_diag_mask(width):
    sub = lax.broadcasted_iota(jnp.int32, (N_HEADS, width), 0)
    lane = lax.broadcasted_iota(jnp.int32, (N_HEADS, width), 1)
    return sub == lane // HEAD_DIM


def _block_diag_queries(q_ref, nt):
    diag = _head_diag_mask(BRANCH_W)
    return jnp.concatenate([jnp.where(diag, q_ref[0, t:t + 1, :], 0.0) for t in range(nt)], axis=0)


def _extract_heads(o, nt):
    diag = _head_diag_mask(BRANCH_W)
    return jnp.concatenate(
        [jnp.sum(jnp.where(diag, o[t * N_HEADS:(t + 1) * N_HEADS], 0.0), axis=0, keepdims=True) for t in range(nt)],
        axis=0)


def _slope_col(rows):
    h = lax.broadcasted_iota(jnp.int32, (rows, 1), 0) % N_HEADS
    col = jnp.zeros((rows, 1), F32)
    for k in range(N_HEADS):
        col = jnp.where(h == k, 2.0 ** (-8.0 * (k + 1) / N_HEADS), col)
    return col


def _row_softmax_step(s, v, m_ref, l_ref, acc_ref):
    m_old = m_ref[...]
    m_new = jnp.maximum(m_old, jnp.max(s, axis=1, keepdims=True))
    alpha = jnp.exp(m_old - m_new)
    p = jnp.exp(s - m_new)
    l_ref[...] = alpha * l_ref[...] + jnp.sum(p, axis=1, keepdims=True)
    acc_ref[...] = alpha * acc_ref[...] + jnp.dot(p.astype(BF16), v, preferred_element_type=F32)
    m_ref[...] = m_new


def _softmax_reset(m_ref, l_ref, acc_ref):
    m_ref[...] = jnp.full(m_ref.shape, NEG, F32)
    l_ref[...] = jnp.zeros(l_ref.shape, F32)
    acc_ref[...] = jnp.zeros(acc_ref.shape, F32)


def _new_token_mask(rows, nt):
    k = lax.broadcasted_iota(jnp.int32, (rows, PAGE), 1)
    t = lax.broadcasted_iota(jnp.int32, (rows, PAGE), 0) // N_HEADS
    return (k <= t) & (k < nt)


def _dfox_kernel(pt_ref, q_ref, pg_ref, lf_ref, npg_ref, nlf_ref, o_ref, qbd_ref, m_ref, l_ref, acc_ref, cf_ref, *, nt):
    del pt_ref
    j = pl.program_id(1)
    rows = nt * N_HEADS

    @pl.when(j == 0)
    def _():
        qbd_ref[...] = (_block_diag_queries(q_ref, nt) * SCALE).astype(BF16)
        _softmax_reset(m_ref, l_ref, acc_ref)
        cf_ref[...] = jnp.zeros_like(cf_ref)

    def step(kv_ref, lf, mask):
        k = kv_ref[0, :, 0:BRANCH_W].astype(BF16)
        v = kv_ref[0, :, BRANCH_W:2 * BRANCH_W].astype(BF16)
        rr = lax.broadcasted_iota(jnp.int32, (rows, N_HEADS), 0) % N_HEADS
        rc = lax.broadcasted_iota(jnp.int32, (rows, N_HEADS), 1)
        lfe = _dot_nt(jnp.where(rr == rc, 1.0, 0.0), lf, precision=HIGHEST)
        a = lax.broadcasted_iota(jnp.int32, (PAGE, PAGE), 0)
        b = lax.broadcasted_iota(jnp.int32, (PAGE, PAGE), 1)
        fk = jnp.dot(lfe, jnp.where(a <= b, 1.0, 0.0), preferred_element_type=F32, precision=HIGHEST) + cf_ref[...]
        cf_ref[...] = fk[:, PAGE - 1:PAGE]
        s = _dot_nt(qbd_ref[...], k) - fk
        if mask is not None:
            s = jnp.where(mask, s, MASKED)
        _row_softmax_step(s, v, m_ref, l_ref, acc_ref)

    step(pg_ref, lf_ref[0], None)

    @pl.when(j == pl.num_programs(1) - 1)
    def _():
        step(npg_ref, nlf_ref[0], _new_token_mask(rows, nt))
        o_ref[0] = _extract_heads(acc_ref[...] * (1.0 / l_ref[...]), nt)


def fox_decode_call(pt, q, cache_kv, cache_lf, new_kv, new_lf):
    n, nt, _ = q.shape
    npages = pt.shape[1]
    rows = nt * N_HEADS
    grid_spec = pltpu.PrefetchScalarGridSpec(
        num_scalar_prefetch=1,
        grid=(n, npages),
        in_specs=[pl.BlockSpec((1, nt, BRANCH_W), lambda b, j, pt: (b, 0, 0)),
                  pl.BlockSpec((1, PAGE, 2 * BRANCH_W), lambda b, j, pt: (pt[b, j], 0, 0)),
                  pl.BlockSpec((1, PAGE, N_HEADS), lambda b, j, pt: (pt[b, j], 0, 0)),
                  pl.BlockSpec((1, PAGE, 2 * BRANCH_W), lambda b, j, pt: (b, 0, 0)),
                  pl.BlockSpec((1, PAGE, N_HEADS), lambda b, j, pt: (b, 0, 0))],
        out_specs=pl.BlockSpec((1, nt, BRANCH_W), lambda b, j, pt: (b, 0, 0)),
        scratch_shapes=[pltpu.VMEM((rows, BRANCH_W), BF16), pltpu.VMEM((rows, 1), F32), pltpu.VMEM((rows, 1), F32),
                        pltpu.VMEM((rows, BRANCH_W), F32), pltpu.VMEM((rows, 1), F32)],
    )
    return pl.pallas_call(
        functools.partial(_dfox_kernel, nt=nt),
        grid_spec=grid_spec,
        out_shape=jax.ShapeDtypeStruct((n, nt, BRANCH_W), F32),
        compiler_params=_cparams("parallel", "arbitrary"),
        name="fox_decode",
    )(pt, q, cache_kv, cache_lf, new_kv, new_lf)


def _dmoba_sel_kernel(pt_ref, q_ref, kp_ref, sel_ref, q32_ref, g_ref, *, nt):
    del pt_ref
    j = pl.program_id(1)
    rows = nt * N_HEADS
    pages_per_block = MOBA_BLOCK // PAGE

    @pl.when(j == 0)
    def _():
        q32_ref[...] = _block_diag_queries(q_ref, nt)
        g_ref[...] = jnp.zeros_like(g_ref)

    ksum = jnp.sum(kp_ref[0], axis=0, keepdims=True)
    prow = lax.broadcasted_iota(jnp.int32, (g_ref.shape[1], BRANCH_W), 0)
    g_ref[...] += _dot_nt(q32_ref[...], jnp.where(prow == j, ksum, 0.0), precision=HIGHEST)

    @pl.when(j == pl.num_programs(1) - 1)
    def _():
        npg = g_ref.shape[1]
        g = g_ref[...]
        lane = lax.broadcasted_iota(jnp.int32, (rows, npg), 1)
        blk = g
        for u in range(1, pages_per_block):
            blk = blk + pltpu.roll(g, npg - u, 1)
        gs = jnp.where(lane % pages_per_block == 0, blk * (1.0 / MOBA_BLOCK), NEG)
        sel = jnp.zeros((rows, npg), F32)
        for _ in range(MOBA_TOPK):
            mx = jnp.max(gs, axis=1, keepdims=True)
            idx = jnp.min(jnp.where(gs == mx, lane, npg), axis=1, keepdims=True)
            hit = lane == idx
            sel = jnp.where(hit, 1.0, sel)
            gs = jnp.where(hit, MASKED, gs)
        out = sel
        for u in range(1, pages_per_block):
            out = out + pltpu.roll(sel, u, 1)
        sel_ref[0] = out


def moba_select_call(pt, q, cache_kv):
    n, nt, _ = q.shape
    npages = pt.shape[1]
    rows = nt * N_HEADS
    grid_spec = pltpu.PrefetchScalarGridSpec(
        num_scalar_prefetch=1,
        grid=(n, npages),
        in_specs=[pl.BlockSpec((1, nt, BRANCH_W), lambda b, j, pt: (b, 0, 0)),
                  pl.BlockSpec((1, PAGE, BRANCH_W), lambda b, j, pt: (pt[b, j], 0, 0))],
        out_specs=pl.BlockSpec((1, rows, npages), lambda b, j, pt: (b, 0, 0)),
        scratch_shapes=[pltpu.VMEM((rows, BRANCH_W), F32), pltpu.VMEM((rows, npages), F32)],
    )
    return pl.pallas_call(
        functools.partial(_dmoba_sel_kernel, nt=nt),
        grid_spec=grid_spec,
        out_shape=jax.ShapeDtypeStruct((n, rows, npages), F32),
        compiler_params=_cparams("parallel", "arbitrary"),
        name="moba_select",
    )(pt, q, cache_kv)


def _dmoba_kernel(pt_ref, q_ref, sel_ref, pg_ref, npg_ref, o_ref, qbd_ref, m_ref, l_ref, acc_ref, *, nt):
    del pt_ref
    j = pl.program_id(1)
    npages = pl.num_programs(1)
    rows = nt * N_HEADS
    slope = _slope_col(rows)
    klane = lax.broadcasted_iota(jnp.int32, (1, PAGE), 1)

    @pl.when(j == 0)
    def _():
        qbd_ref[...] = (_block_diag_queries(q_ref, nt) * SCALE).astype(BF16)
        _softmax_reset(m_ref, l_ref, acc_ref)

    def step(kv_ref, rel, mask):
        k = kv_ref[0, :, 0:BRANCH_W].astype(BF16)
        v = kv_ref[0, :, BRANCH_W:2 * BRANCH_W].astype(BF16)
        s = _dot_nt(qbd_ref[...], k) + slope * rel
        _row_softmax_step(jnp.where(mask, s, MASKED), v, m_ref, l_ref, acc_ref)

    prow = lax.broadcasted_iota(jnp.int32, (sel_ref.shape[2], PAGE), 0)
    keep = jnp.dot(sel_ref[0].astype(BF16), jnp.where(prow == j, 1.0, 0.0).astype(BF16),
                   preferred_element_type=F32) > 0.5
    step(pg_ref, (klane + (j - npages) * PAGE).astype(F32), keep)

    @pl.when(j == npages - 1)
    def _():
        step(npg_ref, klane.astype(F32), _new_token_mask(rows, nt))
        o_ref[0] = _extract_heads(acc_ref[...] * (1.0 / l_ref[...]), nt)


def moba_decode_call(pt, q, sel, cache_kv, new_kv):
    n, nt, _ = q.shape
    npages = pt.shape[1]
    rows = nt * N_HEADS
    grid_spec = pltpu.PrefetchScalarGridSpec(
        num_scalar_prefetch=1,
        grid=(n, npages),
        in_specs=[pl.BlockSpec((1, nt, BRANCH_W), lambda b, j, pt: (b, 0, 0)),
                  pl.BlockSpec((1, rows, npages), lambda b, j, pt: (b, 0, 0)),
                  pl.BlockSpec((1, PAGE, 2 * BRANCH_W), lambda b, j, pt: (pt[b, j], 0, 0)),
                  pl.BlockSpec((1, PAGE, 2 * BRANCH_W), lambda b, j, pt: (b, 0, 0))],
        out_specs=pl.BlockSpec((1, nt, BRANCH_W), lambda b, j, pt: (b, 0, 0)),
        scratch_shapes=[pltpu.VMEM((rows, BRANCH_W), BF16), pltpu.VMEM((rows, 1), F32), pltpu.VMEM((rows, 1), F32),
                        pltpu.VMEM((rows, BRANCH_W), F32)],
    )
    return pl.pallas_call(
        functools.partial(_dmoba_kernel, nt=nt),
        grid_spec=grid_spec,
        out_shape=jax.ShapeDtypeStruct((n, nt, BRANCH_W), F32),
        compiler_params=_cparams("parallel", "arbitrary"),
        name="moba_decode",
    )(pt, q, sel, cache_kv, new_kv)


def _dnsa_kernel(pt_ref, q_ref, gate_ref, kc_ref, vc_ref, pg_ref, npg_ref, win_ref, nwin_ref, o_ref,
                 qbd_ref, sel_ref, ocmp_ref, m_ref, l_ref, acc_ref, *, nt, nsb_pad):
    del pt_ref
    j = pl.program_id(1)
    npages = pl.num_programs(1)
    rows = nt * N_HEADS
    ncmp = kc_ref.shape[1]
    q0 = npages * PAGE
    slope = _slope_col(rows)
    klane = lax.broadcasted_iota(jnp.int32, (1, PAGE), 1)
    blocks_per_page = PAGE // SLC_BLOCK

    @pl.when(j == 0)
    def _():
        pr = lax.broadcasted_iota(jnp.int32, (BRANCH_W, KV_W), 0)
        pc = lax.broadcasted_iota(jnp.int32, (BRANCH_W, KV_W), 1)
        place = jnp.where((pr % HEAD_DIM == pc % HEAD_DIM) & (pc // HEAD_DIM == pr // (NSA_R * HEAD_DIM)), 1.0, 0.0)
        qg = jnp.dot(_block_diag_queries(q_ref, nt), place, preferred_element_type=F32, precision=HIGHEST)
        qbd = (qg * SCALE).astype(BF16)
        qbd_ref[...] = qbd
        c = lax.broadcasted_iota(jnp.int32, (1, ncmp), 1)
        s = _dot_nt(qbd, kc_ref[0]) + slope * (CMP_S * c + (CMP_S - 1) - q0).astype(F32)
        valid = c >= 1
        s = jnp.where(valid, s, MASKED)
        p = jnp.where(valid, jnp.exp(s - jnp.max(s, axis=1, keepdims=True)), 0.0)
        p = p * (1.0 / jnp.sum(p, axis=1, keepdims=True))
        ocmp_ref[...] = gate_ref[0, :, 0:1] * jnp.dot(p.astype(BF16), vc_ref[0], preferred_element_type=F32)
        gr = lax.broadcasted_iota(jnp.int32, (nt * NSA_G, rows), 0)
        gc = lax.broadcasted_iota(jnp.int32, (nt * NSA_G, rows), 1) // NSA_R
        imp = jnp.dot(jnp.where(gr == gc, 1.0, 0.0), p, preferred_element_type=F32, precision=HIGHEST)
        ratio = SLC_BLOCK // CMP_S
        mr = lax.broadcasted_iota(jnp.int32, (ncmp, nsb_pad), 0)
        mb = lax.broadcasted_iota(jnp.int32, (ncmp, nsb_pad), 1)
        gather_m = jnp.where((mr >= ratio * mb) & (mr <= ratio * mb + ratio) & (mr >= 1), 1.0, 0.0)
        p_slc = jnp.dot(imp, gather_m, preferred_element_type=F32, precision=HIGHEST)
        bj = lax.broadcasted_iota(jnp.int32, p_slc.shape, 1)
        own = q0 // SLC_BLOCK
        forced = (bj == 0) | (bj >= own - 1)
        allowed = bj <= own
        score = jnp.where(allowed, jnp.where(forced, p_slc + FORCE_SCORE, p_slc), -1.0)
        sel = jnp.zeros(p_slc.shape, F32)
        for _ in range(SLC_N):
            mx = jnp.max(score, axis=1, keepdims=True)
            idx = jnp.min(jnp.where(score == mx, bj, nsb_pad), axis=1, keepdims=True)
            hit = bj == idx
            sel = jnp.where(hit, 1.0, sel)
            score = jnp.where(hit, -2.0, score)
        sel = jnp.where(allowed, sel, 0.0)
        er = lax.broadcasted_iota(jnp.int32, (rows, nt * NSA_G), 0) // NSA_R
        ec = lax.broadcasted_iota(jnp.int32, (rows, nt * NSA_G), 1)
        sel_ref[...] = jnp.dot(jnp.where(er == ec, 1.0, 0.0), sel, preferred_element_type=F32).astype(BF16)
        _softmax_reset(m_ref, l_ref, acc_ref)

    def step(kv_ref, rel, mask):
        k = kv_ref[0, :, 0:KV_W].astype(BF16)
        v = kv_ref[0, :, KV_W:2 * KV_W].astype(BF16)
        s = _dot_nt(qbd_ref[...], k) + slope * rel
        if mask is not None:
            s = jnp.where(mask, s, MASKED)
        _row_softmax_step(s, v, m_ref, l_ref, acc_ref)

    brow = lax.broadcasted_iota(jnp.int32, (nsb_pad, PAGE), 0)
    bcol = lax.broadcasted_iota(jnp.int32, (nsb_pad, PAGE), 1) // SLC_BLOCK
    expand = jnp.where(brow == bcol + j * blocks_per_page, 1.0, 0.0).astype(BF16)
    keep = jnp.dot(sel_ref[...], expand, preferred_element_type=F32) > 0.5
    step(pg_ref, (klane + (j - npages) * PAGE).astype(F32), keep)

    @pl.when(j == npages - 1)
    def _():
        new_mask = _new_token_mask(rows, nt)
        step(npg_ref, klane.astype(F32), new_mask)
        o_slc = acc_ref[...] * (1.0 / l_ref[...])
        _softmax_reset(m_ref, l_ref, acc_ref)
        wlen = win_ref.shape[1]
        wi = lax.broadcasted_iota(jnp.int32, (rows, wlen), 1)
        wt = lax.broadcasted_iota(jnp.int32, (rows, wlen), 0) // N_HEADS
        wrel = (lax.broadcasted_iota(jnp.int32, (1, wlen), 1) - wlen).astype(F32)
        step(win_ref, wrel, wi > wt + (wlen - WINDOW))
        step(nwin_ref, klane.astype(F32), new_mask)
        o_win = acc_ref[...] * (1.0 / l_ref[...])
        o = ocmp_ref[...] + gate_ref[0, :, 1:2] * o_slc + gate_ref[0, :, 2:3] * o_win
        ur = lax.broadcasted_iota(jnp.int32, (KV_W, BRANCH_W), 0)
        uc = lax.broadcasted_iota(jnp.int32, (KV_W, BRANCH_W), 1)
        unplace = jnp.where((ur % HEAD_DIM == uc % HEAD_DIM) & (ur // HEAD_DIM == uc // (NSA_R * HEAD_DIM)), 1.0, 0.0)
        o_ref[0] = _extract_heads(jnp.dot(o, unplace, preferred_element_type=F32, precision=HIGHEST), nt)


def nsa_decode_call(pt, q, gate, kcmp, vcmp, cache_kv, new_kv, win, new_win):
    n, nt, _ = q.shape
    npages = pt.shape[1]
    rows = nt * N_HEADS
    ncmp = kcmp.shape[1]
    nsb = (npages * PAGE) // SLC_BLOCK + 1
    nsb_pad = -(-nsb // 128) * 128
    wlen = win.shape[1]
    grid_spec = pltpu.PrefetchScalarGridSpec(
        num_scalar_prefetch=1,
        grid=(n, npages),
        in_specs=[pl.BlockSpec((1, nt, BRANCH_W), lambda b, j, pt: (b, 0, 0)),
                  pl.BlockSpec((1, rows, 3), lambda b, j, pt: (b, 0, 0)),
                  pl.BlockSpec((1, ncmp, KV_W), lambda b, j, pt: (b, 0, 0)),
                  pl.BlockSpec((1, ncmp, KV_W), lambda b, j, pt: (b, 0, 0)),
                  pl.BlockSpec((1, PAGE, 2 * KV_W), lambda b, j, pt: (pt[b, j], 0, 1)),
                  pl.BlockSpec((1, PAGE, 2 * KV_W), lambda b, j, pt: (b, 0, 0)),
                  pl.BlockSpec((1, wlen, 2 * KV_W), lambda b, j, pt: (b, 0, 0)),
                  pl.BlockSpec((1, PAGE, 2 * KV_W), lambda b, j, pt: (b, 0, 0))],
        out_specs=pl.BlockSpec((1, nt, BRANCH_W), lambda b, j, pt: (b, 0, 0)),
        scratch_shapes=[pltpu.VMEM((rows, KV_W), BF16), pltpu.VMEM((rows, nsb_pad), BF16),
                        pltpu.VMEM((rows, KV_W), F32), pltpu.VMEM((rows, 1), F32), pltpu.VMEM((rows, 1), F32),
                        pltpu.VMEM((rows, KV_W), F32)],
    )
    return pl.pallas_call(
        functools.partial(_dnsa_kernel, nt=nt, nsb_pad=nsb_pad),
        grid_spec=grid_spec,
        out_shape=jax.ShapeDtypeStruct((n, nt, BRANCH_W), F32),
        compiler_params=_cparams("parallel", "arbitrary"),
        name="nsa_decode",
    )(pt, q, gate, kcmp, vcmp, cache_kv, new_kv, win, new_win)


def _conv_sample_kernel(z_ref, pre1_ref, pre2_ref, w_ref, o_ref, u_ref, *, nt):
    bw = BRANCH_W
    u = z_ref[:, 2 * bw:3 * bw] * z_ref[:, 0:bw]
    t = lax.broadcasted_iota(jnp.int32, u.shape, 0) % nt
    u1 = jnp.where(t < 1, pre1_ref[...], pltpu.roll(u, 1, 0))
    u2 = jnp.where(t < 2, pre2_ref[...], pltpu.roll(u, 2, 0))
    y = u2 * w_ref[0:1, :] + u1 * w_ref[1:2, :] + u * w_ref[2:3, :]
    o_ref[...] = z_ref[:, bw:2 * bw] * y
    u_ref[...] = u


def conv_sample_call(z_conv, state, conv_w, n, nt):
    zero = jnp.zeros((n, nt, BRANCH_W), F32)
    pre1 = zero.at[:, 0].set(state[:, 1]).reshape(n * nt, BRANCH_W)
    pre2 = zero.at[:, 0].set(state[:, 0]).at[:, 1].set(state[:, 1]).reshape(n * nt, BRANCH_W)
    return pl.pallas_call(
        functools.partial(_conv_sample_kernel, nt=nt),
        out_shape=[jax.ShapeDtypeStruct((n * nt, BRANCH_W), F32), jax.ShapeDtypeStruct((n * nt, BRANCH_W), F32)],
        name="conv_sample",
    )(z_conv, pre1, pre2, conv_w)


PPS = 16


def _split3(x):
    hi = x.astype(BF16)
    r1 = x - hi.astype(F32)
    mid = r1.astype(BF16)
    lo = (r1 - mid.astype(F32)).astype(BF16)
    return jnp.concatenate([hi, mid, lo], axis=0)


def _softmax_step_t(s, vts, m_ref, l_ref, acc_ref):
    m_old = m_ref[...]
    m_new = jnp.maximum(m_old, jnp.max(s, axis=1, keepdims=True))
    alpha = jnp.exp(m_old - m_new)
    p = jnp.exp(s - m_new)
    l_ref[...] = alpha * l_ref[...] + jnp.sum(p, axis=1, keepdims=True)
    p = p.astype(BF16)
    pv, off = None, 0
    for vt in vts:
        nk = vt.shape[1]
        term = _dot_nt(p[:, off:off + nk], vt)
        pv = term if pv is None else pv + term
        off += nk
    acc_ref[...] = alpha * acc_ref[...] + pv
    m_ref[...] = m_new


def _page_specs(rows, row_block, pps):
    return [pl.BlockSpec((1, rows, PAGE), functools.partial(lambda b, j, pt, u: (pt[b, j * pps + u], row_block, 0), u=u))
            for u in range(pps)]


def _dfox_t_kernel(pt_ref, q_ref, *refs, nt, pps):
    del pt_ref
    kv_refs, lf_refs = refs[:pps], refs[pps:2 * pps]
    nkv_ref, nlf_ref, o_ref, qbd_ref, m_ref, l_ref, acc_ref, cf_ref = refs[2 * pps:]
    j = pl.program_id(1)
    rows = nt * N_HEADS

    @pl.when(j == 0)
    def _():
        qbd_ref[...] = (_block_diag_queries(q_ref, nt) * SCALE).astype(BF16)
        _softmax_reset(m_ref, l_ref, acc_ref)
        cf_ref[...] = jnp.zeros_like(cf_ref)

    def block(kvs, lfs, mask):
        a = lax.broadcasted_iota(jnp.int32, (PAGE, PAGE), 0)
        b = lax.broadcasted_iota(jnp.int32, (PAGE, PAGE), 1)
        tri = jnp.where(a <= b, 1.0, 0.0).astype(BF16)
        carry = cf_ref[...]
        parts, vts = [], []
        for kv_ref, lf_ref in zip(kvs, lfs):
            lfe = jnp.concatenate([lf_ref[0]] * nt, axis=0)
            c3 = jnp.dot(_split3(lfe), tri, preferred_element_type=F32)
            fk = c3[0:rows] + c3[rows:2 * rows] + c3[2 * rows:3 * rows] + carry
            carry = fk[:, PAGE - 1:PAGE]
            parts.append(jnp.dot(qbd_ref[...], kv_ref[0, 0:BRANCH_W, :].astype(BF16), preferred_element_type=F32) - fk)
            vts.append(kv_ref[0, BRANCH_W:2 * BRANCH_W, :].astype(BF16))
        cf_ref[...] = carry
        s = parts[0] if len(parts) == 1 else jnp.concatenate(parts, axis=1)
        if mask is not None:
            s = jnp.where(mask, s, MASKED)
        _softmax_step_t(s, vts, m_ref, l_ref, acc_ref)

    block(kv_refs, lf_refs, None)

    @pl.when(j == pl.num_programs(1) - 1)
    def _():
        block([nkv_ref], [nlf_ref], _new_token_mask(rows, nt))
        o_ref[0] = _extract_heads(acc_ref[...] * (1.0 / l_ref[...]), nt)


def fox_decode_t_call(pt, q, cache_kvt, cache_lft, new_kvt, new_lft):
    n, nt, _ = q.shape
    npages = pt.shape[1]
    pps = PPS if npages % PPS == 0 else 1
    rows = nt * N_HEADS
    grid_spec = pltpu.PrefetchScalarGridSpec(
        num_scalar_prefetch=1,
        grid=(n, npages // pps),
        in_specs=[pl.BlockSpec((1, nt, BRANCH_W), lambda b, j, pt: (b, 0, 0))]
        + _page_specs(2 * BRANCH_W, 0, pps) + _page_specs(N_HEADS, 0, pps)
        + [pl.BlockSpec((1, 2 * BRANCH_W, PAGE), lambda b, j, pt: (b, 0, 0)),
           pl.BlockSpec((1, N_HEADS, PAGE), lambda b, j, pt: (b, 0, 0))],
        out_specs=pl.BlockSpec((1, nt, BRANCH_W), lambda b, j, pt: (b, 0, 0)),
        scratch_shapes=[pltpu.VMEM((rows, BRANCH_W), BF16), pltpu.VMEM((rows, 1), F32), pltpu.VMEM((rows, 1), F32),
                        pltpu.VMEM((rows, BRANCH_W), F32), pltpu.VMEM((rows, 1), F32)],
    )
    return pl.pallas_call(
        functools.partial(_dfox_t_kernel, nt=nt, pps=pps),
        grid_spec=grid_spec,
        out_shape=jax.ShapeDtypeStruct((n, nt, BRANCH_W), F32),
        compiler_params=_cparams("parallel", "arbitrary"),
        name="fox_decode",
    )(pt, q, *([cache_kvt] * pps), *([cache_lft] * pps), new_kvt, new_lft)


def _dmoba_sel_t_kernel(pt_ref, q_ref, *refs, nt, pps):
    del pt_ref
    k_refs = refs[:pps]
    sel_ref, q3_ref, g_ref = refs[pps:]
    j = pl.program_id(1)
    rows = nt * N_HEADS
    npg = g_ref.shape[1]
    pages_per_block = MOBA_BLOCK // PAGE
    lane = lax.broadcasted_iota(jnp.int32, (rows, npg), 1)

    @pl.when(j == 0)
    def _():
        q3_ref[...] = _split3(_block_diag_queries(q_ref, nt))
        g_ref[...] = jnp.zeros_like(g_ref)

    g = g_ref[...]
    for u, k_ref in enumerate(k_refs):
        qk3 = jnp.dot(q3_ref[...], k_ref[0].astype(BF16), preferred_element_type=F32)
        qk = qk3[0:rows] + qk3[rows:2 * rows] + qk3[2 * rows:3 * rows]
        g = jnp.where(lane == j * pps + u, jnp.sum(qk, axis=1, keepdims=True), g)
    g_ref[...] = g

    @pl.when(j == pl.num_programs(1) - 1)
    def _():
        blk = g
        for u in range(1, pages_per_block):
            blk = blk + pltpu.roll(g, npg - u, 1)
        gs = jnp.where(lane % pages_per_block == 0, blk * (1.0 / MOBA_BLOCK), NEG)
        sel = jnp.zeros((rows, npg), F32)
        for _ in range(MOBA_TOPK):
            mx = jnp.max(gs, axis=1, keepdims=True)
            idx = jnp.min(jnp.where(gs == mx, lane, npg), axis=1, keepdims=True)
            hit = lane == idx
            sel = jnp.where(hit, 1.0, sel)
            gs = jnp.where(hit, MASKED, gs)
        out = sel
        for u in range(1, pages_per_block):
            out = out + pltpu.roll(sel, u, 1)
        sel_ref[0] = out


def moba_select_t_call(pt, q, cache_kvt):
    n, nt, _ = q.shape
    npages = pt.shape[1]
    pps = PPS if npages % PPS == 0 else 1
    rows = nt * N_HEADS
    grid_spec = pltpu.PrefetchScalarGridSpec(
        num_scalar_prefetch=1,
        grid=(n, npages // pps),
        in_specs=[pl.BlockSpec((1, nt, BRANCH_W), lambda b, j, pt: (b, 0, 0))] + _page_specs(BRANCH_W, 0, pps),
        out_specs=pl.BlockSpec((1, rows, npages), lambda b, j, pt: (b, 0, 0)),
        scratch_shapes=[pltpu.VMEM((3 * rows, BRANCH_W), BF16), pltpu.VMEM((rows, npages), F32)],
    )
    return pl.pallas_call(
        functools.partial(_dmoba_sel_t_kernel, nt=nt, pps=pps),
        grid_spec=grid_spec,
        out_shape=jax.ShapeDtypeStruct((n, rows, npages), F32),
        compiler_params=_cparams("parallel", "arbitrary"),
        name="moba_select",
    )(pt, q, *([cache_kvt] * pps))


def _dmoba_t_kernel(pt_ref, q_ref, sel_ref, *refs, nt, pps):
    del pt_ref
    kv_refs = refs[:pps]
    nkv_ref, o_ref, qbd_ref, m_ref, l_ref, acc_ref = refs[pps:]
    j = pl.program_id(1)
    nsteps = pl.num_programs(1)
    rows = nt * N_HEADS
    slope = _slope_col(rows)

    @pl.when(j == 0)
    def _():
        qbd_ref[...] = (_block_diag_queries(q_ref, nt) * SCALE).astype(BF16)
        _softmax_reset(m_ref, l_ref, acc_ref)

    def block(kvs, rel, mask):
        parts = [jnp.dot(qbd_ref[...], kv_ref[0, 0:BRANCH_W, :].astype(BF16), preferred_element_type=F32)
                 for kv_ref in kvs]
        s = parts[0] if len(parts) == 1 else jnp.concatenate(parts, axis=1)
        s = jnp.where(mask, s + slope * rel, MASKED)
        _softmax_step_t(s, [kv_ref[0, BRANCH_W:2 * BRANCH_W, :].astype(BF16) for kv_ref in kvs], m_ref, l_ref, acc_ref)

    npg = sel_ref.shape[2]
    prow = lax.broadcasted_iota(jnp.int32, (npg, pps * PAGE), 0)
    pcol = lax.broadcasted_iota(jnp.int32, (npg, pps * PAGE), 1) // PAGE
    keep = jnp.dot(sel_ref[0].astype(BF16), jnp.where(prow == pcol + j * pps, 1.0, 0.0).astype(BF16),
                   preferred_element_type=F32) > 0.5
    klane = lax.broadcasted_iota(jnp.int32, (1, pps * PAGE), 1)
    block(kv_refs, (klane + (j - nsteps) * (pps * PAGE)).astype(F32), keep)

    @pl.when(j == nsteps - 1)
    def _():
        block([nkv_ref], lax.broadcasted_iota(jnp.int32, (1, PAGE), 1).astype(F32), _new_token_mask(rows, nt))
        o_ref[0] = _extract_heads(acc_ref[...] * (1.0 / l_ref[...]), nt)


def moba_decode_t_call(pt, q, sel, cache_kvt, new_kvt):
    n, nt, _ = q.shape
    npages = pt.shape[1]
    pps = PPS if npages % PPS == 0 else 1
    rows = nt * N_HEADS
    grid_spec = pltpu.PrefetchScalarGridSpec(
        num_scalar_prefetch=1,
        grid=(n, npages // pps),
        in_specs=[pl.BlockSpec((1, nt, BRANCH_W), lambda b, j, pt: (b, 0, 0)),
                  pl.BlockSpec((1, rows, npages), lambda b, j, pt: (b, 0, 0))]
        + _page_specs(2 * BRANCH_W, 0, pps)
        + [pl.BlockSpec((1, 2 * BRANCH_W, PAGE), lambda b, j, pt: (b, 0, 0))],
        out_specs=pl.BlockSpec((1, nt, BRANCH_W), lambda b, j, pt: (b, 0, 0)),
        scratch_shapes=[pltpu.VMEM((rows, BRANCH_W), BF16), pltpu.VMEM((rows, 1), F32), pltpu.VMEM((rows, 1), F32),
                        pltpu.VMEM((rows, BRANCH_W), F32)],
    )
    return pl.pallas_call(
        functools.partial(_dmoba_t_kernel, nt=nt, pps=pps),
        grid_spec=grid_spec,
        out_shape=jax.ShapeDtypeStruct((n, nt, BRANCH_W), F32),
        compiler_params=_cparams("parallel", "arbitrary"),
        name="moba_decode",
    )(pt, q, sel, *([cache_kvt] * pps), new_kvt)


def _compress_t_kernel(pt_ref, *refs, rows, pps):
    del pt_ref
    a_refs = refs[:pps]
    pet_ref, peb_ref, wt_ref, wb_ref, w2k_ref, w2v_ref, w2vt_ref, kc_ref, vc_ref, vct_ref, rows_ref = refs[pps:]
    j = pl.program_id(1)
    pr = lax.broadcasted_iota(jnp.int32, (PAGE, PAGE), 0)
    pc = lax.broadcasted_iota(jnp.int32, (PAGE, PAGE), 1)
    perm = jnp.where(pc == CMP_S * (pr % PAGE_ROWS) + pr // PAGE_ROWS, 1.0, 0.0).astype(BF16)
    w2 = 2 * KV_W
    for u, a_ref in enumerate(a_refs):
        p3 = _dot_nt(perm, _split3(a_ref[0]))
        pg = p3[:, 0:w2] + p3[:, w2:2 * w2] + p3[:, 2 * w2:3 * w2]
        r0 = pl.multiple_of((j * pps + u) * PAGE_ROWS, PAGE_ROWS)
        for i in range(CMP_S):
            rows_ref[pl.ds(r0, PAGE_ROWS), i * w2:(i + 1) * w2] = pg[i * PAGE_ROWS:(i + 1) * PAGE_ROWS, :]

    @pl.when(j == pl.num_programs(1) - 1)
    def _():
        chunk = min(256, rows)
        prev_top = jnp.zeros((1, 4 * CMP_HID), F32)
        for c in range(rows // chunk):
            a = rows_ref[c * chunk:(c + 1) * chunk, :]
            ht = jnp.dot((a + pet_ref[...]).astype(BF16), wt_ref[...], preferred_element_type=F32)
            hb = jnp.dot((a + peb_ref[...]).astype(BF16), wb_ref[...], preferred_element_type=F32)
            row = lax.broadcasted_iota(jnp.int32, ht.shape, 0)
            shifted = jnp.where(row == 0, prev_top, pltpu.roll(ht, 1, 0))
            prev_top = ht[chunk - 1:chunk, :]
            g = _gelu_tanh(shifted + hb).astype(BF16)
            kc_ref[0, c * chunk:(c + 1) * chunk, :] = jnp.dot(
                g[:, :2 * CMP_HID], w2k_ref[...], preferred_element_type=F32).astype(BF16)
            vc_ref[0, c * chunk:(c + 1) * chunk, :] = jnp.dot(
                g[:, 2 * CMP_HID:], w2v_ref[...], preferred_element_type=F32).astype(BF16)
            vct_ref[0, :, c * chunk:(c + 1) * chunk] = _dot_nt(w2vt_ref[...], g[:, 2 * CMP_HID:]).astype(BF16)


def compress_t_call(pages_t, page_idx, pe_top, pe_bot, wt, wb, w2k, w2v, w2vt, n, npages):
    rows = npages * PAGE_ROWS
    pps = PPS if npages % PPS == 0 else 1
    const = lambda b, j, pt: (0, 0)
    grid_spec = pltpu.PrefetchScalarGridSpec(
        num_scalar_prefetch=1,
        grid=(n, npages // pps),
        in_specs=_page_specs(2 * KV_W, 0, pps)
        + [pl.BlockSpec((1, CMP_ROW), const),
           pl.BlockSpec((1, CMP_ROW), const),
           pl.BlockSpec((CMP_ROW, 4 * CMP_HID), const, pipeline_mode=pl.Buffered(1)),
           pl.BlockSpec((CMP_ROW, 4 * CMP_HID), const, pipeline_mode=pl.Buffered(1)),
           pl.BlockSpec((2 * CMP_HID, KV_W), const),
           pl.BlockSpec((2 * CMP_HID, KV_W), const),
           pl.BlockSpec((KV_W, 2 * CMP_HID), const)],
        out_specs=[pl.BlockSpec((1, rows, KV_W), lambda b, j, pt: (b, 0, 0)),
                   pl.BlockSpec((1, rows, KV_W), lambda b, j, pt: (b, 0, 0)),
                   pl.BlockSpec((1, KV_W, rows), lambda b, j, pt: (b, 0, 0))],
        scratch_shapes=[pltpu.VMEM((rows, CMP_ROW), F32)],
    )
    return pl.pallas_call(
        functools.partial(_compress_t_kernel, rows=rows, pps=pps),
        grid_spec=grid_spec,
        out_shape=[jax.ShapeDtypeStruct((n, rows, KV_W), BF16),
                   jax.ShapeDtypeStruct((n, rows, KV_W), BF16),
                   jax.ShapeDtypeStruct((n, KV_W, rows), BF16)],
        compiler_params=_cparams("parallel", "arbitrary"),
        name="nsa_compress",
    )(page_idx, *([pages_t] * pps), pe_top, pe_bot, wt, wb, w2k, w2v, w2vt)


def _dnsa_t_kernel(pt_ref, q_ref, gate_ref, kc_ref, vc_ref, *refs, nt, nsb_pad, pps):
    del pt_ref
    kv_refs = refs[:pps]
    nkv_ref, win_ref, nwin_ref, o_ref, qbd_ref, sel_ref, ocmp_ref, m_ref, l_ref, acc_ref = refs[pps:]
    j = pl.program_id(1)
    nsteps = pl.num_programs(1)
    rows = nt * N_HEADS
    ncmp = kc_ref.shape[1]
    q0 = nsteps * (pps * PAGE)
    slope = _slope_col(rows)
    klane = lax.broadcasted_iota(jnp.int32, (1, PAGE), 1)
    blocks_per_page = PAGE // SLC_BLOCK

    @pl.when(j == 0)
    def _():
        pr = lax.broadcasted_iota(jnp.int32, (BRANCH_W, KV_W), 0)
        pc = lax.broadcasted_iota(jnp.int32, (BRANCH_W, KV_W), 1)
        place = jnp.where((pr % HEAD_DIM == pc % HEAD_DIM) & (pc // HEAD_DIM == pr // (NSA_R * HEAD_DIM)), 1.0, 0.0)
        qg = jnp.dot(_block_diag_queries(q_ref, nt), place, preferred_element_type=F32, precision=HIGHEST)
        qbd = (qg * SCALE).astype(BF16)
        qbd_ref[...] = qbd
        c = lax.broadcasted_iota(jnp.int32, (1, ncmp), 1)
        s = _dot_nt(qbd, kc_ref[0]) + slope * (CMP_S * c + (CMP_S - 1) - q0).astype(F32)
        valid = c >= 1
        s = jnp.where(valid, s, MASKED)
        p = jnp.where(valid, jnp.exp(s - jnp.max(s, axis=1, keepdims=True)), 0.0)
        p = p * (1.0 / jnp.sum(p, axis=1, keepdims=True))
        ocmp_ref[...] = gate_ref[0, :, 0:1] * jnp.dot(p.astype(BF16), vc_ref[0], preferred_element_type=F32)
        gr = lax.broadcasted_iota(jnp.int32, (nt * NSA_G, rows), 0)
        gc = lax.broadcasted_iota(jnp.int32, (nt * NSA_G, rows), 1) // NSA_R
        imp = jnp.dot(jnp.where(gr == gc, 1.0, 0.0), p, preferred_element_type=F32, precision=HIGHEST)
        ratio = SLC_BLOCK // CMP_S
        mr = lax.broadcasted_iota(jnp.int32, (ncmp, nsb_pad), 0)
        mb = lax.broadcasted_iota(jnp.int32, (ncmp, nsb_pad), 1)
        gather_m = jnp.where((mr >= ratio * mb) & (mr <= ratio * mb + ratio) & (mr >= 1), 1.0, 0.0)
        p_slc = jnp.dot(imp, gather_m, preferred_element_type=F32, precision=HIGHEST)
        bj = lax.broadcasted_iota(jnp.int32, p_slc.shape, 1)
        own = q0 // SLC_BLOCK
        forced = (bj == 0) | (bj >= own - 1)
        allowed = bj <= own
        score = jnp.where(allowed, jnp.where(forced, p_slc + FORCE_SCORE, p_slc), -1.0)
        sel = jnp.zeros(p_slc.shape, F32)
        for _ in range(SLC_N):
            mx = jnp.max(score, axis=1, keepdims=True)
            idx = jnp.min(jnp.where(score == mx, bj, nsb_pad), axis=1, keepdims=True)
            hit = bj == idx
            sel = jnp.where(hit, 1.0, sel)
            score = jnp.where(hit, -2.0, score)
        sel = jnp.where(allowed, sel, 0.0)
        er = lax.broadcasted_iota(jnp.int32, (rows, nt * NSA_G), 0) // NSA_R
        ec = lax.broadcasted_iota(jnp.int32, (rows, nt * NSA_G), 1)
        sel_ref[...] = jnp.dot(jnp.where(er == ec, 1.0, 0.0), sel, preferred_element_type=F32).astype(BF16)
        _softmax_reset(m_ref, l_ref, acc_ref)

    def block(kvs, rel, mask):
        parts = [jnp.dot(qbd_ref[...], kv_ref[0, 0:KV_W, :].astype(BF16), preferred_element_type=F32) for kv_ref in kvs]
        s = (parts[0] if len(parts) == 1 else jnp.concatenate(parts, axis=1)) + slope * rel
        if mask is not None:
            s = jnp.where(mask, s, MASKED)
        _softmax_step_t(s, [kv_ref[0, KV_W:2 * KV_W, :].astype(BF16) for kv_ref in kvs], m_ref, l_ref, acc_ref)

    brow = lax.broadcasted_iota(jnp.int32, (nsb_pad, pps * PAGE), 0)
    bcol = lax.broadcasted_iota(jnp.int32, (nsb_pad, pps * PAGE), 1) // SLC_BLOCK
    expand = jnp.where(brow == bcol + j * (pps * blocks_per_page), 1.0, 0.0).astype(BF16)
    keep = jnp.dot(sel_ref[...], expand, preferred_element_type=F32) > 0.5
    kl = lax.broadcasted_iota(jnp.int32, (1, pps * PAGE), 1)
    block(kv_refs, (kl + (j - nsteps) * (pps * PAGE)).astype(F32), keep)

    @pl.when(j == nsteps - 1)
    def _():
        new_mask = _new_token_mask(rows, nt)
        block([nkv_ref], klane.astype(F32), new_mask)
        o_slc = acc_ref[...] * (1.0 / l_ref[...])
        _softmax_reset(m_ref, l_ref, acc_ref)
        wlen = win_ref.shape[2]
        wi = lax.broadcasted_iota(jnp.int32, (rows, wlen), 1)
        wt = lax.broadcasted_iota(jnp.int32, (rows, wlen), 0) // N_HEADS
        wrel = (lax.broadcasted_iota(jnp.int32, (1, wlen), 1) - wlen).astype(F32)
        block([win_ref], wrel, wi > wt + (wlen - WINDOW))
        block([nwin_ref], klane.astype(F32), new_mask)
        o_win = acc_ref[...] * (1.0 / l_ref[...])
        o = ocmp_ref[...] + gate_ref[0, :, 1:2] * o_slc + gate_ref[0, :, 2:3] * o_win
        ur = lax.broadcasted_iota(jnp.int32, (KV_W, BRANCH_W), 0)
        uc = lax.broadcasted_iota(jnp.int32, (KV_W, BRANCH_W), 1)
        unplace = jnp.where((ur % HEAD_DIM == uc % HEAD_DIM) & (ur // HEAD_DIM == uc // (NSA_R * HEAD_DIM)), 1.0, 0.0)
        o_ref[0] = _extract_heads(jnp.dot(o, unplace, preferred_element_type=F32, precision=HIGHEST), nt)


def nsa_decode_t_call(pt, q, gate, kcmp, vcmp, cache_kvt, new_kvt, win_t, new_win_t):
    n, nt, _ = q.shape
    npages = pt.shape[1]
    pps = PPS if npages % PPS == 0 else 1
    rows = nt * N_HEADS
    ncmp = kcmp.shape[1]
    nsb = (npages * PAGE) // SLC_BLOCK + 1
    nsb_pad = -(-nsb // 128) * 128
    wlen = win_t.shape[2]
    grid_spec = pltpu.PrefetchScalarGridSpec(
        num_scalar_prefetch=1,
        grid=(n, npages // pps),
        in_specs=[pl.BlockSpec((1, nt, BRANCH_W), lambda b, j, pt: (b, 0, 0)),
                  pl.BlockSpec((1, rows, 3), lambda b, j, pt: (b, 0, 0)),
                  pl.BlockSpec((1, ncmp, KV_W), lambda b, j, pt: (b, 0, 0)),
                  pl.BlockSpec((1, ncmp, KV_W), lambda b, j, pt: (b, 0, 0))]
        + _page_specs(2 * KV_W, 1, pps)
        + [pl.BlockSpec((1, 2 * KV_W, PAGE), lambda b, j, pt: (b, 0, 0)),
           pl.BlockSpec((1, 2 * KV_W, wlen), lambda b, j, pt: (b, 0, 0)),
           pl.BlockSpec((1, 2 * KV_W, PAGE), lambda b, j, pt: (b, 0, 0))],
        out_specs=pl.BlockSpec((1, nt, BRANCH_W), lambda b, j, pt: (b, 0, 0)),
        scratch_shapes=[pltpu.VMEM((rows, KV_W), BF16), pltpu.VMEM((rows, nsb_pad), BF16),
                        pltpu.VMEM((rows, KV_W), F32), pltpu.VMEM((rows, 1), F32), pltpu.VMEM((rows, 1), F32),
                        pltpu.VMEM((rows, KV_W), F32)],
    )
    return pl.pallas_call(
        functools.partial(_dnsa_t_kernel, nt=nt, nsb_pad=nsb_pad, pps=pps),
        grid_spec=grid_spec,
        out_shape=jax.ShapeDtypeStruct((n, nt, BRANCH_W), F32),
        compiler_params=_cparams("parallel", "arbitrary"),
        name="nsa_decode",
    )(pt, q, gate, kcmp, vcmp, *([cache_kvt] * pps), new_kvt, win_t, new_win_t)


def _compress_weights(cmp_w1, cmp_w2, cmp_pe):
    w1 = cmp_w1.reshape(2, CMP_L, HEAD_DIM, CMP_HID)
    eye_w = jnp.eye(2, dtype=F32)
    eye_g = jnp.eye(NSA_G, dtype=F32)
    big = jnp.einsum('widh,sw,gk->isgdwkh', w1, eye_w, eye_g)
    big = big.reshape(CMP_L, 2 * KV_W, 4 * CMP_HID)
    wt = big[:CMP_S].reshape(CMP_ROW, 4 * CMP_HID).astype(BF16)
    wb = big[CMP_S:].reshape(CMP_ROW, 4 * CMP_HID).astype(BF16)
    w2k = jnp.einsum('hd,gk->ghkd', cmp_w2[0], eye_g).reshape(2 * CMP_HID, KV_W).astype(BF16)
    w2v = jnp.einsum('hd,gk->ghkd', cmp_w2[1], eye_g).reshape(2 * CMP_HID, KV_W).astype(BF16)
    w2vt = jnp.einsum('hd,gk->kdgh', cmp_w2[1], eye_g).reshape(KV_W, 2 * CMP_HID).astype(BF16)
    pe = jnp.broadcast_to(cmp_pe[:, :, None, :], (2, CMP_L, NSA_G, HEAD_DIM))
    pe = jnp.transpose(pe, (1, 0, 2, 3)).reshape(CMP_L, 2 * KV_W)
    pe_top = pe[:CMP_S].reshape(1, CMP_ROW)
    pe_bot = pe[CMP_S:].reshape(1, CMP_ROW)
    return pe_top, pe_bot, wt, wb, w2k, w2v, w2vt


def _layer_weights(l, w_in, w_branch, w_out, w_up, w_down):
    wl = w_in[l]
    small = jnp.concatenate(
        [wl[:, OFF_FOX_F:OFF_FOX_F + N_HEADS], wl[:, OFF_NSA_GATE:OFF_NSA_GATE + 3 * N_HEADS],
         jnp.zeros((D_MODEL, SMALL_W - 4 * N_HEADS), wl.dtype)], axis=1)
    return dict(
        conv=wl[:, OFF_CONV:OFF_FOX].astype(BF16),
        fox=wl[:, OFF_FOX:OFF_FOX_F].astype(BF16),
        moba=wl[:, OFF_MOBA:OFF_NSA].astype(BF16),
        nsa=wl[:, OFF_NSA:OFF_NSA_GATE].astype(BF16),
        small=small.astype(BF16),
        gate=wl[:, OFF_MERGE:].astype(BF16),
        branch=w_branch[l].astype(BF16),
        out=w_out[l].astype(BF16),
        up=w_up[l].astype(BF16),
        down=w_down[l].astype(BF16),
    )


def _finish_layer(x, h, branches, w, g_mlp, g_next, next_dtype, tm):
    merged = merge_call(h, branches, w["gate"], w["branch"], tm, 512)
    x1, hm = outproj_call(merged, w["out"], x, g_mlp, tm)
    return mlp_call(hm, w["up"], w["down"], x1, g_next, next_dtype, tm, 1024)


def prompt_mixers(h, n, t, w, b_forget, conv_w, cmp_w, tm):
    bw = BRANCH_W
    (z_conv,) = proj_call(h, w["conv"], [("f32", 3 * bw)], [(a, a + 512, ((0, a),)) for a in (0, 512, 1024)],
                          tm, "proj_conv")
    qkv_defs = [("f32", bw), ("f32", 2 * bw), ("bf16", bw), ("bf16T", bw)]
    qkv_plan = [(0, bw, ((0, 0),)), (bw, 2 * bw, ((1, 0), (2, 0))), (2 * bw, 3 * bw, ((1, bw), (3, 0)))]
    fox_q, fox_kv, fox_kb, fox_vt = proj_call(h, w["fox"], qkv_defs, qkv_plan, tm, "proj_fox")
    moba_q, moba_kv, moba_kb, moba_vt, moba_km = proj_call(
        h, w["moba"], qkv_defs + [("blockmean", bw)],
        [qkv_plan[0], (bw, 2 * bw, ((1, 0), (2, 0), (4, 0))), qkv_plan[2]], tm, "proj_moba")
    kv = KV_W
    nsa_q, nsa_kv, nsa_win, nsa_ks, nsa_vst, nsa_kw, nsa_vwt, nsa_cmp_t = proj_call(
        h, w["nsa"],
        [("f32", bw), ("f32", 4 * kv), ("f32", 2 * kv), ("bf16", kv), ("bf16T", kv), ("bf16", kv), ("bf16T", kv),
         ("f32T", 2 * kv)],
        [(0, bw, ((0, 0),)), (bw, bw + 2 * kv, ((1, 0), (7, 0))),
         (bw + 2 * kv, bw + 3 * kv, ((1, 2 * kv), (3, 0))), (bw + 3 * kv, bw + 4 * kv, ((1, 3 * kv), (4, 0))),
         (bw + 4 * kv, bw + 5 * kv, ((2, 0), (5, 0))), (bw + 5 * kv, bw + 6 * kv, ((2, kv), (6, 0)))],
        tm, "proj_nsa")
    (z_small,) = proj_call(h, w["small"], [("f32", SMALL_W)], [(0, SMALL_W, ((0, 0),))], tm, "proj_small")

    out_a, new_conv = conv_prompt_call(z_conv, conv_w, n, t, tm)
    lg, fk = small_call(z_small, b_forget, n, t, tm)
    logf = lg[:, :N_HEADS].reshape(n, t, N_HEADS)
    gate_t = lg[:, N_HEADS:4 * N_HEADS].T

    out_b = fox_prompt_call(fox_q, fox_kb, fox_vt, fk, n, t)
    out_c = moba_prompt_call(moba_q, moba_kb, moba_vt, moba_km, n, t)

    npages = t // PAGE
    page_idx = jnp.arange(n * npages, dtype=jnp.int32).reshape(n, npages)
    kcmp, _, vcmpt = compress_t_call(nsa_cmp_t, page_idx, *cmp_w, n, npages)
    out_d = nsa_prompt_call(nsa_q, kcmp, vcmpt, nsa_ks, nsa_vst, nsa_kw, nsa_vwt, gate_t, n, t)

    wb = min(WINDOW, t)
    new_state = (new_conv,
                 fox_kv.reshape(n, t, 2, N_HEADS, HEAD_DIM),
                 logf,
                 moba_kv.reshape(n, t, 2, N_HEADS, HEAD_DIM),
                 nsa_kv.reshape(n, t, 2, 2, NSA_G, HEAD_DIM),
                 nsa_win.reshape(n, t, 2, NSA_G, HEAD_DIM)[:, t - wb:])
    return [out_a, out_b, out_c, out_d], new_state


def sample_mixers(h, n, t, q0, past, w, b_forget, conv_w, cmp_w1, cmp_w2, cmp_pe, tm):
    dt = F32
    bw = BRANCH_W
    kv = KV_W
    (z_conv,) = proj_call(h, w["conv"], [("f32", 3 * bw)], [(0, 3 * bw, ((0, 0),))], tm, "proj_conv_s")
    fox_q, fox_kv = proj_call(h, w["fox"], [("f32", bw), ("f32", 2 * bw)],
                              [(0, bw, ((0, 0),)), (bw, 3 * bw, ((1, 0),))], tm, "proj_fox_s")
    moba_q, moba_kv = proj_call(h, w["moba"], [("f32", bw), ("f32", 2 * bw)],
                                [(0, bw, ((0, 0),)), (bw, 3 * bw, ((1, 0),))], tm, "proj_moba_s")
    nsa_q, nsa_kv, nsa_win = proj_call(
        h, w["nsa"], [("f32", bw), ("f32", 4 * kv), ("f32", 2 * kv)],
        [(0, bw, ((0, 0),)), (bw, bw + 4 * kv, ((1, 0),)), (bw + 4 * kv, bw + 6 * kv, ((2, 0),))], tm, "proj_nsa_s")
    (z_small,) = proj_call(h, w["small"], [("f32", SMALL_W)], [(0, SMALL_W, ((0, 0),))], tm, "proj_small_s")

    def heads(a, nh):
        return a.reshape(n, t, nh, HEAD_DIM)

    def cat(old, new):
        return jnp.concatenate([old.astype(new.dtype), new], axis=1)

    zc = z_conv.reshape(n, t, 3 * bw)
    conv_x, conv_b, conv_c = zc[..., :bw], zc[..., bw:2 * bw], zc[..., 2 * bw:]
    u = conv_c * conv_x
    ext = jnp.concatenate([past['conv'].astype(dt), u], axis=1)
    y_conv = ext[:, 0:t] * conv_w[0]
    for j in range(1, CONV_W):
        y_conv = y_conv + ext[:, j:j + t] * conv_w[j]
    out_a = conv_b * y_conv
    new_conv = ext[:, -(CONV_W - 1):]

    fkv = fox_kv.reshape(n, t, 2, N_HEADS, HEAD_DIM)
    fq, fk, fv = heads(fox_q, N_HEADS), fkv[:, :, 0], fkv[:, :, 1]
    fox_f = z_small.reshape(n, t, SMALL_W)[..., :N_HEADS]
    logf = jax.nn.log_sigmoid(fox_f + b_forget)
    lf_all = jnp.concatenate([past['fox_logf'].astype(F32), logf], axis=1)
    out_b = fox_attention(fq, cat(past['fox_k'], fk), cat(past['fox_v'], fv), lf_all, q0)

    mkv = moba_kv.reshape(n, t, 2, N_HEADS, HEAD_DIM)
    mq, mk, mv = heads(moba_q, N_HEADS), mkv[:, :, 0], mkv[:, :, 1]
    out_c = moba_attention(mq, cat(past['moba_k'], mk), cat(past['moba_v'], mv), q0, alibi_slopes(N_HEADS))

    nq = heads(nsa_q, N_HEADS)
    nkv = nsa_kv.reshape(n, t, 2, 2, NSA_G, HEAD_DIM)
    kc, vc, ks, vs = nkv[:, :, 0, 0], nkv[:, :, 0, 1], nkv[:, :, 1, 0], nkv[:, :, 1, 1]
    nwin = nsa_win.reshape(n, t, 2, NSA_G, HEAD_DIM)
    kw, vw = nwin[:, :, 0], nwin[:, :, 1]
    ngate = jax.nn.sigmoid(z_small.reshape(n, t, SMALL_W)[..., N_HEADS:4 * N_HEADS]).reshape(n, t, NSA_G, NSA_R, 3)
    kw_all = cat(past['win_k'], kw)
    vw_all = cat(past['win_v'], vw)
    w_ofs = q0 - past['win_k'].shape[1]
    out_d = nsa_attention(nq, cat(past['nsa_kc'], kc), cat(past['nsa_vc'], vc),
                          cat(past['nsa_ks'], ks), cat(past['nsa_vs'], vs),
                          kw_all, vw_all, ngate, q0, w_ofs, alibi_slopes(N_HEADS), cmp_w1, cmp_w2, cmp_pe)
    branches = [a.reshape(n * t, bw) for a in (out_a, out_b, out_c, out_d)]
    wb = past['win_k'].shape[1]
    new_state = (new_conv, fkv, logf.astype(dt), mkv, nkv,
                 jnp.stack([kw_all, vw_all], axis=2)[:, -wb:])
    return branches, new_state


def decode_mixers(h, n, nt, pt, caches, state_conv, state_win, w, b_forget, conv_w, cmp_w, tm):
    bw = BRANCH_W
    kv = KV_W
    cache_fox, cache_lf, cache_moba, cache_nsa = caches
    (z_conv,) = proj_call(h, w["conv"], [("f32", 3 * bw)], [(0, 3 * bw, ((0, 0),))], tm, "proj_conv_s")
    fox_q, fox_kv = proj_call(h, w["fox"], [("f32", bw), ("f32", 2 * bw)],
                              [(0, bw, ((0, 0),)), (bw, 3 * bw, ((1, 0),))], tm, "proj_fox_s")
    moba_q, moba_kv = proj_call(h, w["moba"], [("f32", bw), ("f32", 2 * bw)],
                                [(0, bw, ((0, 0),)), (bw, 3 * bw, ((1, 0),))], tm, "proj_moba_s")
    nsa_q, nsa_kv, nsa_win = proj_call(
        h, w["nsa"], [("f32", bw), ("f32", 4 * kv), ("f32", 2 * kv)],
        [(0, bw, ((0, 0),)), (bw, bw + 4 * kv, ((1, 0),)), (bw + 4 * kv, bw + 6 * kv, ((2, 0),))], tm, "proj_nsa_s")
    (z_small,) = proj_call(h, w["small"], [("f32", SMALL_W)], [(0, SMALL_W, ((0, 0),))], tm, "proj_small_s")

    def new_page(a):
        at = jnp.transpose(a.reshape(n, nt, a.shape[-1]), (0, 2, 1))
        return jnp.pad(at, ((0, 0), (0, 0), (0, PAGE - nt)))

    out_a, u = conv_sample_call(z_conv, state_conv, conv_w, n, nt)
    new_conv = u.reshape(n, nt, bw)[:, nt - (CONV_W - 1):]

    lg, _ = small_call(z_small, b_forget, 1, n * nt, n * nt)
    logf = lg[:, :N_HEADS]
    gate = lg[:, N_HEADS:4 * N_HEADS].reshape(n, nt * N_HEADS, 3)

    out_b = fox_decode_t_call(pt, fox_q.reshape(n, nt, bw), cache_fox, cache_lf, new_page(fox_kv), new_page(logf))
    mq = moba_q.reshape(n, nt, bw)
    sel = moba_select_t_call(pt, mq, cache_moba)
    out_c = moba_decode_t_call(pt, mq, sel, cache_moba, new_page(moba_kv))

    npages = pt.shape[1]
    kcmp, vcmp, _ = compress_t_call(cache_nsa, pt, *cmp_w, n, npages)
    win = state_win.reshape(n, state_win.shape[1], 2 * kv)
    out_d = nsa_decode_t_call(pt, nsa_q.reshape(n, nt, bw), gate, kcmp, vcmp, cache_nsa,
                              new_page(nsa_kv[:, 2 * kv:]), jnp.transpose(win, (0, 2, 1)), new_page(nsa_win))

    wb = win.shape[1]
    win_all = jnp.concatenate([win, nsa_win.reshape(n, nt, 2 * kv)], axis=1)[:, nt:]
    new_state = (new_conv,
                 fox_kv.reshape(n, nt, 2, N_HEADS, HEAD_DIM),
                 logf.reshape(n, nt, N_HEADS),
                 moba_kv.reshape(n, nt, 2, N_HEADS, HEAD_DIM),
                 nsa_kv.reshape(n, nt, 2, 2, NSA_G, HEAD_DIM),
                 win_all.reshape(n, wb, 2, NSA_G, HEAD_DIM))
    branches = [out_a] + [o.reshape(n * nt, bw) for o in (out_b, out_c, out_d)]
    return branches, new_state


def kernel(x_prompt, x_sample, state_conv, cache_fox_kv, cache_fox_logf, cache_moba_kv, cache_nsa_kv,
           state_nsa_win, page_table, g_mix, w_in, b_forget, conv_w, cmp_w1, cmp_w2, cmp_pe,
           w_branch, w_out, g_mlp, w_up, w_down, g_final):
    depth = w_in.shape[0]
    nb, seq, _ = x_prompt.shape
    db, dseq, _ = x_sample.shape
    tm_p, tm_s = 512, db * dseq
    pool, page = cache_fox_kv.shape[1], cache_fox_kv.shape[2]
    def pages_t(c, width):
        return jnp.transpose(c.reshape(depth * pool, page, width), (0, 2, 1))

    caches = (pages_t(cache_fox_kv, 2 * BRANCH_W), pages_t(cache_fox_logf, N_HEADS),
              pages_t(cache_moba_kv, 2 * BRANCH_W), pages_t(cache_nsa_kv, 4 * KV_W))

    xp = x_prompt.reshape(nb * seq, D_MODEL)
    xs = x_sample.reshape(db * dseq, D_MODEL)
    hp = rms_norm_call(xp, g_mix[0], BF16, tm_p)
    hs = rms_norm_call(xs, g_mix[0], BF16, tm_s)
    new_p, new_s = [], []
    for l in range(depth):
        w = _layer_weights(l, w_in, w_branch, w_out, w_up, w_down)
        cmp_w = _compress_weights(cmp_w1[l], cmp_w2[l], cmp_pe[l])
        last = l == depth - 1
        g_next = g_final if last else g_mix[l + 1]
        next_dtype = F32 if last else BF16

        branches, st_p = prompt_mixers(hp, nb, seq, w, b_forget[l], conv_w[l], cmp_w, tm_p)
        xp, hp = _finish_layer(xp, hp, branches, w, g_mlp[l], g_next, next_dtype, tm_p)
        new_p.append(st_p)

        branches, st_s = decode_mixers(hs, db, dseq, page_table + l * pool, caches, state_conv[l],
                                       state_nsa_win[l], w, b_forget[l], conv_w[l], cmp_w, tm_s)
        xs, hs = _finish_layer(xs, hs, branches, w, g_mlp[l], g_next, next_dtype, tm_s)
        new_s.append(st_s)
    y_prompt = hp.reshape(nb, seq, D_MODEL)
    y_sample = hs.reshape(db, dseq, D_MODEL)
    conv_p, fox_kv_p, fox_logf_p, moba_kv_p, nsa_kv_p, win_p = [jnp.stack(a) for a in zip(*new_p)]
    conv_s, fox_kv_s, fox_logf_s, moba_kv_s, nsa_kv_s, win_s = [jnp.stack(a) for a in zip(*new_s)]
    return (y_prompt, y_sample, conv_p, conv_s, fox_kv_p, fox_kv_s, fox_logf_p, fox_logf_s,
            moba_kv_p, moba_kv_s, nsa_kv_p, nsa_kv_s, win_p, win_s)
```

```python
import functools

import jax
import jax.numpy as jnp
from jax import lax
from jax.experimental import pallas as pl
from jax.experimental.pallas import tpu as pltpu

F32 = jnp.float32
BF16 = jnp.bfloat16
HIGHEST = lax.Precision.HIGHEST

D_MODEL = 2048
HEAD_DIM = 64
N_BRANCH = 4
BRANCH_W = D_MODEL // N_BRANCH
N_HEADS = BRANCH_W // HEAD_DIM
CONV_W = 3
NSA_G = 2
NSA_R = N_HEADS // NSA_G
MOBA_BLOCK = 256
MOBA_TOPK = 3
CMP_L = 32
CMP_S = 16
CMP_HID = 4 * HEAD_DIM
SLC_BLOCK = 64
SLC_N = 16
WINDOW = 512
D_FF = 4 * D_MODEL
Q_BLOCK = 128
SPARSE_Q_BLOCK = 32
RMS_EPS = 1e-6
NEG = -1e30
MASKED = 2.0 * NEG
FORCE_SCORE = 1e4
KV_W = NSA_G * HEAD_DIM
SCALE = HEAD_DIM ** -0.5
LOG2E = 1.4426950408889634
QSCALE2 = SCALE * LOG2E
PAIR_W = 2 * HEAD_DIM

OFF_CONV = 0
OFF_FOX = 3 * BRANCH_W
OFF_FOX_F = OFF_FOX + 3 * BRANCH_W
OFF_MOBA = OFF_FOX_F + N_HEADS
OFF_NSA = OFF_MOBA + 3 * BRANCH_W
OFF_NSA_GATE = OFF_NSA + BRANCH_W + 6 * KV_W
OFF_MERGE = OFF_NSA_GATE + 3 * N_HEADS
IN_W = OFF_MERGE + N_BRANCH * D_MODEL
SMALL_W = 128

TQ = 256
TK = 256
CMP_ROW = CMP_S * 2 * KV_W
PAGE_ROWS = 8

VMEM_LIMIT = 56 * 1024 * 1024


def _cparams(*sem):
    return pltpu.CompilerParams(dimension_semantics=sem, vmem_limit_bytes=VMEM_LIMIT)


def _rms(x, g):
    return x * lax.rsqrt(jnp.mean(x * x, axis=-1, keepdims=True) + RMS_EPS) * g


def _dot_nt(a, b, precision=None):
    return lax.dot_general(a, b, (((1,), (1,)), ((), ())), preferred_element_type=F32, precision=precision)


def _norm_kernel(x_ref, g_ref, o_ref):
    o_ref[...] = _rms(x_ref[...], g_ref[...]).astype(o_ref.dtype)


def rms_norm_call(x, g, out_dtype, tm):
    t, d = x.shape
    return pl.pallas_call(
        _norm_kernel,
        grid=(t // tm,),
        in_specs=[pl.BlockSpec((tm, d), lambda i: (i, 0)),
                  pl.BlockSpec((1, d), lambda i: (0, 0))],
        out_specs=pl.BlockSpec((tm, d), lambda i: (i, 0)),
        out_shape=jax.ShapeDtypeStruct((t, d), out_dtype),
        compiler_params=_cparams("parallel"),
        name="rms_norm",
    )(x, g.reshape(1, d))


def _proj_kernel(h_ref, w_ref, *out_refs, kinds, plan):
    h = h_ref[...]
    for c0, c1, dests in plan:
        z = jnp.dot(h, w_ref[:, c0:c1], preferred_element_type=F32)
        for idx, off in dests:
            o_ref, kind = out_refs[idx], kinds[idx]
            if kind == "bf16T":
                for r in range(z.shape[0] // TK):
                    o_ref[r, off:off + c1 - c0, :] = z[r * TK:(r + 1) * TK].T.astype(BF16)
            elif kind == "f32T":
                for r in range(z.shape[0] // 128):
                    o_ref[r, off:off + c1 - c0, :] = z[r * 128:(r + 1) * 128].T
            elif kind == "blockmean":
                for r in range(z.shape[0] // MOBA_BLOCK):
                    o_ref[r, :, off:off + c1 - c0] = jnp.mean(
                        z[r * MOBA_BLOCK:(r + 1) * MOBA_BLOCK], axis=0, keepdims=True)
            else:
                o_ref[:, off:off + c1 - c0] = z.astype(o_ref.dtype)


def proj_call(h, w, out_defs, plan, tm, name):
    t, d = h.shape
    n = w.shape[1]
    out_specs, out_shapes = [], []
    for kind, width in out_defs:
        if kind == "bf16T":
            out_specs.append(pl.BlockSpec((tm // TK, width, TK), lambda i: (i, 0, 0)))
            out_shapes.append(jax.ShapeDtypeStruct((t // TK, width, TK), BF16))
        elif kind == "f32T":
            out_specs.append(pl.BlockSpec((tm // 128, width, 128), lambda i: (i, 0, 0)))
            out_shapes.append(jax.ShapeDtypeStruct((t // 128, width, 128), F32))
        elif kind == "blockmean":
            out_specs.append(pl.BlockSpec((tm // MOBA_BLOCK, 1, width), lambda i: (i, 0, 0)))
            out_shapes.append(jax.ShapeDtypeStruct((t // MOBA_BLOCK, 1, width), F32))
        else:
            out_specs.append(pl.BlockSpec((tm, width), lambda i: (i, 0)))
            out_shapes.append(jax.ShapeDtypeStruct((t, width), BF16 if kind == "bf16" else F32))
    return pl.pallas_call(
        functools.partial(_proj_kernel, kinds=tuple(k for k, _ in out_defs), plan=tuple(plan)),
        grid=(t // tm,),
        in_specs=[pl.BlockSpec((tm, d), lambda i: (i, 0)),
                  pl.BlockSpec((d, n), lambda i: (0, 0))],
        out_specs=out_specs,
        out_shape=out_shapes,
        compiler_params=_cparams("parallel"),
        name=name,
    )(h, w)


def _gate_kernel(h_ref, w_ref, o_ref):
    z = jnp.dot(h_ref[...], w_ref[...], preferred_element_type=F32)
    o_ref[...] = jax.nn.sigmoid(z)


def gate_call(h, w, tm, tn):
    t, d = h.shape
    n = w.shape[1]
    return pl.pallas_call(
        _gate_kernel,
        grid=(t // tm, n // tn),
        in_specs=[pl.BlockSpec((tm, d), lambda i, j: (i, 0)),
                  pl.BlockSpec((d, tn), lambda i, j: (0, j))],
        out_specs=pl.BlockSpec((tm, tn), lambda i, j: (i, j)),
        out_shape=jax.ShapeDtypeStruct((t, n), F32),
        compiler_params=_cparams("parallel", "arbitrary"),
        name="merge_gate_proj",
    )(h, w)


def _merge_kernel(h_ref, oa_ref, ob_ref, oc_ref, od_ref, g0_ref, g1_ref, g2_ref, g3_ref, wb_ref, o_ref):
    h = h_ref[...]
    acc = None
    for b, (o, wg) in enumerate(zip((oa_ref, ob_ref, oc_ref, od_ref), (g0_ref, g1_ref, g2_ref, g3_ref))):
        gate = jax.nn.sigmoid(jnp.dot(h, wg[...], preferred_element_type=F32))
        br = jnp.dot(o[...].astype(BF16), wb_ref[b], preferred_element_type=F32)
        term = gate * br
        acc = term if acc is None else acc + term
    o_ref[...] = acc.astype(o_ref.dtype)


def merge_call(h, branches, w_gate, w_branch, tm, tn):
    t, d = h.shape
    nj = D_MODEL // tn
    gate_specs = [pl.BlockSpec((d, tn), functools.partial(lambda i, j, b: (0, b * nj + j), b=b))
                  for b in range(N_BRANCH)]
    return pl.pallas_call(
        _merge_kernel,
        grid=(t // tm, nj),
        in_specs=[pl.BlockSpec((tm, d), lambda i, j: (i, 0))]
        + [pl.BlockSpec((tm, BRANCH_W), lambda i, j: (i, 0))] * N_BRANCH + gate_specs
        + [pl.BlockSpec((N_BRANCH, BRANCH_W, tn), lambda i, j: (0, 0, j))],
        out_specs=pl.BlockSpec((tm, tn), lambda i, j: (i, j)),
        out_shape=jax.ShapeDtypeStruct((t, D_MODEL), BF16),
        compiler_params=_cparams("parallel", "arbitrary"),
        name="branch_merge",
    )(h, *branches, w_gate, w_gate, w_gate, w_gate, w_branch)


def _outproj_kernel(m_ref, w_ref, x_ref, g_ref, xo_ref, hn_ref):
    xn = x_ref[...] + jnp.dot(m_ref[...], w_ref[...], preferred_element_type=F32)
    xo_ref[...] = xn
    hn_ref[...] = _rms(xn, g_ref[...]).astype(hn_ref.dtype)


def outproj_call(merged, w_out, x, g_next, tm):
    t = x.shape[0]
    return pl.pallas_call(
        _outproj_kernel,
        grid=(t // tm,),
        in_specs=[pl.BlockSpec((tm, D_MODEL), lambda i: (i, 0)),
                  pl.BlockSpec((D_MODEL, D_MODEL), lambda i: (0, 0)),
                  pl.BlockSpec((tm, D_MODEL), lambda i: (i, 0)),
                  pl.BlockSpec((1, D_MODEL), lambda i: (0, 0))],
        out_specs=[pl.BlockSpec((tm, D_MODEL), lambda i: (i, 0)),
                   pl.BlockSpec((tm, D_MODEL), lambda i: (i, 0))],
        out_shape=[jax.ShapeDtypeStruct((t, D_MODEL), F32),
                   jax.ShapeDtypeStruct((t, D_MODEL), BF16)],
        compiler_params=_cparams("parallel"),
        name="out_proj",
    )(merged, w_out, x, g_next.reshape(1, D_MODEL))


def _mlp_kernel(h_ref, wu_ref, wd_ref, x_ref, g_ref, xo_ref, hn_ref, acc_ref):
    j = pl.program_id(1)

    @pl.when(j == 0)
    def _():
        acc_ref[...] = jnp.zeros_like(acc_ref)

    a = jnp.dot(h_ref[...], wu_ref[...], preferred_element_type=F32)
    a = jnp.square(jnp.maximum(a, 0.0)).astype(BF16)
    acc_ref[...] += jnp.dot(a, wd_ref[...], preferred_element_type=F32)

    @pl.when(j == pl.num_programs(1) - 1)
    def _():
        xn = x_ref[...] + acc_ref[...]
        xo_ref[...] = xn
        hn_ref[...] = _rms(xn, g_ref[...]).astype(hn_ref.dtype)


def mlp_call(h, w_up, w_down, x, g_next, next_dtype, tm, tf):
    t = x.shape[0]
    return pl.pallas_call(
        _mlp_kernel,
        grid=(t // tm, D_FF // tf),
        in_specs=[pl.BlockSpec((tm, D_MODEL), lambda i, j: (i, 0)),
                  pl.BlockSpec((D_MODEL, tf), lambda i, j: (0, j)),
                  pl.BlockSpec((tf, D_MODEL), lambda i, j: (j, 0)),
                  pl.BlockSpec((tm, D_MODEL), lambda i, j: (i, 0)),
                  pl.BlockSpec((1, D_MODEL), lambda i, j: (0, 0))],
        out_specs=[pl.BlockSpec((tm, D_MODEL), lambda i, j: (i, 0)),
                   pl.BlockSpec((tm, D_MODEL), lambda i, j: (i, 0))],
        out_shape=[jax.ShapeDtypeStruct((t, D_MODEL), F32),
                   jax.ShapeDtypeStruct((t, D_MODEL), next_dtype)],
        scratch_shapes=[pltpu.VMEM((tm, D_MODEL), F32)],
        compiler_params=_cparams("parallel", "arbitrary"),
        name="mlp",
    )(h, w_up, w_down, x, g_next.reshape(1, D_MODEL))


def _small_kernel(z_ref, b_ref, a_ref, f_ref, carry_ref):
    @pl.when(pl.program_id(1) == 0)
    def _():
        carry_ref[...] = jnp.zeros_like(carry_ref)

    z = z_ref[...]
    tm = z.shape[0]
    lane = lax.broadcasted_iota(jnp.int32, z.shape, 1)
    pre = z + b_ref[...]
    lf = jnp.minimum(pre, 0.0) - jnp.log1p(jnp.exp(-jnp.abs(pre)))
    lf = jnp.where(lane < N_HEADS, lf, 0.0)
    a_ref[...] = jnp.where(lane < N_HEADS, lf, jnp.where(lane < 4 * N_HEADS, jax.nn.sigmoid(z), 0.0))
    row = lax.broadcasted_iota(jnp.int32, (tm, tm), 0)
    col = lax.broadcasted_iota(jnp.int32, (tm, tm), 1)
    tril = jnp.where(col <= row, 1.0, 0.0)
    f = jnp.dot(tril, lf, preferred_element_type=F32, precision=HIGHEST) + carry_ref[0:1, :]
    f_ref[...] = f
    carry_ref[0:1, :] = f[tm - 1:tm, :]


def small_call(z_small, b_forget, n, t, tm):
    bias = jnp.zeros((1, SMALL_W), F32).at[0, :N_HEADS].set(b_forget)
    nt = t // tm
    return pl.pallas_call(
        _small_kernel,
        grid=(n, nt),
        in_specs=[pl.BlockSpec((tm, SMALL_W), lambda b, j: (b * nt + j, 0)),
                  pl.BlockSpec((1, SMALL_W), lambda b, j: (0, 0))],
        out_specs=[pl.BlockSpec((tm, SMALL_W), lambda b, j: (b * nt + j, 0)),
                   pl.BlockSpec((tm, SMALL_W), lambda b, j: (b * nt + j, 0))],
        out_shape=[jax.ShapeDtypeStruct((n * t, SMALL_W), F32),
                   jax.ShapeDtypeStruct((n * t, SMALL_W), F32)],
        scratch_shapes=[pltpu.VMEM((8, SMALL_W), F32)],
        compiler_params=_cparams("parallel", "arbitrary"),
        name="forget_and_gates",
    )(z_small, bias)


def _conv_prompt_kernel(z_ref, w_ref, o_ref, st_ref, prev_ref):
    @pl.when(pl.program_id(1) == 0)
    def _():
        prev_ref[...] = jnp.zeros_like(prev_ref)

    bw = BRANCH_W
    u = z_ref[:, 2 * bw:3 * bw] * z_ref[:, 0:bw]
    tm = u.shape[0]
    row = lax.broadcasted_iota(jnp.int32, u.shape, 0)
    u1 = jnp.where(row == 0, prev_ref[7:8, :], pltpu.roll(u, 1, 0))
    u2 = jnp.where(row == 0, prev_ref[6:7, :], jnp.where(row == 1, prev_ref[7:8, :], pltpu.roll(u, 2, 0)))
    y = u2 * w_ref[0:1, :] + u1 * w_ref[1:2, :] + u * w_ref[2:3, :]
    o_ref[...] = z_ref[:, bw:2 * bw] * y
    prev_ref[...] = u[tm - 8:tm]
    st_ref[0] = u[tm - 2:tm]


def conv_prompt_call(z_conv, conv_w, n, t, tm):
    nt = t // tm
    return pl.pallas_call(
        _conv_prompt_kernel,
        grid=(n, nt),
        in_specs=[pl.BlockSpec((tm, 3 * BRANCH_W), lambda b, j: (b * nt + j, 0)),
                  pl.BlockSpec((CONV_W, BRANCH_W), lambda b, j: (0, 0))],
        out_specs=[pl.BlockSpec((tm, BRANCH_W), lambda b, j: (b * nt + j, 0)),
                   pl.BlockSpec((1, CONV_W - 1, BRANCH_W), lambda b, j: (b, 0, 0))],
        out_shape=[jax.ShapeDtypeStruct((n * t, BRANCH_W), F32),
                   jax.ShapeDtypeStruct((n, CONV_W - 1, BRANCH_W), F32)],
        scratch_shapes=[pltpu.VMEM((8, BRANCH_W), F32)],
        compiler_params=_cparams("parallel", "arbitrary"),
        name="conv_prompt",
    )(z_conv, conv_w)


def _online_step(s, vt, carry):
    m, l, acc = carry
    m_new = jnp.maximum(m, jnp.max(s, axis=0, keepdims=True))
    alpha = jnp.exp2(m - m_new)
    p = jnp.exp2(s - m_new)
    l = alpha * l + jnp.sum(p, axis=0, keepdims=True)
    acc = alpha * acc + jnp.dot(vt, p.astype(BF16), preferred_element_type=F32)
    return m_new, l, acc


def _softmax_init(width=TQ):
    return (jnp.full((1, width), NEG, F32), jnp.zeros((1, width), F32), jnp.zeros((HEAD_DIM, width), F32))


def _query_pair(q_ref, h, half):
    hp, e = divmod(h, 2)
    qp = q_ref[:, hp * PAIR_W:(hp + 1) * PAIR_W]
    if e != half:
        qp = pltpu.roll(qp, HEAD_DIM, 1)
    lane = lax.broadcasted_iota(jnp.int32, qp.shape, 1)
    return jnp.where((lane // HEAD_DIM) == half, qp, 0.0)


def _tile_masks():
    sub = lax.broadcasted_iota(jnp.int32, (TK, TQ), 0)
    lane = lax.broadcasted_iota(jnp.int32, (TK, TQ), 1)
    return sub <= lane, sub > lane


def _key_col():
    return lax.broadcasted_iota(jnp.int32, (TK, 1), 0).astype(F32)


def _fox_kernel(q_ref, k_ref, vt_ref, fk_ref, o_ref, ot_ref):
    i = pl.program_id(1)
    causal, _ = _tile_masks()
    qcats = [jnp.concatenate([(_query_pair(q_ref, 2 * hp + e, e) * QSCALE2).astype(BF16) for e in (0, 1)], axis=0)
             for hp in range(N_HEADS // 2)]

    def tile(j, carry, diag):
        r0 = pl.multiple_of(j * TK, TK)
        out = []
        for hp in range(N_HEADS // 2):
            s2 = _dot_nt(k_ref[pl.ds(r0, TK), hp * PAIR_W:(hp + 1) * PAIR_W], qcats[hp])
            for e in (0, 1):
                h = 2 * hp + e
                s = s2[:, e * TQ:(e + 1) * TQ] - LOG2E * fk_ref[pl.ds(r0, TK), h:h + 1]
                if diag:
                    s = jnp.where(causal, s, MASKED)
                out.append(_online_step(s, vt_ref[j, h * HEAD_DIM:(h + 1) * HEAD_DIM, :], carry[h]))
        return tuple(out)

    carry = lax.fori_loop(0, i, lambda j, c: tile(j, c, False), tuple(_softmax_init() for _ in range(N_HEADS)))
    for h, (_, l, acc) in enumerate(tile(i, carry, True)):
        ot_ref[h * HEAD_DIM:(h + 1) * HEAD_DIM, :] = acc * (1.0 / l)
    o_ref[...] = ot_ref[...].T


def fox_prompt_call(q, kb, vt, fk, n, t):
    nq = t // TQ
    return pl.pallas_call(
        _fox_kernel,
        grid=(n, nq),
        in_specs=[pl.BlockSpec((TQ, BRANCH_W), lambda b, i: (b * nq + i, 0)),
                  pl.BlockSpec((t, BRANCH_W), lambda b, i: (b, 0)),
                  pl.BlockSpec((t // TK, BRANCH_W, TK), lambda b, i: (b, 0, 0)),
                  pl.BlockSpec((t, SMALL_W), lambda b, i: (b, 0))],
        out_specs=pl.BlockSpec((TQ, BRANCH_W), lambda b, i: (b * nq + i, 0)),
        out_shape=jax.ShapeDtypeStruct((n * t, BRANCH_W), F32),
        scratch_shapes=[pltpu.VMEM((BRANCH_W, TQ), F32)],
        compiler_params=_cparams("parallel", "arbitrary"),
        name="fox_prompt",
    )(q, kb, vt, fk)


def _rank_before(score, bidx, nblk):
    cnt = jnp.zeros(score.shape, F32)
    for b2 in range(nblk):
        row = score[b2:b2 + 1, :]
        beats = (row > score) | ((row == score) & (b2 < bidx))
        cnt = cnt + jnp.where(beats, 1.0, 0.0)
    return cnt


def _moba_kernel(q_ref, k_ref, vt_ref, km_ref, o_ref, ot_ref, sel_ref, *, nblk):
    i = pl.program_id(1)
    causal, _ = _tile_masks()
    kcol = _key_col()
    bidx = lax.broadcasted_iota(jnp.int32, (nblk, TQ), 0)
    qcats = []
    for hp in range(N_HEADS // 2):
        qpads = []
        for e in (0, 1):
            h = 2 * hp + e
            q32 = _query_pair(q_ref, h, e)
            qpads.append((q32 * QSCALE2).astype(BF16))
            gs = _dot_nt(km_ref[0, :, hp * PAIR_W:(hp + 1) * PAIR_W], q32, precision=HIGHEST)
            gs = jnp.where(bidx < i, gs, NEG)
            sel = (_rank_before(gs, bidx, nblk) < MOBA_TOPK) & (bidx < i)
            sel_ref[:, h * TQ:(h + 1) * TQ] = jnp.where(sel, 1.0, 0.0)
        qcats.append(jnp.concatenate(qpads, axis=0))

    def tile(j, carry, diag):
        r0 = pl.multiple_of(j * TK, TK)
        dist = kcol - ((i - j) * TK).astype(F32)
        out = []
        for hp in range(N_HEADS // 2):
            s2 = _dot_nt(k_ref[pl.ds(r0, TK), hp * PAIR_W:(hp + 1) * PAIR_W], qcats[hp])
            for e in (0, 1):
                h = 2 * hp + e
                s = s2[:, e * TQ:(e + 1) * TQ] + LOG2E * 2.0 ** (-8.0 * (h + 1) / N_HEADS) * dist
                if diag:
                    s = jnp.where(causal, s, MASKED)
                else:
                    s = jnp.where(sel_ref[pl.ds(j, 1), h * TQ:(h + 1) * TQ] > 0.5, s, MASKED)
                out.append(_online_step(s, vt_ref[j, h * HEAD_DIM:(h + 1) * HEAD_DIM, :], carry[h]))
        return tuple(out)

    carry = lax.fori_loop(0, i, lambda j, c: tile(j, c, False), tuple(_softmax_init() for _ in range(N_HEADS)))
    for h, (_, l, acc) in enumerate(tile(i, carry, True)):
        ot_ref[h * HEAD_DIM:(h + 1) * HEAD_DIM, :] = acc * (1.0 / l)
    o_ref[...] = ot_ref[...].T


def moba_prompt_call(q, kb, vt, kmean, n, t):
    nq = t // TQ
    nblk = t // MOBA_BLOCK
    return pl.pallas_call(
        functools.partial(_moba_kernel, nblk=nblk),
        grid=(n, nq),
        in_specs=[pl.BlockSpec((TQ, BRANCH_W), lambda b, i: (b * nq + i, 0)),
                  pl.BlockSpec((t, BRANCH_W), lambda b, i: (b, 0)),
                  pl.BlockSpec((t // TK, BRANCH_W, TK), lambda b, i: (b, 0, 0)),
                  pl.BlockSpec((1, nblk, BRANCH_W), lambda b, i: (b, 0, 0))],
        out_specs=pl.BlockSpec((TQ, BRANCH_W), lambda b, i: (b * nq + i, 0)),
        out_shape=jax.ShapeDtypeStruct((n * t, BRANCH_W), F32),
        scratch_shapes=[pltpu.VMEM((BRANCH_W, TQ), F32), pltpu.VMEM((nblk, N_HEADS * TQ), F32)],
        compiler_params=_cparams("parallel", "arbitrary"),
        name="moba_prompt",
    )(q, kb, vt, kmean.reshape(n, nblk, BRANCH_W))


def _gelu_tanh(x):
    return 0.5 * x * (1.0 + jnp.tanh(0.7978845608028654 * (x + 0.044715 * x * x * x)))


def _compress_kernel(pt_ref, a_ref, pet_ref, peb_ref, wt_ref, wb_ref, w2k_ref, w2v_ref, w2vt_ref,
                     kc_ref, vc_ref, vct_ref, rows_ref, *, rows):
    del pt_ref
    j = pl.program_id(1)
    r0 = pl.multiple_of(j * PAGE_ROWS, PAGE_ROWS)
    for i in range(CMP_S):
        rows_ref[pl.ds(r0, PAGE_ROWS), i * 2 * KV_W:(i + 1) * 2 * KV_W] = a_ref[0, :, i * 4 * KV_W:i * 4 * KV_W + 2 * KV_W]

    @pl.when(j == pl.num_programs(1) - 1)
    def _():
        chunk = min(256, rows)
        prev_top = jnp.zeros((1, 4 * CMP_HID), F32)
        for c in range(rows // chunk):
            a = rows_ref[c * chunk:(c + 1) * chunk, :]
            ht = jnp.dot((a + pet_ref[...]).astype(BF16), wt_ref[...], preferred_element_type=F32)
            hb = jnp.dot((a + peb_ref[...]).astype(BF16), wb_ref[...], preferred_element_type=F32)
            row = lax.broadcasted_iota(jnp.int32, ht.shape, 0)
            shifted = jnp.where(row == 0, prev_top, pltpu.roll(ht, 1, 0))
            prev_top = ht[chunk - 1:chunk, :]
            g = _gelu_tanh(shifted + hb).astype(BF16)
            kc_ref[0, c * chunk:(c + 1) * chunk, :] = jnp.dot(
                g[:, :2 * CMP_HID], w2k_ref[...], preferred_element_type=F32).astype(BF16)
            vc_ref[0, c * chunk:(c + 1) * chunk, :] = jnp.dot(
                g[:, 2 * CMP_HID:], w2v_ref[...], preferred_element_type=F32).astype(BF16)
            vct_ref[0, :, c * chunk:(c + 1) * chunk] = _dot_nt(w2vt_ref[...], g[:, 2 * CMP_HID:]).astype(BF16)


def compress_call(pages, page_idx, pe_top, pe_bot, wt, wb, w2k, w2v, w2vt, n, npages):
    rows = npages * PAGE_ROWS
    const = lambda b, j, pt: (0, 0)
    grid_spec = pltpu.PrefetchScalarGridSpec(
        num_scalar_prefetch=1,
        grid=(n, npages),
        in_specs=[pl.BlockSpec((1, PAGE_ROWS, CMP_S * 4 * KV_W), lambda b, j, pt: (pt[b, j], 0, 0)),
                  pl.BlockSpec((1, CMP_ROW), const),
                  pl.BlockSpec((1, CMP_ROW), const),
                  pl.BlockSpec((CMP_ROW, 4 * CMP_HID), const, pipeline_mode=pl.Buffered(1)),
                  pl.BlockSpec((CMP_ROW, 4 * CMP_HID), const, pipeline_mode=pl.Buffered(1)),
                  pl.BlockSpec((2 * CMP_HID, KV_W), const),
                  pl.BlockSpec((2 * CMP_HID, KV_W), const),
                  pl.BlockSpec((KV_W, 2 * CMP_HID), const)],
        out_specs=[pl.BlockSpec((1, rows, KV_W), lambda b, j, pt: (b, 0, 0)),
                   pl.BlockSpec((1, rows, KV_W), lambda b, j, pt: (b, 0, 0)),
                   pl.BlockSpec((1, KV_W, rows), lambda b, j, pt: (b, 0, 0))],
        scratch_shapes=[pltpu.VMEM((rows, CMP_ROW), F32)],
    )
    return pl.pallas_call(
        functools.partial(_compress_kernel, rows=rows),
        grid_spec=grid_spec,
        out_shape=[jax.ShapeDtypeStruct((n, rows, KV_W), BF16),
                   jax.ShapeDtypeStruct((n, rows, KV_W), BF16),
                   jax.ShapeDtypeStruct((n, KV_W, rows), BF16)],
        compiler_params=_cparams("parallel", "arbitrary"),
        name="nsa_compress",
    )(page_idx, pages, pe_top, pe_bot, wt, wb, w2k, w2v, w2vt)


def _nsa_kernel(q_ref, kc_ref, vct_ref, ks_ref, vst_ref, kw_ref, vwt_ref, gt_ref, o_ref,
                ot_ref, sel_ref, *, ncmp, nsb):
    i = pl.program_id(1)
    causal, below = _tile_masks()
    kcol = _key_col()
    lane_q = lax.broadcasted_iota(jnp.int32, (1, TQ), 1)
    qpos = i * TQ + lane_q
    crow = lax.broadcasted_iota(jnp.int32, (ncmp, 1), 0)
    cpos = CMP_S * crow + (CMP_S - 1)
    cvalid = (crow >= 1) & (cpos <= qpos)
    cposf = cpos.astype(F32)
    mj = lax.broadcasted_iota(jnp.int32, (nsb, ncmp), 0)
    mc = lax.broadcasted_iota(jnp.int32, (nsb, ncmp), 1)
    ratio = SLC_BLOCK // CMP_S
    gather_m = jnp.where((mc >= ratio * mj) & (mc <= ratio * mj + ratio) & (mc >= 1), 1.0, 0.0)
    bj = lax.broadcasted_iota(jnp.int32, (nsb, TQ), 0)
    own = qpos // SLC_BLOCK
    forced = (bj == 0) | (bj >= own - 1)
    allowed = bj <= own
    ecol = lax.broadcasted_iota(jnp.int32, (TK, nsb), 1)
    erow = lax.broadcasted_iota(jnp.int32, (TK, nsb), 0) // SLC_BLOCK
    gw = NSA_R * TQ
    sub4 = lax.broadcasted_iota(jnp.int32, (TK, gw), 0)
    lane4 = lax.broadcasted_iota(jnp.int32, (TK, gw), 1) % TQ
    causal4, below4 = sub4 <= lane4, sub4 > lane4
    cvalid4 = (crow >= 1) & (cpos <= i * TQ + lax.broadcasted_iota(jnp.int32, (1, gw), 1) % TQ)

    for g in range(NSA_G):
        heads = [g * NSA_R + r for r in range(NSA_R)]
        q4 = jnp.concatenate([(_query_pair(q_ref, h, g) * QSCALE2).astype(BF16) for h in heads], axis=0)
        slope_row = jnp.concatenate([jnp.full((1, TQ), LOG2E * 2.0 ** (-8.0 * (h + 1) / N_HEADS), F32) for h in heads],
                                    axis=1)
        gates = [jnp.concatenate([gt_ref[3 * h + c:3 * h + c + 1, :] for h in heads], axis=1) for c in range(3)]
        v_rows = slice(g * HEAD_DIM, (g + 1) * HEAD_DIM)
        s = _dot_nt(kc_ref[0], q4) + slope_row * (cposf - (i * TQ).astype(F32))
        s = jnp.where(cvalid4, s, MASKED)
        p = jnp.where(cvalid4, jnp.exp2(s - jnp.max(s, axis=0, keepdims=True)), 0.0)
        l = jnp.sum(p, axis=0, keepdims=True)
        p = p * (1.0 / jnp.where(l > 0.0, l, 1.0))
        imp = p[:, 0:TQ]
        for r in range(1, NSA_R):
            imp = imp + p[:, r * TQ:(r + 1) * TQ]
        o_all = gates[0] * jnp.dot(vct_ref[0, v_rows, :], p.astype(BF16), preferred_element_type=F32)
        p_slc = jnp.dot(gather_m, imp, preferred_element_type=F32, precision=HIGHEST)
        score = jnp.where(allowed, jnp.where(forced, p_slc + FORCE_SCORE, p_slc), -1.0)
        sel = jnp.where((_rank_before(score, bj, nsb) < SLC_N) & allowed, 1.0, 0.0)
        sel_ref[...] = jnp.concatenate([sel] * NSA_R, axis=1).astype(BF16)

        def alibi(j):
            return slope_row * (kcol - ((i - j) * TK).astype(F32))

        def slc_tile(j, carry, diag):
            r0 = pl.multiple_of(j * TK, TK)
            expand = jnp.where(ecol == erow + j * (TK // SLC_BLOCK), 1.0, 0.0).astype(BF16)
            keep = jnp.dot(expand, sel_ref[...], preferred_element_type=F32) > 0.5
            if diag:
                keep = keep & causal4
            s = jnp.where(keep, _dot_nt(ks_ref[pl.ds(r0, TK), :], q4) + alibi(j), MASKED)
            return _online_step(s, vst_ref[j, v_rows, :], carry)

        carry = lax.fori_loop(0, i, lambda j, c: slc_tile(j, c, False), _softmax_init(gw))
        _, l, acc = slc_tile(i, carry, True)
        o_all = o_all + gates[1] * (acc * (1.0 / l))

        def win_tile(j, carry, mask):
            r0 = pl.multiple_of(j * TK, TK)
            s = _dot_nt(kw_ref[pl.ds(r0, TK), :], q4) + alibi(j)
            if mask is not None:
                s = jnp.where(mask, s, MASKED)
            return _online_step(s, vwt_ref[j, v_rows, :], carry)

        carry = _softmax_init(gw)
        carry = lax.cond(i >= 2, lambda c: win_tile(i - 2, c, below4), lambda c: c, carry)
        carry = lax.cond(i >= 1, lambda c: win_tile(i - 1, c, None), lambda c: c, carry)
        _, l, acc = win_tile(i, carry, causal4)
        o_all = o_all + gates[2] * (acc * (1.0 / l))
        for r, h in enumerate(heads):
            ot_ref[h * HEAD_DIM:(h + 1) * HEAD_DIM, :] = o_all[:, r * TQ:(r + 1) * TQ]
    o_ref[...] = ot_ref[...].T


def nsa_prompt_call(q, kcmp, vcmpt, ks, vst, kw, vwt, gate_t, n, t):
    nq = t // TQ
    ncmp = kcmp.shape[1]
    nsb = t // SLC_BLOCK
    return pl.pallas_call(
        functools.partial(_nsa_kernel, ncmp=ncmp, nsb=nsb),
        grid=(n, nq),
        in_specs=[pl.BlockSpec((TQ, BRANCH_W), lambda b, i: (b * nq + i, 0)),
                  pl.BlockSpec((1, ncmp, KV_W), lambda b, i: (b, 0, 0)),
                  pl.BlockSpec((1, KV_W, ncmp), lambda b, i: (b, 0, 0)),
                  pl.BlockSpec((t, KV_W), lambda b, i: (b, 0)),
                  pl.BlockSpec((t // TK, KV_W, TK), lambda b, i: (b, 0, 0)),
                  pl.BlockSpec((t, KV_W), lambda b, i: (b, 0)),
                  pl.BlockSpec((t // TK, KV_W, TK), lambda b, i: (b, 0, 0)),
                  pl.BlockSpec((3 * N_HEADS, TQ), lambda b, i: (0, b * nq + i))],
        out_specs=pl.BlockSpec((TQ, BRANCH_W), lambda b, i: (b * nq + i, 0)),
        out_shape=jax.ShapeDtypeStruct((n * t, BRANCH_W), F32),
        scratch_shapes=[pltpu.VMEM((BRANCH_W, TQ), F32), pltpu.VMEM((nsb, NSA_R * TQ), BF16)],
        compiler_params=_cparams("parallel", "arbitrary"),
        name="nsa_prompt",
    )(q, kcmp, vcmpt, ks, vst, kw, vwt, gate_t)


def alibi_slopes(n):
    return jnp.exp2(-8.0 * jnp.arange(1, n + 1, dtype=jnp.float32) / n)


def masked_softmax(s, mask, axis=-1):
    p = jax.nn.softmax(jnp.where(mask, s, NEG), axis=axis)
    return p * mask


def sweep_queries(fn, block, *qs):
    n, t = qs[0].shape[:2]
    qb = block if t % block == 0 else t
    nb = t // qb
    xs = tuple(jnp.moveaxis(a.reshape((n, nb, qb) + a.shape[2:]), 1, 0) for a in qs)
    out = lax.map(lambda args: fn(args[0] * qb, *args[1:]), (jnp.arange(nb, dtype=jnp.int32),) + xs)
    return jnp.moveaxis(out, 0, 1).reshape((n, t) + out.shape[3:])


def gather_pages(pool, page_table):
    g = pool[page_table]
    return g.reshape((g.shape[0], g.shape[1] * g.shape[2]) + g.shape[3:])


def fox_attention(q, k, v, logf, q0):
    d = q.shape[-1]
    L = k.shape[1]
    scale = d ** -0.5
    F = jnp.cumsum(logf.astype(jnp.float32), axis=1)
    Fk = jnp.moveaxis(F, 1, 2)
    Fq = F[:, q0:]
    kpos = jnp.arange(L)

    def block(start, qb, fq):
        qpos = q0 + start + jnp.arange(qb.shape[1])
        s = jnp.einsum('nqhd,nkhd->nhqk', qb, k).astype(jnp.float32) * scale
        s = s + jnp.moveaxis(fq, 1, 2)[..., None] - Fk[:, :, None, :]
        p = masked_softmax(s, kpos[None, :] <= qpos[:, None])
        return jnp.einsum('nhqk,nkhd->nqhd', p.astype(q.dtype), v)

    return sweep_queries(block, Q_BLOCK, q, Fq)


def moba_attention(q, k, v, q0, slopes):
    n, tq, nh, d = q.shape
    L = k.shape[1]
    scale = d ** -0.5
    nbk = -(-L // MOBA_BLOCK)
    pad = nbk * MOBA_BLOCK - L

    def to_blocks(a):
        a = jnp.pad(a, ((0, 0), (0, pad), (0, 0), (0, 0)))
        return a.reshape(n, nbk, MOBA_BLOCK, nh, d).transpose(0, 3, 1, 2, 4)

    kb, vb = to_blocks(k), to_blocks(v)
    kmean = jnp.mean(kb.astype(jnp.float32), axis=3)
    topk = min(MOBA_TOPK, nbk)
    bidx = jnp.arange(nbk)
    ni = jnp.arange(n)[:, None, None, None]
    hi = jnp.arange(nh)[None, :, None, None]
    sl = slopes[None, :, None, None, None]

    def block(start, qb):
        m = qb.shape[1]
        qpos = q0 + start + jnp.arange(m)
        own = qpos // MOBA_BLOCK
        gs = jnp.einsum('nqhd,nhbd->nhqb', qb.astype(jnp.float32), kmean)
        gs = jnp.where(bidx[None, :] < own[:, None], gs, NEG)
        _, top = lax.top_k(gs, topk)
        sel_ok = top < own[:, None]
        idx = jnp.concatenate([top, jnp.broadcast_to(own[:, None], (n, nh, m, 1))], axis=-1)
        ok = jnp.concatenate([sel_ok, jnp.ones((n, nh, m, 1), bool)], axis=-1)
        kg = kb[ni, hi, idx]
        vg = vb[ni, hi, idx]
        kpos = idx[..., None] * MOBA_BLOCK + jnp.arange(MOBA_BLOCK)
        s = jnp.einsum('nqhd,nhqsjd->nhqsj', qb, kg).astype(jnp.float32) * scale
        s = s - sl * (qpos[:, None, None] - kpos).astype(jnp.float32)
        mask = ok[..., None] & (kpos <= qpos[:, None, None])
        p = masked_softmax(s, mask, axis=(-2, -1))
        return jnp.einsum('nhqsj,nhqsjd->nqhd', p.astype(q.dtype), vg)

    return sweep_queries(block, SPARSE_Q_BLOCK, q)


def compress_tokens(a, w1, w2, pe):
    n, L, g, d = a.shape
    nc = (L - CMP_L) // CMP_S + 1
    idx = jnp.arange(nc)[:, None] * CMP_S + jnp.arange(CMP_L)[None, :]
    blocks = a[:, idx] + pe[:, None, :].astype(a.dtype)
    flat = jnp.swapaxes(blocks, 2, 3).reshape(n, nc, g, CMP_L * d)
    return jax.nn.gelu(flat @ w1) @ w2


def nsa_attention(q, kc, vc, ks, vs, kw, vw, gate, q0, w_ofs, slopes, cmp_w1, cmp_w2, cmp_pe):
    n, tq, nh, d = q.shape
    L = kc.shape[1]
    dt = q.dtype
    scale = d ** -0.5
    qg = q.reshape(n, tq, NSA_G, NSA_R, d)
    k_cmp = compress_tokens(kc, cmp_w1[0], cmp_w2[0], cmp_pe[0])
    v_cmp = compress_tokens(vc, cmp_w1[1], cmp_w2[1], cmp_pe[1])
    nc = k_cmp.shape[1]
    cpos = jnp.arange(nc) * CMP_S + (CMP_L - 1)
    nsb = -(-L // SLC_BLOCK)
    padl = nsb * SLC_BLOCK - L

    def to_blocks(a):
        a = jnp.pad(a, ((0, 0), (0, padl), (0, 0), (0, 0)))
        return a.reshape(n, nsb, SLC_BLOCK, NSA_G, d).transpose(0, 3, 1, 2, 4)

    ks_b, vs_b = to_blocks(ks), to_blocks(vs)
    nsel = min(SLC_N, nsb)
    front = CMP_L // CMP_S - 1
    ratio = SLC_BLOCK // CMP_S
    width = ratio + front
    back = ratio * nsb + width - front - nc
    kw_p = jnp.pad(kw, ((0, 0), (WINDOW, 0), (0, 0), (0, 0)))
    vw_p = jnp.pad(vw, ((0, 0), (WINDOW, 0), (0, 0), (0, 0)))
    sl = slopes.reshape(NSA_G, NSA_R)[None, :, :, None, None]
    ni = jnp.arange(n)[:, None, None, None]
    gi = jnp.arange(NSA_G)[None, :, None, None]
    bj = jnp.arange(nsb)

    def block(start, qb, gb):
        m = qb.shape[1]
        qpos = q0 + start + jnp.arange(m)
        s = jnp.einsum('nqgrd,ncgd->ngrqc', qb, k_cmp).astype(jnp.float32) * scale
        s = s - sl * (qpos[:, None] - cpos[None, :]).astype(jnp.float32)
        p_cmp = masked_softmax(s, cpos[None, :] <= qpos[:, None])
        o_cmp = jnp.einsum('ngrqc,ncgd->nqgrd', p_cmp.astype(dt), v_cmp)
        imp = jnp.pad(p_cmp.sum(axis=2), ((0, 0), (0, 0), (0, 0), (front, back)))
        p_slc = imp[..., 0:ratio * nsb:ratio]
        for u in range(1, width):
            p_slc = p_slc + imp[..., u:u + ratio * nsb:ratio]
        own = qpos // SLC_BLOCK
        forced = (bj[None, :] == 0) | (bj[None, :] >= own[:, None] - 1)
        allowed = bj[None, :] <= own[:, None]
        score = jnp.where(allowed, jnp.where(forced, p_slc + FORCE_SCORE, p_slc), -1.0)
        _, top = lax.top_k(score, nsel)
        ok = top <= own[:, None]
        kg = ks_b[ni, gi, top]
        vg = vs_b[ni, gi, top]
        kpos = top[..., None] * SLC_BLOCK + jnp.arange(SLC_BLOCK)
        dist = (qpos[:, None, None] - kpos)[:, :, None].astype(jnp.float32)
        s2 = jnp.einsum('nqgrd,ngqsjd->ngrqsj', qb, kg).astype(jnp.float32) * scale - sl[..., None] * dist
        mask2 = (ok[..., None] & (kpos <= qpos[:, None, None]))[:, :, None]
        p2 = masked_softmax(s2, mask2, axis=(-2, -1))
        o_slc = jnp.einsum('ngrqsj,ngqsjd->nqgrd', p2.astype(dt), vg)
        off = q0 + start - w_ofs
        kwin = lax.dynamic_slice_in_dim(kw_p, off, WINDOW + m, axis=1)
        vwin = lax.dynamic_slice_in_dim(vw_p, off, WINDOW + m, axis=1)
        wpos = q0 + start - WINDOW + jnp.arange(WINDOW + m)
        s3 = jnp.einsum('nqgrd,nkgd->ngrqk', qb, kwin).astype(jnp.float32) * scale
        s3 = s3 - sl * (qpos[:, None] - wpos[None, :]).astype(jnp.float32)
        wmask = (wpos[None, :] <= qpos[:, None]) & (wpos[None, :] > qpos[:, None] - WINDOW) & (wpos[None, :] >= 0)
        p3 = masked_softmax(s3, wmask)
        o_win = jnp.einsum('ngrqk,nkgd->nqgrd', p3.astype(dt), vwin)
        return gb[..., 0:1] * o_cmp + gb[..., 1:2] * o_slc + gb[..., 2:3] * o_win

    return sweep_queries(block, SPARSE_Q_BLOCK, qg, gate)


PAGE = 128


def _head_diag_mask(width):
    sub = lax.broadcasted_iota(jnp.int32, (N_HEADS, width), 0)
    lane = lax.broadcasted_iota(jnp.int32, (N_HEADS, width), 1)
    return sub == lane // HEAD_DIM


def _block_diag_queries(q_ref, nt):
    diag = _head_diag_mask(BRANCH_W)
    return jnp.concatenate([jnp.where(diag, q_ref[0, t:t + 1, :], 0.0) for t in range(nt)], axis=0)


def _extract_heads(o, nt):
    diag = _head_diag_mask(BRANCH_W)
    return jnp.concatenate(
        [jnp.sum(jnp.where(diag, o[t * N_HEADS:(t + 1) * N_HEADS], 0.0), axis=0, keepdims=True) for t in range(nt)],
        axis=0)


def _slope_col(rows):
    h = lax.broadcasted_iota(jnp.int32, (rows, 1), 0) % N_HEADS
    col = jnp.zeros((rows, 1), F32)
    for k in range(N_HEADS):
        col = jnp.where(h == k, 2.0 ** (-8.0 * (k + 1) / N_HEADS), col)
    return col


def _row_softmax_step(s, v, m_ref, l_ref, acc_ref):
    m_old = m_ref[...]
    m_new = jnp.maximum(m_old, jnp.max(s, axis=1, keepdims=True))
    alpha = jnp.exp(m_old - m_new)
    p = jnp.exp(s - m_new)
    l_ref[...] = alpha * l_ref[...] + jnp.sum(p, axis=1, keepdims=True)
    acc_ref[...] = alpha * acc_ref[...] + jnp.dot(p.astype(BF16), v, preferred_element_type=F32)
    m_ref[...] = m_new


def _softmax_reset(m_ref, l_ref, acc_ref):
    m_ref[...] = jnp.full(m_ref.shape, NEG, F32)
    l_ref[...] = jnp.zeros(l_ref.shape, F32)
    acc_ref[...] = jnp.zeros(acc_ref.shape, F32)


def _new_token_mask(rows, nt):
    k = lax.broadcasted_iota(jnp.int32, (rows, PAGE), 1)
    t = lax.broadcasted_iota(jnp.int32, (rows, PAGE), 0) // N_HEADS
    return (k <= t) & (k < nt)


def _dfox_kernel(pt_ref, q_ref, pg_ref, lf_ref, npg_ref, nlf_ref, o_ref, qbd_ref, m_ref, l_ref, acc_ref, cf_ref, *, nt):
    del pt_ref
    j = pl.program_id(1)
    rows = nt * N_HEADS

    @pl.when(j == 0)
    def _():
        qbd_ref[...] = (_block_diag_queries(q_ref, nt) * SCALE).astype(BF16)
        _softmax_reset(m_ref, l_ref, acc_ref)
        cf_ref[...] = jnp.zeros_like(cf_ref)

    def step(kv_ref, lf, mask):
        k = kv_ref[0, :, 0:BRANCH_W].astype(BF16)
        v = kv_ref[0, :, BRANCH_W:2 * BRANCH_W].astype(BF16)
        rr = lax.broadcasted_iota(jnp.int32, (rows, N_HEADS), 0) % N_HEADS
        rc = lax.broadcasted_iota(jnp.int32, (rows, N_HEADS), 1)
        lfe = _dot_nt(jnp.where(rr == rc, 1.0, 0.0), lf, precision=HIGHEST)
        a = lax.broadcasted_iota(jnp.int32, (PAGE, PAGE), 0)
        b = lax.broadcasted_iota(jnp.int32, (PAGE, PAGE), 1)
        fk = jnp.dot(lfe, jnp.where(a <= b, 1.0, 0.0), preferred_element_type=F32, precision=HIGHEST) + cf_ref[...]
        cf_ref[...] = fk[:, PAGE - 1:PAGE]
        s = _dot_nt(qbd_ref[...], k) - fk
        if mask is not None:
            s = jnp.where(mask, s, MASKED)
        _row_softmax_step(s, v, m_ref, l_ref, acc_ref)

    step(pg_ref, lf_ref[0], None)

    @pl.when(j == pl.num_programs(1) - 1)
    def _():
        step(npg_ref, nlf_ref[0], _new_token_mask(rows, nt))
        o_ref[0] = _extract_heads(acc_ref[...] * (1.0 / l_ref[...]), nt)


def fox_decode_call(pt, q, cache_kv, cache_lf, new_kv, new_lf):
    n, nt, _ = q.shape
    npages = pt.shape[1]
    rows = nt * N_HEADS
    grid_spec = pltpu.PrefetchScalarGridSpec(
        num_scalar_prefetch=1,
        grid=(n, npages),
        in_specs=[pl.BlockSpec((1, nt, BRANCH_W), lambda b, j, pt: (b, 0, 0)),
                  pl.BlockSpec((1, PAGE, 2 * BRANCH_W), lambda b, j, pt: (pt[b, j], 0, 0)),
                  pl.BlockSpec((1, PAGE, N_HEADS), lambda b, j, pt: (pt[b, j], 0, 0)),
                  pl.BlockSpec((1, PAGE, 2 * BRANCH_W), lambda b, j, pt: (b, 0, 0)),
                  pl.BlockSpec((1, PAGE, N_HEADS), lambda b, j, pt: (b, 0, 0))],
        out_specs=pl.BlockSpec((1, nt, BRANCH_W), lambda b, j, pt: (b, 0, 0)),
        scratch_shapes=[pltpu.VMEM((rows, BRANCH_W), BF16), pltpu.VMEM((rows, 1), F32), pltpu.VMEM((rows, 1), F32),
                        pltpu.VMEM((rows, BRANCH_W), F32), pltpu.VMEM((rows, 1), F32)],
    )
    return pl.pallas_call(
        functools.partial(_dfox_kernel, nt=nt),
        grid_spec=grid_spec,
        out_shape=jax.ShapeDtypeStruct((n, nt, BRANCH_W), F32),
        compiler_params=_cparams("parallel", "arbitrary"),
        name="fox_decode",
    )(pt, q, cache_kv, cache_lf, new_kv, new_lf)


def _dmoba_sel_kernel(pt_ref, q_ref, kp_ref, sel_ref, q32_ref, g_ref, *, nt):
    del pt_ref
    j = pl.program_id(1)
    rows = nt * N_HEADS
    pages_per_block = MOBA_BLOCK // PAGE

    @pl.when(j == 0)
    def _():
        q32_ref[...] = _block_diag_queries(q_ref, nt)
        g_ref[...] = jnp.zeros_like(g_ref)

    ksum = jnp.sum(kp_ref[0], axis=0, keepdims=True)
    prow = lax.broadcasted_iota(jnp.int32, (g_ref.shape[1], BRANCH_W), 0)
    g_ref[...] += _dot_nt(q32_ref[...], jnp.where(prow == j, ksum, 0.0), precision=HIGHEST)

    @pl.when(j == pl.num_programs(1) - 1)
    def _():
        npg = g_ref.shape[1]
        g = g_ref[...]
        lane = lax.broadcasted_iota(jnp.int32, (rows, npg), 1)
        blk = g
        for u in range(1, pages_per_block):
            blk = blk + pltpu.roll(g, npg - u, 1)
        gs = jnp.where(lane % pages_per_block == 0, blk * (1.0 / MOBA_BLOCK), NEG)
        sel = jnp.zeros((rows, npg), F32)
        for _ in range(MOBA_TOPK):
            mx = jnp.max(gs, axis=1, keepdims=True)
            idx = jnp.min(jnp.where(gs == mx, lane, npg), axis=1, keepdims=True)
            hit = lane == idx
            sel = jnp.where(hit, 1.0, sel)
            gs = jnp.where(hit, MASKED, gs)
        out = sel
        for u in range(1, pages_per_block):
            out = out + pltpu.roll(sel, u, 1)
        sel_ref[0] = out


def moba_select_call(pt, q, cache_kv):
    n, nt, _ = q.shape
    npages = pt.shape[1]
    rows = nt * N_HEADS
    grid_spec = pltpu.PrefetchScalarGridSpec(
        num_scalar_prefetch=1,
        grid=(n, npages),
        in_specs=[pl.BlockSpec((1, nt, BRANCH_W), lambda b, j, pt: (b, 0, 0)),
                  pl.BlockSpec((1, PAGE, BRANCH_W), lambda b, j, pt: (pt[b, j], 0, 0))],
        out_specs=pl.BlockSpec((1, rows, npages), lambda b, j, pt: (b, 0, 0)),
        scratch_shapes=[pltpu.VMEM((rows, BRANCH_W), F32), pltpu.VMEM((rows, npages), F32)],
    )
    return pl.pallas_call(
        functools.partial(_dmoba_sel_kernel, nt=nt),
        grid_spec=grid_spec,
        out_shape=jax.ShapeDtypeStruct((n, rows, npages), F32),
        compiler_params=_cparams("parallel", "arbitrary"),
        name="moba_select",
    )(pt, q, cache_kv)


def _dmoba_kernel(pt_ref, q_ref, sel_ref, pg_ref, npg_ref, o_ref, qbd_ref, m_ref, l_ref, acc_ref, *, nt):
    del pt_ref
    j = pl.program_id(1)
    npages = pl.num_programs(1)
    rows = nt * N_HEADS
    slope = _slope_col(rows)
    klane = lax.broadcasted_iota(jnp.int32, (1, PAGE), 1)

    @pl.when(j == 0)
    def _():
        qbd_ref[...] = (_block_diag_queries(q_ref, nt) * SCALE).astype(BF16)
        _softmax_reset(m_ref, l_ref, acc_ref)

    def step(kv_ref, rel, mask):
        k = kv_ref[0, :, 0:BRANCH_W].astype(BF16)
        v = kv_ref[0, :, BRANCH_W:2 * BRANCH_W].astype(BF16)
        s = _dot_nt(qbd_ref[...], k) + slope * rel
        _row_softmax_step(jnp.where(mask, s, MASKED), v, m_ref, l_ref, acc_ref)

    prow = lax.broadcasted_iota(jnp.int32, (sel_ref.shape[2], PAGE), 0)
    keep = jnp.dot(sel_ref[0].astype(BF16), jnp.where(prow == j, 1.0, 0.0).astype(BF16),
                   preferred_element_type=F32) > 0.5
    step(pg_ref, (klane + (j - npages) * PAGE).astype(F32), keep)

    @pl.when(j == npages - 1)
    def _():
        step(npg_ref, klane.astype(F32), _new_token_mask(rows, nt))
        o_ref[0] = _extract_heads(acc_ref[...] * (1.0 / l_ref[...]), nt)


def moba_decode_call(pt, q, sel, cache_kv, new_kv):
    n, nt, _ = q.shape
    npages = pt.shape[1]
    rows = nt * N_HEADS
    grid_spec = pltpu.PrefetchScalarGridSpec(
        num_scalar_prefetch=1,
        grid=(n, npages),
        in_specs=[pl.BlockSpec((1, nt, BRANCH_W), lambda b, j, pt: (b, 0, 0)),
                  pl.BlockSpec((1, rows, npages), lambda b, j, pt: (b, 0, 0)),
                  pl.BlockSpec((1, PAGE, 2 * BRANCH_W), lambda b, j, pt: (pt[b, j], 0, 0)),
                  pl.BlockSpec((1, PAGE, 2 * BRANCH_W), lambda b, j, pt: (b, 0, 0))],
        out_specs=pl.BlockSpec((1, nt, BRANCH_W), lambda b, j, pt: (b, 0, 0)),
        scratch_shapes=[pltpu.VMEM((rows, BRANCH_W), BF16), pltpu.VMEM((rows, 1), F32), pltpu.VMEM((rows, 1), F32),
                        pltpu.VMEM((rows, BRANCH_W), F32)],
    )
    return pl.pallas_call(
        functools.partial(_dmoba_kernel, nt=nt),
        grid_spec=grid_spec,
        out_shape=jax.ShapeDtypeStruct((n, nt, BRANCH_W), F32),
        compiler_params=_cparams("parallel", "arbitrary"),
        name="moba_decode",
    )(pt, q, sel, cache_kv, new_kv)


def _dnsa_kernel(pt_ref, q_ref, gate_ref, kc_ref, vc_ref, pg_ref, npg_ref, win_ref, nwin_ref, o_ref,
                 qbd_ref, sel_ref, ocmp_ref, m_ref, l_ref, acc_ref, *, nt, nsb_pad):
    del pt_ref
    j = pl.program_id(1)
    npages = pl.num_programs(1)
    rows = nt * N_HEADS
    ncmp = kc_ref.shape[1]
    q0 = npages * PAGE
    slope = _slope_col(rows)
    klane = lax.broadcasted_iota(jnp.int32, (1, PAGE), 1)
    blocks_per_page = PAGE // SLC_BLOCK

    @pl.when(j == 0)
    def _():
        pr = lax.broadcasted_iota(jnp.int32, (BRANCH_W, KV_W), 0)
        pc = lax.broadcasted_iota(jnp.int32, (BRANCH_W, KV_W), 1)
        place = jnp.where((pr % HEAD_DIM == pc % HEAD_DIM) & (pc // HEAD_DIM == pr // (NSA_R * HEAD_DIM)), 1.0, 0.0)
        qg = jnp.dot(_block_diag_queries(q_ref, nt), place, preferred_element_type=F32, precision=HIGHEST)
        qbd = (qg * SCALE).astype(BF16)
        qbd_ref[...] = qbd
        c = lax.broadcasted_iota(jnp.int32, (1, ncmp), 1)
        s = _dot_nt(qbd, kc_ref[0]) + slope * (CMP_S * c + (CMP_S - 1) - q0).astype(F32)
        valid = c >= 1
        s = jnp.where(valid, s, MASKED)
        p = jnp.where(valid, jnp.exp(s - jnp.max(s, axis=1, keepdims=True)), 0.0)
        p = p * (1.0 / jnp.sum(p, axis=1, keepdims=True))
        ocmp_ref[...] = gate_ref[0, :, 0:1] * jnp.dot(p.astype(BF16), vc_ref[0], preferred_element_type=F32)
        gr = lax.broadcasted_iota(jnp.int32, (nt * NSA_G, rows), 0)
        gc = lax.broadcasted_iota(jnp.int32, (nt * NSA_G, rows), 1) // NSA_R
        imp = jnp.dot(jnp.where(gr == gc, 1.0, 0.0), p, preferred_element_type=F32, precision=HIGHEST)
        ratio = SLC_BLOCK // CMP_S
        mr = lax.broadcasted_iota(jnp.int32, (ncmp, nsb_pad), 0)
        mb = lax.broadcasted_iota(jnp.int32, (ncmp, nsb_pad), 1)
        gather_m = jnp.where((mr >= ratio * mb) & (mr <= ratio * mb + ratio) & (mr >= 1), 1.0, 0.0)
        p_slc = jnp.dot(imp, gather_m, preferred_element_type=F32, precision=HIGHEST)
        bj = lax.broadcasted_iota(jnp.int32, p_slc.shape, 1)
        own = q0 // SLC_BLOCK
        forced = (bj == 0) | (bj >= own - 1)
        allowed = bj <= own
        score = jnp.where(allowed, jnp.where(forced, p_slc + FORCE_SCORE, p_slc), -1.0)
        sel = jnp.zeros(p_slc.shape, F32)
        for _ in range(SLC_N):
            mx = jnp.max(score, axis=1, keepdims=True)
            idx = jnp.min(jnp.where(score == mx, bj, nsb_pad), axis=1, keepdims=True)
            hit = bj == idx
            sel = jnp.where(hit, 1.0, sel)
            score = jnp.where(hit, -2.0, score)
        sel = jnp.where(allowed, sel, 0.0)
        er = lax.broadcasted_iota(jnp.int32, (rows, nt * NSA_G), 0) // NSA_R
        ec = lax.broadcasted_iota(jnp.int32, (rows, nt * NSA_G), 1)
        sel_ref[...] = jnp.dot(jnp.where(er == ec, 1.0, 0.0), sel, preferred_element_type=F32).astype(BF16)
        _softmax_reset(m_ref, l_ref, acc_ref)

    def step(kv_ref, rel, mask):
        k = kv_ref[0, :, 0:KV_W].astype(BF16)
        v = kv_ref[0, :, KV_W:2 * KV_W].astype(BF16)
        s = _dot_nt(qbd_ref[...], k) + slope * rel
        if mask is not None:
            s = jnp.where(mask, s, MASKED)
        _row_softmax_step(s, v, m_ref, l_ref, acc_ref)

    brow = lax.broadcasted_iota(jnp.int32, (nsb_pad, PAGE), 0)
    bcol = lax.broadcasted_iota(jnp.int32, (nsb_pad, PAGE), 1) // SLC_BLOCK
    expand = jnp.where(brow == bcol + j * blocks_per_page, 1.0, 0.0).astype(BF16)
    keep = jnp.dot(sel_ref[...], expand, preferred_element_type=F32) > 0.5
    step(pg_ref, (klane + (j - npages) * PAGE).astype(F32), keep)

    @pl.when(j == npages - 1)
    def _():
        new_mask = _new_token_mask(rows, nt)
        step(npg_ref, klane.astype(F32), new_mask)
        o_slc = acc_ref[...] * (1.0 / l_ref[...])
        _softmax_reset(m_ref, l_ref, acc_ref)
        wlen = win_ref.shape[1]
        wi = lax.broadcasted_iota(jnp.int32, (rows, wlen), 1)
        wt = lax.broadcasted_iota(jnp.int32, (rows, wlen), 0) // N_HEADS
        wrel = (lax.broadcasted_iota(jnp.int32, (1, wlen), 1) - wlen).astype(F32)
        step(win_ref, wrel, wi > wt + (wlen - WINDOW))
        step(nwin_ref, klane.astype(F32), new_mask)
        o_win = acc_ref[...] * (1.0 / l_ref[...])
        o = ocmp_ref[...] + gate_ref[0, :, 1:2] * o_slc + gate_ref[0, :, 2:3] * o_win
        ur = lax.broadcasted_iota(jnp.int32, (KV_W, BRANCH_W), 0)
        uc = lax.broadcasted_iota(jnp.int32, (KV_W, BRANCH_W), 1)
        unplace = jnp.where((ur % HEAD_DIM == uc % HEAD_DIM) & (ur // HEAD_DIM == uc // (NSA_R * HEAD_DIM)), 1.0, 0.0)
        o_ref[0] = _extract_heads(jnp.dot(o, unplace, preferred_element_type=F32, precision=HIGHEST), nt)


def nsa_decode_call(pt, q, gate, kcmp, vcmp, cache_kv, new_kv, win, new_win):
    n, nt, _ = q.shape
    npages = pt.shape[1]
    rows = nt * N_HEADS
    ncmp = kcmp.shape[1]
    nsb = (npages * PAGE) // SLC_BLOCK + 1
    nsb_pad = -(-nsb // 128) * 128
    wlen = win.shape[1]
    grid_spec = pltpu.PrefetchScalarGridSpec(
        num_scalar_prefetch=1,
        grid=(n, npages),
        in_specs=[pl.BlockSpec((1, nt, BRANCH_W), lambda b, j, pt: (b, 0, 0)),
                  pl.BlockSpec((1, rows, 3), lambda b, j, pt: (b, 0, 0)),
                  pl.BlockSpec((1, ncmp, KV_W), lambda b, j, pt: (b, 0, 0)),
                  pl.BlockSpec((1, ncmp, KV_W), lambda b, j, pt: (b, 0, 0)),
                  pl.BlockSpec((1, PAGE, 2 * KV_W), lambda b, j, pt: (pt[b, j], 0, 1)),
                  pl.BlockSpec((1, PAGE, 2 * KV_W), lambda b, j, pt: (b, 0, 0)),
                  pl.BlockSpec((1, wlen, 2 * KV_W), lambda b, j, pt: (b, 0, 0)),
                  pl.BlockSpec((1, PAGE, 2 * KV_W), lambda b, j, pt: (b, 0, 0))],
        out_specs=pl.BlockSpec((1, nt, BRANCH_W), lambda b, j, pt: (b, 0, 0)),
        scratch_shapes=[pltpu.VMEM((rows, KV_W), BF16), pltpu.VMEM((rows, nsb_pad), BF16),
                        pltpu.VMEM((rows, KV_W), F32), pltpu.VMEM((rows, 1), F32), pltpu.VMEM((rows, 1), F32),
                        pltpu.VMEM((rows, KV_W), F32)],
    )
    return pl.pallas_call(
        functools.partial(_dnsa_kernel, nt=nt, nsb_pad=nsb_pad),
        grid_spec=grid_spec,
        out_shape=jax.ShapeDtypeStruct((n, nt, BRANCH_W), F32),
        compiler_params=_cparams("parallel", "arbitrary"),
        name="nsa_decode",
    )(pt, q, gate, kcmp, vcmp, cache_kv, new_kv, win, new_win)


def _conv_sample_kernel(z_ref, pre1_ref, pre2_ref, w_ref, o_ref, u_ref, *, nt):
    bw = BRANCH_W
    u = z_ref[:, 2 * bw:3 * bw] * z_ref[:, 0:bw]
    t = lax.broadcasted_iota(jnp.int32, u.shape, 0) % nt
    u1 = jnp.where(t < 1, pre1_ref[...], pltpu.roll(u, 1, 0))
    u2 = jnp.where(t < 2, pre2_ref[...], pltpu.roll(u, 2, 0))
    y = u2 * w_ref[0:1, :] + u1 * w_ref[1:2, :] + u * w_ref[2:3, :]
    o_ref[...] = z_ref[:, bw:2 * bw] * y
    u_ref[...] = u


def conv_sample_call(z_conv, state, conv_w, n, nt):
    zero = jnp.zeros((n, nt, BRANCH_W), F32)
    pre1 = zero.at[:, 0].set(state[:, 1]).reshape(n * nt, BRANCH_W)
    pre2 = zero.at[:, 0].set(state[:, 0]).at[:, 1].set(state[:, 1]).reshape(n * nt, BRANCH_W)
    return pl.pallas_call(
        functools.partial(_conv_sample_kernel, nt=nt),
        out_shape=[jax.ShapeDtypeStruct((n * nt, BRANCH_W), F32), jax.ShapeDtypeStruct((n * nt, BRANCH_W), F32)],
        name="conv_sample",
    )(z_conv, pre1, pre2, conv_w)


PPS = 16


def _split3(x):
    hi = x.astype(BF16)
    r1 = x - hi.astype(F32)
    mid = r1.astype(BF16)
    lo = (r1 - mid.astype(F32)).astype(BF16)
    return jnp.concatenate([hi, mid, lo], axis=0)


def _softmax_step_t(s, vts, m_ref, l_ref, acc_ref):
    m_old = m_ref[...]
    m_new = jnp.maximum(m_old, jnp.max(s, axis=1, keepdims=True))
    alpha = jnp.exp(m_old - m_new)
    p = jnp.exp(s - m_new)
    l_ref[...] = alpha * l_ref[...] + jnp.sum(p, axis=1, keepdims=True)
    p = p.astype(BF16)
    pv, off = None, 0
    for vt in vts:
        nk = vt.shape[1]
        term = _dot_nt(p[:, off:off + nk], vt)
        pv = term if pv is None else pv + term
        off += nk
    acc_ref[...] = alpha * acc_ref[...] + pv
    m_ref[...] = m_new


def _page_specs(rows, row_block, pps):
    return [pl.BlockSpec((1, rows, PAGE), functools.partial(lambda b, j, pt, u: (pt[b, j * pps + u], row_block, 0), u=u))
            for u in range(pps)]


def _dfox_t_kernel(pt_ref, q_ref, *refs, nt, pps):
    del pt_ref
    kv_refs, lf_refs = refs[:pps], refs[pps:2 * pps]
    nkv_ref, nlf_ref, o_ref, qbd_ref, m_ref, l_ref, acc_ref, cf_ref = refs[2 * pps:]
    j = pl.program_id(1)
    rows = nt * N_HEADS

    @pl.when(j == 0)
    def _():
        qbd_ref[...] = (_block_diag_queries(q_ref, nt) * SCALE).astype(BF16)
        _softmax_reset(m_ref, l_ref, acc_ref)
        cf_ref[...] = jnp.zeros_like(cf_ref)

    def block(kvs, lfs, mask):
        a = lax.broadcasted_iota(jnp.int32, (PAGE, PAGE), 0)
        b = lax.broadcasted_iota(jnp.int32, (PAGE, PAGE), 1)
        tri = jnp.where(a <= b, 1.0, 0.0).astype(BF16)
        carry = cf_ref[...]
        parts, vts = [], []
        for kv_ref, lf_ref in zip(kvs, lfs):
            lfe = jnp.concatenate([lf_ref[0]] * nt, axis=0)
            c3 = jnp.dot(_split3(lfe), tri, preferred_element_type=F32)
            fk = c3[0:rows] + c3[rows:2 * rows] + c3[2 * rows:3 * rows] + carry
            carry = fk[:, PAGE - 1:PAGE]
            parts.append(jnp.dot(qbd_ref[...], kv_ref[0, 0:BRANCH_W, :].astype(BF16), preferred_element_type=F32) - fk)
            vts.append(kv_ref[0, BRANCH_W:2 * BRANCH_W, :].astype(BF16))
        cf_ref[...] = carry
        s = parts[0] if len(parts) == 1 else jnp.concatenate(parts, axis=1)
        if mask is not None:
            s = jnp.where(mask, s, MASKED)
        _softmax_step_t(s, vts, m_ref, l_ref, acc_ref)

    block(kv_refs, lf_refs, None)

    @pl.when(j == pl.num_programs(1) - 1)
    def _():
        block([nkv_ref], [nlf_ref], _new_token_mask(rows, nt))
        o_ref[0] = _extract_heads(acc_ref[...] * (1.0 / l_ref[...]), nt)


def fox_decode_t_call(pt, q, cache_kvt, cache_lft, new_kvt, new_lft):
    n, nt, _ = q.shape
    npages = pt.shape[1]
    pps = PPS if npages % PPS == 0 else 1
    rows = nt * N_HEADS
    grid_spec = pltpu.PrefetchScalarGridSpec(
        num_scalar_prefetch=1,
        grid=(n, npages // pps),
        in_specs=[pl.BlockSpec((1, nt, BRANCH_W), lambda b, j, pt: (b, 0, 0))]
        + _page_specs(2 * BRANCH_W, 0, pps) + _page_specs(N_HEADS, 0, pps)
        + [pl.BlockSpec((1, 2 * BRANCH_W, PAGE), lambda b, j, pt: (b, 0, 0)),
           pl.BlockSpec((1, N_HEADS, PAGE), lambda b, j, pt: (b, 0, 0))],
        out_specs=pl.BlockSpec((1, nt, BRANCH_W), lambda b, j, pt: (b, 0, 0)),
        scratch_shapes=[pltpu.VMEM((rows, BRANCH_W), BF16), pltpu.VMEM((rows, 1), F32), pltpu.VMEM((rows, 1), F32),
                        pltpu.VMEM((rows, BRANCH_W), F32), pltpu.VMEM((rows, 1), F32)],
    )
    return pl.pallas_call(
        functools.partial(_dfox_t_kernel, nt=nt, pps=pps),
        grid_spec=grid_spec,
        out_shape=jax.ShapeDtypeStruct((n, nt, BRANCH_W), F32),
        compiler_params=_cparams("parallel", "arbitrary"),
        name="fox_decode",
    )(pt, q, *([cache_kvt] * pps), *([cache_lft] * pps), new_kvt, new_lft)


def _dmoba_sel_t_kernel(pt_ref, q_ref, *refs, nt, pps):
    del pt_ref
    k_refs = refs[:pps]
    sel_ref, q3_ref, g_ref = refs[pps:]
    j = pl.program_id(1)
    rows = nt * N_HEADS
    npg = g_ref.shape[1]
    pages_per_block = MOBA_BLOCK // PAGE
    lane = lax.broadcasted_iota(jnp.int32, (rows, npg), 1)

    @pl.when(j == 0)
    def _():
        q3_ref[...] = _split3(_block_diag_queries(q_ref, nt))
        g_ref[...] = jnp.zeros_like(g_ref)

    g = g_ref[...]
    for u, k_ref in enumerate(k_refs):
        qk3 = jnp.dot(q3_ref[...], k_ref[0].astype(BF16), preferred_element_type=F32)
        qk = qk3[0:rows] + qk3[rows:2 * rows] + qk3[2 * rows:3 * rows]
        g = jnp.where(lane == j * pps + u, jnp.sum(qk, axis=1, keepdims=True), g)
    g_ref[...] = g

    @pl.when(j == pl.num_programs(1) - 1)
    def _():
        blk = g
        for u in range(1, pages_per_block):
            blk = blk + pltpu.roll(g, npg - u, 1)
        gs = jnp.where(lane % pages_per_block == 0, blk * (1.0 / MOBA_BLOCK), NEG)
        sel = jnp.zeros((rows, npg), F32)
        for _ in range(MOBA_TOPK):
            mx = jnp.max(gs, axis=1, keepdims=True)
            idx = jnp.min(jnp.where(gs == mx, lane, npg), axis=1, keepdims=True)
            hit = lane == idx
            sel = jnp.where(hit, 1.0, sel)
            gs = jnp.where(hit, MASKED, gs)
        out = sel
        for u in range(1, pages_per_block):
            out = out + pltpu.roll(sel, u, 1)
        sel_ref[0] = out


def moba_select_t_call(pt, q, cache_kvt):
    n, nt, _ = q.shape
    npages = pt.shape[1]
    pps = PPS if npages % PPS == 0 else 1
    rows = nt * N_HEADS
    grid_spec = pltpu.PrefetchScalarGridSpec(
        num_scalar_prefetch=1,
        grid=(n, npages // pps),
        in_specs=[pl.BlockSpec((1, nt, BRANCH_W), lambda b, j, pt: (b, 0, 0))] + _page_specs(BRANCH_W, 0, pps),
        out_specs=pl.BlockSpec((1, rows, npages), lambda b, j, pt: (b, 0, 0)),
        scratch_shapes=[pltpu.VMEM((3 * rows, BRANCH_W), BF16), pltpu.VMEM((rows, npages), F32)],
    )
    return pl.pallas_call(
        functools.partial(_dmoba_sel_t_kernel, nt=nt, pps=pps),
        grid_spec=grid_spec,
        out_shape=jax.ShapeDtypeStruct((n, rows, npages), F32),
        compiler_params=_cparams("parallel", "arbitrary"),
        name="moba_select",
    )(pt, q, *([cache_kvt] * pps))


def _dmoba_t_kernel(pt_ref, q_ref, sel_ref, *refs, nt, pps):
    del pt_ref
    kv_refs = refs[:pps]
    nkv_ref, o_ref, qbd_ref, m_ref, l_ref, acc_ref = refs[pps:]
    j = pl.program_id(1)
    nsteps = pl.num_programs(1)
    rows = nt * N_HEADS
    slope = _slope_col(rows)

    @pl.when(j == 0)
    def _():
        qbd_ref[...] = (_block_diag_queries(q_ref, nt) * SCALE).astype(BF16)
        _softmax_reset(m_ref, l_ref, acc_ref)

    def block(kvs, rel, mask):
        parts = [jnp.dot(qbd_ref[...], kv_ref[0, 0:BRANCH_W, :].astype(BF16), preferred_element_type=F32)
                 for kv_ref in kvs]
        s = parts[0] if len(parts) == 1 else jnp.concatenate(parts, axis=1)
        s = jnp.where(mask, s + slope * rel, MASKED)
        _softmax_step_t(s, [kv_ref[0, BRANCH_W:2 * BRANCH_W, :].astype(BF16) for kv_ref in kvs], m_ref, l_ref, acc_ref)

    npg = sel_ref.shape[2]
    prow = lax.broadcasted_iota(jnp.int32, (npg, pps * PAGE), 0)
    pcol = lax.broadcasted_iota(jnp.int32, (npg, pps * PAGE), 1) // PAGE
    keep = jnp.dot(sel_ref[0].astype(BF16), jnp.where(prow == pcol + j * pps, 1.0, 0.0).astype(BF16),
                   preferred_element_type=F32) > 0.5
    klane = lax.broadcasted_iota(jnp.int32, (1, pps * PAGE), 1)
    block(kv_refs, (klane + (j - nsteps) * (pps * PAGE)).astype(F32), keep)

    @pl.when(j == nsteps - 1)
    def _():
        block([nkv_ref], lax.broadcasted_iota(jnp.int32, (1, PAGE), 1).astype(F32), _new_token_mask(rows, nt))
        o_ref[0] = _extract_heads(acc_ref[...] * (1.0 / l_ref[...]), nt)


def moba_decode_t_call(pt, q, sel, cache_kvt, new_kvt):
    n, nt, _ = q.shape
    npages = pt.shape[1]
    pps = PPS if npages % PPS == 0 else 1
    rows = nt * N_HEADS
    grid_spec = pltpu.PrefetchScalarGridSpec(
        num_scalar_prefetch=1,
        grid=(n, npages // pps),
        in_specs=[pl.BlockSpec((1, nt, BRANCH_W), lambda b, j, pt: (b, 0, 0)),
                  pl.BlockSpec((1, rows, npages), lambda b, j, pt: (b, 0, 0))]
        + _page_specs(2 * BRANCH_W, 0, pps)
        + [pl.BlockSpec((1, 2 * BRANCH_W, PAGE), lambda b, j, pt: (b, 0, 0))],
        out_specs=pl.BlockSpec((1, nt, BRANCH_W), lambda b, j, pt: (b, 0, 0)),
        scratch_shapes=[pltpu.VMEM((rows, BRANCH_W), BF16), pltpu.VMEM((rows, 1), F32), pltpu.VMEM((rows, 1), F32),
                        pltpu.VMEM((rows, BRANCH_W), F32)],
    )
    return pl.pallas_call(
        functools.partial(_dmoba_t_kernel, nt=nt, pps=pps),
        grid_spec=grid_spec,
        out_shape=jax.ShapeDtypeStruct((n, nt, BRANCH_W), F32),
        compiler_params=_cparams("parallel", "arbitrary"),
        name="moba_decode",
    )(pt, q, sel, *([cache_kvt] * pps), new_kvt)


def _compress_t_kernel(pt_ref, *refs, rows, pps):
    del pt_ref
    a_refs = refs[:pps]
    pet_ref, peb_ref, wt_ref, wb_ref, w2k_ref, w2v_ref, w2vt_ref, kc_ref, vc_ref, vct_ref, rows_ref = refs[pps:]
    j = pl.program_id(1)
    pr = lax.broadcasted_iota(jnp.int32, (PAGE, PAGE), 0)
    pc = lax.broadcasted_iota(jnp.int32, (PAGE, PAGE), 1)
    perm = jnp.where(pc == CMP_S * (pr % PAGE_ROWS) + pr // PAGE_ROWS, 1.0, 0.0).astype(BF16)
    w2 = 2 * KV_W
    for u, a_ref in enumerate(a_refs):
        p3 = _dot_nt(perm, _split3(a_ref[0]))
        pg = p3[:, 0:w2] + p3[:, w2:2 * w2] + p3[:, 2 * w2:3 * w2]
        r0 = pl.multiple_of((j * pps + u) * PAGE_ROWS, PAGE_ROWS)
        for i in range(CMP_S):
            rows_ref[pl.ds(r0, PAGE_ROWS), i * w2:(i + 1) * w2] = pg[i * PAGE_ROWS:(i + 1) * PAGE_ROWS, :]

    @pl.when(j == pl.num_programs(1) - 1)
    def _():
        chunk = min(256, rows)
        prev_top = jnp.zeros((1, 4 * CMP_HID), F32)
        for c in range(rows // chunk):
            a = rows_ref[c * chunk:(c + 1) * chunk, :]
            ht = jnp.dot((a + pet_ref[...]).astype(BF16), wt_ref[...], preferred_element_type=F32)
            hb = jnp.dot((a + peb_ref[...]).astype(BF16), wb_ref[...], preferred_element_type=F32)
            row = lax.broadcasted_iota(jnp.int32, ht.shape, 0)
            shifted = jnp.where(row == 0, prev_top, pltpu.roll(ht, 1, 0))
            prev_top = ht[chunk - 1:chunk, :]
            g = _gelu_tanh(shifted + hb).astype(BF16)
            kc_ref[0, c * chunk:(c + 1) * chunk, :] = jnp.dot(
                g[:, :2 * CMP_HID], w2k_ref[...], preferred_element_type=F32).astype(BF16)
            vc_ref[0, c * chunk:(c + 1) * chunk, :] = jnp.dot(
                g[:, 2 * CMP_HID:], w2v_ref[...], preferred_element_type=F32).astype(BF16)
            vct_ref[0, :, c * chunk:(c + 1) * chunk] = _dot_nt(w2vt_ref[...], g[:, 2 * CMP_HID:]).astype(BF16)


def compress_t_call(pages_t, page_idx, pe_top, pe_bot, wt, wb, w2k, w2v, w2vt, n, npages):
    rows = npages * PAGE_ROWS
    pps = PPS if npages % PPS == 0 else 1
    const = lambda b, j, pt: (0, 0)
    grid_spec = pltpu.PrefetchScalarGridSpec(
        num_scalar_prefetch=1,
        grid=(n, npages // pps),
        in_specs=_page_specs(2 * KV_W, 0, pps)
        + [pl.BlockSpec((1, CMP_ROW), const),
           pl.BlockSpec((1, CMP_ROW), const),
           pl.BlockSpec((CMP_ROW, 4 * CMP_HID), const, pipeline_mode=pl.Buffered(1)),
           pl.BlockSpec((CMP_ROW, 4 * CMP_HID), const, pipeline_mode=pl.Buffered(1)),
           pl.BlockSpec((2 * CMP_HID, KV_W), const),
           pl.BlockSpec((2 * CMP_HID, KV_W), const),
           pl.BlockSpec((KV_W, 2 * CMP_HID), const)],
        out_specs=[pl.BlockSpec((1, rows, KV_W), lambda b, j, pt: (b, 0, 0)),
                   pl.BlockSpec((1, rows, KV_W), lambda b, j, pt: (b, 0, 0)),
                   pl.BlockSpec((1, KV_W, rows), lambda b, j, pt: (b, 0, 0))],
        scratch_shapes=[pltpu.VMEM((rows, CMP_ROW), F32)],
    )
    return pl.pallas_call(
        functools.partial(_compress_t_kernel, rows=rows, pps=pps),
        grid_spec=grid_spec,
        out_shape=[jax.ShapeDtypeStruct((n, rows, KV_W), BF16),
                   jax.ShapeDtypeStruct((n, rows, KV_W), BF16),
                   jax.ShapeDtypeStruct((n, KV_W, rows), BF16)],
        compiler_params=_cparams("parallel", "arbitrary"),
        name="nsa_compress",
    )(page_idx, *([pages_t] * pps), pe_top, pe_bot, wt, wb, w2k, w2v, w2vt)


def _dnsa_t_kernel(pt_ref, q_ref, gate_ref, kc_ref, vc_ref, *refs, nt, nsb_pad, pps):
    del pt_ref
    kv_refs = refs[:pps]
    nkv_ref, win_ref, nwin_ref, o_ref, qbd_ref, sel_ref, ocmp_ref, m_ref, l_ref, acc_ref = refs[pps:]
    j = pl.program_id(1)
    nsteps = pl.num_programs(1)
    rows = nt * N_HEADS
    ncmp = kc_ref.shape[1]
    q0 = nsteps * (pps * PAGE)
    slope = _slope_col(rows)
    klane = lax.broadcasted_iota(jnp.int32, (1, PAGE), 1)
    blocks_per_page = PAGE // SLC_BLOCK

    @pl.when(j == 0)
    def _():
        pr = lax.broadcasted_iota(jnp.int32, (BRANCH_W, KV_W), 0)
        pc = lax.broadcasted_iota(jnp.int32, (BRANCH_W, KV_W), 1)
        place = jnp.where((pr % HEAD_DIM == pc % HEAD_DIM) & (pc // HEAD_DIM == pr // (NSA_R * HEAD_DIM)), 1.0, 0.0)
        qg = jnp.dot(_block_diag_queries(q_ref, nt), place, preferred_element_type=F32, precision=HIGHEST)
        qbd = (qg * SCALE).astype(BF16)
        qbd_ref[...] = qbd
        c = lax.broadcasted_iota(jnp.int32, (1, ncmp), 1)
        s = _dot_nt(qbd, kc_ref[0]) + slope * (CMP_S * c + (CMP_S - 1) - q0).astype(F32)
        valid = c >= 1
        s = jnp.where(valid, s, MASKED)
        p = jnp.where(valid, jnp.exp(s - jnp.max(s, axis=1, keepdims=True)), 0.0)
        p = p * (1.0 / jnp.sum(p, axis=1, keepdims=True))
        ocmp_ref[...] = gate_ref[0, :, 0:1] * jnp.dot(p.astype(BF16), vc_ref[0], preferred_element_type=F32)
        gr = lax.broadcasted_iota(jnp.int32, (nt * NSA_G, rows), 0)
        gc = lax.broadcasted_iota(jnp.int32, (nt * NSA_G, rows), 1) // NSA_R
        imp = jnp.dot(jnp.where(gr == gc, 1.0, 0.0), p, preferred_element_type=F32, precision=HIGHEST)
        ratio = SLC_BLOCK // CMP_S
        mr = lax.broadcasted_iota(jnp.int32, (ncmp, nsb_pad), 0)
        mb = lax.broadcasted_iota(jnp.int32, (ncmp, nsb_pad), 1)
        gather_m = jnp.where((mr >= ratio * mb) & (mr <= ratio * mb + ratio) & (mr >= 1), 1.0, 0.0)
        p_slc = jnp.dot(imp, gather_m, preferred_element_type=F32, precision=HIGHEST)
        bj = lax.broadcasted_iota(jnp.int32, p_slc.shape, 1)
        own = q0 // SLC_BLOCK
        forced = (bj == 0) | (bj >= own - 1)
        allowed = bj <= own
        score = jnp.where(allowed, jnp.where(forced, p_slc + FORCE_SCORE, p_slc), -1.0)
        sel = jnp.zeros(p_slc.shape, F32)
        for _ in range(SLC_N):
            mx = jnp.max(score, axis=1, keepdims=True)
            idx = jnp.min(jnp.where(score == mx, bj, nsb_pad), axis=1, keepdims=True)
            hit = bj == idx
            sel = jnp.where(hit, 1.0, sel)
            score = jnp.where(hit, -2.0, score)
        sel = jnp.where(allowed, sel, 0.0)
        er = lax.broadcasted_iota(jnp.int32, (rows, nt * NSA_G), 0) // NSA_R
        ec = lax.broadcasted_iota(jnp.int32, (rows, nt * NSA_G), 1)
        sel_ref[...] = jnp.dot(jnp.where(er == ec, 1.0, 0.0), sel, preferred_element_type=F32).astype(BF16)
        _softmax_reset(m_ref, l_ref, acc_ref)

    def block(kvs, rel, mask):
        parts = [jnp.dot(qbd_ref[...], kv_ref[0, 0:KV_W, :].astype(BF16), preferred_element_type=F32) for kv_ref in kvs]
        s = (parts[0] if len(parts) == 1 else jnp.concatenate(parts, axis=1)) + slope * rel
        if mask is not None:
            s = jnp.where(mask, s, MASKED)
        _softmax_step_t(s, [kv_ref[0, KV_W:2 * KV_W, :].astype(BF16) for kv_ref in kvs], m_ref, l_ref, acc_ref)

    brow = lax.broadcasted_iota(jnp.int32, (nsb_pad, pps * PAGE), 0)
    bcol = lax.broadcasted_iota(jnp.int32, (nsb_pad, pps * PAGE), 1) // SLC_BLOCK
    expand = jnp.where(brow == bcol + j * (pps * blocks_per_page), 1.0, 0.0).astype(BF16)
    keep = jnp.dot(sel_ref[...], expand, preferred_element_type=F32) > 0.5
    kl = lax.broadcasted_iota(jnp.int32, (1, pps * PAGE), 1)
    block(kv_refs, (kl + (j - nsteps) * (pps * PAGE)).astype(F32), keep)

    @pl.when(j == nsteps - 1)
    def _():
        new_mask = _new_token_mask(rows, nt)
        block([nkv_ref], klane.astype(F32), new_mask)
        o_slc = acc_ref[...] * (1.0 / l_ref[...])
        _softmax_reset(m_ref, l_ref, acc_ref)
        wlen = win_ref.shape[2]
        wi = lax.broadcasted_iota(jnp.int32, (rows, wlen), 1)
        wt = lax.broadcasted_iota(jnp.int32, (rows, wlen), 0) // N_HEADS
        wrel = (lax.broadcasted_iota(jnp.int32, (1, wlen), 1) - wlen).astype(F32)
        block([win_ref], wrel, wi > wt + (wlen - WINDOW))
        block([nwin_ref], klane.astype(F32), new_mask)
        o_win = acc_ref[...] * (1.0 / l_ref[...])
        o = ocmp_ref[...] + gate_ref[0, :, 1:2] * o_slc + gate_ref[0, :, 2:3] * o_win
        ur = lax.broadcasted_iota(jnp.int32, (KV_W, BRANCH_W), 0)
        uc = lax.broadcasted_iota(jnp.int32, (KV_W, BRANCH_W), 1)
        unplace = jnp.where((ur % HEAD_DIM == uc % HEAD_DIM) & (ur // HEAD_DIM == uc // (NSA_R * HEAD_DIM)), 1.0, 0.0)
        o_ref[0] = _extract_heads(jnp.dot(o, unplace, preferred_element_type=F32, precision=HIGHEST), nt)


def nsa_decode_t_call(pt, q, gate, kcmp, vcmp, cache_kvt, new_kvt, win_t, new_win_t):
    n, nt, _ = q.shape
    npages = pt.shape[1]
    pps = PPS if npages % PPS == 0 else 1
    rows = nt * N_HEADS
    ncmp = kcmp.shape[1]
    nsb = (npages * PAGE) // SLC_BLOCK + 1
    nsb_pad = -(-nsb // 128) * 128
    wlen = win_t.shape[2]
    grid_spec = pltpu.PrefetchScalarGridSpec(
        num_scalar_prefetch=1,
        grid=(n, npages // pps),
        in_specs=[pl.BlockSpec((1, nt, BRANCH_W), lambda b, j, pt: (b, 0, 0)),
                  pl.BlockSpec((1, rows, 3), lambda b, j, pt: (b, 0, 0)),
                  pl.BlockSpec((1, ncmp, KV_W), lambda b, j, pt: (b, 0, 0)),
                  pl.BlockSpec((1, ncmp, KV_W), lambda b, j, pt: (b, 0, 0))]
        + _page_specs(2 * KV_W, 1, pps)
        + [pl.BlockSpec((1, 2 * KV_W, PAGE), lambda b, j, pt: (b, 0, 0)),
           pl.BlockSpec((1, 2 * KV_W, wlen), lambda b, j, pt: (b, 0, 0)),
           pl.BlockSpec((1, 2 * KV_W, PAGE), lambda b, j, pt: (b, 0, 0))],
        out_specs=pl.BlockSpec((1, nt, BRANCH_W), lambda b, j, pt: (b, 0, 0)),
        scratch_shapes=[pltpu.VMEM((rows, KV_W), BF16), pltpu.VMEM((rows, nsb_pad), BF16),
                        pltpu.VMEM((rows, KV_W), F32), pltpu.VMEM((rows, 1), F32), pltpu.VMEM((rows, 1), F32),
                        pltpu.VMEM((rows, KV_W), F32)],
    )
    return pl.pallas_call(
        functools.partial(_dnsa_t_kernel, nt=nt, nsb_pad=nsb_pad, pps=pps),
        grid_spec=grid_spec,
        out_shape=jax.ShapeDtypeStruct((n, nt, BRANCH_W), F32),
        compiler_params=_cparams("parallel", "arbitrary"),
        name="nsa_decode",
    )(pt, q, gate, kcmp, vcmp, *([cache_kvt] * pps), new_kvt, win_t, new_win_t)


def _compress_weights(cmp_w1, cmp_w2, cmp_pe):
    w1 = cmp_w1.reshape(2, CMP_L, HEAD_DIM, CMP_HID)
    eye_w = jnp.eye(2, dtype=F32)
    eye_g = jnp.eye(NSA_G, dtype=F32)
    big = jnp.einsum('widh,sw,gk->isgdwkh', w1, eye_w, eye_g)
    big = big.reshape(CMP_L, 2 * KV_W, 4 * CMP_HID)
    wt = big[:CMP_S].reshape(CMP_ROW, 4 * CMP_HID).astype(BF16)
    wb = big[CMP_S:].reshape(CMP_ROW, 4 * CMP_HID).astype(BF16)
    w2k = jnp.einsum('hd,gk->ghkd', cmp_w2[0], eye_g).reshape(2 * CMP_HID, KV_W).astype(BF16)
    w2v = jnp.einsum('hd,gk->ghkd', cmp_w2[1], eye_g).reshape(2 * CMP_HID, KV_W).astype(BF16)
    w2vt = jnp.einsum('hd,gk->kdgh', cmp_w2[1], eye_g).reshape(KV_W, 2 * CMP_HID).astype(BF16)
    pe = jnp.broadcast_to(cmp_pe[:, :, None, :], (2, CMP_L, NSA_G, HEAD_DIM))
    pe = jnp.transpose(pe, (1, 0, 2, 3)).reshape(CMP_L, 2 * KV_W)
    pe_top = pe[:CMP_S].reshape(1, CMP_ROW)
    pe_bot = pe[CMP_S:].reshape(1, CMP_ROW)
    return pe_top, pe_bot, wt, wb, w2k, w2v, w2vt


def _layer_weights(l, w_in, w_branch, w_out, w_up, w_down):
    wl = w_in[l]
    small = jnp.concatenate(
        [wl[:, OFF_FOX_F:OFF_FOX_F + N_HEADS], wl[:, OFF_NSA_GATE:OFF_NSA_GATE + 3 * N_HEADS],
         jnp.zeros((D_MODEL, SMALL_W - 4 * N_HEADS), wl.dtype)], axis=1)
    return dict(
        conv=wl[:, OFF_CONV:OFF_FOX].astype(BF16),
        fox=wl[:, OFF_FOX:OFF_FOX_F].astype(BF16),
        moba=wl[:, OFF_MOBA:OFF_NSA].astype(BF16),
        nsa=wl[:, OFF_NSA:OFF_NSA_GATE].astype(BF16),
        small=small.astype(BF16),
        gate=wl[:, OFF_MERGE:].astype(BF16),
        branch=w_branch[l].astype(BF16),
        out=w_out[l].astype(BF16),
        up=w_up[l].astype(BF16),
        down=w_down[l].astype(BF16),
    )


def _finish_layer(x, h, branches, w, g_mlp, g_next, next_dtype, tm):
    merged = merge_call(h, branches, w["gate"], w["branch"], tm, 512)
    x1, hm = outproj_call(merged, w["out"], x, g_mlp, tm)
    return mlp_call(hm, w["up"], w["down"], x1, g_next, next_dtype, tm, 1024)


def prompt_mixers(h, n, t, w, b_forget, conv_w, cmp_w, tm):
    bw = BRANCH_W
    (z_conv,) = proj_call(h, w["conv"], [("f32", 3 * bw)], [(a, a + 512, ((0, a),)) for a in (0, 512, 1024)],
                          tm, "proj_conv")
    qkv_defs = [("f32", bw), ("f32", 2 * bw), ("bf16", bw), ("bf16T", bw)]
    qkv_plan = [(0, bw, ((0, 0),)), (bw, 2 * bw, ((1, 0), (2, 0))), (2 * bw, 3 * bw, ((1, bw), (3, 0)))]
    fox_q, fox_kv, fox_kb, fox_vt = proj_call(h, w["fox"], qkv_defs, qkv_plan, tm, "proj_fox")
    moba_q, moba_kv, moba_kb, moba_vt, moba_km = proj_call(
        h, w["moba"], qkv_defs + [("blockmean", bw)],
        [qkv_plan[0], (bw, 2 * bw, ((1, 0), (2, 0), (4, 0))), qkv_plan[2]], tm, "proj_moba")
    kv = KV_W
    nsa_q, nsa_kv, nsa_win, nsa_ks, nsa_vst, nsa_kw, nsa_vwt, nsa_cmp_t = proj_call(
        h, w["nsa"],
        [("f32", bw), ("f32", 4 * kv), ("f32", 2 * kv), ("bf16", kv), ("bf16T", kv), ("bf16", kv), ("bf16T", kv),
         ("f32T", 2 * kv)],
        [(0, bw, ((0, 0),)), (bw, bw + 2 * kv, ((1, 0), (7, 0))),
         (bw + 2 * kv, bw + 3 * kv, ((1, 2 * kv), (3, 0))), (bw + 3 * kv, bw + 4 * kv, ((1, 3 * kv), (4, 0))),
         (bw + 4 * kv, bw + 5 * kv, ((2, 0), (5, 0))), (bw + 5 * kv, bw + 6 * kv, ((2, kv), (6, 0)))],
        tm, "proj_nsa")
    (z_small,) = proj_call(h, w["small"], [("f32", SMALL_W)], [(0, SMALL_W, ((0, 0),))], tm, "proj_small")

    out_a, new_conv = conv_prompt_call(z_conv, conv_w, n, t, tm)
    lg, fk = small_call(z_small, b_forget, n, t, tm)
    logf = lg[:, :N_HEADS].reshape(n, t, N_HEADS)
    gate_t = lg[:, N_HEADS:4 * N_HEADS].T

    out_b = fox_prompt_call(fox_q, fox_kb, fox_vt, fk, n, t)
    out_c = moba_prompt_call(moba_q, moba_kb, moba_vt, moba_km, n, t)

    npages = t // PAGE
    page_idx = jnp.arange(n * npages, dtype=jnp.int32).reshape(n, npages)
    kcmp, _, vcmpt = compress_t_call(nsa_cmp_t, page_idx, *cmp_w, n, npages)
    out_d = nsa_prompt_call(nsa_q, kcmp, vcmpt, nsa_ks, nsa_vst, nsa_kw, nsa_vwt, gate_t, n, t)

    wb = min(WINDOW, t)
    new_state = (new_conv,
                 fox_kv.reshape(n, t, 2, N_HEADS, HEAD_DIM),
                 logf,
                 moba_kv.reshape(n, t, 2, N_HEADS, HEAD_DIM),
                 nsa_kv.reshape(n, t, 2, 2, NSA_G, HEAD_DIM),
                 nsa_win.reshape(n, t, 2, NSA_G, HEAD_DIM)[:, t - wb:])
    return [out_a, out_b, out_c, out_d], new_state


def sample_mixers(h, n, t, q0, past, w, b_forget, conv_w, cmp_w1, cmp_w2, cmp_pe, tm):
    dt = F32
    bw = BRANCH_W
    kv = KV_W
    (z_conv,) = proj_call(h, w["conv"], [("f32", 3 * bw)], [(0, 3 * bw, ((0, 0),))], tm, "proj_conv_s")
    fox_q, fox_kv = proj_call(h, w["fox"], [("f32", bw), ("f32", 2 * bw)],
                              [(0, bw, ((0, 0),)), (bw, 3 * bw, ((1, 0),))], tm, "proj_fox_s")
    moba_q, moba_kv = proj_call(h, w["moba"], [("f32", bw), ("f32", 2 * bw)],
                                [(0, bw, ((0, 0),)), (bw, 3 * bw, ((1, 0),))], tm, "proj_moba_s")
    nsa_q, nsa_kv, nsa_win = proj_call(
        h, w["nsa"], [("f32", bw), ("f32", 4 * kv), ("f32", 2 * kv)],
        [(0, bw, ((0, 0),)), (bw, bw + 4 * kv, ((1, 0),)), (bw + 4 * kv, bw + 6 * kv, ((2, 0),))], tm, "proj_nsa_s")
    (z_small,) = proj_call(h, w["small"], [("f32", SMALL_W)], [(0, SMALL_W, ((0, 0),))], tm, "proj_small_s")

    def heads(a, nh):
        return a.reshape(n, t, nh, HEAD_DIM)

    def cat(old, new):
        return jnp.concatenate([old.astype(new.dtype), new], axis=1)

    zc = z_conv.reshape(n, t, 3 * bw)
    conv_x, conv_b, conv_c = zc[..., :bw], zc[..., bw:2 * bw], zc[..., 2 * bw:]
    u = conv_c * conv_x
    ext = jnp.concatenate([past['conv'].astype(dt), u], axis=1)
    y_conv = ext[:, 0:t] * conv_w[0]
    for j in range(1, CONV_W):
        y_conv = y_conv + ext[:, j:j + t] * conv_w[j]
    out_a = conv_b * y_conv
    new_conv = ext[:, -(CONV_W - 1):]

    fkv = fox_kv.reshape(n, t, 2, N_HEADS, HEAD_DIM)
    fq, fk, fv = heads(fox_q, N_HEADS), fkv[:, :, 0], fkv[:, :, 1]
    fox_f = z_small.reshape(n, t, SMALL_W)[..., :N_HEADS]
    logf = jax.nn.log_sigmoid(fox_f + b_forget)
    lf_all = jnp.concatenate([past['fox_logf'].astype(F32), logf], axis=1)
    out_b = fox_attention(fq, cat(past['fox_k'], fk), cat(past['fox_v'], fv), lf_all, q0)

    mkv = moba_kv.reshape(n, t, 2, N_HEADS, HEAD_DIM)
    mq, mk, mv = heads(moba_q, N_HEADS), mkv[:, :, 0], mkv[:, :, 1]
    out_c = moba_attention(mq, cat(past['moba_k'], mk), cat(past['moba_v'], mv), q0, alibi_slopes(N_HEADS))

    nq = heads(nsa_q, N_HEADS)
    nkv = nsa_kv.reshape(n, t, 2, 2, NSA_G, HEAD_DIM)
    kc, vc, ks, vs = nkv[:, :, 0, 0], nkv[:, :, 0, 1], nkv[:, :, 1, 0], nkv[:, :, 1, 1]
    nwin = nsa_win.reshape(n, t, 2, NSA_G, HEAD_DIM)
    kw, vw = nwin[:, :, 0], nwin[:, :, 1]
    ngate = jax.nn.sigmoid(z_small.reshape(n, t, SMALL_W)[..., N_HEADS:4 * N_HEADS]).reshape(n, t, NSA_G, NSA_R, 3)
    kw_all = cat(past['win_k'], kw)
    vw_all = cat(past['win_v'], vw)
    w_ofs = q0 - past['win_k'].shape[1]
    out_d = nsa_attention(nq, cat(past['nsa_kc'], kc), cat(past['nsa_vc'], vc),
                          cat(past['nsa_ks'], ks), cat(past['nsa_vs'], vs),
                          kw_all, vw_all, ngate, q0, w_ofs, alibi_slopes(N_HEADS), cmp_w1, cmp_w2, cmp_pe)
    branches = [a.reshape(n * t, bw) for a in (out_a, out_b, out_c, out_d)]
    wb = past['win_k'].shape[1]
    new_state = (new_conv, fkv, logf.astype(dt), mkv, nkv,
                 jnp.stack([kw_all, vw_all], axis=2)[:, -wb:])
    return branches, new_state


def decode_mixers(h, n, nt, pt, caches, state_conv, state_win, w, b_forget, conv_w, cmp_w, tm):
    bw = BRANCH_W
    kv = KV_W
    cache_fox, cache_lf, cache_moba, cache_nsa = caches
    (z_conv,) = proj_call(h, w["conv"], [("f32", 3 * bw)], [(0, 3 * bw, ((0, 0),))], tm, "proj_conv_s")
    fox_q, fox_kv = proj_call(h, w["fox"], [("f32", bw), ("f32", 2 * bw)],
                              [(0, bw, ((0, 0),)), (bw, 3 * bw, ((1, 0),))], tm, "proj_fox_s")
    moba_q, moba_kv = proj_call(h, w["moba"], [("f32", bw), ("f32", 2 * bw)],
                                [(0, bw, ((0, 0),)), (bw, 3 * bw, ((1, 0),))], tm, "proj_moba_s")
    nsa_q, nsa_kv, nsa_win = proj_call(
        h, w["nsa"], [("f32", bw), ("f32", 4 * kv), ("f32", 2 * kv)],
        [(0, bw, ((0, 0),)), (bw, bw + 4 * kv, ((1, 0),)), (bw + 4 * kv, bw + 6 * kv, ((2, 0),))], tm, "proj_nsa_s")
    (z_small,) = proj_call(h, w["small"], [("f32", SMALL_W)], [(0, SMALL_W, ((0, 0),))], tm, "proj_small_s")

    def new_page(a):
        at = jnp.transpose(a.reshape(n, nt, a.shape[-1]), (0, 2, 1))
        return jnp.pad(at, ((0, 0), (0, 0), (0, PAGE - nt)))

    out_a, u = conv_sample_call(z_conv, state_conv, conv_w, n, nt)
    new_conv = u.reshape(n, nt, bw)[:, nt - (CONV_W - 1):]

    lg, _ = small_call(z_small, b_forget, 1, n * nt, n * nt)
    logf = lg[:, :N_HEADS]
    gate = lg[:, N_HEADS:4 * N_HEADS].reshape(n, nt * N_HEADS, 3)

    out_b = fox_decode_t_call(pt, fox_q.reshape(n, nt, bw), cache_fox, cache_lf, new_page(fox_kv), new_page(logf))
    mq = moba_q.reshape(n, nt, bw)
    sel = moba_select_t_call(pt, mq, cache_moba)
    out_c = moba_decode_t_call(pt, mq, sel, cache_moba, new_page(moba_kv))

    npages = pt.shape[1]
    kcmp, vcmp, _ = compress_t_call(cache_nsa, pt, *cmp_w, n, npages)
    win = state_win.reshape(n, state_win.shape[1], 2 * kv)
    out_d = nsa_decode_t_call(pt, nsa_q.reshape(n, nt, bw), gate, kcmp, vcmp, cache_nsa,
                              new_page(nsa_kv[:, 2 * kv:]), jnp.transpose(win, (0, 2, 1)), new_page(nsa_win))

    wb = win.shape[1]
    win_all = jnp.concatenate([win, nsa_win.reshape(n, nt, 2 * kv)], axis=1)[:, nt:]
    new_state = (new_conv,
                 fox_kv.reshape(n, nt, 2, N_HEADS, HEAD_DIM),
                 logf.reshape(n, nt, N_HEADS),
                 moba_kv.reshape(n, nt, 2, N_HEADS, HEAD_DIM),
                 nsa_kv.reshape(n, nt, 2, 2, NSA_G, HEAD_DIM),
                 win_all.reshape(n, wb, 2, NSA_G, HEAD_DIM))
    branches = [out_a] + [o.reshape(n * nt, bw) for o in (out_b, out_c, out_d)]
    return branches, new_state


def kernel(x_prompt, x_sample, state_conv, cache_fox_kv, cache_fox_logf, cache_moba_kv, cache_nsa_kv,
           state_nsa_win, page_table, g_mix, w_in, b_forget, conv_w, cmp_w1, cmp_w2, cmp_pe,
           w_branch, w_out, g_mlp, w_up, w_down, g_final):
    depth = w_in.shape[0]
    nb, seq, _ = x_prompt.shape
    db, dseq, _ = x_sample.shape
    tm_p, tm_s = 512, db * dseq
    pool, page = cache_fox_kv.shape[1], cache_fox_kv.shape[2]
    def pages_t(c, width):
        return jnp.transpose(c.reshape(depth * pool, page, width), (0, 2, 1))

    caches = (pages_t(cache_fox_kv, 2 * BRANCH_W), pages_t(cache_fox_logf, N_HEADS),
              pages_t(cache_moba_kv, 2 * BRANCH_W), pages_t(cache_nsa_kv, 4 * KV_W))

    xp = x_prompt.reshape(nb * seq, D_MODEL)
    xs = x_sample.reshape(db * dseq, D_MODEL)
    hp = rms_norm_call(xp, g_mix[0], BF16, tm_p)
    hs = rms_norm_call(xs, g_mix[0], BF16, tm_s)
    new_p, new_s = [], []
    for l in range(depth):
        w = _layer_weights(l, w_in, w_branch, w_out, w_up, w_down)
        cmp_w = _compress_weights(cmp_w1[l], cmp_w2[l], cmp_pe[l])
        last = l == depth - 1
        g_next = g_final if last else g_mix[l + 1]
        next_dtype = F32 if last else BF16

        branches, st_p = prompt_mixers(hp, nb, seq, w, b_forget[l], conv_w[l], cmp_w, tm_p)
        xp, hp = _finish_layer(xp, hp, branches, w, g_mlp[l], g_next, next_dtype, tm_p)
        new_p.append(st_p)

        branches, st_s = decode_mixers(hs, db, dseq, page_table + l * pool, caches, state_conv[l],
                                       state_nsa_win[l], w, b_forget[l], conv_w[l], cmp_w, tm_s)
        xs, hs = _finish_layer(xs, hs, branches, w, g_mlp[l], g_next, next_dtype, tm_s)
        new_s.append(st_s)
    y_prompt = hp.reshape(nb, seq, D_MODEL)
    y_sample = hs.reshape(db, dseq, D_MODEL)
    conv_p, fox_kv_p, fox_logf_p, moba_kv_p, nsa_kv_p, win_p = [jnp.stack(a) for a in zip(*new_p)]
    conv_s, fox_kv_s, fox_logf_s, moba_kv_s, nsa_kv_s, win_s = [jnp.stack(a) for a in zip(*new_s)]
    return (y_prompt, y_sample, conv_p, conv_s, fox_kv_p, fox_kv_s, fox_logf_p, fox_logf_s,
            moba_kv_p, moba_kv_s, nsa_kv_p, nsa_kv_s, win_p, win_s)
```

```python
import functools

import jax
import jax.numpy as jnp
from jax import lax
from jax.experimental import pallas as pl
from jax.experimental.pallas import tpu as pltpu

F32 = jnp.float32
BF16 = jnp.bfloat16
HIGHEST = lax.Precision.HIGHEST

D_MODEL = 2048
HEAD_DIM = 64
N_BRANCH = 4
BRANCH_W = D_MODEL // N_BRANCH
N_HEADS = BRANCH_W // HEAD_DIM
CONV_W = 3
NSA_G = 2
NSA_R = N_HEADS // NSA_G
MOBA_BLOCK = 256
MOBA_TOPK = 3
CMP_L = 32
CMP_S = 16
CMP_HID = 4 * HEAD_DIM
SLC_BLOCK = 64
SLC_N = 16
WINDOW = 512
D_FF = 4 * D_MODEL
Q_BLOCK = 128
SPARSE_Q_BLOCK = 32
RMS_EPS = 1e-6
NEG = -1e30
MASKED = 2.0 * NEG
FORCE_SCORE = 1e4
KV_W = NSA_G * HEAD_DIM
SCALE = HEAD_DIM ** -0.5
LOG2E = 1.4426950408889634
QSCALE2 = SCALE * LOG2E
PAIR_W = 2 * HEAD_DIM

OFF_CONV = 0
OFF_FOX = 3 * BRANCH_W
OFF_FOX_F = OFF_FOX + 3 * BRANCH_W
OFF_MOBA = OFF_FOX_F + N_HEADS
OFF_NSA = OFF_MOBA + 3 * BRANCH_W
OFF_NSA_GATE = OFF_NSA + BRANCH_W + 6 * KV_W
OFF_MERGE = OFF_NSA_GATE + 3 * N_HEADS
IN_W = OFF_MERGE + N_BRANCH * D_MODEL
SMALL_W = 128

TQ = 256
TK = 256
CMP_ROW = CMP_S * 2 * KV_W
PAGE_ROWS = 8

VMEM_LIMIT = 56 * 1024 * 1024


def _cparams(*sem):
    return pltpu.CompilerParams(dimension_semantics=sem, vmem_limit_bytes=VMEM_LIMIT)


def _rms(x, g):
    return x * lax.rsqrt(jnp.mean(x * x, axis=-1, keepdims=True) + RMS_EPS) * g


def _dot_nt(a, b, precision=None):
    return lax.dot_general(a, b, (((1,), (1,)), ((), ())), preferred_element_type=F32, precision=precision)


def _norm_kernel(x_ref, g_ref, o_ref):
    o_ref[...] = _rms(x_ref[...], g_ref[...]).astype(o_ref.dtype)


def rms_norm_call(x, g, out_dtype, tm):
    t, d = x.shape
    return pl.pallas_call(
        _norm_kernel,
        grid=(t // tm,),
        in_specs=[pl.BlockSpec((tm, d), lambda i: (i, 0)),
                  pl.BlockSpec((1, d), lambda i: (0, 0))],
        out_specs=pl.BlockSpec((tm, d), lambda i: (i, 0)),
        out_shape=jax.ShapeDtypeStruct((t, d), out_dtype),
        compiler_params=_cparams("parallel"),
        name="rms_norm",
    )(x, g.reshape(1, d))


def _proj_kernel(h_ref, w_ref, *out_refs, kinds, plan):
    h = h_ref[...]
    for c0, c1, dests in plan:
        z = jnp.dot(h, w_ref[:, c0:c1], preferred_element_type=F32)
        for idx, off in dests:
            o_ref, kind = out_refs[idx], kinds[idx]
            if kind == "bf16T":
                for r in range(z.shape[0] // TK):
                    o_ref[r, off:off + c1 - c0, :] = z[r * TK:(r + 1) * TK].T.astype(BF16)
            elif kind == "f32T":
                for r in range(z.shape[0] // 128):
                    o_ref[r, off:off + c1 - c0, :] = z[r * 128:(r + 1) * 128].T
            elif kind == "blockmean":
                for r in range(z.shape[0] // MOBA_BLOCK):
                    o_ref[r, :, off:off + c1 - c0] = jnp.mean(
                        z[r * MOBA_BLOCK:(r + 1) * MOBA_BLOCK], axis=0, keepdims=True)
            else:
                o_ref[:, off:off + c1 - c0] = z.astype(o_ref.dtype)


def proj_call(h, w, out_defs, plan, tm, name):
    t, d = h.shape
    n = w.shape[1]
    out_specs, out_shapes = [], []
    for kind, width in out_defs:
        if kind == "bf16T":
            out_specs.append(pl.BlockSpec((tm // TK, width, TK), lambda i: (i, 0, 0)))
            out_shapes.append(jax.ShapeDtypeStruct((t // TK, width, TK), BF16))
        elif kind == "f32T":
            out_specs.append(pl.BlockSpec((tm // 128, width, 128), lambda i: (i, 0, 0)))
            out_shapes.append(jax.ShapeDtypeStruct((t // 128, width, 128), F32))
        elif kind == "blockmean":
            out_specs.append(pl.BlockSpec((tm // MOBA_BLOCK, 1, width), lambda i: (i, 0, 0)))
            out_shapes.append(jax.ShapeDtypeStruct((t // MOBA_BLOCK, 1, width), F32))
        else:
            out_specs.append(pl.BlockSpec((tm, width), lambda i: (i, 0)))
            out_shapes.append(jax.ShapeDtypeStruct((t, width), BF16 if kind == "bf16" else F32))
    return pl.pallas_call(
        functools.partial(_proj_kernel, kinds=tuple(k for k, _ in out_defs), plan=tuple(plan)),
        grid=(t // tm,),
        in_specs=[pl.BlockSpec((tm, d), lambda i: (i, 0)),
                  pl.BlockSpec((d, n), lambda i: (0, 0))],
        out_specs=out_specs,
        out_shape=out_shapes,
        compiler_params=_cparams("parallel"),
        name=name,
    )(h, w)


def _gate_kernel(h_ref, w_ref, o_ref):
    z = jnp.dot(h_ref[...], w_ref[...], preferred_element_type=F32)
    o_ref[...] = jax.nn.sigmoid(z)


def gate_call(h, w, tm, tn):
    t, d = h.shape
    n = w.shape[1]
    return pl.pallas_call(
        _gate_kernel,
        grid=(t // tm, n // tn),
        in_specs=[pl.BlockSpec((tm, d), lambda i, j: (i, 0)),
                  pl.BlockSpec((d, tn), lambda i, j: (0, j))],
        out_specs=pl.BlockSpec((tm, tn), lambda i, j: (i, j)),
        out_shape=jax.ShapeDtypeStruct((t, n), F32),
        compiler_params=_cparams("parallel", "arbitrary"),
        name="merge_gate_proj",
    )(h, w)


def _merge_kernel(h_ref, oa_ref, ob_ref, oc_ref, od_ref, g0_ref, g1_ref, g2_ref, g3_ref, wb_ref, o_ref):
    h = h_ref[...]
    acc = None
    for b, (o, wg) in enumerate(zip((oa_ref, ob_ref, oc_ref, od_ref), (g0_ref, g1_ref, g2_ref, g3_ref))):
        gate = jax.nn.sigmoid(jnp.dot(h, wg[...], preferred_element_type=F32))
        br = jnp.dot(o[...].astype(BF16), wb_ref[b], preferred_element_type=F32)
        term = gate * br
        acc = term if acc is None else acc + term
    o_ref[...] = acc.astype(o_ref.dtype)


def merge_call(h, branches, w_gate, w_branch, tm, tn):
    t, d = h.shape
    nj = D_MODEL // tn
    gate_specs = [pl.BlockSpec((d, tn), functools.partial(lambda i, j, b: (0, b * nj + j), b=b))
                  for b in range(N_BRANCH)]
    return pl.pallas_call(
        _merge_kernel,
        grid=(t // tm, nj),
        in_specs=[pl.BlockSpec((tm, d), lambda i, j: (i, 0))]
        + [pl.BlockSpec((tm, BRANCH_W), lambda i, j: (i, 0))] * N_BRANCH + gate_specs
        + [pl.BlockSpec((N_BRANCH, BRANCH_W, tn), lambda i, j: (0, 0, j))],
        out_specs=pl.BlockSpec((tm, tn), lambda i, j: (i, j)),
        out_shape=jax.ShapeDtypeStruct((t, D_MODEL), BF16),
        compiler_params=_cparams("parallel", "arbitrary"),
        name="branch_merge",
    )(h, *branches, w_gate, w_gate, w_gate, w_gate, w_branch)


def _outproj_kernel(m_ref, w_ref, x_ref, g_ref, xo_ref, hn_ref):
    xn = x_ref[...] + jnp.dot(m_ref[...], w_ref[...], preferred_element_type=F32)
    xo_ref[...] = xn
    hn_ref[...] = _rms(xn, g_ref[...]).astype(hn_ref.dtype)


def outproj_call(merged, w_out, x, g_next, tm):
    t = x.shape[0]
    return pl.pallas_call(
        _outproj_kernel,
        grid=(t // tm,),
        in_specs=[pl.BlockSpec((tm, D_MODEL), lambda i: (i, 0)),
                  pl.BlockSpec((D_MODEL, D_MODEL), lambda i: (0, 0)),
                  pl.BlockSpec((tm, D_MODEL), lambda i: (i, 0)),
                  pl.BlockSpec((1, D_MODEL), lambda i: (0, 0))],
        out_specs=[pl.BlockSpec((tm, D_MODEL), lambda i: (i, 0)),
                   pl.BlockSpec((tm, D_MODEL), lambda i: (i, 0))],
        out_shape=[jax.ShapeDtypeStruct((t, D_MODEL), F32),
                   jax.ShapeDtypeStruct((t, D_MODEL), BF16)],
        compiler_params=_cparams("parallel"),
        name="out_proj",
    )(merged, w_out, x, g_next.reshape(1, D_MODEL))


def _mlp_kernel(h_ref, wu_ref, wd_ref, x_ref, g_ref, xo_ref, hn_ref, acc_ref):
    j = pl.program_id(1)

    @pl.when(j == 0)
    def _():
        acc_ref[...] = jnp.zeros_like(acc_ref)

    a = jnp.dot(h_ref[...], wu_ref[...], preferred_element_type=F32)
    a = jnp.square(jnp.maximum(a, 0.0)).astype(BF16)
    acc_ref[...] += jnp.dot(a, wd_ref[...], preferred_element_type=F32)

    @pl.when(j == pl.num_programs(1) - 1)
    def _():
        xn = x_ref[...] + acc_ref[...]
        xo_ref[...] = xn
        hn_ref[...] = _rms(xn, g_ref[...]).astype(hn_ref.dtype)


def mlp_call(h, w_up, w_down, x, g_next, next_dtype, tm, tf):
    t = x.shape[0]
    return pl.pallas_call(
        _mlp_kernel,
        grid=(t // tm, D_FF // tf),
        in_specs=[pl.BlockSpec((tm, D_MODEL), lambda i, j: (i, 0)),
                  pl.BlockSpec((D_MODEL, tf), lambda i, j: (0, j)),
                  pl.BlockSpec((tf, D_MODEL), lambda i, j: (j, 0)),
                  pl.BlockSpec((tm, D_MODEL), lambda i, j: (i, 0)),
                  pl.BlockSpec((1, D_MODEL), lambda i, j: (0, 0))],
        out_specs=[pl.BlockSpec((tm, D_MODEL), lambda i, j: (i, 0)),
                   pl.BlockSpec((tm, D_MODEL), lambda i, j: (i, 0))],
        out_shape=[jax.ShapeDtypeStruct((t, D_MODEL), F32),
                   jax.ShapeDtypeStruct((t, D_MODEL), next_dtype)],
        scratch_shapes=[pltpu.VMEM((tm, D_MODEL), F32)],
        compiler_params=_cparams("parallel", "arbitrary"),
        name="mlp",
    )(h, w_up, w_down, x, g_next.reshape(1, D_MODEL))


def _small_kernel(z_ref, b_ref, a_ref, f_ref, carry_ref):
    @pl.when(pl.program_id(1) == 0)
    def _():
        carry_ref[...] = jnp.zeros_like(carry_ref)

    z = z_ref[...]
    tm = z.shape[0]
    lane = lax.broadcasted_iota(jnp.int32, z.shape, 1)
    pre = z + b_ref[...]
    lf = jnp.minimum(pre, 0.0) - jnp.log1p(jnp.exp(-jnp.abs(pre)))
    lf = jnp.where(lane < N_HEADS, lf, 0.0)
    a_ref[...] = jnp.where(lane < N_HEADS, lf, jnp.where(lane < 4 * N_HEADS, jax.nn.sigmoid(z), 0.0))
    row = lax.broadcasted_iota(jnp.int32, (tm, tm), 0)
    col = lax.broadcasted_iota(jnp.int32, (tm, tm), 1)
    tril = jnp.where(col <= row, 1.0, 0.0)
    f = jnp.dot(tril, lf, preferred_element_type=F32, precision=HIGHEST) + carry_ref[0:1, :]
    f_ref[...] = f
    carry_ref[0:1, :] = f[tm - 1:tm, :]


def small_call(z_small, b_forget, n, t, tm):
    bias = jnp.zeros((1, SMALL_W), F32).at[0, :N_HEADS].set(b_forget)
    nt = t // tm
    return pl.pallas_call(
        _small_kernel,
        grid=(n, nt),
        in_specs=[pl.BlockSpec((tm, SMALL_W), lambda b, j: (b * nt + j, 0)),
                  pl.BlockSpec((1, SMALL_W), lambda b, j: (0, 0))],
        out_specs=[pl.BlockSpec((tm, SMALL_W), lambda b, j: (b * nt + j, 0)),
                   pl.BlockSpec((tm, SMALL_W), lambda b, j: (b * nt + j, 0))],
        out_shape=[jax.ShapeDtypeStruct((n * t, SMALL_W), F32),
                   jax.ShapeDtypeStruct((n * t, SMALL_W), F32)],
        scratch_shapes=[pltpu.VMEM((8, SMALL_W), F32)],
        compiler_params=_cparams("parallel", "arbitrary"),
        name="forget_and_gates",
    )(z_small, bias)


def _conv_prompt_kernel(z_ref, w_ref, o_ref, st_ref, prev_ref):
    @pl.when(pl.program_id(1) == 0)
    def _():
        prev_ref[...] = jnp.zeros_like(prev_ref)

    bw = BRANCH_W
    u = z_ref[:, 2 * bw:3 * bw] * z_ref[:, 0:bw]
    tm = u.shape[0]
    row = lax.broadcasted_iota(jnp.int32, u.shape, 0)
    u1 = jnp.where(row == 0, prev_ref[7:8, :], pltpu.roll(u, 1, 0))
    u2 = jnp.where(row == 0, prev_ref[6:7, :], jnp.where(row == 1, prev_ref[7:8, :], pltpu.roll(u, 2, 0)))
    y = u2 * w_ref[0:1, :] + u1 * w_ref[1:2, :] + u * w_ref[2:3, :]
    o_ref[...] = z_ref[:, bw:2 * bw] * y
    prev_ref[...] = u[tm - 8:tm]
    st_ref[0] = u[tm - 2:tm]


def conv_prompt_call(z_conv, conv_w, n, t, tm):
    nt = t // tm
    return pl.pallas_call(
        _conv_prompt_kernel,
        grid=(n, nt),
        in_specs=[pl.BlockSpec((tm, 3 * BRANCH_W), lambda b, j: (b * nt + j, 0)),
                  pl.BlockSpec((CONV_W, BRANCH_W), lambda b, j: (0, 0))],
        out_specs=[pl.BlockSpec((tm, BRANCH_W), lambda b, j: (b * nt + j, 0)),
                   pl.BlockSpec((1, CONV_W - 1, BRANCH_W), lambda b, j: (b, 0, 0))],
        out_shape=[jax.ShapeDtypeStruct((n * t, BRANCH_W), F32),
                   jax.ShapeDtypeStruct((n, CONV_W - 1, BRANCH_W), F32)],
        scratch_shapes=[pltpu.VMEM((8, BRANCH_W), F32)],
        compiler_params=_cparams("parallel", "arbitrary"),
        name="conv_prompt",
    )(z_conv, conv_w)


def _online_step(s, vt, carry):
    m, l, acc = carry
    m_new = jnp.maximum(m, jnp.max(s, axis=0, keepdims=True))
    alpha = jnp.exp2(m - m_new)
    p = jnp.exp2(s - m_new)
    l = alpha * l + jnp.sum(p, axis=0, keepdims=True)
    acc = alpha * acc + jnp.dot(vt, p.astype(BF16), preferred_element_type=F32)
    return m_new, l, acc


def _softmax_init(width=TQ):
    return (jnp.full((1, width), NEG, F32), jnp.zeros((1, width), F32), jnp.zeros((HEAD_DIM, width), F32))


def _query_pair(q_ref, h, half):
    hp, e = divmod(h, 2)
    qp = q_ref[:, hp * PAIR_W:(hp + 1) * PAIR_W]
    if e != half:
        qp = pltpu.roll(qp, HEAD_DIM, 1)
    lane = lax.broadcasted_iota(jnp.int32, qp.shape, 1)
    return jnp.where((lane // HEAD_DIM) == half, qp, 0.0)


def _tile_masks():
    sub = lax.broadcasted_iota(jnp.int32, (TK, TQ), 0)
    lane = lax.broadcasted_iota(jnp.int32, (TK, TQ), 1)
    return sub <= lane, sub > lane


def _key_col():
    return lax.broadcasted_iota(jnp.int32, (TK, 1), 0).astype(F32)


def _fox_kernel(q_ref, k_ref, vt_ref, fk_ref, o_ref, ot_ref):
    i = pl.program_id(1)
    causal, _ = _tile_masks()
    qcats = [jnp.concatenate([(_query_pair(q_ref, 2 * hp + e, e) * QSCALE2).astype(BF16) for e in (0, 1)], axis=0)
             for hp in range(N_HEADS // 2)]

    def tile(j, carry, diag):
        r0 = pl.multiple_of(j * TK, TK)
        out = []
        for hp in range(N_HEADS // 2):
            s2 = _dot_nt(k_ref[pl.ds(r0, TK), hp * PAIR_W:(hp + 1) * PAIR_W], qcats[hp])
            for e in (0, 1):
                h = 2 * hp + e
                s = s2[:, e * TQ:(e + 1) * TQ] - LOG2E * fk_ref[pl.ds(r0, TK), h:h + 1]
                if diag:
                    s = jnp.where(causal, s, MASKED)
                out.append(_online_step(s, vt_ref[j, h * HEAD_DIM:(h + 1) * HEAD_DIM, :], carry[h]))
        return tuple(out)

    carry = lax.fori_loop(0, i, lambda j, c: tile(j, c, False), tuple(_softmax_init() for _ in range(N_HEADS)))
    for h, (_, l, acc) in enumerate(tile(i, carry, True)):
        ot_ref[h * HEAD_DIM:(h + 1) * HEAD_DIM, :] = acc * (1.0 / l)
    o_ref[...] = ot_ref[...].T


def fox_prompt_call(q, kb, vt, fk, n, t):
    nq = t // TQ
    return pl.pallas_call(
        _fox_kernel,
        grid=(n, nq),
        in_specs=[pl.BlockSpec((TQ, BRANCH_W), lambda b, i: (b * nq + i, 0)),
                  pl.BlockSpec((t, BRANCH_W), lambda b, i: (b, 0)),
                  pl.BlockSpec((t // TK, BRANCH_W, TK), lambda b, i: (b, 0, 0)),
                  pl.BlockSpec((t, SMALL_W), lambda b, i: (b, 0))],
        out_specs=pl.BlockSpec((TQ, BRANCH_W), lambda b, i: (b * nq + i, 0)),
        out_shape=jax.ShapeDtypeStruct((n * t, BRANCH_W), F32),
        scratch_shapes=[pltpu.VMEM((BRANCH_W, TQ), F32)],
        compiler_params=_cparams("parallel", "arbitrary"),
        name="fox_prompt",
    )(q, kb, vt, fk)


def _rank_before(score, bidx, nblk):
    cnt = jnp.zeros(score.shape, F32)
    for b2 in range(nblk):
        row = score[b2:b2 + 1, :]
        beats = (row > score) | ((row == score) & (b2 < bidx))
        cnt = cnt + jnp.where(beats, 1.0, 0.0)
    return cnt


def _moba_kernel(q_ref, k_ref, vt_ref, km_ref, o_ref, ot_ref, sel_ref, *, nblk):
    i = pl.program_id(1)
    causal, _ = _tile_masks()
    kcol = _key_col()
    bidx = lax.broadcasted_iota(jnp.int32, (nblk, TQ), 0)
    qcats = []
    for hp in range(N_HEADS // 2):
        qpads = []
        for e in (0, 1):
            h = 2 * hp + e
            q32 = _query_pair(q_ref, h, e)
            qpads.append((q32 * QSCALE2).astype(BF16))
            gs = _dot_nt(km_ref[0, :, hp * PAIR_W:(hp + 1) * PAIR_W], q32, precision=HIGHEST)
            gs = jnp.where(bidx < i, gs, NEG)
            sel = (_rank_before(gs, bidx, nblk) < MOBA_TOPK) & (bidx < i)
            sel_ref[:, h * TQ:(h + 1) * TQ] = jnp.where(sel, 1.0, 0.0)
        qcats.append(jnp.concatenate(qpads, axis=0))

    def tile(j, carry, diag):
        r0 = pl.multiple_of(j * TK, TK)
        dist = kcol - ((i - j) * TK).astype(F32)
        out = []
        for hp in range(N_HEADS // 2):
            s2 = _dot_nt(k_ref[pl.ds(r0, TK), hp * PAIR_W:(hp + 1) * PAIR_W], qcats[hp])
            for e in (0, 1):
                h = 2 * hp + e
                s = s2[:, e * TQ:(e + 1) * TQ] + LOG2E * 2.0 ** (-8.0 * (h + 1) / N_HEADS) * dist
                if diag:
                    s = jnp.where(causal, s, MASKED)
                else:
                    s = jnp.where(sel_ref[pl.ds(j, 1), h * TQ:(h + 1) * TQ] > 0.5, s, MASKED)
                out.append(_online_step(s, vt_ref[j, h * HEAD_DIM:(h + 1) * HEAD_DIM, :], carry[h]))
        return tuple(out)

    carry = lax.fori_loop(0, i, lambda j, c: tile(j, c, False), tuple(_softmax_init() for _ in range(N_HEADS)))
    for h, (_, l, acc) in enumerate(tile(i, carry, True)):
        ot_ref[h * HEAD_DIM:(h + 1) * HEAD_DIM, :] = acc * (1.0 / l)
    o_ref[...] = ot_ref[...].T


def moba_prompt_call(q, kb, vt, kmean, n, t):
    nq = t // TQ
    nblk = t // MOBA_BLOCK
    return pl.pallas_call(
        functools.partial(_moba_kernel, nblk=nblk),
        grid=(n, nq),
        in_specs=[pl.BlockSpec((TQ, BRANCH_W), lambda b, i: (b * nq + i, 0)),
                  pl.BlockSpec((t, BRANCH_W), lambda b, i: (b, 0)),
                  pl.BlockSpec((t // TK, BRANCH_W, TK), lambda b, i: (b, 0, 0)),
                  pl.BlockSpec((1, nblk, BRANCH_W), lambda b, i: (b, 0, 0))],
        out_specs=pl.BlockSpec((TQ, BRANCH_W), lambda b, i: (b * nq + i, 0)),
        out_shape=jax.ShapeDtypeStruct((n * t, BRANCH_W), F32),
        scratch_shapes=[pltpu.VMEM((BRANCH_W, TQ), F32), pltpu.VMEM((nblk, N_HEADS * TQ), F32)],
        compiler_params=_cparams("parallel", "arbitrary"),
        name="moba_prompt",
    )(q, kb, vt, kmean.reshape(n, nblk, BRANCH_W))


def _gelu_tanh(x):
    return 0.5 * x * (1.0 + jnp.tanh(0.7978845608028654 * (x + 0.044715 * x * x * x)))


def _compress_kernel(pt_ref, a_ref, pet_ref, peb_ref, wt_ref, wb_ref, w2k_ref, w2v_ref, w2vt_ref,
                     kc_ref, vc_ref, vct_ref, rows_ref, *, rows):
    del pt_ref
    j = pl.program_id(1)
    r0 = pl.multiple_of(j * PAGE_ROWS, PAGE_ROWS)
    for i in range(CMP_S):
        rows_ref[pl.ds(r0, PAGE_ROWS), i * 2 * KV_W:(i + 1) * 2 * KV_W] = a_ref[0, :, i * 4 * KV_W:i * 4 * KV_W + 2 * KV_W]

    @pl.when(j == pl.num_programs(1) - 1)
    def _():
        chunk = min(256, rows)
        prev_top = jnp.zeros((1, 4 * CMP_HID), F32)
        for c in range(rows // chunk):
            a = rows_ref[c * chunk:(c + 1) * chunk, :]
            ht = jnp.dot((a + pet_ref[...]).astype(BF16), wt_ref[...], preferred_element_type=F32)
            hb = jnp.dot((a + peb_ref[...]).astype(BF16), wb_ref[...], preferred_element_type=F32)
            row = lax.broadcasted_iota(jnp.int32, ht.shape, 0)
            shifted = jnp.where(row == 0, prev_top, pltpu.roll(ht, 1, 0))
            prev_top = ht[chunk - 1:chunk, :]
            g = _gelu_tanh(shifted + hb).astype(BF16)
            kc_ref[0, c * chunk:(c + 1) * chunk, :] = jnp.dot(
                g[:, :2 * CMP_HID], w2k_ref[...], preferred_element_type=F32).astype(BF16)
            vc_ref[0, c * chunk:(c + 1) * chunk, :] = jnp.dot(
                g[:, 2 * CMP_HID:], w2v_ref[...], preferred_element_type=F32).astype(BF16)
            vct_ref[0, :, c * chunk:(c + 1) * chunk] = _dot_nt(w2vt_ref[...], g[:, 2 * CMP_HID:]).astype(BF16)


def compress_call(pages, page_idx, pe_top, pe_bot, wt, wb, w2k, w2v, w2vt, n, npages):
    rows = npages * PAGE_ROWS
    const = lambda b, j, pt: (0, 0)
    grid_spec = pltpu.PrefetchScalarGridSpec(
        num_scalar_prefetch=1,
        grid=(n, npages),
        in_specs=[pl.BlockSpec((1, PAGE_ROWS, CMP_S * 4 * KV_W), lambda b, j, pt: (pt[b, j], 0, 0)),
                  pl.BlockSpec((1, CMP_ROW), const),
                  pl.BlockSpec((1, CMP_ROW), const),
                  pl.BlockSpec((CMP_ROW, 4 * CMP_HID), const, pipeline_mode=pl.Buffered(1)),
                  pl.BlockSpec((CMP_ROW, 4 * CMP_HID), const, pipeline_mode=pl.Buffered(1)),
                  pl.BlockSpec((2 * CMP_HID, KV_W), const),
                  pl.BlockSpec((2 * CMP_HID, KV_W), const),
                  pl.BlockSpec((KV_W, 2 * CMP_HID), const)],
        out_specs=[pl.BlockSpec((1, rows, KV_W), lambda b, j, pt: (b, 0, 0)),
                   pl.BlockSpec((1, rows, KV_W), lambda b, j, pt: (b, 0, 0)),
                   pl.BlockSpec((1, KV_W, rows), lambda b, j, pt: (b, 0, 0))],
        scratch_shapes=[pltpu.VMEM((rows, CMP_ROW), F32)],
    )
    return pl.pallas_call(
        functools.partial(_compress_kernel, rows=rows),
        grid_spec=grid_spec,
        out_shape=[jax.ShapeDtypeStruct((n, rows, KV_W), BF16),
                   jax.ShapeDtypeStruct((n, rows, KV_W), BF16),
                   jax.ShapeDtypeStruct((n, KV_W, rows), BF16)],
        compiler_params=_cparams("parallel", "arbitrary"),
        name="nsa_compress",
    )(page_idx, pages, pe_top, pe_bot, wt, wb, w2k, w2v, w2vt)


def _nsa_kernel(q_ref, kc_ref, vct_ref, ks_ref, vst_ref, kw_ref, vwt_ref, gt_ref, o_ref,
                ot_ref, sel_ref, *, ncmp, nsb):
    i = pl.program_id(1)
    causal, below = _tile_masks()
    kcol = _key_col()
    lane_q = lax.broadcasted_iota(jnp.int32, (1, TQ), 1)
    qpos = i * TQ + lane_q
    crow = lax.broadcasted_iota(jnp.int32, (ncmp, 1), 0)
    cpos = CMP_S * crow + (CMP_S - 1)
    cvalid = (crow >= 1) & (cpos <= qpos)
    cposf = cpos.astype(F32)
    mj = lax.broadcasted_iota(jnp.int32, (nsb, ncmp), 0)
    mc = lax.broadcasted_iota(jnp.int32, (nsb, ncmp), 1)
    ratio = SLC_BLOCK // CMP_S
    gather_m = jnp.where((mc >= ratio * mj) & (mc <= ratio * mj + ratio) & (mc >= 1), 1.0, 0.0)
    bj = lax.broadcasted_iota(jnp.int32, (nsb, TQ), 0)
    own = qpos // SLC_BLOCK
    forced = (bj == 0) | (bj >= own - 1)
    allowed = bj <= own
    ecol = lax.broadcasted_iota(jnp.int32, (TK, nsb), 1)
    erow = lax.broadcasted_iota(jnp.int32, (TK, nsb), 0) // SLC_BLOCK
    gw = NSA_R * TQ
    sub4 = lax.broadcasted_iota(jnp.int32, (TK, gw), 0)
    lane4 = lax.broadcasted_iota(jnp.int32, (TK, gw), 1) % TQ
    causal4, below4 = sub4 <= lane4, sub4 > lane4
    cvalid4 = (crow >= 1) & (cpos <= i * TQ + lax.broadcasted_iota(jnp.int32, (1, gw), 1) % TQ)

    for g in range(NSA_G):
        heads = [g * NSA_R + r for r in range(NSA_R)]
        q4 = jnp.concatenate([(_query_pair(q_ref, h, g) * QSCALE2).astype(BF16) for h in heads], axis=0)
        slope_row = jnp.concatenate([jnp.full((1, TQ), LOG2E * 2.0 ** (-8.0 * (h + 1) / N_HEADS), F32) for h in heads],
                                    axis=1)
        gates = [jnp.concatenate([gt_ref[3 * h + c:3 * h + c + 1, :] for h in heads], axis=1) for c in range(3)]
        v_rows = slice(g * HEAD_DIM, (g + 1) * HEAD_DIM)
        s = _dot_nt(kc_ref[0], q4) + slope_row * (cposf - (i * TQ).astype(F32))
        s = jnp.where(cvalid4, s, MASKED)
        p = jnp.where(cvalid4, jnp.exp2(s - jnp.max(s, axis=0, keepdims=True)), 0.0)
        l = jnp.sum(p, axis=0, keepdims=True)
        p = p * (1.0 / jnp.where(l > 0.0, l, 1.0))
        imp = p[:, 0:TQ]
        for r in range(1, NSA_R):
            imp = imp + p[:, r * TQ:(r + 1) * TQ]
        o_all = gates[0] * jnp.dot(vct_ref[0, v_rows, :], p.astype(BF16), preferred_element_type=F32)
        p_slc = jnp.dot(gather_m, imp, preferred_element_type=F32, precision=HIGHEST)
        score = jnp.where(allowed, jnp.where(forced, p_slc + FORCE_SCORE, p_slc), -1.0)
        sel = jnp.where((_rank_before(score, bj, nsb) < SLC_N) & allowed, 1.0, 0.0)
        sel_ref[...] = jnp.concatenate([sel] * NSA_R, axis=1).astype(BF16)

        def alibi(j):
            return slope_row * (kcol - ((i - j) * TK).astype(F32))

        def slc_tile(j, carry, diag):
            r0 = pl.multiple_of(j * TK, TK)
            expand = jnp.where(ecol == erow + j * (TK // SLC_BLOCK), 1.0, 0.0).astype(BF16)
            keep = jnp.dot(expand, sel_ref[...], preferred_element_type=F32) > 0.5
            if diag:
                keep = keep & causal4
            s = jnp.where(keep, _dot_nt(ks_ref[pl.ds(r0, TK), :], q4) + alibi(j), MASKED)
            return _online_step(s, vst_ref[j, v_rows, :], carry)

        carry = lax.fori_loop(0, i, lambda j, c: slc_tile(j, c, False), _softmax_init(gw))
        _, l, acc = slc_tile(i, carry, True)
        o_all = o_all + gates[1] * (acc * (1.0 / l))

        def win_tile(j, carry, mask):
            r0 = pl.multiple_of(j * TK, TK)
            s = _dot_nt(kw_ref[pl.ds(r0, TK), :], q4) + alibi(j)
            if mask is not None:
                s = jnp.where(mask, s, MASKED)
            return _online_step(s, vwt_ref[j, v_rows, :], carry)

        carry = _softmax_init(gw)
        carry = lax.cond(i >= 2, lambda c: win_tile(i - 2, c, below4), lambda c: c, carry)
        carry = lax.cond(i >= 1, lambda c: win_tile(i - 1, c, None), lambda c: c, carry)
        _, l, acc = win_tile(i, carry, causal4)
        o_all = o_all + gates[2] * (acc * (1.0 / l))
        for r, h in enumerate(heads):
            ot_ref[h * HEAD_DIM:(h + 1) * HEAD_DIM, :] = o_all[:, r * TQ:(r + 1) * TQ]
    o_ref[...] = ot_ref[...].T


def nsa_prompt_call(q, kcmp, vcmpt, ks, vst, kw, vwt, gate_t, n, t):
    nq = t // TQ
    ncmp = kcmp.shape[1]
    nsb = t // SLC_BLOCK
    return pl.pallas_call(
        functools.partial(_nsa_kernel, ncmp=ncmp, nsb=nsb),
        grid=(n, nq),
        in_specs=[pl.BlockSpec((TQ, BRANCH_W), lambda b, i: (b * nq + i, 0)),
                  pl.BlockSpec((1, ncmp, KV_W), lambda b, i: (b, 0, 0)),
                  pl.BlockSpec((1, KV_W, ncmp), lambda b, i: (b, 0, 0)),
                  pl.BlockSpec((t, KV_W), lambda b, i: (b, 0)),
                  pl.BlockSpec((t // TK, KV_W, TK), lambda b, i: (b, 0, 0)),
                  pl.BlockSpec((t, KV_W), lambda b, i: (b, 0)),
                  pl.BlockSpec((t // TK, KV_W, TK), lambda b, i: (b, 0, 0)),
                  pl.BlockSpec((3 * N_HEADS, TQ), lambda b, i: (0, b * nq + i))],
        out_specs=pl.BlockSpec((TQ, BRANCH_W), lambda b, i: (b * nq + i, 0)),
        out_shape=jax.ShapeDtypeStruct((n * t, BRANCH_W), F32),
        scratch_shapes=[pltpu.VMEM((BRANCH_W, TQ), F32), pltpu.VMEM((nsb, NSA_R * TQ), BF16)],
        compiler_params=_cparams("parallel", "arbitrary"),
        name="nsa_prompt",
    )(q, kcmp, vcmpt, ks, vst, kw, vwt, gate_t)


def alibi_slopes(n):
    return jnp.exp2(-8.0 * jnp.arange(1, n + 1, dtype=jnp.float32) / n)


def masked_softmax(s, mask, axis=-1):
    p = jax.nn.softmax(jnp.where(mask, s, NEG), axis=axis)
    return p * mask


def sweep_queries(fn, block, *qs):
    n, t = qs[0].shape[:2]
    qb = block if t % block == 0 else t
    nb = t // qb
    xs = tuple(jnp.moveaxis(a.reshape((n, nb, qb) + a.shape[2:]), 1, 0) for a in qs)
    out = lax.map(lambda args: fn(args[0] * qb, *args[1:]), (jnp.arange(nb, dtype=jnp.int32),) + xs)
    return jnp.moveaxis(out, 0, 1).reshape((n, t) + out.shape[3:])


def gather_pages(pool, page_table):
    g = pool[page_table]
    return g.reshape((g.shape[0], g.shape[1] * g.shape[2]) + g.shape[3:])


def fox_attention(q, k, v, logf, q0):
    d = q.shape[-1]
    L = k.shape[1]
    scale = d ** -0.5
    F = jnp.cumsum(logf.astype(jnp.float32), axis=1)
    Fk = jnp.moveaxis(F, 1, 2)
    Fq = F[:, q0:]
    kpos = jnp.arange(L)

    def block(start, qb, fq):
        qpos = q0 + start + jnp.arange(qb.shape[1])
        s = jnp.einsum('nqhd,nkhd->nhqk', qb, k).astype(jnp.float32) * scale
        s = s + jnp.moveaxis(fq, 1, 2)[..., None] - Fk[:, :, None, :]
        p = masked_softmax(s, kpos[None, :] <= qpos[:, None])
        return jnp.einsum('nhqk,nkhd->nqhd', p.astype(q.dtype), v)

    return sweep_queries(block, Q_BLOCK, q, Fq)


def moba_attention(q, k, v, q0, slopes):
    n, tq, nh, d = q.shape
    L = k.shape[1]
    scale = d ** -0.5
    nbk = -(-L // MOBA_BLOCK)
    pad = nbk * MOBA_BLOCK - L

    def to_blocks(a):
        a = jnp.pad(a, ((0, 0), (0, pad), (0, 0), (0, 0)))
        return a.reshape(n, nbk, MOBA_BLOCK, nh, d).transpose(0, 3, 1, 2, 4)

    kb, vb = to_blocks(k), to_blocks(v)
    kmean = jnp.mean(kb.astype(jnp.float32), axis=3)
    topk = min(MOBA_TOPK, nbk)
    bidx = jnp.arange(nbk)
    ni = jnp.arange(n)[:, None, None, None]
    hi = jnp.arange(nh)[None, :, None, None]
    sl = slopes[None, :, None, None, None]

    def block(start, qb):
        m = qb.shape[1]
        qpos = q0 + start + jnp.arange(m)
        own = qpos // MOBA_BLOCK
        gs = jnp.einsum('nqhd,nhbd->nhqb', qb.astype(jnp.float32), kmean)
        gs = jnp.where(bidx[None, :] < own[:, None], gs, NEG)
        _, top = lax.top_k(gs, topk)
        sel_ok = top < own[:, None]
        idx = jnp.concatenate([top, jnp.broadcast_to(own[:, None], (n, nh, m, 1))], axis=-1)
        ok = jnp.concatenate([sel_ok, jnp.ones((n, nh, m, 1), bool)], axis=-1)
        kg = kb[ni, hi, idx]
        vg = vb[ni, hi, idx]
        kpos = idx[..., None] * MOBA_BLOCK + jnp.arange(MOBA_BLOCK)
        s = jnp.einsum('nqhd,nhqsjd->nhqsj', qb, kg).astype(jnp.float32) * scale
        s = s - sl * (qpos[:, None, None] - kpos).astype(jnp.float32)
        mask = ok[..., None] & (kpos <= qpos[:, None, None])
        p = masked_softmax(s, mask, axis=(-2, -1))
        return jnp.einsum('nhqsj,nhqsjd->nqhd', p.astype(q.dtype), vg)

    return sweep_queries(block, SPARSE_Q_BLOCK, q)


def compress_tokens(a, w1, w2, pe):
    n, L, g, d = a.shape
    nc = (L - CMP_L) // CMP_S + 1
    idx = jnp.arange(nc)[:, None] * CMP_S + jnp.arange(CMP_L)[None, :]
    blocks = a[:, idx] + pe[:, None, :].astype(a.dtype)
    flat = jnp.swapaxes(blocks, 2, 3).reshape(n, nc, g, CMP_L * d)
    return jax.nn.gelu(flat @ w1) @ w2


def nsa_attention(q, kc, vc, ks, vs, kw, vw, gate, q0, w_ofs, slopes, cmp_w1, cmp_w2, cmp_pe):
    n, tq, nh, d = q.shape
    L = kc.shape[1]
    dt = q.dtype
    scale = d ** -0.5
    qg = q.reshape(n, tq, NSA_G, NSA_R, d)
    k_cmp = compress_tokens(kc, cmp_w1[0], cmp_w2[0], cmp_pe[0])
    v_cmp = compress_tokens(vc, cmp_w1[1], cmp_w2[1], cmp_pe[1])
    nc = k_cmp.shape[1]
    cpos = jnp.arange(nc) * CMP_S + (CMP_L - 1)
    nsb = -(-L // SLC_BLOCK)
    padl = nsb * SLC_BLOCK - L

    def to_blocks(a):
        a = jnp.pad(a, ((0, 0), (0, padl), (0, 0), (0, 0)))
        return a.reshape(n, nsb, SLC_BLOCK, NSA_G, d).transpose(0, 3, 1, 2, 4)

    ks_b, vs_b = to_blocks(ks), to_blocks(vs)
    nsel = min(SLC_N, nsb)
    front = CMP_L // CMP_S - 1
    ratio = SLC_BLOCK // CMP_S
    width = ratio + front
    back = ratio * nsb + width - front - nc
    kw_p = jnp.pad(kw, ((0, 0), (WINDOW, 0), (0, 0), (0, 0)))
    vw_p = jnp.pad(vw, ((0, 0), (WINDOW, 0), (0, 0), (0, 0)))
    sl = slopes.reshape(NSA_G, NSA_R)[None, :, :, None, None]
    ni = jnp.arange(n)[:, None, None, None]
    gi = jnp.arange(NSA_G)[None, :, None, None]
    bj = jnp.arange(nsb)

    def block(start, qb, gb):
        m = qb.shape[1]
        qpos = q0 + start + jnp.arange(m)
        s = jnp.einsum('nqgrd,ncgd->ngrqc', qb, k_cmp).astype(jnp.float32) * scale
        s = s - sl * (qpos[:, None] - cpos[None, :]).astype(jnp.float32)
        p_cmp = masked_softmax(s, cpos[None, :] <= qpos[:, None])
        o_cmp = jnp.einsum('ngrqc,ncgd->nqgrd', p_cmp.astype(dt), v_cmp)
        imp = jnp.pad(p_cmp.sum(axis=2), ((0, 0), (0, 0), (0, 0), (front, back)))
        p_slc = imp[..., 0:ratio * nsb:ratio]
        for u in range(1, width):
            p_slc = p_slc + imp[..., u:u + ratio * nsb:ratio]
        own = qpos // SLC_BLOCK
        forced = (bj[None, :] == 0) | (bj[None, :] >= own[:, None] - 1)
        allowed = bj[None, :] <= own[:, None]
        score = jnp.where(allowed, jnp.where(forced, p_slc + FORCE_SCORE, p_slc), -1.0)
        _, top = lax.top_k(score, nsel)
        ok = top <= own[:, None]
        kg = ks_b[ni, gi, top]
        vg = vs_b[ni, gi, top]
        kpos = top[..., None] * SLC_BLOCK + jnp.arange(SLC_BLOCK)
        dist = (qpos[:, None, None] - kpos)[:, :, None].astype(jnp.float32)
        s2 = jnp.einsum('nqgrd,ngqsjd->ngrqsj', qb, kg).astype(jnp.float32) * scale - sl[..., None] * dist
        mask2 = (ok[..., None] & (kpos <= qpos[:, None, None]))[:, :, None]
        p2 = masked_softmax(s2, mask2, axis=(-2, -1))
        o_slc = jnp.einsum('ngrqsj,ngqsjd->nqgrd', p2.astype(dt), vg)
        off = q0 + start - w_ofs
        kwin = lax.dynamic_slice_in_dim(kw_p, off, WINDOW + m, axis=1)
        vwin = lax.dynamic_slice_in_dim(vw_p, off, WINDOW + m, axis=1)
        wpos = q0 + start - WINDOW + jnp.arange(WINDOW + m)
        s3 = jnp.einsum('nqgrd,nkgd->ngrqk', qb, kwin).astype(jnp.float32) * scale
        s3 = s3 - sl * (qpos[:, None] - wpos[None, :]).astype(jnp.float32)
        wmask = (wpos[None, :] <= qpos[:, None]) & (wpos[None, :] > qpos[:, None] - WINDOW) & (wpos[None, :] >= 0)
        p3 = masked_softmax(s3, wmask)
        o_win = jnp.einsum('ngrqk,nkgd->nqgrd', p3.astype(dt), vwin)
        return gb[..., 0:1] * o_cmp + gb[..., 1:2] * o_slc + gb[..., 2:3] * o_win

    return sweep_queries(block, SPARSE_Q_BLOCK, qg, gate)


PAGE = 128


def _head_diag_mask(width):
    sub = lax.broadcasted_iota(jnp.int32, (N_HEADS, width), 0)
    lane = lax.broadcasted_iota(jnp.int32, (N_HEADS, width), 1)
    return sub == lane // HEAD_DIM


def _block_diag_queries(q_ref, nt):
    diag = _head_diag_mask(BRANCH_W)
    return jnp.concatenate([jnp.where(diag, q_ref[0, t:t + 1, :], 0.0) for t in range(nt)], axis=0)


def _extract_heads(o, nt):
    diag = _head_diag_mask(BRANCH_W)
    return jnp.concatenate(
        [jnp.sum(jnp.where(diag, o[t * N_HEADS:(t + 1) * N_HEADS], 0.0), axis=0, keepdims=True) for t in range(nt)],
        axis=0)


def _slope_col(rows):
    h = lax.broadcasted_iota(jnp.int32, (rows, 1), 0) % N_HEADS
    col = jnp.zeros((rows, 1), F32)
    for k in range(N_HEADS):
        col = jnp.where(h == k, 2.0 ** (-8.0 * (k + 1) / N_HEADS), col)
    return col


def _row_softmax_step(s, v, m_ref, l_ref, acc_ref):
    m_old = m_ref[...]
    m_new = jnp.maximum(m_old, jnp.max(s, axis=1, keepdims=True))
    alpha = jnp.exp(m_old - m_new)
    p = jnp.exp(s - m_new)
    l_ref[...] = alpha * l_ref[...] + jnp.sum(p, axis=1, keepdims=True)
    acc_ref[...] = alpha * acc_ref[...] + jnp.dot(p.astype(BF16), v, preferred_element_type=F32)
    m_ref[...] = m_new


def _softmax_reset(m_ref, l_ref, acc_ref):
    m_ref[...] = jnp.full(m_ref.shape, NEG, F32)
    l_ref[...] = jnp.zeros(l_ref.shape, F32)
    acc_ref[...] = jnp.zeros(acc_ref.shape, F32)


def _new_token_mask(rows, nt):
    k = lax.broadcasted_iota(jnp.int32, (rows, PAGE), 1)
    t = lax.broadcasted_iota(jnp.int32, (rows, PAGE), 0) // N_HEADS
    return (k <= t) & (k < nt)


def _dfox_kernel(pt_ref, q_ref, pg_ref, lf_ref, npg_ref, nlf_ref, o_ref, qbd_ref, m_ref, l_ref, acc_ref, cf_ref, *, nt):
    del pt_ref
    j = pl.program_id(1)
    rows = nt * N_HEADS

    @pl.when(j == 0)
    def _():
        qbd_ref[...] = (_block_diag_queries(q_ref, nt) * SCALE).astype(BF16)
        _softmax_reset(m_ref, l_ref, acc_ref)
        cf_ref[...] = jnp.zeros_like(cf_ref)

    def step(kv_ref, lf, mask):
        k = kv_ref[0, :, 0:BRANCH_W].astype(BF16)
        v = kv_ref[0, :, BRANCH_W:2 * BRANCH_W].astype(BF16)
        rr = lax.broadcasted_iota(jnp.int32, (rows, N_HEADS), 0) % N_HEADS
        rc = lax.broadcasted_iota(jnp.int32, (rows, N_HEADS), 1)
        lfe = _dot_nt(jnp.where(rr == rc, 1.0, 0.0), lf, precision=HIGHEST)
        a = lax.broadcasted_iota(jnp.int32, (PAGE, PAGE), 0)
        b = lax.broadcasted_iota(jnp.int32, (PAGE, PAGE), 1)
        fk = jnp.dot(lfe, jnp.where(a <= b, 1.0, 0.0), preferred_element_type=F32, precision=HIGHEST) + cf_ref[...]
        cf_ref[...] = fk[:, PAGE - 1:PAGE]
        s = _dot_nt(qbd_ref[...], k) - fk
        if mask is not None:
            s = jnp.where(mask, s, MASKED)
        _row_softmax_step(s, v, m_ref, l_ref, acc_ref)

    step(pg_ref, lf_ref[0], None)

    @pl.when(j == pl.num_programs(1) - 1)
    def _():
        step(npg_ref, nlf_ref[0], _new_token_mask(rows, nt))
        o_ref[0] = _extract_heads(acc_ref[...] * (1.0 / l_ref[...]), nt)


def fox_decode_call(pt, q, cache_kv, cache_lf, new_kv, new_lf):
    n, nt, _ = q.shape
    npages = pt.shape[1]
    rows = nt * N_HEADS
    grid_spec = pltpu.PrefetchScalarGridSpec(
        num_scalar_prefetch=1,
        grid=(n, npages),
        in_specs=[pl.BlockSpec((1, nt, BRANCH_W), lambda b, j, pt: (b, 0, 0)),
                  pl.BlockSpec((1, PAGE, 2 * BRANCH_W), lambda b, j, pt: (pt[b, j], 0, 0)),
                  pl.BlockSpec((1, PAGE, N_HEADS), lambda b, j, pt: (pt[b, j], 0, 0)),
                  pl.BlockSpec((1, PAGE, 2 * BRANCH_W), lambda b, j, pt: (b, 0, 0)),
                  pl.BlockSpec((1, PAGE, N_HEADS), lambda b, j, pt: (b, 0, 0))],
        out_specs=pl.BlockSpec((1, nt, BRANCH_W), lambda b, j, pt: (b, 0, 0)),
        scratch_shapes=[pltpu.VMEM((rows, BRANCH_W), BF16), pltpu.VMEM((rows, 1), F32), pltpu.VMEM((rows, 1), F32),
                        pltpu.VMEM((rows, BRANCH_W), F32), pltpu.VMEM((rows, 1), F32)],
    )
    return pl.pallas_call(
        functools.partial(_dfox_kernel, nt=nt),
        grid_spec=grid_spec,
        out_shape=jax.ShapeDtypeStruct((n, nt, BRANCH_W), F32),
        compiler_params=_cparams("parallel", "arbitrary"),
        name="fox_decode",
    )(pt, q, cache_kv, cache_lf, new_kv, new_lf)


def _dmoba_sel_kernel(pt_ref, q_ref, kp_ref, sel_ref, q32_ref, g_ref, *, nt):
    del pt_ref
    j = pl.program_id(1)
    rows = nt * N_HEADS
    pages_per_block = MOBA_BLOCK // PAGE

    @pl.when(j == 0)
    def _():
        q32_ref[...] = _block_diag_queries(q_ref, nt)
        g_ref[...] = jnp.zeros_like(g_ref)

    ksum = jnp.sum(kp_ref[0], axis=0, keepdims=True)
    prow = lax.broadcasted_iota(jnp.int32, (g_ref.shape[1], BRANCH_W), 0)
    g_ref[...] += _dot_nt(q32_ref[...], jnp.where(prow == j, ksum, 0.0), precision=HIGHEST)

    @pl.when(j == pl.num_programs(1) - 1)
    def _():
        npg = g_ref.shape[1]
        g = g_ref[...]
        lane = lax.broadcasted_iota(jnp.int32, (rows, npg), 1)
        blk = g
        for u in range(1, pages_per_block):
            blk = blk + pltpu.roll(g, npg - u, 1)
        gs = jnp.where(lane % pages_per_block == 0, blk * (1.0 / MOBA_BLOCK), NEG)
        sel = jnp.zeros((rows, npg), F32)
        for _ in range(MOBA_TOPK):
            mx = jnp.max(gs, axis=1, keepdims=True)
            idx = jnp.min(jnp.where(gs == mx, lane, npg), axis=1, keepdims=True)
            hit = lane == idx
            sel = jnp.where(hit, 1.0, sel)
            gs = jnp.where(hit, MASKED, gs)
        out = sel
        for u in range(1, pages_per_block):
            out = out + pltpu.roll(sel, u, 1)
        sel_ref[0] = out


def moba_select_call(pt, q, cache_kv):
    n, nt, _ = q.shape
    npages = pt.shape[1]
    rows = nt * N_HEADS
    grid_spec = pltpu.PrefetchScalarGridSpec(
        num_scalar_prefetch=1,
        grid=(n, npages),
        in_specs=[pl.BlockSpec((1, nt, BRANCH_W), lambda b, j, pt: (b, 0, 0)),
                  pl.BlockSpec((1, PAGE, BRANCH_W), lambda b, j, pt: (pt[b, j], 0, 0))],
        out_specs=pl.BlockSpec((1, rows, npages), lambda b, j, pt: (b, 0, 0)),
        scratch_shapes=[pltpu.VMEM((rows, BRANCH_W), F32), pltpu.VMEM((rows, npages), F32)],
    )
    return pl.pallas_call(
        functools.partial(_dmoba_sel_kernel, nt=nt),
        grid_spec=grid_spec,
        out_shape=jax.ShapeDtypeStruct((n, rows, npages), F32),
        compiler_params=_cparams("parallel", "arbitrary"),
        name="moba_select",
    )(pt, q, cache_kv)


def _dmoba_kernel(pt_ref, q_ref, sel_ref, pg_ref, npg_ref, o_ref, qbd_ref, m_ref, l_ref, acc_ref, *, nt):
    del pt_ref
    j = pl.program_id(1)
    npages = pl.num_programs(1)
    rows = nt * N_HEADS
    slope = _slope_col(rows)
    klane = lax.broadcasted_iota(jnp.int32, (1, PAGE), 1)

    @pl.when(j == 0)
    def _():
        qbd_ref[...] = (_block_diag_queries(q_ref, nt) * SCALE).astype(BF16)
        _softmax_reset(m_ref, l_ref, acc_ref)

    def step(kv_ref, rel, mask):
        k = kv_ref[0, :, 0:BRANCH_W].astype(BF16)
        v = kv_ref[0, :, BRANCH_W:2 * BRANCH_W].astype(BF16)
        s = _dot_nt(qbd_ref[...], k) + slope * rel
        _row_softmax_step(jnp.where(mask, s, MASKED), v, m_ref, l_ref, acc_ref)

    prow = lax.broadcasted_iota(jnp.int32, (sel_ref.shape[2], PAGE), 0)
    keep = jnp.dot(sel_ref[0].astype(BF16), jnp.where(prow == j, 1.0, 0.0).astype(BF16),
                   preferred_element_type=F32) > 0.5
    step(pg_ref, (klane + (j - npages) * PAGE).astype(F32), keep)

    @pl.when(j == npages - 1)
    def _():
        step(npg_ref, klane.astype(F32), _new_token_mask(rows, nt))
        o_ref[0] = _extract_heads(acc_ref[...] * (1.0 / l_ref[...]), nt)


def moba_decode_call(pt, q, sel, cache_kv, new_kv):
    n, nt, _ = q.shape
    npages = pt.shape[1]
    rows = nt * N_HEADS
    grid_spec = pltpu.PrefetchScalarGridSpec(
        num_scalar_prefetch=1,
        grid=(n, npages),
        in_specs=[pl.BlockSpec((1, nt, BRANCH_W), lambda b, j, pt: (b, 0, 0)),
                  pl.BlockSpec((1, rows, npages), lambda b, j, pt: (b, 0, 0)),
                  pl.BlockSpec((1, PAGE, 2 * BRANCH_W), lambda b, j, pt: (pt[b, j], 0, 0)),
                  pl.BlockSpec((1, PAGE, 2 * BRANCH_W), lambda b, j, pt: (b, 0, 0))],
        out_specs=pl.BlockSpec((1, nt, BRANCH_W), lambda b, j, pt: (b, 0, 0)),
        scratch_shapes=[pltpu.VMEM((rows, BRANCH_W), BF16), pltpu.VMEM((rows, 1), F32), pltpu.VMEM((rows, 1), F32),
                        pltpu.VMEM((rows, BRANCH_W), F32)],
    )
    return pl.pallas_call(
        functools.partial(_dmoba_kernel, nt=nt),
        grid_spec=grid_spec,
        out_shape=jax.ShapeDtypeStruct((n, nt, BRANCH_W), F32),
        compiler_params=_cparams("parallel", "arbitrary"),
        name="moba_decode",
    )(pt, q, sel, cache_kv, new_kv)


def _dnsa_kernel(pt_ref, q_ref, gate_ref, kc_ref, vc_ref, pg_ref, npg_ref, win_ref, nwin_ref, o_ref,
                 qbd_ref, sel_ref, ocmp_ref, m_ref, l_ref, acc_ref, *, nt, nsb_pad):
    del pt_ref
    j = pl.program_id(1)
    npages = pl.num_programs(1)
    rows = nt * N_HEADS
    ncmp = kc_ref.shape[1]
    q0 = npages * PAGE
    slope = _slope_col(rows)
    klane = lax.broadcasted_iota(jnp.int32, (1, PAGE), 1)
    blocks_per_page = PAGE // SLC_BLOCK

    @pl.when(j == 0)
    def _():
        pr = lax.broadcasted_iota(jnp.int32, (BRANCH_W, KV_W), 0)
        pc = lax.broadcasted_iota(jnp.int32, (BRANCH_W, KV_W), 1)
        place = jnp.where((pr % HEAD_DIM == pc % HEAD_DIM) & (pc // HEAD_DIM == pr // (NSA_R * HEAD_DIM)), 1.0, 0.0)
        qg = jnp.dot(_block_diag_queries(q_ref, nt), place, preferred_element_type=F32, precision=HIGHEST)
        qbd = (qg * SCALE).astype(BF16)
        qbd_ref[...] = qbd
        c = lax.broadcasted_iota(jnp.int32, (1, ncmp), 1)
        s = _dot_nt(qbd, kc_ref[0]) + slope * (CMP_S * c + (CMP_S - 1) - q0).astype(F32)
        valid = c >= 1
        s = jnp.where(valid, s, MASKED)
        p = jnp.where(valid, jnp.exp(s - jnp.max(s, axis=1, keepdims=True)), 0.0)
        p = p * (1.0 / jnp.sum(p, axis=1, keepdims=True))
        ocmp_ref[...] = gate_ref[0, :, 0:1] * jnp.dot(p.astype(BF16), vc_ref[0], preferred_element_type=F32)
        gr = lax.broadcasted_iota(jnp.int32, (nt * NSA_G, rows), 0)
        gc = lax.broadcasted_iota(jnp.int32, (nt * NSA_G, rows), 1) // NSA_R
        imp = jnp.dot(jnp.where(gr == gc, 1.0, 0.0), p, preferred_element_type=F32, precision=HIGHEST)
        ratio = SLC_BLOCK // CMP_S
        mr = lax.broadcasted_iota(jnp.int32, (ncmp, nsb_pad), 0)
        mb = lax.broadcasted_iota(jnp.int32, (ncmp, nsb_pad), 1)
        gather_m = jnp.where((mr >= ratio * mb) & (mr <= ratio * mb + ratio) & (mr >= 1), 1.0, 0.0)
        p_slc = jnp.dot(imp, gather_m, preferred_element_type=F32, precision=HIGHEST)
        bj = lax.broadcasted_iota(jnp.int32, p_slc.shape, 1)
        own = q0 // SLC_BLOCK
        forced = (bj == 0) | (bj >= own - 1)
        allowed = bj <= own
        score = jnp.where(allowed, jnp.where(forced, p_slc + FORCE_SCORE, p_slc), -1.0)
        sel = jnp.zeros(p_slc.shape, F32)
        for _ in range(SLC_N):
            mx = jnp.max(score, axis=1, keepdims=True)
            idx = jnp.min(jnp.where(score == mx, bj, nsb_pad), axis=1, keepdims=True)
            hit = bj == idx
            sel = jnp.where(hit, 1.0, sel)
            score = jnp.where(hit, -2.0, score)
        sel = jnp.where(allowed, sel, 0.0)
        er = lax.broadcasted_iota(jnp.int32, (rows, nt * NSA_G), 0) // NSA_R
        ec = lax.broadcasted_iota(jnp.int32, (rows, nt * NSA_G), 1)
        sel_ref[...] = jnp.dot(jnp.where(er == ec, 1.0, 0.0), sel, preferred_element_type=F32).astype(BF16)
        _softmax_reset(m_ref, l_ref, acc_ref)

    def step(kv_ref, rel, mask):
        k = kv_ref[0, :, 0:KV_W].astype(BF16)
        v = kv_ref[0, :, KV_W:2 * KV_W].astype(BF16)
        s = _dot_nt(qbd_ref[...], k) + slope * rel
        if mask is not None:
            s = jnp.where(mask, s, MASKED)
        _row_softmax_step(s, v, m_ref, l_ref, acc_ref)

    brow = lax.broadcasted_iota(jnp.int32, (nsb_pad, PAGE), 0)
    bcol = lax.broadcasted_iota(jnp.int32, (nsb_pad, PAGE), 1) // SLC_BLOCK
    expand = jnp.where(brow == bcol + j * blocks_per_page, 1.0, 0.0).astype(BF16)
    keep = jnp.dot(sel_ref[...], expand, preferred_element_type=F32) > 0.5
    step(pg_ref, (klane + (j - npages) * PAGE).astype(F32), keep)

    @pl.when(j == npages - 1)
    def _():
        new_mask = _new_token_mask(rows, nt)
        step(npg_ref, klane.astype(F32), new_mask)
        o_slc = acc_ref[...] * (1.0 / l_ref[...])
        _softmax_reset(m_ref, l_ref, acc_ref)
        wlen = win_ref.shape[1]
        wi = lax.broadcasted_iota(jnp.int32, (rows, wlen), 1)
        wt = lax.broadcasted_iota(jnp.int32, (rows, wlen), 0) // N_HEADS
        wrel = (lax.broadcasted_iota(jnp.int32, (1, wlen), 1) - wlen).astype(F32)
        step(win_ref, wrel, wi > wt + (wlen - WINDOW))
        step(nwin_ref, klane.astype(F32), new_mask)
        o_win = acc_ref[...] * (1.0 / l_ref[...])
        o = ocmp_ref[...] + gate_ref[0, :, 1:2] * o_slc + gate_ref[0, :, 2:3] * o_win
        ur = lax.broadcasted_iota(jnp.int32, (KV_W, BRANCH_W), 0)
        uc = lax.broadcasted_iota(jnp.int32, (KV_W, BRANCH_W), 1)
        unplace = jnp.where((ur % HEAD_DIM == uc % HEAD_DIM) & (ur // HEAD_DIM == uc // (NSA_R * HEAD_DIM)), 1.0, 0.0)
        o_ref[0] = _extract_heads(jnp.dot(o, unplace, preferred_element_type=F32, precision=HIGHEST), nt)


def nsa_decode_call(pt, q, gate, kcmp, vcmp, cache_kv, new_kv, win, new_win):
    n, nt, _ = q.shape
    npages = pt.shape[1]
    rows = nt * N_HEADS
    ncmp = kcmp.shape[1]
    nsb = (npages * PAGE) // SLC_BLOCK + 1
    nsb_pad = -(-nsb // 128) * 128
    wlen = win.shape[1]
    grid_spec = pltpu.PrefetchScalarGridSpec(
        num_scalar_prefetch=1,
        grid=(n, npages),
        in_specs=[pl.BlockSpec((1, nt, BRANCH_W), lambda b, j, pt: (b, 0, 0)),
                  pl.BlockSpec((1, rows, 3), lambda b, j, pt: (b, 0, 0)),
                  pl.BlockSpec((1, ncmp, KV_W), lambda b, j, pt: (b, 0, 0)),
                  pl.BlockSpec((1, ncmp, KV_W), lambda b, j, pt: (b, 0, 0)),
                  pl.BlockSpec((1, PAGE, 2 * KV_W), lambda b, j, pt: (pt[b, j], 0, 1)),
                  pl.BlockSpec((1, PAGE, 2 * KV_W), lambda b, j, pt: (b, 0, 0)),
                  pl.BlockSpec((1, wlen, 2 * KV_W), lambda b, j, pt: (b, 0, 0)),
                  pl.BlockSpec((1, PAGE, 2 * KV_W), lambda b, j, pt: (b, 0, 0))],
        out_specs=pl.BlockSpec((1, nt, BRANCH_W), lambda b, j, pt: (b, 0, 0)),
        scratch_shapes=[pltpu.VMEM((rows, KV_W), BF16), pltpu.VMEM((rows, nsb_pad), BF16),
                        pltpu.VMEM((rows, KV_W), F32), pltpu.VMEM((rows, 1), F32), pltpu.VMEM((rows, 1), F32),
                        pltpu.VMEM((rows, KV_W), F32)],
    )
    return pl.pallas_call(
        functools.partial(_dnsa_kernel, nt=nt, nsb_pad=nsb_pad),
        grid_spec=grid_spec,
        out_shape=jax.ShapeDtypeStruct((n, nt, BRANCH_W), F32),
        compiler_params=_cparams("parallel", "arbitrary"),
        name="nsa_decode",
    )(pt, q, gate, kcmp, vcmp, cache_kv, new_kv, win, new_win)


def _conv_sample_kernel(z_ref, pre1_ref, pre2_ref, w_ref, o_ref, u_ref, *, nt):
    bw = BRANCH_W
    u = z_ref[:, 2 * bw:3 * bw] * z_ref[:, 0:bw]
    t = lax.broadcasted_iota(jnp.int32, u.shape, 0) % nt
    u1 = jnp.where(t < 1, pre1_ref[...], pltpu.roll(u, 1, 0))
    u2 = jnp.where(t < 2, pre2_ref[...], pltpu.roll(u, 2, 0))
    y = u2 * w_ref[0:1, :] + u1 * w_ref[1:2, :] + u * w_ref[2:3, :]
    o_ref[...] = z_ref[:, bw:2 * bw] * y
    u_ref[...] = u


def conv_sample_call(z_conv, state, conv_w, n, nt):
    zero = jnp.zeros((n, nt, BRANCH_W), F32)
    pre1 = zero.at[:, 0].set(state[:, 1]).reshape(n * nt, BRANCH_W)
    pre2 = zero.at[:, 0].set(state[:, 0]).at[:, 1].set(state[:, 1]).reshape(n * nt, BRANCH_W)
    return pl.pallas_call(
        functools.partial(_conv_sample_kernel, nt=nt),
        out_shape=[jax.ShapeDtypeStruct((n * nt, BRANCH_W), F32), jax.ShapeDtypeStruct((n * nt, BRANCH_W), F32)],
        name="conv_sample",
    )(z_conv, pre1, pre2, conv_w)


PPS = 32


def _split3(x):
    hi = x.astype(BF16)
    r1 = x - hi.astype(F32)
    mid = r1.astype(BF16)
    lo = (r1 - mid.astype(F32)).astype(BF16)
    return jnp.concatenate([hi, mid, lo], axis=0)


def _softmax_step_t(s, vts, m_ref, l_ref, acc_ref):
    m_old = m_ref[...]
    m_new = jnp.maximum(m_old, jnp.max(s, axis=1, keepdims=True))
    alpha = jnp.exp(m_old - m_new)
    p = jnp.exp(s - m_new)
    l_ref[...] = alpha * l_ref[...] + jnp.sum(p, axis=1, keepdims=True)
    p = p.astype(BF16)
    pv, off = None, 0
    for vt in vts:
        nk = vt.shape[1]
        term = _dot_nt(p[:, off:off + nk], vt)
        pv = term if pv is None else pv + term
        off += nk
    acc_ref[...] = alpha * acc_ref[...] + pv
    m_ref[...] = m_new


def _page_specs(rows, row_block, pps):
    return [pl.BlockSpec((1, rows, PAGE), functools.partial(lambda b, j, pt, u: (pt[b, j * pps + u], row_block, 0), u=u))
            for u in range(pps)]


def _dfox_t_kernel(pt_ref, q_ref, *refs, nt, pps):
    del pt_ref
    kv_refs, lf_refs = refs[:pps], refs[pps:2 * pps]
    nkv_ref, nlf_ref, o_ref, qbd_ref, m_ref, l_ref, acc_ref, cf_ref = refs[2 * pps:]
    j = pl.program_id(1)
    rows = nt * N_HEADS

    @pl.when(j == 0)
    def _():
        qbd_ref[...] = (_block_diag_queries(q_ref, nt) * SCALE).astype(BF16)
        _softmax_reset(m_ref, l_ref, acc_ref)
        cf_ref[...] = jnp.zeros_like(cf_ref)

    def block(kvs, lfs, mask):
        a = lax.broadcasted_iota(jnp.int32, (PAGE, PAGE), 0)
        b = lax.broadcasted_iota(jnp.int32, (PAGE, PAGE), 1)
        tri = jnp.where(a <= b, 1.0, 0.0).astype(BF16)
        carry = cf_ref[...]
        parts, vts = [], []
        for kv_ref, lf_ref in zip(kvs, lfs):
            lfe = jnp.concatenate([lf_ref[0]] * nt, axis=0)
            c3 = jnp.dot(_split3(lfe), tri, preferred_element_type=F32)
            fk = c3[0:rows] + c3[rows:2 * rows] + c3[2 * rows:3 * rows] + carry
            carry = fk[:, PAGE - 1:PAGE]
            parts.append(jnp.dot(qbd_ref[...], kv_ref[0, 0:BRANCH_W, :].astype(BF16), preferred_element_type=F32) - fk)
            vts.append(kv_ref[0, BRANCH_W:2 * BRANCH_W, :].astype(BF16))
        cf_ref[...] = carry
        s = parts[0] if len(parts) == 1 else jnp.concatenate(parts, axis=1)
        if mask is not None:
            s = jnp.where(mask, s, MASKED)
        _softmax_step_t(s, vts, m_ref, l_ref, acc_ref)

    block(kv_refs, lf_refs, None)

    @pl.when(j == pl.num_programs(1) - 1)
    def _():
        block([nkv_ref], [nlf_ref], _new_token_mask(rows, nt))
        o_ref[0] = _extract_heads(acc_ref[...] * (1.0 / l_ref[...]), nt)


def fox_decode_t_call(pt, q, cache_kvt, cache_lft, new_kvt, new_lft):
    n, nt, _ = q.shape
    npages = pt.shape[1]
    pps = PPS if npages % PPS == 0 else 1
    rows = nt * N_HEADS
    grid_spec = pltpu.PrefetchScalarGridSpec(
        num_scalar_prefetch=1,
        grid=(n, npages // pps),
        in_specs=[pl.BlockSpec((1, nt, BRANCH_W), lambda b, j, pt: (b, 0, 0))]
        + _page_specs(2 * BRANCH_W, 0, pps) + _page_specs(N_HEADS, 0, pps)
        + [pl.BlockSpec((1, 2 * BRANCH_W, PAGE), lambda b, j, pt: (b, 0, 0)),
           pl.BlockSpec((1, N_HEADS, PAGE), lambda b, j, pt: (b, 0, 0))],
        out_specs=pl.BlockSpec((1, nt, BRANCH_W), lambda b, j, pt: (b, 0, 0)),
        scratch_shapes=[pltpu.VMEM((rows, BRANCH_W), BF16), pltpu.VMEM((rows, 1), F32), pltpu.VMEM((rows, 1), F32),
                        pltpu.VMEM((rows, BRANCH_W), F32), pltpu.VMEM((rows, 1), F32)],
    )
    return pl.pallas_call(
        functools.partial(_dfox_t_kernel, nt=nt, pps=pps),
        grid_spec=grid_spec,
        out_shape=jax.ShapeDtypeStruct((n, nt, BRANCH_W), F32),
        compiler_params=_cparams("parallel", "arbitrary"),
        name="fox_decode",
    )(pt, q, *([cache_kvt] * pps), *([cache_lft] * pps), new_kvt, new_lft)


def _dmoba_sel_t_kernel(pt_ref, q_ref, *refs, nt, pps):
    del pt_ref
    k_refs = refs[:pps]
    sel_ref, q3_ref, g_ref = refs[pps:]
    j = pl.program_id(1)
    rows = nt * N_HEADS
    npg = g_ref.shape[1]
    pages_per_block = MOBA_BLOCK // PAGE
    lane = lax.broadcasted_iota(jnp.int32, (rows, npg), 1)

    @pl.when(j == 0)
    def _():
        q3_ref[...] = _split3(_block_diag_queries(q_ref, nt))
        g_ref[...] = jnp.zeros_like(g_ref)

    g = g_ref[...]
    for u, k_ref in enumerate(k_refs):
        qk3 = jnp.dot(q3_ref[...], k_ref[0].astype(BF16), preferred_element_type=F32)
        qk = qk3[0:rows] + qk3[rows:2 * rows] + qk3[2 * rows:3 * rows]
        g = jnp.where(lane == j * pps + u, jnp.sum(qk, axis=1, keepdims=True), g)
    g_ref[...] = g

    @pl.when(j == pl.num_programs(1) - 1)
    def _():
        blk = g
        for u in range(1, pages_per_block):
            blk = blk + pltpu.roll(g, npg - u, 1)
        gs = jnp.where(lane % pages_per_block == 0, blk * (1.0 / MOBA_BLOCK), NEG)
        sel = jnp.zeros((rows, npg), F32)
        for _ in range(MOBA_TOPK):
            mx = jnp.max(gs, axis=1, keepdims=True)
            idx = jnp.min(jnp.where(gs == mx, lane, npg), axis=1, keepdims=True)
            hit = lane == idx
            sel = jnp.where(hit, 1.0, sel)
            gs = jnp.where(hit, MASKED, gs)
        out = sel
        for u in range(1, pages_per_block):
            out = out + pltpu.roll(sel, u, 1)
        sel_ref[0] = out


def moba_select_t_call(pt, q, cache_kvt):
    n, nt, _ = q.shape
    npages = pt.shape[1]
    pps = PPS if npages % PPS == 0 else 1
    rows = nt * N_HEADS
    grid_spec = pltpu.PrefetchScalarGridSpec(
        num_scalar_prefetch=1,
        grid=(n, npages // pps),
        in_specs=[pl.BlockSpec((1, nt, BRANCH_W), lambda b, j, pt: (b, 0, 0))] + _page_specs(BRANCH_W, 0, pps),
        out_specs=pl.BlockSpec((1, rows, npages), lambda b, j, pt: (b, 0, 0)),
        scratch_shapes=[pltpu.VMEM((3 * rows, BRANCH_W), BF16), pltpu.VMEM((rows, npages), F32)],
    )
    return pl.pallas_call(
        functools.partial(_dmoba_sel_t_kernel, nt=nt, pps=pps),
        grid_spec=grid_spec,
        out_shape=jax.ShapeDtypeStruct((n, rows, npages), F32),
        compiler_params=_cparams("parallel", "arbitrary"),
        name="moba_select",
    )(pt, q, *([cache_kvt] * pps))


def _dmoba_t_kernel(pt_ref, q_ref, sel_ref, *refs, nt, pps):
    del pt_ref
    kv_refs = refs[:pps]
    nkv_ref, o_ref, qbd_ref, m_ref, l_ref, acc_ref = refs[pps:]
    j = pl.program_id(1)
    nsteps = pl.num_programs(1)
    rows = nt * N_HEADS
    slope = _slope_col(rows)

    @pl.when(j == 0)
    def _():
        qbd_ref[...] = (_block_diag_queries(q_ref, nt) * SCALE).astype(BF16)
        _softmax_reset(m_ref, l_ref, acc_ref)

    def block(kvs, rel, mask):
        parts = [jnp.dot(qbd_ref[...], kv_ref[0, 0:BRANCH_W, :].astype(BF16), preferred_element_type=F32)
                 for kv_ref in kvs]
        s = parts[0] if len(parts) == 1 else jnp.concatenate(parts, axis=1)
        s = jnp.where(mask, s + slope * rel, MASKED)
        _softmax_step_t(s, [kv_ref[0, BRANCH_W:2 * BRANCH_W, :].astype(BF16) for kv_ref in kvs], m_ref, l_ref, acc_ref)

    npg = sel_ref.shape[2]
    prow = lax.broadcasted_iota(jnp.int32, (npg, pps * PAGE), 0)
    pcol = lax.broadcasted_iota(jnp.int32, (npg, pps * PAGE), 1) // PAGE
    keep = jnp.dot(sel_ref[0].astype(BF16), jnp.where(prow == pcol + j * pps, 1.0, 0.0).astype(BF16),
                   preferred_element_type=F32) > 0.5
    klane = lax.broadcasted_iota(jnp.int32, (1, pps * PAGE), 1)
    block(kv_refs, (klane + (j - nsteps) * (pps * PAGE)).astype(F32), keep)

    @pl.when(j == nsteps - 1)
    def _():
        block([nkv_ref], lax.broadcasted_iota(jnp.int32, (1, PAGE), 1).astype(F32), _new_token_mask(rows, nt))
        o_ref[0] = _extract_heads(acc_ref[...] * (1.0 / l_ref[...]), nt)


def moba_decode_t_call(pt, q, sel, cache_kvt, new_kvt):
    n, nt, _ = q.shape
    npages = pt.shape[1]
    pps = PPS if npages % PPS == 0 else 1
    rows = nt * N_HEADS
    grid_spec = pltpu.PrefetchScalarGridSpec(
        num_scalar_prefetch=1,
        grid=(n, npages // pps),
        in_specs=[pl.BlockSpec((1, nt, BRANCH_W), lambda b, j, pt: (b, 0, 0)),
                  pl.BlockSpec((1, rows, npages), lambda b, j, pt: (b, 0, 0))]
        + _page_specs(2 * BRANCH_W, 0, pps)
        + [pl.BlockSpec((1, 2 * BRANCH_W, PAGE), lambda b, j, pt: (b, 0, 0))],
        out_specs=pl.BlockSpec((1, nt, BRANCH_W), lambda b, j, pt: (b, 0, 0)),
        scratch_shapes=[pltpu.VMEM((rows, BRANCH_W), BF16), pltpu.VMEM((rows, 1), F32), pltpu.VMEM((rows, 1), F32),
                        pltpu.VMEM((rows, BRANCH_W), F32)],
    )
    return pl.pallas_call(
        functools.partial(_dmoba_t_kernel, nt=nt, pps=pps),
        grid_spec=grid_spec,
        out_shape=jax.ShapeDtypeStruct((n, nt, BRANCH_W), F32),
        compiler_params=_cparams("parallel", "arbitrary"),
        name="moba_decode",
    )(pt, q, sel, *([cache_kvt] * pps), new_kvt)


def _compress_t_kernel(pt_ref, *refs, rows, pps):
    del pt_ref
    a_refs = refs[:pps]
    pet_ref, peb_ref, wt_ref, wb_ref, w2k_ref, w2v_ref, w2vt_ref, kc_ref, vc_ref, vct_ref, rows_ref = refs[pps:]
    j = pl.program_id(1)
    pr = lax.broadcasted_iota(jnp.int32, (PAGE, PAGE), 0)
    pc = lax.broadcasted_iota(jnp.int32, (PAGE, PAGE), 1)
    perm = jnp.where(pc == CMP_S * (pr % PAGE_ROWS) + pr // PAGE_ROWS, 1.0, 0.0).astype(BF16)
    w2 = 2 * KV_W
    for u, a_ref in enumerate(a_refs):
        p3 = _dot_nt(perm, _split3(a_ref[0]))
        pg = p3[:, 0:w2] + p3[:, w2:2 * w2] + p3[:, 2 * w2:3 * w2]
        r0 = pl.multiple_of((j * pps + u) * PAGE_ROWS, PAGE_ROWS)
        for i in range(CMP_S):
            rows_ref[pl.ds(r0, PAGE_ROWS), i * w2:(i + 1) * w2] = pg[i * PAGE_ROWS:(i + 1) * PAGE_ROWS, :]

    @pl.when(j == pl.num_programs(1) - 1)
    def _():
        chunk = min(256, rows)
        prev_top = jnp.zeros((1, 4 * CMP_HID), F32)
        for c in range(rows // chunk):
            a = rows_ref[c * chunk:(c + 1) * chunk, :]
            ht = jnp.dot((a + pet_ref[...]).astype(BF16), wt_ref[...], preferred_element_type=F32)
            hb = jnp.dot((a + peb_ref[...]).astype(BF16), wb_ref[...], preferred_element_type=F32)
            row = lax.broadcasted_iota(jnp.int32, ht.shape, 0)
            shifted = jnp.where(row == 0, prev_top, pltpu.roll(ht, 1, 0))
            prev_top = ht[chunk - 1:chunk, :]
            g = _gelu_tanh(shifted + hb).astype(BF16)
            kc_ref[0, c * chunk:(c + 1) * chunk, :] = jnp.dot(
                g[:, :2 * CMP_HID], w2k_ref[...], preferred_element_type=F32).astype(BF16)
            vc_ref[0, c * chunk:(c + 1) * chunk, :] = jnp.dot(
                g[:, 2 * CMP_HID:], w2v_ref[...], preferred_element_type=F32).astype(BF16)
            vct_ref[0, :, c * chunk:(c + 1) * chunk] = _dot_nt(w2vt_ref[...], g[:, 2 * CMP_HID:]).astype(BF16)


def compress_t_call(pages_t, page_idx, pe_top, pe_bot, wt, wb, w2k, w2v, w2vt, n, npages):
    rows = npages * PAGE_ROWS
    pps = PPS if npages % PPS == 0 else 1
    const = lambda b, j, pt: (0, 0)
    grid_spec = pltpu.PrefetchScalarGridSpec(
        num_scalar_prefetch=1,
        grid=(n, npages // pps),
        in_specs=_page_specs(2 * KV_W, 0, pps)
        + [pl.BlockSpec((1, CMP_ROW), const),
           pl.BlockSpec((1, CMP_ROW), const),
           pl.BlockSpec((CMP_ROW, 4 * CMP_HID), const, pipeline_mode=pl.Buffered(1)),
           pl.BlockSpec((CMP_ROW, 4 * CMP_HID), const, pipeline_mode=pl.Buffered(1)),
           pl.BlockSpec((2 * CMP_HID, KV_W), const),
           pl.BlockSpec((2 * CMP_HID, KV_W), const),
           pl.BlockSpec((KV_W, 2 * CMP_HID), const)],
        out_specs=[pl.BlockSpec((1, rows, KV_W), lambda b, j, pt: (b, 0, 0)),
                   pl.BlockSpec((1, rows, KV_W), lambda b, j, pt: (b, 0, 0)),
                   pl.BlockSpec((1, KV_W, rows), lambda b, j, pt: (b, 0, 0))],
        scratch_shapes=[pltpu.VMEM((rows, CMP_ROW), F32)],
    )
    return pl.pallas_call(
        functools.partial(_compress_t_kernel, rows=rows, pps=pps),
        grid_spec=grid_spec,
        out_shape=[jax.ShapeDtypeStruct((n, rows, KV_W), BF16),
                   jax.ShapeDtypeStruct((n, rows, KV_W), BF16),
                   jax.ShapeDtypeStruct((n, KV_W, rows), BF16)],
        compiler_params=_cparams("parallel", "arbitrary"),
        name="nsa_compress",
    )(page_idx, *([pages_t] * pps), pe_top, pe_bot, wt, wb, w2k, w2v, w2vt)


def _dnsa_t_kernel(pt_ref, q_ref, gate_ref, kc_ref, vc_ref, *refs, nt, nsb_pad, pps):
    del pt_ref
    kv_refs = refs[:pps]
    nkv_ref, win_ref, nwin_ref, o_ref, qbd_ref, sel_ref, ocmp_ref, m_ref, l_ref, acc_ref = refs[pps:]
    j = pl.program_id(1)
    nsteps = pl.num_programs(1)
    rows = nt * N_HEADS
    ncmp = kc_ref.shape[1]
    q0 = nsteps * (pps * PAGE)
    slope = _slope_col(rows)
    klane = lax.broadcasted_iota(jnp.int32, (1, PAGE), 1)
    blocks_per_page = PAGE // SLC_BLOCK

    @pl.when(j == 0)
    def _():
        pr = lax.broadcasted_iota(jnp.int32, (BRANCH_W, KV_W), 0)
        pc = lax.broadcasted_iota(jnp.int32, (BRANCH_W, KV_W), 1)
        place = jnp.where((pr % HEAD_DIM == pc % HEAD_DIM) & (pc // HEAD_DIM == pr // (NSA_R * HEAD_DIM)), 1.0, 0.0)
        qg = jnp.dot(_block_diag_queries(q_ref, nt), place, preferred_element_type=F32, precision=HIGHEST)
        qbd = (qg * SCALE).astype(BF16)
        qbd_ref[...] = qbd
        c = lax.broadcasted_iota(jnp.int32, (1, ncmp), 1)
        s = _dot_nt(qbd, kc_ref[0]) + slope * (CMP_S * c + (CMP_S - 1) - q0).astype(F32)
        valid = c >= 1
        s = jnp.where(valid, s, MASKED)
        p = jnp.where(valid, jnp.exp(s - jnp.max(s, axis=1, keepdims=True)), 0.0)
        p = p * (1.0 / jnp.sum(p, axis=1, keepdims=True))
        ocmp_ref[...] = gate_ref[0, :, 0:1] * jnp.dot(p.astype(BF16), vc_ref[0], preferred_element_type=F32)
        gr = lax.broadcasted_iota(jnp.int32, (nt * NSA_G, rows), 0)
        gc = lax.broadcasted_iota(jnp.int32, (nt * NSA_G, rows), 1) // NSA_R
        imp = jnp.dot(jnp.where(gr == gc, 1.0, 0.0), p, preferred_element_type=F32, precision=HIGHEST)
        ratio = SLC_BLOCK // CMP_S
        mr = lax.broadcasted_iota(jnp.int32, (ncmp, nsb_pad), 0)
        mb = lax.broadcasted_iota(jnp.int32, (ncmp, nsb_pad), 1)
        gather_m = jnp.where((mr >= ratio * mb) & (mr <= ratio * mb + ratio) & (mr >= 1), 1.0, 0.0)
        p_slc = jnp.dot(imp, gather_m, preferred_element_type=F32, precision=HIGHEST)
        bj = lax.broadcasted_iota(jnp.int32, p_slc.shape, 1)
        own = q0 // SLC_BLOCK
        forced = (bj == 0) | (bj >= own - 1)
        allowed = bj <= own
        score = jnp.where(allowed, jnp.where(forced, p_slc + FORCE_SCORE, p_slc), -1.0)
        sel = jnp.zeros(p_slc.shape, F32)
        for _ in range(SLC_N):
            mx = jnp.max(score, axis=1, keepdims=True)
            idx = jnp.min(jnp.where(score == mx, bj, nsb_pad), axis=1, keepdims=True)
            hit = bj == idx
            sel = jnp.where(hit, 1.0, sel)
            score = jnp.where(hit, -2.0, score)
        sel = jnp.where(allowed, sel, 0.0)
        er = lax.broadcasted_iota(jnp.int32, (rows, nt * NSA_G), 0) // NSA_R
        ec = lax.broadcasted_iota(jnp.int32, (rows, nt * NSA_G), 1)
        sel_ref[...] = jnp.dot(jnp.where(er == ec, 1.0, 0.0), sel, preferred_element_type=F32).astype(BF16)
        _softmax_reset(m_ref, l_ref, acc_ref)

    def block(kvs, rel, mask):
        parts = [jnp.dot(qbd_ref[...], kv_ref[0, 0:KV_W, :].astype(BF16), preferred_element_type=F32) for kv_ref in kvs]
        s = (parts[0] if len(parts) == 1 else jnp.concatenate(parts, axis=1)) + slope * rel
        if mask is not None:
            s = jnp.where(mask, s, MASKED)
        _softmax_step_t(s, [kv_ref[0, KV_W:2 * KV_W, :].astype(BF16) for kv_ref in kvs], m_ref, l_ref, acc_ref)

    brow = lax.broadcasted_iota(jnp.int32, (nsb_pad, pps * PAGE), 0)
    bcol = lax.broadcasted_iota(jnp.int32, (nsb_pad, pps * PAGE), 1) // SLC_BLOCK
    expand = jnp.where(brow == bcol + j * (pps * blocks_per_page), 1.0, 0.0).astype(BF16)
    keep = jnp.dot(sel_ref[...], expand, preferred_element_type=F32) > 0.5
    kl = lax.broadcasted_iota(jnp.int32, (1, pps * PAGE), 1)
    block(kv_refs, (kl + (j - nsteps) * (pps * PAGE)).astype(F32), keep)

    @pl.when(j == nsteps - 1)
    def _():
        new_mask = _new_token_mask(rows, nt)
        block([nkv_ref], klane.astype(F32), new_mask)
        o_slc = acc_ref[...] * (1.0 / l_ref[...])
        _softmax_reset(m_ref, l_ref, acc_ref)
        wlen = win_ref.shape[2]
        wi = lax.broadcasted_iota(jnp.int32, (rows, wlen), 1)
        wt = lax.broadcasted_iota(jnp.int32, (rows, wlen), 0) // N_HEADS
        wrel = (lax.broadcasted_iota(jnp.int32, (1, wlen), 1) - wlen).astype(F32)
        block([win_ref], wrel, wi > wt + (wlen - WINDOW))
        block([nwin_ref], klane.astype(F32), new_mask)
        o_win = acc_ref[...] * (1.0 / l_ref[...])
        o = ocmp_ref[...] + gate_ref[0, :, 1:2] * o_slc + gate_ref[0, :, 2:3] * o_win
        ur = lax.broadcasted_iota(jnp.int32, (KV_W, BRANCH_W), 0)
        uc = lax.broadcasted_iota(jnp.int32, (KV_W, BRANCH_W), 1)
        unplace = jnp.where((ur % HEAD_DIM == uc % HEAD_DIM) & (ur // HEAD_DIM == uc // (NSA_R * HEAD_DIM)), 1.0, 0.0)
        o_ref[0] = _extract_heads(jnp.dot(o, unplace, preferred_element_type=F32, precision=HIGHEST), nt)


def nsa_decode_t_call(pt, q, gate, kcmp, vcmp, cache_kvt, new_kvt, win_t, new_win_t):
    n, nt, _ = q.shape
    npages = pt.shape[1]
    pps = PPS if npages % PPS == 0 else 1
    rows = nt * N_HEADS
    ncmp = kcmp.shape[1]
    nsb = (npages * PAGE) // SLC_BLOCK + 1
    nsb_pad = -(-nsb // 128) * 128
    wlen = win_t.shape[2]
    grid_spec = pltpu.PrefetchScalarGridSpec(
        num_scalar_prefetch=1,
        grid=(n, npages // pps),
        in_specs=[pl.BlockSpec((1, nt, BRANCH_W), lambda b, j, pt: (b, 0, 0)),
                  pl.BlockSpec((1, rows, 3), lambda b, j, pt: (b, 0, 0)),
                  pl.BlockSpec((1, ncmp, KV_W), lambda b, j, pt: (b, 0, 0)),
                  pl.BlockSpec((1, ncmp, KV_W), lambda b, j, pt: (b, 0, 0))]
        + _page_specs(2 * KV_W, 1, pps)
        + [pl.BlockSpec((1, 2 * KV_W, PAGE), lambda b, j, pt: (b, 0, 0)),
           pl.BlockSpec((1, 2 * KV_W, wlen), lambda b, j, pt: (b, 0, 0)),
           pl.BlockSpec((1, 2 * KV_W, PAGE), lambda b, j, pt: (b, 0, 0))],
        out_specs=pl.BlockSpec((1, nt, BRANCH_W), lambda b, j, pt: (b, 0, 0)),
        scratch_shapes=[pltpu.VMEM((rows, KV_W), BF16), pltpu.VMEM((rows, nsb_pad), BF16),
                        pltpu.VMEM((rows, KV_W), F32), pltpu.VMEM((rows, 1), F32), pltpu.VMEM((rows, 1), F32),
                        pltpu.VMEM((rows, KV_W), F32)],
    )
    return pl.pallas_call(
        functools.partial(_dnsa_t_kernel, nt=nt, nsb_pad=nsb_pad, pps=pps),
        grid_spec=grid_spec,
        out_shape=jax.ShapeDtypeStruct((n, nt, BRANCH_W), F32),
        compiler_params=_cparams("parallel", "arbitrary"),
        name="nsa_decode",
    )(pt, q, gate, kcmp, vcmp, *([cache_kvt] * pps), new_kvt, win_t, new_win_t)


def _compress_weights(cmp_w1, cmp_w2, cmp_pe):
    w1 = cmp_w1.reshape(2, CMP_L, HEAD_DIM, CMP_HID)
    eye_w = jnp.eye(2, dtype=F32)
    eye_g = jnp.eye(NSA_G, dtype=F32)
    big = jnp.einsum('widh,sw,gk->isgdwkh', w1, eye_w, eye_g)
    big = big.reshape(CMP_L, 2 * KV_W, 4 * CMP_HID)
    wt = big[:CMP_S].reshape(CMP_ROW, 4 * CMP_HID).astype(BF16)
    wb = big[CMP_S:].reshape(CMP_ROW, 4 * CMP_HID).astype(BF16)
    w2k = jnp.einsum('hd,gk->ghkd', cmp_w2[0], eye_g).reshape(2 * CMP_HID, KV_W).astype(BF16)
    w2v = jnp.einsum('hd,gk->ghkd', cmp_w2[1], eye_g).reshape(2 * CMP_HID, KV_W).astype(BF16)
    w2vt = jnp.einsum('hd,gk->kdgh', cmp_w2[1], eye_g).reshape(KV_W, 2 * CMP_HID).astype(BF16)
    pe = jnp.broadcast_to(cmp_pe[:, :, None, :], (2, CMP_L, NSA_G, HEAD_DIM))
    pe = jnp.transpose(pe, (1, 0, 2, 3)).reshape(CMP_L, 2 * KV_W)
    pe_top = pe[:CMP_S].reshape(1, CMP_ROW)
    pe_bot = pe[CMP_S:].reshape(1, CMP_ROW)
    return pe_top, pe_bot, wt, wb, w2k, w2v, w2vt


def _layer_weights(l, w_in, w_branch, w_out, w_up, w_down):
    wl = w_in[l]
    small = jnp.concatenate(
        [wl[:, OFF_FOX_F:OFF_FOX_F + N_HEADS], wl[:, OFF_NSA_GATE:OFF_NSA_GATE + 3 * N_HEADS],
         jnp.zeros((D_MODEL, SMALL_W - 4 * N_HEADS), wl.dtype)], axis=1)
    return dict(
        conv=wl[:, OFF_CONV:OFF_FOX].astype(BF16),
        fox=wl[:, OFF_FOX:OFF_FOX_F].astype(BF16),
        moba=wl[:, OFF_MOBA:OFF_NSA].astype(BF16),
        nsa=wl[:, OFF_NSA:OFF_NSA_GATE].astype(BF16),
        small=small.astype(BF16),
        gate=wl[:, OFF_MERGE:].astype(BF16),
        branch=w_branch[l].astype(BF16),
        out=w_out[l].astype(BF16),
        up=w_up[l].astype(BF16),
        down=w_down[l].astype(BF16),
    )


def _finish_layer(x, h, branches, w, g_mlp, g_next, next_dtype, tm):
    merged = merge_call(h, branches, w["gate"], w["branch"], tm, 512)
    x1, hm = outproj_call(merged, w["out"], x, g_mlp, tm)
    return mlp_call(hm, w["up"], w["down"], x1, g_next, next_dtype, tm, 1024)


def prompt_mixers(h, n, t, w, b_forget, conv_w, cmp_w, tm):
    bw = BRANCH_W
    (z_conv,) = proj_call(h, w["conv"], [("f32", 3 * bw)], [(a, a + 512, ((0, a),)) for a in (0, 512, 1024)],
                          tm, "proj_conv")
    qkv_defs = [("f32", bw), ("f32", 2 * bw), ("bf16", bw), ("bf16T", bw)]
    qkv_plan = [(0, bw, ((0, 0),)), (bw, 2 * bw, ((1, 0), (2, 0))), (2 * bw, 3 * bw, ((1, bw), (3, 0)))]
    fox_q, fox_kv, fox_kb, fox_vt = proj_call(h, w["fox"], qkv_defs, qkv_plan, tm, "proj_fox")
    moba_q, moba_kv, moba_kb, moba_vt, moba_km = proj_call(
        h, w["moba"], qkv_defs + [("blockmean", bw)],
        [qkv_plan[0], (bw, 2 * bw, ((1, 0), (2, 0), (4, 0))), qkv_plan[2]], tm, "proj_moba")
    kv = KV_W
    nsa_q, nsa_kv, nsa_win, nsa_ks, nsa_vst, nsa_kw, nsa_vwt, nsa_cmp_t = proj_call(
        h, w["nsa"],
        [("f32", bw), ("f32", 4 * kv), ("f32", 2 * kv), ("bf16", kv), ("bf16T", kv), ("bf16", kv), ("bf16T", kv),
         ("f32T", 2 * kv)],
        [(0, bw, ((0, 0),)), (bw, bw + 2 * kv, ((1, 0), (7, 0))),
         (bw + 2 * kv, bw + 3 * kv, ((1, 2 * kv), (3, 0))), (bw + 3 * kv, bw + 4 * kv, ((1, 3 * kv), (4, 0))),
         (bw + 4 * kv, bw + 5 * kv, ((2, 0), (5, 0))), (bw + 5 * kv, bw + 6 * kv, ((2, kv), (6, 0)))],
        tm, "proj_nsa")
    (z_small,) = proj_call(h, w["small"], [("f32", SMALL_W)], [(0, SMALL_W, ((0, 0),))], tm, "proj_small")

    out_a, new_conv = conv_prompt_call(z_conv, conv_w, n, t, tm)
    lg, fk = small_call(z_small, b_forget, n, t, tm)
    logf = lg[:, :N_HEADS].reshape(n, t, N_HEADS)
    gate_t = lg[:, N_HEADS:4 * N_HEADS].T

    out_b = fox_prompt_call(fox_q, fox_kb, fox_vt, fk, n, t)
    out_c = moba_prompt_call(moba_q, moba_kb, moba_vt, moba_km, n, t)

    npages = t // PAGE
    page_idx = jnp.arange(n * npages, dtype=jnp.int32).reshape(n, npages)
    kcmp, _, vcmpt = compress_t_call(nsa_cmp_t, page_idx, *cmp_w, n, npages)
    out_d = nsa_prompt_call(nsa_q, kcmp, vcmpt, nsa_ks, nsa_vst, nsa_kw, nsa_vwt, gate_t, n, t)

    wb = min(WINDOW, t)
    new_state = (new_conv,
                 fox_kv.reshape(n, t, 2, N_HEADS, HEAD_DIM),
                 logf,
                 moba_kv.reshape(n, t, 2, N_HEADS, HEAD_DIM),
                 nsa_kv.reshape(n, t, 2, 2, NSA_G, HEAD_DIM),
                 nsa_win.reshape(n, t, 2, NSA_G, HEAD_DIM)[:, t - wb:])
    return [out_a, out_b, out_c, out_d], new_state


def sample_mixers(h, n, t, q0, past, w, b_forget, conv_w, cmp_w1, cmp_w2, cmp_pe, tm):
    dt = F32
    bw = BRANCH_W
    kv = KV_W
    (z_conv,) = proj_call(h, w["conv"], [("f32", 3 * bw)], [(0, 3 * bw, ((0, 0),))], tm, "proj_conv_s")
    fox_q, fox_kv = proj_call(h, w["fox"], [("f32", bw), ("f32", 2 * bw)],
                              [(0, bw, ((0, 0),)), (bw, 3 * bw, ((1, 0),))], tm, "proj_fox_s")
    moba_q, moba_kv = proj_call(h, w["moba"], [("f32", bw), ("f32", 2 * bw)],
                                [(0, bw, ((0, 0),)), (bw, 3 * bw, ((1, 0),))], tm, "proj_moba_s")
    nsa_q, nsa_kv, nsa_win = proj_call(
        h, w["nsa"], [("f32", bw), ("f32", 4 * kv), ("f32", 2 * kv)],
        [(0, bw, ((0, 0),)), (bw, bw + 4 * kv, ((1, 0),)), (bw + 4 * kv, bw + 6 * kv, ((2, 0),))], tm, "proj_nsa_s")
    (z_small,) = proj_call(h, w["small"], [("f32", SMALL_W)], [(0, SMALL_W, ((0, 0),))], tm, "proj_small_s")

    def heads(a, nh):
        return a.reshape(n, t, nh, HEAD_DIM)

    def cat(old, new):
        return jnp.concatenate([old.astype(new.dtype), new], axis=1)

    zc = z_conv.reshape(n, t, 3 * bw)
    conv_x, conv_b, conv_c = zc[..., :bw], zc[..., bw:2 * bw], zc[..., 2 * bw:]
    u = conv_c * conv_x
    ext = jnp.concatenate([past['conv'].astype(dt), u], axis=1)
    y_conv = ext[:, 0:t] * conv_w[0]
    for j in range(1, CONV_W):
        y_conv = y_conv + ext[:, j:j + t] * conv_w[j]
    out_a = conv_b * y_conv
    new_conv = ext[:, -(CONV_W - 1):]

    fkv = fox_kv.reshape(n, t, 2, N_HEADS, HEAD_DIM)
    fq, fk, fv = heads(fox_q, N_HEADS), fkv[:, :, 0], fkv[:, :, 1]
    fox_f = z_small.reshape(n, t, SMALL_W)[..., :N_HEADS]
    logf = jax.nn.log_sigmoid(fox_f + b_forget)
    lf_all = jnp.concatenate([past['fox_logf'].astype(F32), logf], axis=1)
    out_b = fox_attention(fq, cat(past['fox_k'], fk), cat(past['fox_v'], fv), lf_all, q0)

    mkv = moba_kv.reshape(n, t, 2, N_HEADS, HEAD_DIM)
    mq, mk, mv = heads(moba_q, N_HEADS), mkv[:, :, 0], mkv[:, :, 1]
    out_c = moba_attention(mq, cat(past['moba_k'], mk), cat(past['moba_v'], mv), q0, alibi_slopes(N_HEADS))

    nq = heads(nsa_q, N_HEADS)
    nkv = nsa_kv.reshape(n, t, 2, 2, NSA_G, HEAD_DIM)
    kc, vc, ks, vs = nkv[:, :, 0, 0], nkv[:, :, 0, 1], nkv[:, :, 1, 0], nkv[:, :, 1, 1]
    nwin = nsa_win.reshape(n, t, 2, NSA_G, HEAD_DIM)
    kw, vw = nwin[:, :, 0], nwin[:, :, 1]
    ngate = jax.nn.sigmoid(z_small.reshape(n, t, SMALL_W)[..., N_HEADS:4 * N_HEADS]).reshape(n, t, NSA_G, NSA_R, 3)
    kw_all = cat(past['win_k'], kw)
    vw_all = cat(past['win_v'], vw)
    w_ofs = q0 - past['win_k'].shape[1]
    out_d = nsa_attention(nq, cat(past['nsa_kc'], kc), cat(past['nsa_vc'], vc),
                          cat(past['nsa_ks'], ks), cat(past['nsa_vs'], vs),
                          kw_all, vw_all, ngate, q0, w_ofs, alibi_slopes(N_HEADS), cmp_w1, cmp_w2, cmp_pe)
    branches = [a.reshape(n * t, bw) for a in (out_a, out_b, out_c, out_d)]
    wb = past['win_k'].shape[1]
    new_state = (new_conv, fkv, logf.astype(dt), mkv, nkv,
                 jnp.stack([kw_all, vw_all], axis=2)[:, -wb:])
    return branches, new_state


def decode_mixers(h, n, nt, pt, caches, state_conv, state_win, w, b_forget, conv_w, cmp_w, tm):
    bw = BRANCH_W
    kv = KV_W
    cache_fox, cache_lf, cache_moba, cache_nsa = caches
    (z_conv,) = proj_call(h, w["conv"], [("f32", 3 * bw)], [(0, 3 * bw, ((0, 0),))], tm, "proj_conv_s")
    fox_q, fox_kv = proj_call(h, w["fox"], [("f32", bw), ("f32", 2 * bw)],
                              [(0, bw, ((0, 0),)), (bw, 3 * bw, ((1, 0),))], tm, "proj_fox_s")
    moba_q, moba_kv = proj_call(h, w["moba"], [("f32", bw), ("f32", 2 * bw)],
                                [(0, bw, ((0, 0),)), (bw, 3 * bw, ((1, 0),))], tm, "proj_moba_s")
    nsa_q, nsa_kv, nsa_win = proj_call(
        h, w["nsa"], [("f32", bw), ("f32", 4 * kv), ("f32", 2 * kv)],
        [(0, bw, ((0, 0),)), (bw, bw + 4 * kv, ((1, 0),)), (bw + 4 * kv, bw + 6 * kv, ((2, 0),))], tm, "proj_nsa_s")
    (z_small,) = proj_call(h, w["small"], [("f32", SMALL_W)], [(0, SMALL_W, ((0, 0),))], tm, "proj_small_s")

    def new_page(a):
        at = jnp.transpose(a.reshape(n, nt, a.shape[-1]), (0, 2, 1))
        return jnp.pad(at, ((0, 0), (0, 0), (0, PAGE - nt)))

    out_a, u = conv_sample_call(z_conv, state_conv, conv_w, n, nt)
    new_conv = u.reshape(n, nt, bw)[:, nt - (CONV_W - 1):]

    lg, _ = small_call(z_small, b_forget, 1, n * nt, n * nt)
    logf = lg[:, :N_HEADS]
    gate = lg[:, N_HEADS:4 * N_HEADS].reshape(n, nt * N_HEADS, 3)

    out_b = fox_decode_t_call(pt, fox_q.reshape(n, nt, bw), cache_fox, cache_lf, new_page(fox_kv), new_page(logf))
    mq = moba_q.reshape(n, nt, bw)
    sel = moba_select_t_call(pt, mq, cache_moba)
    out_c = moba_decode_t_call(pt, mq, sel, cache_moba, new_page(moba_kv))

    npages = pt.shape[1]
    kcmp, vcmp, _ = compress_t_call(cache_nsa, pt, *cmp_w, n, npages)
    win = state_win.reshape(n, state_win.shape[1], 2 * kv)
    out_d = nsa_decode_t_call(pt, nsa_q.reshape(n, nt, bw), gate, kcmp, vcmp, cache_nsa,
                              new_page(nsa_kv[:, 2 * kv:]), jnp.transpose(win, (0, 2, 1)), new_page(nsa_win))

    wb = win.shape[1]
    win_all = jnp.concatenate([win, nsa_win.reshape(n, nt, 2 * kv)], axis=1)[:, nt:]
    new_state = (new_conv,
                 fox_kv.reshape(n, nt, 2, N_HEADS, HEAD_DIM),
                 logf.reshape(n, nt, N_HEADS),
                 moba_kv.reshape(n, nt, 2, N_HEADS, HEAD_DIM),
                 nsa_kv.reshape(n, nt, 2, 2, NSA_G, HEAD_DIM),
                 win_all.reshape(n, wb, 2, NSA_G, HEAD_DIM))
    branches = [out_a] + [o.reshape(n * nt, bw) for o in (out_b, out_c, out_d)]
    return branches, new_state


def kernel(x_prompt, x_sample, state_conv, cache_fox_kv, cache_fox_logf, cache_moba_kv, cache_nsa_kv,
           state_nsa_win, page_table, g_mix, w_in, b_forget, conv_w, cmp_w1, cmp_w2, cmp_pe,
           w_branch, w_out, g_mlp, w_up, w_down, g_final):
    depth = w_in.shape[0]
    nb, seq, _ = x_prompt.shape
    db, dseq, _ = x_sample.shape
    tm_p, tm_s = 512, db * dseq
    pool, page = cache_fox_kv.shape[1], cache_fox_kv.shape[2]
    def pages_t(c, width):
        return jnp.transpose(c.reshape(depth * pool, page, width), (0, 2, 1))

    caches = (pages_t(cache_fox_kv, 2 * BRANCH_W), pages_t(cache_fox_logf, N_HEADS),
              pages_t(cache_moba_kv, 2 * BRANCH_W), pages_t(cache_nsa_kv, 4 * KV_W))

    xp = x_prompt.reshape(nb * seq, D_MODEL)
    xs = x_sample.reshape(db * dseq, D_MODEL)
    hp = rms_norm_call(xp, g_mix[0], BF16, tm_p)
    hs = rms_norm_call(xs, g_mix[0], BF16, tm_s)
    new_p, new_s = [], []
    for l in range(depth):
        w = _layer_weights(l, w_in, w_branch, w_out, w_up, w_down)
        cmp_w = _compress_weights(cmp_w1[l], cmp_w2[l], cmp_pe[l])
        last = l == depth - 1
        g_next = g_final if last else g_mix[l + 1]
        next_dtype = F32 if last else BF16

        branches, st_p = prompt_mixers(hp, nb, seq, w, b_forget[l], conv_w[l], cmp_w, tm_p)
        xp, hp = _finish_layer(xp, hp, branches, w, g_mlp[l], g_next, next_dtype, tm_p)
        new_p.append(st_p)

        branches, st_s = decode_mixers(hs, db, dseq, page_table + l * pool, caches, state_conv[l],
                                       state_nsa_win[l], w, b_forget[l], conv_w[l], cmp_w, tm_s)
        xs, hs = _finish_layer(xs, hs, branches, w, g_mlp[l], g_next, next_dtype, tm_s)
        new_s.append(st_s)
    y_prompt = hp.reshape(nb, seq, D_MODEL)
    y_sample = hs.reshape(db, dseq, D_MODEL)
    conv_p, fox_kv_p, fox_logf_p, moba_kv_p, nsa_kv_p, win_p = [jnp.stack(a) for a in zip(*new_p)]
    conv_s, fox_kv_s, fox_logf_s, moba_kv_s, nsa_kv_s, win_s = [jnp.stack(a) for a in zip(*new_s)]
    return (y_prompt, y_sample, conv_p, conv_s, fox_kv_p, fox_kv_s, fox_logf_p, fox_logf_s,
            moba_kv_p, moba_kv_s, nsa_kv_p, nsa_kv_s, win_p, win_s)
```
